```python
import math
import jax, jax.numpy as jnp
from jax import lax
import numpy as np

D_MODEL = 1024
BATCH = 2
SEQ = 8192
DEPTH = 2

PLE_DIM = 256
NORM_EPS = 1e-6
NEG_INF = -1e30
N_BRANCH = 4
BRANCH_WIDTH = D_MODEL // 2
ATTN_BLOCK = 128

SWA_HEADS = 8
SWA_KV_HEADS = 2
SWA_HEAD_DIM = 64
SWA_WINDOW = 128

HGRN_HEADS = 4
HGRN_KEY_DIM = 128
HGRN_VAL_DIM = 128
HGRN_CHUNK = 32
HGRN_MIN_F = 1e-6

NSA_HEADS = 8
NSA_KV_HEADS = 2
NSA_HEAD_DIM = 64
NSA_CMP_LEN = 32
NSA_CMP_STRIDE = 16
NSA_CMP_HIDDEN = 256
NSA_SEL_LEN = 64
NSA_TOP_N = 16
NSA_WINDOW = 512
NSA_FORCE_SCORE = 1e6

SSM_HEADS = 8
SSM_HEAD_DIM = 64
SSM_GROUPS = 2
SSM_STATE = 64
SSM_CONV = 4
SSM_CHUNK = 64
SSM_INNER = SSM_HEADS * SSM_HEAD_DIM
SSM_CONV_DIM = SSM_INNER + 2 * SSM_GROUPS * SSM_STATE

MOE_GROUPS = 4
MOE_EXPERTS_PER_GROUP = 8
MOE_EXPERTS = MOE_GROUPS * MOE_EXPERTS_PER_GROUP
MOE_TOP_K = 2
MOE_FF = D_MODEL // 4

IN_SPLITS = (
    ('swa_q', SWA_HEADS * SWA_HEAD_DIM),
    ('swa_k', SWA_KV_HEADS * SWA_HEAD_DIM),
    ('swa_v', SWA_KV_HEADS * SWA_HEAD_DIM),
    ('hgrn_q', HGRN_HEADS * HGRN_KEY_DIM),
    ('hgrn_f', HGRN_HEADS * HGRN_KEY_DIM),
    ('hgrn_i', HGRN_HEADS * HGRN_VAL_DIM),
    ('hgrn_g', HGRN_HEADS * HGRN_VAL_DIM),
    ('nsa_q', NSA_HEADS * NSA_HEAD_DIM),
    ('nsa_k_cmp', NSA_KV_HEADS * NSA_HEAD_DIM),
    ('nsa_v_cmp', NSA_KV_HEADS * NSA_HEAD_DIM),
    ('nsa_k_slc', NSA_KV_HEADS * NSA_HEAD_DIM),
    ('nsa_v_slc', NSA_KV_HEADS * NSA_HEAD_DIM),
    ('nsa_k_win', NSA_KV_HEADS * NSA_HEAD_DIM),
    ('nsa_v_win', NSA_KV_HEADS * NSA_HEAD_DIM),
    ('nsa_gate', NSA_HEADS * 3),
    ('ssm_z', SSM_INNER),
    ('ssm_xbc', SSM_CONV_DIM),
    ('ssm_dt', SSM_HEADS),
    ('merge_gate', N_BRANCH * D_MODEL),
)
IN_WIDTH = sum(w for _, w in IN_SPLITS)

kernel_name = 'hybrid_parallel_swa_hgrn2_nsa_ssd_hmoe'


def rmsnorm(x, g):
    xf = x.astype(jnp.float32)
    y = xf * lax.rsqrt(jnp.mean(xf * xf, axis=-1, keepdims=True) + NORM_EPS)
    return (y * g.astype(jnp.float32)).astype(x.dtype)


def alibi_slopes(n_heads):
    return 2.0 ** (-8.0 * jnp.arange(1, n_heads + 1, dtype=jnp.float32) / n_heads)


def split_proj(proj):
    out = {}
    off = 0
    for name, w in IN_SPLITS:
        out[name] = proj[..., off:off + w]
        off += w
    return out


def banded_attention(q, k, v, window, slopes, sinks=None):
    B, L, H, d = q.shape
    G = k.shape[2]
    R = H // G
    blk = ATTN_BLOCK
    nb = L // blk
    nprev = (window - 1 + blk - 1) // blk
    nk = (nprev + 1) * blk
    qb = q.reshape(B, nb, blk, G, R, d)
    pad = ((0, 0), (nprev * blk, 0), (0, 0), (0, 0))
    kp = jnp.pad(k, pad).reshape(B, nb + nprev, blk, G, d)
    vp = jnp.pad(v, pad).reshape(B, nb + nprev, blk, G, d)
    kb = jnp.concatenate([kp[:, j:j + nb] for j in range(nprev + 1)], axis=2)
    vb = jnp.concatenate([vp[:, j:j + nb] for j in range(nprev + 1)], axis=2)
    s = jnp.einsum('bnqgrd,bnkgd->bngrqk', qb, kb).astype(jnp.float32) * (d ** -0.5)
    qi = jnp.arange(blk)[:, None]
    kj = jnp.arange(nk)[None, :]
    rel = qi + nprev * blk - kj
    kpos = jnp.arange(nb)[:, None, None] * blk + kj - nprev * blk
    mask = (rel >= 0) & (rel < window) & (kpos >= 0)
    s = s - slopes.reshape(G, R)[:, :, None, None] * rel.astype(jnp.float32)
    s = jnp.where(mask[None, :, None, None], s, NEG_INF)
    if sinks is not None:
        sink = sinks.astype(jnp.float32).reshape(1, 1, G, R, 1, 1)
        m = jnp.maximum(jnp.max(s, axis=-1, keepdims=True), sink)
        e = jnp.exp(s - m)
        pr = e / (jnp.sum(e, axis=-1, keepdims=True) + jnp.exp(sink - m))
    else:
        pr = jax.nn.softmax(s, axis=-1)
    o = jnp.einsum('bngrqk,bnkgd->bnqgrd', pr.astype(v.dtype), vb)
    return o.reshape(B, L, H * d).astype(q.dtype)


def swa_mixer(seg, sinks, slopes):
    B, L, _ = seg['swa_q'].shape
    q = seg['swa_q'].reshape(B, L, SWA_HEADS, SWA_HEAD_DIM)
    k = seg['swa_k'].reshape(B, L, SWA_KV_HEADS, SWA_HEAD_DIM)
    v = seg['swa_v'].reshape(B, L, SWA_KV_HEADS, SWA_HEAD_DIM)
    return banded_attention(q, k, v, SWA_WINDOW, slopes, sinks)


def gla_chunked(q, k, v, log_f, chunk):
    B, L, H, dk = q.shape
    dv = v.shape[-1]
    nc = L // chunk
    f32 = jnp.float32
    qc = q.astype(f32).reshape(B, nc, chunk, H, dk)
    kc = k.astype(f32).reshape(B, nc, chunk, H, dk)
    vc = v.astype(f32).reshape(B, nc, chunk, H, dv)
    b = jnp.cumsum(log_f.astype(f32).reshape(B, nc, chunk, H, dk), axis=2)
    b_mid = b[:, :, chunk // 2:chunk // 2 + 1]
    b_last = b[:, :, -1:]
    A = jnp.einsum('bcthk,bcshk->bchts', qc * jnp.exp(b - b_mid), kc * jnp.exp(b_mid - b))
    causal = jnp.tril(jnp.ones((chunk, chunk), dtype=bool))
    A = jnp.where(causal, A, 0.0)
    o_intra = jnp.einsum('bchts,bcshv->bcthv', A, vc)
    u = jnp.einsum('bcshk,bcshv->bchkv', kc * jnp.exp(b_last - b), vc)
    decay = jnp.exp(b_last[:, :, 0])

    def step(S, inp):
        d_c, u_c = inp
        return d_c[..., None] * S + u_c, S

    _, S_prev = lax.scan(step, jnp.zeros((B, H, dk, dv), f32),
                         (jnp.moveaxis(decay, 1, 0), jnp.moveaxis(u, 1, 0)))
    S_prev = jnp.moveaxis(S_prev, 0, 1)
    o_inter = jnp.einsum('bcthk,bchkv->bcthv', qc * jnp.exp(b), S_prev)
    return (o_intra + o_inter).reshape(B, L, H, dv).astype(v.dtype)


def hgrn2_mixer(seg, lb, norm_g):
    B, L, _ = seg['hgrn_q'].shape
    H, dk, dv = HGRN_HEADS, HGRN_KEY_DIM, HGRN_VAL_DIM
    q = seg['hgrn_q'].reshape(B, L, H, dk)
    z = seg['hgrn_f'].reshape(B, L, H, dk).astype(jnp.float32)
    i_in = seg['hgrn_i'].reshape(B, L, H, dv)
    lb = lb.reshape(H, dk)
    f = lb + (1.0 - lb) * jax.nn.sigmoid(z)
    log_f = jnp.log(jnp.maximum(f, HGRN_MIN_F))
    k = (1.0 - lb) * jax.nn.sigmoid(-z)
    o = gla_chunked(q, k, i_in, log_f, HGRN_CHUNK)
    o = rmsnorm(o, norm_g.reshape(H, dv)).reshape(B, L, H * dv)
    return o * jax.nn.silu(seg['hgrn_g'])


def nsa_mixer(seg, pos_k, pos_v, w1_k, w2_k, w1_v, w2_v, slopes):
    B, L, _ = seg['nsa_q'].shape
    H, G, d = NSA_HEADS, NSA_KV_HEADS, NSA_HEAD_DIM
    R = H // G
    blk = ATTN_BLOCK
    nb = L // blk
    scale = d ** -0.5
    q = seg['nsa_q'].reshape(B, L, H, d)

    def kvh(name):
        return seg[name].reshape(B, L, G, d)

    n_cmp = (L - NSA_CMP_LEN) // NSA_CMP_STRIDE + 1
    cmp_start = np.arange(n_cmp, dtype=np.int32) * NSA_CMP_STRIDE
    cmp_idx = cmp_start[:, None] + np.arange(NSA_CMP_LEN, dtype=np.int32)[None, :]
    cmp_end = cmp_start + NSA_CMP_LEN - 1

    def compress(kx, pos, w1, w2):
        blocks = kx[:, cmp_idx] + pos[None, None, :, None, :]
        flat = blocks.transpose(0, 1, 3, 2, 4).reshape(B, n_cmp, G, NSA_CMP_LEN * d)
        return jax.nn.gelu(flat @ w1) @ w2

    k_cmp = compress(kvh('nsa_k_cmp'), pos_k, w1_k, w2_k)
    v_cmp = compress(kvh('nsa_v_cmp'), pos_v, w1_v, w2_v)

    n_sel_blocks = L // NSA_SEL_LEN
    top_n = min(NSA_TOP_N, n_sel_blocks)
    sel_start = np.arange(n_sel_blocks, dtype=np.int32) * NSA_SEL_LEN
    overlap = np.clip(np.minimum(cmp_start[:, None] + NSA_CMP_LEN, sel_start[None, :] + NSA_SEL_LEN)
                      - np.maximum(cmp_start[:, None], sel_start[None, :]), 0, None) / NSA_CMP_LEN
    overlap = jnp.asarray(overlap, dtype=jnp.float32)
    k_sel = kvh('nsa_k_slc').reshape(B, n_sel_blocks, NSA_SEL_LEN, G, d).transpose(0, 3, 1, 2, 4)
    v_sel = kvh('nsa_v_slc').reshape(B, n_sel_blocks, NSA_SEL_LEN, G, d).transpose(0, 3, 1, 2, 4)
    sl = slopes.reshape(G, R)
    b_idx = jnp.arange(B)[:, None, None, None]
    g_idx = jnp.arange(G)[None, :, None, None]
    jblk = jnp.arange(n_sel_blocks)
    q_blocks = q.reshape(B, nb, blk, G, R, d).transpose(1, 0, 2, 3, 4, 5)

    def per_block(args):
        qb, n = args
        qpos = n * blk + jnp.arange(blk)
        dist_c = (qpos[:, None] - cmp_end[None, :]).astype(jnp.float32)
        valid_c = dist_c >= 0
        s_c = jnp.einsum('bqgrd,bcgd->bgrqc', qb, k_cmp).astype(jnp.float32) * scale
        s_c = jnp.where(valid_c, s_c - sl[:, :, None, None] * dist_c, NEG_INF)
        p_c = jnp.where(valid_c, jax.nn.softmax(s_c, axis=-1), 0.0)
        o_c = jnp.einsum('bgrqc,bcgd->bqgrd', p_c.astype(v_cmp.dtype), v_cmp)
        imp = jnp.einsum('bgrqc,cj->bgqj', p_c, overlap)
        cur = qpos[:, None] // NSA_SEL_LEN
        forced = (jblk[None, :] == 0) | (jblk[None, :] == cur) | (jblk[None, :] == cur - 1)
        causal_blk = jblk[None, :] * NSA_SEL_LEN <= qpos[:, None]
        score = jnp.where(forced, NSA_FORCE_SCORE, jnp.where(causal_blk, imp, NEG_INF))
        _, sel = lax.top_k(score, top_n)
        ks = k_sel[b_idx, g_idx, sel]
        vs = v_sel[b_idx, g_idx, sel]
        kpos = sel[..., None] * NSA_SEL_LEN + jnp.arange(NSA_SEL_LEN)
        dist_s = (qpos[None, None, :, None, None] - kpos).astype(jnp.float32)
        valid_s = (dist_s >= 0)[:, :, None]
        s_s = jnp.einsum('bqgrd,bgqjkd->bgrqjk', qb, ks).astype(jnp.float32) * scale
        s_s = jnp.where(valid_s, s_s - sl[None, :, :, None, None, None] * dist_s[:, :, None], NEG_INF)
        p_s = jax.nn.softmax(s_s.reshape(B, G, R, blk, top_n * NSA_SEL_LEN), axis=-1)
        p_s = p_s.reshape(s_s.shape).astype(vs.dtype)
        o_s = jnp.einsum('bgrqjk,bgqjkd->bqgrd', p_s, vs)
        return o_c, o_s

    o_c, o_s = lax.map(per_block, (q_blocks, jnp.arange(nb)))
    o_c = o_c.transpose(1, 0, 2, 3, 4, 5).reshape(B, L, H, d)
    o_s = o_s.transpose(1, 0, 2, 3, 4, 5).reshape(B, L, H, d)
    o_w = banded_attention(q, kvh('nsa_k_win'), kvh('nsa_v_win'), NSA_WINDOW, slopes).reshape(B, L, H, d)
    gate = jax.nn.sigmoid(seg['nsa_gate'].reshape(B, L, H, 3))
    o = gate[..., 0:1] * o_c + gate[..., 1:2] * o_s + gate[..., 2:3] * o_w
    return o.reshape(B, L, H * d)


def causal_conv(x, w, b):
    K = w.shape[0]
    L = x.shape[1]
    xp = jnp.pad(x, ((0, 0), (K - 1, 0), (0, 0)))
    y = b + w[0] * xp[:, 0:L]
    for j in range(1, K):
        y = y + w[j] * xp[:, j:j + L]
    return y


def ssd_chunked(x, dt, A, Bm, Cm, chunk):
    Bsz, L, G, R, P = x.shape
    N = Bm.shape[-1]
    nc = L // chunk
    f32 = jnp.float32
    xc = x.astype(f32).reshape(Bsz, nc, chunk, G, R, P)
    dtc = dt.astype(f32).reshape(Bsz, nc, chunk, G, R)
    Bc = Bm.astype(f32).reshape(Bsz, nc, chunk, G, N)
    Cc = Cm.astype(f32).reshape(Bsz, nc, chunk, G, N)
    acs = jnp.cumsum(dtc * A.reshape(G, R), axis=2)
    at = acs.transpose(0, 1, 3, 4, 2)
    seg = at[..., :, None] - at[..., None, :]
    causal = jnp.tril(jnp.ones((chunk, chunk), dtype=bool))
    Lmat = jnp.exp(jnp.where(causal, seg, NEG_INF))
    CB = jnp.einsum('bctgn,bcsgn->bcgts', Cc, Bc)
    y_diag = jnp.einsum('bcgts,bcgrts,bcsgr,bcsgrp->bctgrp', CB, Lmat, dtc, xc)
    decay_s = jnp.exp(acs[:, :, -1:] - acs)
    states = jnp.einsum('bcsgn,bcsgr,bcsgrp->bcgrpn', Bc, decay_s * dtc, xc)
    chunk_decay = jnp.exp(acs[:, :, -1])

    def step(S, inp):
        d_c, s_c = inp
        return d_c[..., None, None] * S + s_c, S

    _, S_prev = lax.scan(step, jnp.zeros((Bsz, G, R, P, N), f32),
                         (jnp.moveaxis(chunk_decay, 1, 0), jnp.moveaxis(states, 1, 0)))
    S_prev = jnp.moveaxis(S_prev, 0, 1)
    y_off = jnp.einsum('bctgn,bcgrpn,bctgr->bctgrp', Cc, S_prev, jnp.exp(acs))
    return (y_diag + y_off).reshape(Bsz, L, G, R, P)


def mamba2_mixer(seg, conv_w, conv_b, dt_bias, A_log, D_skip, norm_g):
    B, L, _ = seg['ssm_z'].shape
    G, N, P = SSM_GROUPS, SSM_STATE, SSM_HEAD_DIM
    R = SSM_HEADS // G
    xbc = jax.nn.silu(causal_conv(seg['ssm_xbc'], conv_w, conv_b))
    xs = xbc[..., :SSM_INNER].reshape(B, L, G, R, P)
    Bm = xbc[..., SSM_INNER:SSM_INNER + G * N].reshape(B, L, G, N)
    Cm = xbc[..., SSM_INNER + G * N:].reshape(B, L, G, N)
    dt = jax.nn.softplus(seg['ssm_dt'].astype(jnp.float32) + dt_bias.astype(jnp.float32))
    A = -jnp.exp(A_log.astype(jnp.float32))
    y = ssd_chunked(xs, dt, A, Bm, Cm, SSM_CHUNK)
    y = y + D_skip.astype(jnp.float32).reshape(G, R)[:, :, None] * xs.astype(jnp.float32)
    y = y.reshape(B, L, SSM_INNER).astype(seg['ssm_z'].dtype)
    return rmsnorm(y * jax.nn.silu(seg['ssm_z']), norm_g)


def hier_moe(h, w_grp, w_exp, w_gate, w_up, w_down):
    B, L, D = h.shape
    T = B * L
    ht = h.reshape(T, D)
    grp_p = jax.nn.softmax((ht @ w_grp).astype(jnp.float32), axis=-1)
    g_w, g_i = lax.top_k(grp_p, 1)
    exp_logits = (ht @ w_exp).astype(jnp.float32).reshape(T, MOE_GROUPS, MOE_EXPERTS_PER_GROUP)
    in_grp = jnp.take_along_axis(exp_logits, g_i[:, :, None], axis=1)[:, 0]
    e_l, e_i = lax.top_k(in_grp, MOE_TOP_K)
    e_w = jax.nn.softmax(e_l, axis=-1) * g_w
    expert_id = g_i * MOE_EXPERTS_PER_GROUP + e_i
    comb = jnp.sum(jax.nn.one_hot(expert_id, MOE_EXPERTS, dtype=jnp.float32) * e_w[..., None], axis=1)
    comb = comb.astype(h.dtype)
    out = jnp.zeros_like(ht)
    for g in range(MOE_GROUPS):
        sl = slice(g * MOE_EXPERTS_PER_GROUP, (g + 1) * MOE_EXPERTS_PER_GROUP)
        a = jnp.einsum('td,edf->tef', ht, w_gate[sl])
        u = jnp.einsum('td,edf->tef', ht, w_up[sl])
        out = out + jnp.einsum('tef,efd,te->td', jax.nn.silu(a) * u, w_down[sl], comb[:, sl])
    return out.reshape(B, L, D)


def setup_inputs(seed: int = 0) -> dict:
    key = jax.random.key(seed)
    ks = jax.random.split(key, 31)
    f32 = jnp.float32

    def nrm(k, shape, scale):
        return jax.random.normal(k, shape, f32) * scale

    def gain(k, shape):
        return 1.0 + 0.01 * jax.random.normal(k, shape, f32)

    cmp_in = NSA_CMP_LEN * NSA_HEAD_DIM
    dt0 = jnp.exp(jax.random.uniform(ks[16], (DEPTH, SSM_HEADS), f32,
                                     minval=math.log(1e-3), maxval=math.log(1e-1)))
    return {
        'x': nrm(ks[0], (BATCH, SEQ, D_MODEL), 1.0),
        'p': nrm(ks[1], (DEPTH, BATCH, SEQ, PLE_DIM), 1.0),
        'w_in': nrm(ks[2], (DEPTH, D_MODEL, IN_WIDTH), D_MODEL ** -0.5),
        'g_mix': gain(ks[3], (DEPTH, D_MODEL)),
        'attn_sinks': nrm(ks[4], (DEPTH, SWA_HEADS), 0.5),
        'hgrn_lower_bounds': nrm(ks[5], (DEPTH, HGRN_HEADS * HGRN_KEY_DIM), 0.1),
        'hgrn_norm_g': gain(ks[6], (DEPTH, HGRN_HEADS * HGRN_VAL_DIM)),
        'nsa_pos_k': nrm(ks[7], (DEPTH, NSA_CMP_LEN, NSA_HEAD_DIM), 0.1),
        'nsa_pos_v': nrm(ks[8], (DEPTH, NSA_CMP_LEN, NSA_HEAD_DIM), 0.1),
        'nsa_cmp_w1_k': nrm(ks[9], (DEPTH, cmp_in, NSA_CMP_HIDDEN), cmp_in ** -0.5),
        'nsa_cmp_w2_k': nrm(ks[10], (DEPTH, NSA_CMP_HIDDEN, NSA_HEAD_DIM), NSA_CMP_HIDDEN ** -0.5),
        'nsa_cmp_w1_v': nrm(ks[11], (DEPTH, cmp_in, NSA_CMP_HIDDEN), cmp_in ** -0.5),
        'nsa_cmp_w2_v': nrm(ks[12], (DEPTH, NSA_CMP_HIDDEN, NSA_HEAD_DIM), NSA_CMP_HIDDEN ** -0.5),
        'ssm_conv_w': nrm(ks[13], (DEPTH, SSM_CONV, SSM_CONV_DIM), SSM_CONV ** -0.5),
        'ssm_conv_b': nrm(ks[14], (DEPTH, SSM_CONV_DIM), 0.01),
        'ssm_dt_bias': dt0 + jnp.log(-jnp.expm1(-dt0)),
        'ssm_A_log': jnp.log(jax.random.uniform(ks[15], (DEPTH, SSM_HEADS), f32, minval=1.0, maxval=16.0)),
        'ssm_D': gain(ks[17], (DEPTH, SSM_HEADS)),
        'ssm_norm_g': gain(ks[18], (DEPTH, SSM_INNER)),
        'w_branch': nrm(ks[19], (DEPTH, N_BRANCH, BRANCH_WIDTH, D_MODEL), BRANCH_WIDTH ** -0.5),
        'w_out': nrm(ks[20], (DEPTH, D_MODEL, D_MODEL), D_MODEL ** -0.5),
        'g_ffn': gain(ks[21], (DEPTH, D_MODEL)),
        'w_router_grp': nrm(ks[22], (DEPTH, D_MODEL, MOE_GROUPS), D_MODEL ** -0.5),
        'w_router_exp': nrm(ks[23], (DEPTH, D_MODEL, MOE_EXPERTS), D_MODEL ** -0.5),
        'w_exp_gate': nrm(ks[24], (DEPTH, MOE_EXPERTS, D_MODEL, MOE_FF), D_MODEL ** -0.5),
        'w_exp_up': nrm(ks[25], (DEPTH, MOE_EXPERTS, D_MODEL, MOE_FF), D_MODEL ** -0.5),
        'w_exp_down': nrm(ks[26], (DEPTH, MOE_EXPERTS, MOE_FF, D_MODEL), MOE_FF ** -0.5),
        'g_ple': gain(ks[27], (DEPTH, D_MODEL)),
        'w_ple_gate': nrm(ks[28], (DEPTH, D_MODEL, D_MODEL), D_MODEL ** -0.5),
        'w_ple_proj': nrm(ks[29], (DEPTH, PLE_DIM, D_MODEL), PLE_DIM ** -0.5),
        'g_final': gain(ks[30], (D_MODEL,)),
    }


def reference(x, p, w_in, g_mix, attn_sinks, hgrn_lower_bounds, hgrn_norm_g,
              nsa_pos_k, nsa_pos_v, nsa_cmp_w1_k, nsa_cmp_w2_k, nsa_cmp_w1_v, nsa_cmp_w2_v,
              ssm_conv_w, ssm_conv_b, ssm_dt_bias, ssm_A_log, ssm_D, ssm_norm_g,
              w_branch, w_out, g_ffn, w_router_grp, w_router_exp,
              w_exp_gate, w_exp_up, w_exp_down, g_ple, w_ple_gate, w_ple_proj, g_final):
    B, L, D = x.shape
    swa_slopes = alibi_slopes(SWA_HEADS)
    nsa_slopes = alibi_slopes(NSA_HEADS)
    sm = jax.nn.softmax(hgrn_lower_bounds.astype(jnp.float32), axis=0)
    lbs = jnp.cumsum(sm, axis=0) - sm[0:1]
    for i in range(DEPTH):
        h = rmsnorm(x, g_mix[i])
        seg = split_proj(h @ w_in[i])
        y_a = swa_mixer(seg, attn_sinks[i], swa_slopes)
        y_b = hgrn2_mixer(seg, lbs[i], hgrn_norm_g[i])
        y_c = nsa_mixer(seg, nsa_pos_k[i], nsa_pos_v[i], nsa_cmp_w1_k[i], nsa_cmp_w2_k[i],
                        nsa_cmp_w1_v[i], nsa_cmp_w2_v[i], nsa_slopes)
        y_d = mamba2_mixer(seg, ssm_conv_w[i], ssm_conv_b[i], ssm_dt_bias[i], ssm_A_log[i],
                           ssm_D[i], ssm_norm_g[i])
        ys = jnp.stack([y_a, y_b, y_c, y_d], axis=0)
        yp = jnp.einsum('nblc,nce->nble', ys, w_branch[i])
        gates = jax.nn.sigmoid(seg['merge_gate'].reshape(B, L, N_BRANCH, D))
        u = jnp.einsum('blne,nble->ble', gates, yp)
        x = x + u @ w_out[i]
        x = x + hier_moe(rmsnorm(x, g_ffn[i]), w_router_grp[i], w_router_exp[i],
                         w_exp_gate[i], w_exp_up[i], w_exp_down[i])
        ple_gate = jax.nn.sigmoid(rmsnorm(x, g_ple[i]) @ w_ple_gate[i])
        x = x + (p[i] @ w_ple_proj[i]) * ple_gate
    return rmsnorm(x, g_final)
```

```python
import functools
import math

import numpy as np
import jax
import jax.numpy as jnp
from jax import lax
from jax.experimental import pallas as pl
from jax.experimental.pallas import tpu as pltpu

F32 = jnp.float32
BF16 = jnp.bfloat16
HIGHEST = lax.Precision.HIGHEST

D_MODEL = 1024
PLE_DIM = 256
NORM_EPS = 1e-6
NEG_INF = -1e30
REMOVED = -3e38
N_BRANCH = 4
BRANCH_WIDTH = 512
ATTN_BLOCK = 128

ATT_HEADS = 8
ATT_KV_HEADS = 2
ATT_HEAD_DIM = 64
SWA_WINDOW = 128
NSA_WINDOW = 512
NSA_CMP_LEN = 32
NSA_CMP_STRIDE = 16
NSA_CMP_HIDDEN = 256
NSA_SEL_LEN = 64
NSA_TOP_N = 16
NSA_FORCE_SCORE = 1e6
SEL_KEY_TILE = 512

HGRN_HEADS = 4
HGRN_DIM = 128
HGRN_CHUNK = 32
HGRN_MIN_F = 1e-6
HGRN_BLOCK = 256

SSM_HEADS = 8
SSM_HEAD_DIM = 64
SSM_GROUPS = 2
SSM_STATE = 64
SSM_CONV = 4
SSM_CHUNK = 64
SSM_INNER = 512
SSM_CONV_DIM = 768
SSM_BLOCK = 256

MOE_GROUPS = 4
MOE_EPG = 8
MOE_EXPERTS = 32
MOE_FF = 256

LANES = 128
VMEM_LIMIT = 56 * 1024 * 1024

ALIBI_SLOPES = tuple(2.0 ** (-8.0 * (h + 1) / ATT_HEADS) for h in range(ATT_HEADS))

_SPLITS = (
    ('swa_q', 512), ('swa_k', 128), ('swa_v', 128),
    ('hgrn_q', 512), ('hgrn_f', 512), ('hgrn_i', 512), ('hgrn_g', 512),
    ('nsa_q', 512), ('nsa_k_cmp', 128), ('nsa_v_cmp', 128), ('nsa_k_slc', 128),
    ('nsa_v_slc', 128), ('nsa_k_win', 128), ('nsa_v_win', 128), ('nsa_gate', 24),
    ('ssm_z', 512), ('ssm_xbc', 768), ('ssm_dt', 8), ('merge_gate', 4096),
)
_OFF = {}
_o = 0
for _n, _w in _SPLITS:
    _OFF[_n] = (_o, _w)
    _o += _w


def _cparams(*sem):
    return pltpu.CompilerParams(dimension_semantics=sem, vmem_limit_bytes=VMEM_LIMIT)


def _sigmoid(x):
    return 1.0 / (1.0 + jnp.exp(-x))


def _silu(x):
    return x * _sigmoid(x)


def _dot(a, b):
    return jnp.dot(a.astype(BF16), b.astype(BF16), preferred_element_type=F32)


def _dot_nt(a, b):
    return lax.dot_general(a.astype(BF16), b.astype(BF16), (((1,), (1,)), ((), ())),
                           preferred_element_type=F32)


def _dot_tn(a, b):
    return lax.dot_general(a.astype(BF16), b.astype(BF16), (((0,), (0,)), ((), ())),
                           preferred_element_type=F32)


def _rms(x, g):
    ms = jnp.mean(x * x, axis=-1, keepdims=True)
    return x * lax.rsqrt(ms + NORM_EPS) * g


def _norm_mm_body(x_ref, g_ref, w_ref, o_ref, hn_ref):
    @pl.when(pl.program_id(1) == 0)
    def _():
        hn_ref[...] = _rms(x_ref[...], g_ref[...]).astype(BF16)

    o_ref[...] = jnp.dot(hn_ref[...], w_ref[...], preferred_element_type=F32).astype(o_ref.dtype)


def norm_mm(x2d, g, w, out_dtype, tm, tn, name):
    m, k = x2d.shape
    n = w.shape[1]
    return pl.pallas_call(
        _norm_mm_body,
        grid=(m // tm, n // tn),
        in_specs=[pl.BlockSpec((tm, k), lambda i, j: (i, 0)),
                  pl.BlockSpec((1, k), lambda i, j: (0, 0)),
                  pl.BlockSpec((k, tn), lambda i, j: (0, j))],
        out_specs=pl.BlockSpec((tm, tn), lambda i, j: (i, j)),
        out_shape=jax.ShapeDtypeStruct((m, n), out_dtype),
        scratch_shapes=[pltpu.VMEM((tm, k), BF16)],
        compiler_params=_cparams("arbitrary", "arbitrary"),
        name=name,
    )(x2d, g.reshape(1, k), w)


def _banded_body(*refs, window, nprev, use_sink, gate_col):
    if use_sink:
        q_ref, k_ref, v_ref, sink_ref, o_ref = refs
    else:
        q_ref, k_ref, v_ref, gate_ref, o_ref = refs
    blk = ATTN_BLOCK
    hd = ATT_HEAD_DIM
    rep = ATT_HEADS // ATT_KV_HEADS
    n = pl.program_id(1)
    nk = (nprev + 1) * blk
    start = pl.multiple_of(jnp.maximum(n - nprev, 0) * blk, blk)
    kk = k_ref[0, pl.ds(start, nk), :]
    vv = v_ref[0, pl.ds(start, nk), :]
    q = q_ref[0]
    row = lax.broadcasted_iota(jnp.int32, (blk, nk), 0)
    col = lax.broadcasted_iota(jnp.int32, (blk, nk), 1)
    rel = (n * blk + row) - (start + col)
    valid = (rel >= 0) & (rel < window)
    relf = rel.astype(F32)
    scale = hd ** -0.5
    for h in range(ATT_HEADS):
        g = h // rep
        s = _dot_nt(q[:, h * hd:(h + 1) * hd], kk[:, g * hd:(g + 1) * hd])
        s = jnp.where(valid, s * scale - ALIBI_SLOPES[h] * relf, NEG_INF)
        m = jnp.max(s, axis=-1, keepdims=True)
        if use_sink:
            sk = sink_ref[h]
            m = jnp.maximum(m, sk)
        e = jnp.exp(s - m)
        l = jnp.sum(e, axis=-1, keepdims=True)
        if use_sink:
            l = l + jnp.exp(sk - m)
        o = _dot(e, vv[:, g * hd:(g + 1) * hd]) / l
        if not use_sink:
            c = 3 * h + gate_col
            o = o * _sigmoid(gate_ref[0, :, c:c + 1])
        o_ref[0, :, h * hd:(h + 1) * hd] = o


def banded_attention(qkv, q_blk, k_blk, v_blk, *, window, sinks=None, gate=None, gate_blk=None,
                     gate_col=0, name):
    bsz, seq, _ = qkv.shape
    blk = ATTN_BLOCK
    nprev = (window - 1 + blk - 1) // blk
    use_sink = sinks is not None
    in_specs = [pl.BlockSpec((1, blk, 512), lambda b, n: (b, n, q_blk)),
                pl.BlockSpec((1, seq, LANES), lambda b, n: (b, 0, k_blk)),
                pl.BlockSpec((1, seq, LANES), lambda b, n: (b, 0, v_blk))]
    if use_sink:
        in_specs.append(pl.BlockSpec(memory_space=pltpu.SMEM))
        extra = sinks
    else:
        in_specs.append(pl.BlockSpec((1, blk, LANES), lambda b, n: (b, n, gate_blk)))
        extra = gate
    return pl.pallas_call(
        functools.partial(_banded_body, window=window, nprev=nprev, use_sink=use_sink,
                          gate_col=gate_col),
        grid=(bsz, seq // blk),
        in_specs=in_specs,
        out_specs=pl.BlockSpec((1, blk, 512), lambda b, n: (b, n, 0)),
        out_shape=jax.ShapeDtypeStruct((bsz, seq, 512), F32),
        compiler_params=_cparams("arbitrary", "arbitrary"),
        name=name,
    )(qkv, qkv, qkv, extra)


def _compress_body(x_ref, w1a_ref, w1b_ref, w1_ref, pos_ref, w2_ref, o_ref):
    x = x_ref[0]
    p = jnp.dot(x, w1a_ref[...], preferred_element_type=F32)
    q = jnp.dot(x, w1b_ref[...], preferred_element_type=F32)
    ncp = x.shape[0]
    q = pltpu.roll(q, shift=ncp - 1, axis=0)
    posb = jnp.broadcast_to(pos_ref[...], (8, pos_ref.shape[1]))
    bias = _dot(posb, w1_ref[...])[0:1]
    hid = NSA_CMP_HIDDEN
    outs = []
    for g in range(ATT_KV_HEADS):
        pre = p[:, g * hid:(g + 1) * hid] + q[:, g * hid:(g + 1) * hid] + bias
        outs.append(_dot(jax.nn.gelu(pre), w2_ref[...]))
    o_ref[0] = jnp.concatenate(outs, axis=1).astype(o_ref.dtype)


def _expand_w1(w1, half):
    hd, hid, ng = ATT_HEAD_DIM, NSA_CMP_HIDDEN, ATT_KV_HEADS
    w = w1.reshape(NSA_CMP_LEN, hd, hid)[half * 16:(half + 1) * 16]
    eye = jnp.eye(ng, dtype=w1.dtype)
    out = jnp.einsum('ldj,gh->lgdhj', w, eye)
    return out.reshape(16 * ng * hd, ng * hid)


def nsa_compress(x16, pos, w1, w2, name):
    bsz, ncp, wid = x16.shape
    hid = NSA_CMP_HIDDEN
    w1a = _expand_w1(w1, 0).astype(BF16)
    w1b = _expand_w1(w1, 1).astype(BF16)
    return pl.pallas_call(
        _compress_body,
        grid=(bsz,),
        in_specs=[pl.BlockSpec((1, ncp, wid), lambda b: (b, 0, 0)),
                  pl.BlockSpec((wid, 2 * hid), lambda b: (0, 0)),
                  pl.BlockSpec((wid, 2 * hid), lambda b: (0, 0)),
                  pl.BlockSpec((NSA_CMP_LEN * ATT_HEAD_DIM, hid), lambda b: (0, 0)),
                  pl.BlockSpec((1, NSA_CMP_LEN * ATT_HEAD_DIM), lambda b: (0, 0)),
                  pl.BlockSpec((hid, ATT_HEAD_DIM), lambda b: (0, 0))],
        out_specs=pl.BlockSpec((1, ncp, LANES), lambda b: (b, 0, 0)),
        out_shape=jax.ShapeDtypeStruct((bsz, ncp, LANES), BF16),
        compiler_params=_cparams("arbitrary"),
        name=name,
    )(x16, w1a, w1b, w1.astype(BF16), pos.reshape(1, -1), w2.astype(BF16))


def _nsa_cmp_body(q_ref, kc_ref, vc_ref, ov_ref, gate_ref, o_ref, m_ref, *, top_n):
    blk = ATTN_BLOCK
    hd = ATT_HEAD_DIM
    rep = ATT_HEADS // ATT_KV_HEADS
    n = pl.program_id(1)
    q = q_ref[0]
    kc = kc_ref[0]
    vc = vc_ref[0]
    ncp = kc.shape[0]
    ns = ov_ref.shape[1]
    scale = hd ** -0.5
    row = lax.broadcasted_iota(jnp.int32, (blk, ncp), 0)
    col = lax.broadcasted_iota(jnp.int32, (blk, ncp), 1)
    dist = (n * blk + row) - (col * NSA_CMP_STRIDE + (NSA_CMP_LEN - 1))
    valid = dist >= 0
    distf = dist.astype(F32)
    srow = lax.broadcasted_iota(jnp.int32, (blk, ns), 0)
    scol = lax.broadcasted_iota(jnp.int32, (blk, ns), 1)
    qpos = n * blk + srow
    cur = qpos // NSA_SEL_LEN
    forced = (scol == 0) | (scol == cur) | (scol == cur - 1)
    causal_blk = scol * NSA_SEL_LEN <= qpos
    scolf = scol.astype(F32)
    for g in range(ATT_KV_HEADS):
        psum = jnp.zeros((blk, ncp), F32)
        for r in range(rep):
            h = g * rep + r
            s = _dot_nt(q[:, h * hd:(h + 1) * hd], kc[:, g * hd:(g + 1) * hd])
            s = jnp.where(valid, s * scale - ALIBI_SLOPES[h] * distf, NEG_INF)
            m = jnp.max(s, axis=-1, keepdims=True)
            e = jnp.where(valid, jnp.exp(s - m), 0.0)
            l = jnp.sum(e, axis=-1, keepdims=True)
            p = jnp.where(l > 0.0, e / jnp.where(l > 0.0, l, 1.0), 0.0)
            o = _dot(p, vc[:, g * hd:(g + 1) * hd])
            o = o * _sigmoid(gate_ref[0, :, 3 * h:3 * h + 1])
            o_ref[0, :, h * hd:(h + 1) * hd] = o
            psum = psum + p
        imp = _dot(psum, ov_ref[...])
        score = jnp.where(forced, NSA_FORCE_SCORE, jnp.where(causal_blk, imp, NEG_INF))
        sel = jnp.zeros((blk, ns), F32)
        for _ in range(top_n):
            mx = jnp.max(score, axis=-1, keepdims=True)
            idx = jnp.min(jnp.where(score == mx, scolf, float(ns)), axis=-1, keepdims=True)
            hit = scolf == idx
            sel = jnp.where(hit, 1.0, sel)
            score = jnp.where(hit, REMOVED, score)
        m_ref[0, g] = sel.astype(m_ref.dtype)


def _overlap_matrix(seq):
    n_cmp = seq // NSA_CMP_STRIDE
    n_sel = seq // NSA_SEL_LEN
    cs = np.arange(n_cmp)[:, None] * NSA_CMP_STRIDE
    ss = np.arange(n_sel)[None, :] * NSA_SEL_LEN
    ov = np.clip(np.minimum(cs + NSA_CMP_LEN, ss + NSA_SEL_LEN) - np.maximum(cs, ss), 0, None)
    return jnp.asarray(ov / NSA_CMP_LEN, dtype=BF16)


def nsa_cmp_select(qkv, q_blk, kc, vc, gate, gate_blk, name):
    bsz, seq, _ = qkv.shape
    blk = ATTN_BLOCK
    ncp = seq // NSA_CMP_STRIDE
    ns = seq // NSA_SEL_LEN
    top_n = min(NSA_TOP_N, ns)
    return pl.pallas_call(
        functools.partial(_nsa_cmp_body, top_n=top_n),
        grid=(bsz, seq // blk),
        in_specs=[pl.BlockSpec((1, blk, 512), lambda b, n: (b, n, q_blk)),
                  pl.BlockSpec((1, ncp, LANES), lambda b, n: (b, 0, 0)),
                  pl.BlockSpec((1, ncp, LANES), lambda b, n: (b, 0, 0)),
                  pl.BlockSpec((ncp, ns), lambda b, n: (0, 0)),
                  pl.BlockSpec((1, blk, LANES), lambda b, n: (b, n, gate_blk))],
        out_specs=[pl.BlockSpec((1, blk, 512), lambda b, n: (b, n, 0)),
                   pl.BlockSpec((1, ATT_KV_HEADS, blk, ns), lambda b, n: (b, 0, n, 0))],
        out_shape=[jax.ShapeDtypeStruct((bsz, seq, 512), F32),
                   jax.ShapeDtypeStruct((bsz, ATT_KV_HEADS, seq, ns), BF16)],
        compiler_params=_cparams("arbitrary", "arbitrary"),
        name=name,
    )(qkv, kc, vc, _overlap_matrix(seq), gate)


def _nsa_sel_body(q_ref, k_ref, v_ref, m_ref, gate_ref, o_ref):
    blk = ATTN_BLOCK
    hd = ATT_HEAD_DIM
    rep = ATT_HEADS // ATT_KV_HEADS
    tk = SEL_KEY_TILE
    per = tk // NSA_SEL_LEN
    n = pl.program_id(1)
    ns = m_ref.shape[3]
    scale = hd ** -0.5
    q = q_ref[0]
    ntile = (n * blk + blk - 1) // tk + 1
    row = lax.broadcasted_iota(jnp.int32, (blk, tk), 0)
    col = lax.broadcasted_iota(jnp.int32, (blk, tk), 1)
    erow = lax.broadcasted_iota(jnp.int32, (ns, tk), 0)
    ecol = lax.broadcasted_iota(jnp.int32, (ns, tk), 1) // NSA_SEL_LEN
    for g in range(ATT_KV_HEADS):
        mg = m_ref[0, g]
        qs = [q[:, (g * rep + r) * hd:(g * rep + r + 1) * hd] for r in range(rep)]

        def step(t, carry):
            base = pl.multiple_of(t * tk, tk)
            kt = k_ref[0, pl.ds(base, tk), g * hd:(g + 1) * hd]
            vt = v_ref[0, pl.ds(base, tk), g * hd:(g + 1) * hd]
            expand = jnp.where(erow == ecol + t * per, 1.0, 0.0).astype(BF16)
            mexp = jnp.dot(mg, expand, preferred_element_type=F32)
            rel = (n * blk + row) - (base + col)
            valid = (mexp > 0.5) & (rel >= 0)
            relf = rel.astype(F32)
            new = []
            for r in range(rep):
                m, l, acc = carry[r]
                s = _dot_nt(qs[r], kt)
                s = jnp.where(valid, s * scale - ALIBI_SLOPES[g * rep + r] * relf, NEG_INF)
                m_new = jnp.maximum(m, jnp.max(s, axis=-1, keepdims=True))
                alpha = jnp.exp(m - m_new)
                p = jnp.where(valid, jnp.exp(s - m_new), 0.0)
                l = alpha * l + jnp.sum(p, axis=-1, keepdims=True)
                acc = alpha * acc + _dot(p, vt)
                new.append((m_new, l, acc))
            return tuple(new)

        init = tuple((jnp.full((blk, 1), NEG_INF, F32), jnp.zeros((blk, 1), F32),
                      jnp.zeros((blk, hd), F32)) for _ in range(rep))
        res = lax.fori_loop(0, ntile, step, init)
        for r in range(rep):
            h = g * rep + r
            _, l, acc = res[r]
            o = acc / l
            o = o * _sigmoid(gate_ref[0, :, 3 * h + 1:3 * h + 2])
            o_ref[0, :, h * hd:(h + 1) * hd] = o


def nsa_selected(qkv, q_blk, k_blk, v_blk, mask, gate, gate_blk, name):
    bsz, seq, _ = qkv.shape
    blk = ATTN_BLOCK
    ns = seq // NSA_SEL_LEN
    return pl.pallas_call(
        _nsa_sel_body,
        grid=(bsz, seq // blk),
        in_specs=[pl.BlockSpec((1, blk, 512), lambda b, n: (b, n, q_blk)),
                  pl.BlockSpec((1, seq, LANES), lambda b, n: (b, 0, k_blk)),
                  pl.BlockSpec((1, seq, LANES), lambda b, n: (b, 0, v_blk)),
                  pl.BlockSpec((1, ATT_KV_HEADS, blk, ns), lambda b, n: (b, 0, n, 0)),
                  pl.BlockSpec((1, blk, LANES), lambda b, n: (b, n, gate_blk))],
        out_specs=pl.BlockSpec((1, blk, 512), lambda b, n: (b, n, 0)),
        out_shape=jax.ShapeDtypeStruct((bsz, seq, 512), F32),
        compiler_params=_cparams("arbitrary", "arbitrary"),
        name=name,
    )(qkv, qkv, qkv, mask, gate)


def _hgrn_body(q_ref, f_ref, i_ref, g_ref, lbp_ref, ng_ref, o_ref, st_ref, b_ref, k_ref, *, layer):
    blk = HGRN_BLOCK
    ch = HGRN_CHUNK
    dk = HGRN_DIM

    @pl.when(pl.program_id(1) == 0)
    def _():
        st_ref[...] = jnp.zeros_like(st_ref)

    lbp = lbp_ref[...]
    e = jnp.exp(lbp - jnp.max(lbp, axis=0, keepdims=True))
    sm = e / jnp.sum(e, axis=0, keepdims=True)
    lb = jnp.zeros((1, lbp.shape[1]), F32)
    for d in range(1, layer + 1):
        lb = lb + sm[d:d + 1]
    z = f_ref[0]
    f = lb + (1.0 - lb) * _sigmoid(z)
    logf = jnp.log(jnp.maximum(f, HGRN_MIN_F))
    k_ref[...] = (1.0 - lb) * _sigmoid(-z)
    tr = lax.broadcasted_iota(jnp.int32, (blk, blk), 0)
    tc = lax.broadcasted_iota(jnp.int32, (blk, blk), 1)
    tri = jnp.where((tr // ch == tc // ch) & (tc <= tr), 1.0, 0.0).astype(F32)
    b_ref[...] = jnp.dot(tri, logf, precision=HIGHEST, preferred_element_type=F32)
    cr = lax.broadcasted_iota(jnp.int32, (ch, ch), 0)
    cc = lax.broadcasted_iota(jnp.int32, (ch, ch), 1)
    causal = cc <= cr
    ng = ng_ref[...]

    def chunk(c, carry):
        r0 = pl.multiple_of(c * ch, ch)
        bc = b_ref[pl.ds(r0, ch), :]
        qc = q_ref[0, pl.ds(r0, ch), :]
        kc = k_ref[pl.ds(r0, ch), :]
        vc = i_ref[0, pl.ds(r0, ch), :]
        gc = g_ref[0, pl.ds(r0, ch), :]
        b_mid = bc[ch // 2:ch // 2 + 1]
        b_last = bc[ch - 1:ch]
        qa = qc * jnp.exp(bc - b_mid)
        ka = kc * jnp.exp(b_mid - bc)
        qe = qc * jnp.exp(bc)
        kl = kc * jnp.exp(b_last - bc)
        dec = jnp.exp(b_last)
        for h in range(HGRN_HEADS):
            sl = slice(h * dk, (h + 1) * dk)
            a = jnp.where(causal, _dot_nt(qa[:, sl], ka[:, sl]), 0.0)
            st = st_ref[h]
            o = _dot(a, vc[:, sl]) + _dot_nt(qe[:, sl], st)
            st_ref[h] = st * dec[:, sl] + _dot_tn(vc[:, sl], kl[:, sl])
            o = _rms(o, ng[:, sl]) * _silu(gc[:, sl])
            o_ref[0, pl.ds(r0, ch), sl] = o
        return carry

    lax.fori_loop(0, blk // ch, chunk, 0)


def hgrn2(hproj, lower_bounds, norm_g, layer, name):
    bsz, seq, _ = hproj.shape
    blk = HGRN_BLOCK
    wid = HGRN_HEADS * HGRN_DIM
    depth = lower_bounds.shape[0]

    def col(j):
        return pl.BlockSpec((1, blk, wid), lambda b, n: (b, n, j))

    return pl.pallas_call(
        functools.partial(_hgrn_body, layer=layer),
        grid=(bsz, seq // blk),
        in_specs=[col(0), col(1), col(2), col(3),
                  pl.BlockSpec((depth, wid), lambda b, n: (0, 0)),
                  pl.BlockSpec((1, wid), lambda b, n: (0, 0))],
        out_specs=pl.BlockSpec((1, blk, wid), lambda b, n: (b, n, 0)),
        out_shape=jax.ShapeDtypeStruct((bsz, seq, wid), F32),
        scratch_shapes=[pltpu.VMEM((HGRN_HEADS, HGRN_DIM, HGRN_DIM), F32),
                        pltpu.VMEM((blk, wid), F32),
                        pltpu.VMEM((blk, wid), F32)],
        compiler_params=_cparams("arbitrary", "arbitrary"),
        name=name,
    )(hproj, hproj, hproj, hproj, lower_bounds, norm_g.reshape(1, wid))


def _ssd_body(xbc_ref, dt_ref, z_ref, cw_ref, cb_ref, dtb_ref, alog_ref, dskip_ref, ng_ref,
              o_ref, xp_ref, act_ref, st_ref, y_ref):
    blk = SSM_BLOCK
    ch = SSM_CHUNK
    hp = SSM_HEAD_DIM
    ns = SSM_STATE
    rep = SSM_HEADS // SSM_GROUPS
    pad = 8

    @pl.when(pl.program_id(1) == 0)
    def _():
        xp_ref[0:pad, :] = jnp.zeros((pad, SSM_CONV_DIM), F32)
        st_ref[...] = jnp.zeros_like(st_ref)

    xin = xbc_ref[0]
    xp_ref[pad:pad + blk, :] = xin
    cw = cw_ref[...]
    conv = cb_ref[...] + cw[SSM_CONV - 1:SSM_CONV] * xin
    for j in range(SSM_CONV - 1):
        shift = SSM_CONV - 1 - j
        conv = conv + cw[j:j + 1] * xp_ref[pl.ds(pad - shift, blk), :]
    xp_ref[0:pad, :] = xin[blk - pad:blk]
    act_ref[...] = _silu(conv)

    dt_raw = dt_ref[0] + dtb_ref[...]
    dt = jnp.maximum(dt_raw, 0.0) + jnp.log(1.0 + jnp.exp(-jnp.abs(dt_raw)))
    a_all = dt * (-jnp.exp(alog_ref[...]))
    cr = lax.broadcasted_iota(jnp.int32, (ch, ch), 0)
    cc = lax.broadcasted_iota(jnp.int32, (ch, ch), 1)
    causal = cc <= cr
    tri = jnp.where(causal, 1.0, 0.0).astype(F32)
    strict = jnp.where(cr > cc, 1.0, 0.0).astype(F32)

    for c in range(blk // ch):
        r0 = c * ch
        dtc = dt[r0:r0 + ch]
        ac = a_all[r0:r0 + ch]
        acs = jnp.dot(tri, ac, precision=HIGHEST, preferred_element_type=F32)
        for g in range(SSM_GROUPS):
            bm = act_ref[r0:r0 + ch, SSM_INNER + g * ns:SSM_INNER + (g + 1) * ns]
            cm = act_ref[r0:r0 + ch, SSM_INNER + SSM_GROUPS * ns + g * ns:
                         SSM_INNER + SSM_GROUPS * ns + (g + 1) * ns]
            cb = _dot_nt(cm, bm)
            for r in range(rep):
                h = g * rep + r
                xh = act_ref[r0:r0 + ch, h * hp:(h + 1) * hp]
                dth = dtc[:, h:h + 1]
                seg = jnp.dot(tri, ac[:, h:h + 1] * strict, precision=HIGHEST,
                              preferred_element_type=F32)
                lmat = jnp.where(causal, jnp.exp(seg), 0.0)
                acs_h = acs[:, h:h + 1]
                acs_last = acs[ch - 1:ch, h:h + 1]
                st = st_ref[h]
                y = _dot(cb * lmat, dth * xh)
                y = y + jnp.exp(acs_h) * _dot(cm, st)
                y = y + dskip_ref[h] * xh
                bw = bm * (jnp.exp(acs_last - acs_h) * dth)
                st_ref[h] = jnp.exp(acs_last) * st + _dot_tn(bw, xh)
                y_ref[r0:r0 + ch, h * hp:(h + 1) * hp] = y
    yz = y_ref[...] * _silu(z_ref[0])
    o_ref[0] = _rms(yz, ng_ref[...])


def mamba2(sproj, conv_w, conv_b, dt_bias, a_log, d_skip, norm_g, name):
    bsz, seq, _ = sproj.shape
    blk = SSM_BLOCK
    padh = LANES - SSM_HEADS
    dtb = jnp.pad(dt_bias, (0, padh)).reshape(1, LANES)
    alog = jnp.pad(a_log, (0, padh)).reshape(1, LANES)
    return pl.pallas_call(
        _ssd_body,
        grid=(bsz, seq // blk),
        in_specs=[pl.BlockSpec((1, blk, SSM_CONV_DIM), lambda b, n: (b, n, 0)),
                  pl.BlockSpec((1, blk, LANES), lambda b, n: (b, n, 6)),
                  pl.BlockSpec((1, blk, SSM_INNER), lambda b, n: (b, n, 2)),
                  pl.BlockSpec((SSM_CONV, SSM_CONV_DIM), lambda b, n: (0, 0)),
                  pl.BlockSpec((1, SSM_CONV_DIM), lambda b, n: (0, 0)),
                  pl.BlockSpec((1, LANES), lambda b, n: (0, 0)),
                  pl.BlockSpec((1, LANES), lambda b, n: (0, 0)),
                  pl.BlockSpec(memory_space=pltpu.SMEM),
                  pl.BlockSpec((1, SSM_INNER), lambda b, n: (0, 0))],
        out_specs=pl.BlockSpec((1, blk, SSM_INNER), lambda b, n: (b, n, 0)),
        out_shape=jax.ShapeDtypeStruct((bsz, seq, SSM_INNER), F32),
        scratch_shapes=[pltpu.VMEM((blk + 8, SSM_CONV_DIM), F32),
                        pltpu.VMEM((blk, SSM_CONV_DIM), F32),
                        pltpu.VMEM((SSM_HEADS, SSM_STATE, SSM_HEAD_DIM), F32),
                        pltpu.VMEM((blk, SSM_INNER), F32)],
        compiler_params=_cparams("arbitrary", "arbitrary"),
        name=name,
    )(sproj, sproj, sproj, conv_w, conv_b.reshape(1, -1), dtb, alog, d_skip,
      norm_g.reshape(1, -1))


def _merge_body(ya_ref, yb_ref, yc1_ref, yc2_ref, yc3_ref, yd_ref, gate_ref, x_ref, wbr_ref,
                wout_ref, o_ref):
    ys = (ya_ref[...], yb_ref[...], yc1_ref[...] + yc2_ref[...] + yc3_ref[...], yd_ref[...])
    u = None
    for nbr in range(N_BRANCH):
        yp = _dot(ys[nbr], wbr_ref[nbr])
        t = _sigmoid(gate_ref[:, nbr * D_MODEL:(nbr + 1) * D_MODEL]) * yp
        u = t if u is None else u + t
    o_ref[...] = x_ref[...] + _dot(u, wout_ref[...])


def merge(ya, yb, yc1, yc2, yc3, yd, gate, x2d, w_branch, w_out, tm, name):
    m = x2d.shape[0]

    def rows(w):
        return pl.BlockSpec((tm, w), lambda i: (i, 0))

    return pl.pallas_call(
        _merge_body,
        grid=(m // tm,),
        in_specs=[rows(512)] * 6 + [rows(N_BRANCH * D_MODEL), rows(D_MODEL),
                                    pl.BlockSpec((N_BRANCH, BRANCH_WIDTH, D_MODEL), lambda i: (0, 0, 0)),
                                    pl.BlockSpec((D_MODEL, D_MODEL), lambda i: (0, 0))],
        out_specs=rows(D_MODEL),
        out_shape=jax.ShapeDtypeStruct((m, D_MODEL), F32),
        compiler_params=_cparams("arbitrary"),
        name=name,
    )(ya, yb, yc1, yc2, yc3, yd, gate, x2d, w_branch.astype(BF16), w_out.astype(BF16))


def _moe_body(x_ref, g_ref, wr_ref, wgu_ref, wd_ref, o_ref, hn_ref, comb_ref):
    e = pl.program_id(1)
    tm = x_ref.shape[0]
    lane = lax.broadcasted_iota(jnp.int32, (tm, LANES), 1)
    lanef = lane.astype(F32)

    @pl.when(e == 0)
    def _():
        x = x_ref[...]
        hn = _rms(x, g_ref[...])
        hn_ref[...] = hn.astype(BF16)
        o_ref[...] = x
        logits = jnp.dot(hn, wr_ref[...], precision=HIGHEST, preferred_element_type=F32)
        is_grp = (lane >= MOE_EXPERTS) & (lane < MOE_EXPERTS + MOE_GROUPS)
        lg = jnp.where(is_grp, logits, -jnp.inf)
        mg = jnp.max(lg, axis=-1, keepdims=True)
        sg = jnp.sum(jnp.where(is_grp, jnp.exp(lg - mg), 0.0), axis=-1, keepdims=True)
        g_w = 1.0 / sg
        gi = jnp.min(jnp.where(lg == mg, lanef, 1e9), axis=-1, keepdims=True) - MOE_EXPERTS
        in_grp = (lane < MOE_EXPERTS) & ((lane // MOE_EPG).astype(F32) == gi)
        le = jnp.where(in_grp, logits, -jnp.inf)
        m1 = jnp.max(le, axis=-1, keepdims=True)
        i1 = jnp.min(jnp.where(le == m1, lanef, 1e9), axis=-1, keepdims=True)
        le2 = jnp.where(lanef == i1, -jnp.inf, le)
        m2 = jnp.max(le2, axis=-1, keepdims=True)
        i2 = jnp.min(jnp.where(le2 == m2, lanef, 1e9), axis=-1, keepdims=True)
        e2 = jnp.exp(m2 - m1)
        den = 1.0 + e2
        comb_ref[...] = (jnp.where(lanef == i1, g_w / den, 0.0)
                         + jnp.where(lanef == i2, g_w * e2 / den, 0.0))

    a = jnp.dot(hn_ref[...], wgu_ref[0], preferred_element_type=F32)
    ce = jnp.sum(jnp.where(lane == e, comb_ref[...], 0.0), axis=-1, keepdims=True)
    act = _silu(a[:, :MOE_FF]) * a[:, MOE_FF:] * ce
    o_ref[...] += _dot(act, wd_ref[0])


def moe(x2d, g_ffn, w_grp, w_exp, w_gate, w_up, w_down, tm, name):
    m = x2d.shape[0]
    wr = jnp.concatenate([w_exp, w_grp,
                          jnp.zeros((D_MODEL, LANES - MOE_EXPERTS - MOE_GROUPS), F32)], axis=1)
    wgu = jnp.concatenate([w_gate, w_up], axis=2).astype(BF16)
    return pl.pallas_call(
        _moe_body,
        grid=(m // tm, MOE_EXPERTS),
        in_specs=[pl.BlockSpec((tm, D_MODEL), lambda i, e: (i, 0)),
                  pl.BlockSpec((1, D_MODEL), lambda i, e: (0, 0)),
                  pl.BlockSpec((D_MODEL, LANES), lambda i, e: (0, 0)),
                  pl.BlockSpec((1, D_MODEL, 2 * MOE_FF), lambda i, e: (e, 0, 0)),
                  pl.BlockSpec((1, MOE_FF, D_MODEL), lambda i, e: (e, 0, 0))],
        out_specs=pl.BlockSpec((tm, D_MODEL), lambda i, e: (i, 0)),
        out_shape=jax.ShapeDtypeStruct((m, D_MODEL), F32),
        scratch_shapes=[pltpu.VMEM((tm, D_MODEL), BF16), pltpu.VMEM((tm, LANES), F32)],
        compiler_params=_cparams("arbitrary", "arbitrary"),
        name=name,
    )(x2d, g_ffn.reshape(1, -1), wr, wgu, w_down.astype(BF16))


def _ple_body(x_ref, p_ref, g_ref, wg_ref, wp_ref, gf_ref, o_ref, *, final):
    x = x_ref[...]
    gate = _sigmoid(_dot(_rms(x, g_ref[...]), wg_ref[...]))
    xn = x + _dot(p_ref[...], wp_ref[...]) * gate
    if final:
        xn = _rms(xn, gf_ref[...])
    o_ref[...] = xn


def ple(x2d, p2d, g_ple, w_gate, w_proj, g_final, final, tm, name):
    m = x2d.shape[0]
    return pl.pallas_call(
        functools.partial(_ple_body, final=final),
        grid=(m // tm,),
        in_specs=[pl.BlockSpec((tm, D_MODEL), lambda i: (i, 0)),
                  pl.BlockSpec((tm, PLE_DIM), lambda i: (i, 0)),
                  pl.BlockSpec((1, D_MODEL), lambda i: (0, 0)),
                  pl.BlockSpec((D_MODEL, D_MODEL), lambda i: (0, 0)),
                  pl.BlockSpec((PLE_DIM, D_MODEL), lambda i: (0, 0)),
                  pl.BlockSpec((1, D_MODEL), lambda i: (0, 0))],
        out_specs=pl.BlockSpec((tm, D_MODEL), lambda i: (i, 0)),
        out_shape=jax.ShapeDtypeStruct((m, D_MODEL), F32),
        compiler_params=_cparams("arbitrary"),
        name=name,
    )(x2d, p2d, g_ple.reshape(1, -1), w_gate.astype(BF16), w_proj.astype(BF16),
      g_final.reshape(1, -1))


def _cols(w, *names):
    return [w[:, _OFF[n][0]:_OFF[n][0] + _OFF[n][1]] for n in names]


def _padcols(w, width):
    return jnp.pad(w, ((0, 0), (0, width - w.shape[1])))


def _split_w_in(w):
    w_att = jnp.concatenate(_cols(w, 'swa_q', 'nsa_q', 'swa_k', 'swa_v', 'nsa_k_cmp', 'nsa_v_cmp',
                                  'nsa_k_slc', 'nsa_v_slc', 'nsa_k_win', 'nsa_v_win'), axis=1)
    w_hgrn = jnp.concatenate(_cols(w, 'hgrn_q', 'hgrn_f', 'hgrn_i', 'hgrn_g'), axis=1)
    (xbc, dt, ngate, z) = _cols(w, 'ssm_xbc', 'ssm_dt', 'nsa_gate', 'ssm_z')
    w_ssm = jnp.concatenate([xbc, _padcols(dt, LANES), _padcols(ngate, LANES), z], axis=1)
    (w_mg,) = _cols(w, 'merge_gate')
    return [a.astype(BF16) for a in (w_att, w_hgrn, w_ssm, w_mg)]


def _mixers(i, att, hproj, sproj, attn_sinks, hgrn_lower_bounds, hgrn_norm_g, nsa_pos_k, nsa_pos_v,
            nsa_cmp_w1_k, nsa_cmp_w2_k, nsa_cmp_w1_v, nsa_cmp_w2_v, ssm_conv_w, ssm_conv_b,
            ssm_dt_bias, ssm_A_log, ssm_D, ssm_norm_g):
    bsz, seq, _ = att.shape
    y_a = banded_attention(att, 0, 8, 9, window=SWA_WINDOW, sinks=attn_sinks[i], name=f"swa{i}")
    y_b = hgrn2(hproj, hgrn_lower_bounds, hgrn_norm_g[i], i, name=f"hgrn{i}")
    kc_in = att[:, :, 10 * LANES:11 * LANES].reshape(bsz, seq // NSA_CMP_STRIDE, NSA_CMP_STRIDE * LANES)
    vc_in = att[:, :, 11 * LANES:12 * LANES].reshape(bsz, seq // NSA_CMP_STRIDE, NSA_CMP_STRIDE * LANES)
    kc = nsa_compress(kc_in, nsa_pos_k[i], nsa_cmp_w1_k[i], nsa_cmp_w2_k[i], name=f"cmpk{i}")
    vc = nsa_compress(vc_in, nsa_pos_v[i], nsa_cmp_w1_v[i], nsa_cmp_w2_v[i], name=f"cmpv{i}")
    y_c1, mask = nsa_cmp_select(att, 1, kc, vc, sproj, 7, name=f"nsacmp{i}")
    y_c2 = nsa_selected(att, 1, 12, 13, mask, sproj, 7, name=f"nsasel{i}")
    y_c3 = banded_attention(att, 1, 14, 15, window=NSA_WINDOW, gate=sproj, gate_blk=7, gate_col=2,
                            name=f"nsawin{i}")
    y_d = mamba2(sproj, ssm_conv_w[i], ssm_conv_b[i], ssm_dt_bias[i], ssm_A_log[i], ssm_D[i],
                 ssm_norm_g[i], name=f"ssd{i}")
    return y_a, y_b, y_c1, y_c2, y_c3, y_d


def kernel(x, p, w_in, g_mix, attn_sinks, hgrn_lower_bounds, hgrn_norm_g, nsa_pos_k, nsa_pos_v,
           nsa_cmp_w1_k, nsa_cmp_w2_k, nsa_cmp_w1_v, nsa_cmp_w2_v, ssm_conv_w, ssm_conv_b,
           ssm_dt_bias, ssm_A_log, ssm_D, ssm_norm_g, w_branch, w_out, g_ffn, w_router_grp,
           w_router_exp, w_exp_gate, w_exp_up, w_exp_down, g_ple, w_ple_gate, w_ple_proj, g_final):
    bsz, seq, d = x.shape
    depth = w_in.shape[0]
    t = bsz * seq
    x2 = x.reshape(t, d)
    tm_proj = min(1024, t)
    tm_row = min(256, t)
    tm_moe = min(1024, t)
    for i in range(depth):
        w_att, w_hgrn, w_ssm, w_mg = _split_w_in(w_in[i])
        att = norm_mm(x2, g_mix[i], w_att, BF16, tm_proj, 512, f"proj_att{i}").reshape(bsz, seq, -1)
        hproj = norm_mm(x2, g_mix[i], w_hgrn, F32, tm_proj, 512, f"proj_hgrn{i}").reshape(bsz, seq, -1)
        sproj = norm_mm(x2, g_mix[i], w_ssm, F32, tm_proj, 512, f"proj_ssm{i}").reshape(bsz, seq, -1)
        mgate = norm_mm(x2, g_mix[i], w_mg, F32, tm_proj, 512, f"proj_gate{i}")
        ys = _mixers(i, att, hproj, sproj, attn_sinks, hgrn_lower_bounds, hgrn_norm_g, nsa_pos_k,
                     nsa_pos_v, nsa_cmp_w1_k, nsa_cmp_w2_k, nsa_cmp_w1_v, nsa_cmp_w2_v, ssm_conv_w,
                     ssm_conv_b, ssm_dt_bias, ssm_A_log, ssm_D, ssm_norm_g)
        ys = [y.reshape(t, -1) for y in ys]
        x2 = merge(*ys, mgate, x2, w_branch[i], w_out[i], tm_row, f"merge{i}")
        x2 = moe(x2, g_ffn[i], w_router_grp[i], w_router_exp[i], w_exp_gate[i], w_exp_up[i],
                 w_exp_down[i], tm_moe, f"moe{i}")
        x2 = ple(x2, p[i].reshape(t, -1), g_ple[i], w_ple_gate[i], w_ple_proj[i], g_final,
                 i == depth - 1, tm_row, f"ple{i}")
    return x2.reshape(bsz, seq, d)
```

```python
import functools

import numpy as np
import jax
import jax.numpy as jnp
from jax import lax
from jax.experimental import pallas as pl
from jax.experimental.pallas import tpu as pltpu

F32 = jnp.float32
BF16 = jnp.bfloat16
HIGHEST = lax.Precision.HIGHEST

D_MODEL = 1024
PLE_DIM = 256
NORM_EPS = 1e-6
NEG_INF = -1e30
REMOVED = -3e38
N_BRANCH = 4
BRANCH_WIDTH = 512
ATTN_BLOCK = 128

ATT_HEADS = 8
ATT_KV_HEADS = 2
ATT_HEAD_DIM = 64
ATT_REP = ATT_HEADS // ATT_KV_HEADS
SWA_WINDOW = 128
NSA_WINDOW = 512
NSA_CMP_LEN = 32
NSA_CMP_STRIDE = 16
NSA_CMP_HIDDEN = 256
NSA_SEL_LEN = 64
NSA_TOP_N = 16
NSA_FORCE_SCORE = 1e6
SEL_KEY_TILE = 512

HGRN_HEADS = 4
HGRN_DIM = 128
HGRN_CHUNK = 32
HGRN_MIN_F = 1e-6
HGRN_BLOCK = 256

SSM_HEADS = 8
SSM_HEAD_DIM = 64
SSM_GROUPS = 2
SSM_STATE = 64
SSM_CONV = 4
SSM_CHUNK = 64
SSM_INNER = 512
SSM_CONV_DIM = 768
SSM_BLOCK = 256

MOE_GROUPS = 4
MOE_EPG = 8
MOE_EXPERTS = 32
MOE_FF = 256

LANES = 128
VMEM_LIMIT = 56 * 1024 * 1024

ALIBI_SLOPES = tuple(2.0 ** (-8.0 * (h + 1) / ATT_HEADS) for h in range(ATT_HEADS))

ATT_Q_WIDTH = ATT_HEADS * LANES
KV_BLK = {name: 2 * ATT_HEADS + j for j, name in enumerate(
    ('swa_k', 'swa_v', 'nsa_k_cmp', 'nsa_v_cmp', 'nsa_k_slc', 'nsa_v_slc', 'nsa_k_win', 'nsa_v_win'))}

_SPLITS = (
    ('swa_q', 512), ('swa_k', 128), ('swa_v', 128),
    ('hgrn_q', 512), ('hgrn_f', 512), ('hgrn_i', 512), ('hgrn_g', 512),
    ('nsa_q', 512), ('nsa_k_cmp', 128), ('nsa_v_cmp', 128), ('nsa_k_slc', 128),
    ('nsa_v_slc', 128), ('nsa_k_win', 128), ('nsa_v_win', 128), ('nsa_gate', 24),
    ('ssm_z', 512), ('ssm_xbc', 768), ('ssm_dt', 8), ('merge_gate', 4096),
)
_OFF = {}
_o = 0
for _n, _w in _SPLITS:
    _OFF[_n] = (_o, _w)
    _o += _w


def _cparams(*sem):
    return pltpu.CompilerParams(dimension_semantics=sem, vmem_limit_bytes=VMEM_LIMIT)


def _sigmoid(x):
    return 1.0 / (1.0 + jnp.exp(-x))


def _silu(x):
    return x * _sigmoid(x)


def _dot(a, b):
    return jnp.dot(a.astype(BF16), b.astype(BF16), preferred_element_type=F32)


def _dot_nt(a, b):
    return lax.dot_general(a.astype(BF16), b.astype(BF16), (((1,), (1,)), ((), ())),
                           preferred_element_type=F32)


def _dot_tn(a, b):
    return lax.dot_general(a.astype(BF16), b.astype(BF16), (((0,), (0,)), ((), ())),
                           preferred_element_type=F32)


def _rms(x, g):
    ms = jnp.mean(x * x, axis=-1, keepdims=True)
    return x * lax.rsqrt(ms + NORM_EPS) * g


def _norm_mm_body(x_ref, g_ref, w_ref, o_ref, hn_ref):
    @pl.when(pl.program_id(1) == 0)
    def _():
        hn_ref[...] = _rms(x_ref[...], g_ref[...]).astype(BF16)

    o_ref[...] = jnp.dot(hn_ref[...], w_ref[...], preferred_element_type=F32).astype(o_ref.dtype)


def norm_mm(x2d, g, w, out_dtype, tm, tn, name):
    m, k = x2d.shape
    n = w.shape[1]
    return pl.pallas_call(
        _norm_mm_body,
        grid=(m // tm, n // tn),
        in_specs=[pl.BlockSpec((tm, k), lambda i, j: (i, 0)),
                  pl.BlockSpec((1, k), lambda i, j: (0, 0)),
                  pl.BlockSpec((k, tn), lambda i, j: (0, j))],
        out_specs=pl.BlockSpec((tm, tn), lambda i, j: (i, j)),
        out_shape=jax.ShapeDtypeStruct((m, n), out_dtype),
        scratch_shapes=[pltpu.VMEM((tm, k), BF16)],
        compiler_params=_cparams("arbitrary", "arbitrary"),
        name=name,
    )(x2d, g.reshape(1, k), w)


def _ones_row(g):
    return ATT_HEAD_DIM if g == 0 else 0


def _vt_body(v_ref, o_ref):
    vt = jnp.transpose(v_ref[0].astype(F32))
    rowid = lax.broadcasted_iota(jnp.int32, vt.shape, 0)
    for g in range(ATT_KV_HEADS):
        aug = jnp.where(rowid // ATT_HEAD_DIM == g, vt, jnp.where(rowid == _ones_row(g), 1.0, 0.0))
        o_ref[0, g] = aug.astype(BF16)


def v_transposed(arr, v_blk, name):
    bsz, seq, _ = arr.shape
    rows = min(512, seq)
    return pl.pallas_call(
        _vt_body,
        grid=(bsz, seq // rows),
        in_specs=[pl.BlockSpec((1, rows, LANES), lambda b, n: (b, n, v_blk))],
        out_specs=pl.BlockSpec((1, ATT_KV_HEADS, LANES, rows), lambda b, n: (b, 0, 0, n)),
        out_shape=jax.ShapeDtypeStruct((bsz, ATT_KV_HEADS, LANES, seq), BF16),
        compiler_params=_cparams("arbitrary", "arbitrary"),
        name=name,
    )(arr)


def _q_stack(q_ref, g):
    return jnp.concatenate([q_ref[0, :, (g * ATT_REP + r) * LANES:(g * ATT_REP + r + 1) * LANES]
                            for r in range(ATT_REP)], axis=0)


def _head(x, r):
    return x[:, r * ATTN_BLOCK:(r + 1) * ATTN_BLOCK]


def _banded_body(*refs, window, nprev, use_sink, gate_col):
    if use_sink:
        q_ref, k_ref, vt_ref, sink_ref, o_ref, ot_ref = refs
    else:
        q_ref, k_ref, vt_ref, gate_ref, o_ref, ot_ref = refs
    blk = ATTN_BLOCK
    hd = ATT_HEAD_DIM
    n = pl.program_id(1)
    nk = (nprev + 1) * blk
    start = pl.multiple_of(jnp.maximum(n - nprev, 0) * blk, blk)
    k128 = k_ref[0, pl.ds(start, nk), :]
    krow = lax.broadcasted_iota(jnp.int32, (nk, blk), 0)
    qcol = lax.broadcasted_iota(jnp.int32, (nk, blk), 1)
    rel = (n * blk + qcol) - (start + krow)
    negrel = jnp.where((rel >= 0) & (rel < window), -rel.astype(F32), NEG_INF)
    if not use_sink:
        gate_t = jnp.transpose(_sigmoid(gate_ref[0]))
    for g in range(ATT_KV_HEADS):
        one = _ones_row(g)
        st4 = _dot_nt(k128, _q_stack(q_ref, g))
        ps, ms = [], []
        for r in range(ATT_REP):
            h = g * ATT_REP + r
            st = _head(st4, r) + ALIBI_SLOPES[h] * negrel
            m = jnp.max(st, axis=0, keepdims=True)
            if use_sink:
                m = jnp.maximum(m, sink_ref[h])
            ps.append(jnp.exp(st - m).astype(BF16))
            ms.append(m)
        acc4 = jnp.dot(vt_ref[0, g, :, pl.ds(start, nk)], jnp.concatenate(ps, axis=1),
                       preferred_element_type=F32)
        for r in range(ATT_REP):
            h = g * ATT_REP + r
            acc = _head(acc4, r)
            l = acc[one:one + 1]
            if use_sink:
                l = l + jnp.exp(sink_ref[h] - ms[r])
            ot = acc[g * hd:(g + 1) * hd] / l
            if not use_sink:
                c = 3 * h + gate_col
                ot = ot * gate_t[c:c + 1]
            ot_ref[h * hd:(h + 1) * hd, :] = ot
    o_ref[0] = jnp.transpose(ot_ref[...])


def banded_attention(att, q_blk, k_blk, v_blk, *, window, sinks=None, gate=None, gate_blk=None,
                     gate_col=0, name):
    bsz, seq, _ = att.shape
    blk = ATTN_BLOCK
    nprev = (window - 1 + blk - 1) // blk
    use_sink = sinks is not None
    vt = v_transposed(att, v_blk, name + "_vt")
    in_specs = [pl.BlockSpec((1, blk, ATT_Q_WIDTH), lambda b, n: (b, n, q_blk)),
                pl.BlockSpec((1, seq, LANES), lambda b, n: (b, 0, k_blk)),
                pl.BlockSpec((1, ATT_KV_HEADS, LANES, seq), lambda b, n: (b, 0, 0, 0))]
    if use_sink:
        in_specs.append(pl.BlockSpec(memory_space=pltpu.SMEM))
        extra = sinks
    else:
        in_specs.append(pl.BlockSpec((1, blk, LANES), lambda b, n: (b, n, gate_blk)))
        extra = gate
    return pl.pallas_call(
        functools.partial(_banded_body, window=window, nprev=nprev, use_sink=use_sink,
                          gate_col=gate_col),
        grid=(bsz, seq // blk),
        in_specs=in_specs,
        out_specs=pl.BlockSpec((1, blk, 512), lambda b, n: (b, n, 0)),
        out_shape=jax.ShapeDtypeStruct((bsz, seq, 512), F32),
        scratch_shapes=[pltpu.VMEM((512, blk), F32)],
        compiler_params=_cparams("arbitrary", "arbitrary"),
        name=name,
    )(att, att, vt, extra)


def _compress_body(x_ref, w1a_ref, w1b_ref, w1_ref, pos_ref, w2_ref, o_ref):
    x = x_ref[0]
    p = jnp.dot(x, w1a_ref[...], preferred_element_type=F32)
    q = jnp.dot(x, w1b_ref[...], preferred_element_type=F32)
    ncp = x.shape[0]
    q = pltpu.roll(q, shift=ncp - 1, axis=0)
    posb = jnp.broadcast_to(pos_ref[...], (8, pos_ref.shape[1]))
    bias = _dot(posb, w1_ref[...])[0:1]
    hid = NSA_CMP_HIDDEN
    outs = []
    for g in range(ATT_KV_HEADS):
        pre = p[:, g * hid:(g + 1) * hid] + q[:, g * hid:(g + 1) * hid] + bias
        outs.append(_dot(jax.nn.gelu(pre), w2_ref[...]))
    o_ref[0] = jnp.concatenate(outs, axis=1).astype(o_ref.dtype)


def _expand_w1(w1, half):
    hd, hid, ng = ATT_HEAD_DIM, NSA_CMP_HIDDEN, ATT_KV_HEADS
    w = w1.reshape(NSA_CMP_LEN, hd, hid)[half * 16:(half + 1) * 16]
    eye = jnp.eye(ng, dtype=w1.dtype)
    out = jnp.einsum('ldj,gh->lgdhj', w, eye)
    return out.reshape(16 * ng * hd, ng * hid)


def nsa_compress(x16, pos, w1, w2, name):
    bsz, ncp, wid = x16.shape
    hid = NSA_CMP_HIDDEN
    w1a = _expand_w1(w1, 0).astype(BF16)
    w1b = _expand_w1(w1, 1).astype(BF16)
    return pl.pallas_call(
        _compress_body,
        grid=(bsz,),
        in_specs=[pl.BlockSpec((1, ncp, wid), lambda b: (b, 0, 0)),
                  pl.BlockSpec((wid, 2 * hid), lambda b: (0, 0)),
                  pl.BlockSpec((wid, 2 * hid), lambda b: (0, 0)),
                  pl.BlockSpec((NSA_CMP_LEN * ATT_HEAD_DIM, hid), lambda b: (0, 0)),
                  pl.BlockSpec((1, NSA_CMP_LEN * ATT_HEAD_DIM), lambda b: (0, 0)),
                  pl.BlockSpec((hid, ATT_HEAD_DIM), lambda b: (0, 0))],
        out_specs=pl.BlockSpec((1, ncp, LANES), lambda b: (b, 0, 0)),
        out_shape=jax.ShapeDtypeStruct((bsz, ncp, LANES), BF16),
        compiler_params=_cparams("arbitrary"),
        name=name,
    )(x16, w1a, w1b, w1.astype(BF16), pos.reshape(1, -1), w2.astype(BF16))


def _nsa_cmp_body(q_ref, kc_ref, vct_ref, ovt_ref, gate_ref, o_ref, m_ref, act_ref, ot_ref, *, top_n):
    blk = ATTN_BLOCK
    hd = ATT_HEAD_DIM
    n = pl.program_id(1)
    kc = kc_ref[0]
    ncp = kc.shape[0]
    ns = ovt_ref.shape[0]
    crow = lax.broadcasted_iota(jnp.int32, (ncp, blk), 0)
    qcol = lax.broadcasted_iota(jnp.int32, (ncp, blk), 1)
    dist = (n * blk + qcol) - (crow * NSA_CMP_STRIDE + (NSA_CMP_LEN - 1))
    valid = dist >= 0
    negd = jnp.where(valid, -dist.astype(F32), NEG_INF)
    jrow = lax.broadcasted_iota(jnp.int32, (ns, blk), 0)
    qpos = n * blk + lax.broadcasted_iota(jnp.int32, (ns, blk), 1)
    cur = qpos // NSA_SEL_LEN
    forced = (jrow == 0) | (jrow == cur) | (jrow == cur - 1)
    causal_blk = jrow * NSA_SEL_LEN <= qpos
    jrowf = jrow.astype(F32)
    gate_t = jnp.transpose(_sigmoid(gate_ref[0]))
    ones8 = jnp.ones((8, blk), BF16)
    for g in range(ATT_KV_HEADS):
        one = _ones_row(g)
        st4 = _dot_nt(kc, _q_stack(q_ref, g))
        es = []
        for r in range(ATT_REP):
            st = _head(st4, r) + ALIBI_SLOPES[g * ATT_REP + r] * negd
            m = jnp.max(st, axis=0, keepdims=True)
            es.append(jnp.where(valid, jnp.exp(st - m), 0.0))
        acc4 = jnp.dot(vct_ref[0, g], jnp.concatenate([e.astype(BF16) for e in es], axis=1),
                       preferred_element_type=F32)
        psum = jnp.zeros((ncp, blk), F32)
        for r in range(ATT_REP):
            h = g * ATT_REP + r
            acc = _head(acc4, r)
            l = acc[one:one + 1]
            inv = jnp.where(l > 0.0, 1.0 / jnp.where(l > 0.0, l, 1.0), 0.0)
            ot_ref[h * hd:(h + 1) * hd, :] = acc[g * hd:(g + 1) * hd] * inv * gate_t[3 * h:3 * h + 1]
            psum = psum + es[r] * inv
        imp = _dot(ovt_ref[...], psum)
        score = jnp.where(forced, NSA_FORCE_SCORE, jnp.where(causal_blk, imp, NEG_INF))
        sel = jnp.zeros((ns, blk), F32)
        for _ in range(top_n):
            mx = jnp.max(score, axis=0, keepdims=True)
            idx = jnp.min(jnp.where(score == mx, jrowf, float(ns)), axis=0, keepdims=True)
            hit = jrowf == idx
            sel = jnp.where(hit, 1.0, sel)
            score = jnp.where(hit, REMOVED, score)
        m_ref[0, g] = sel
        act_ref[0, 0, g * 8:(g + 1) * 8, :] = _dot_nt(ones8, sel)
    o_ref[0] = jnp.transpose(ot_ref[...])


def _overlap_matrix_t(seq):
    n_cmp = seq // NSA_CMP_STRIDE
    n_sel = seq // NSA_SEL_LEN
    cs = np.arange(n_cmp)[None, :] * NSA_CMP_STRIDE
    ss = np.arange(n_sel)[:, None] * NSA_SEL_LEN
    ov = np.clip(np.minimum(cs + NSA_CMP_LEN, ss + NSA_SEL_LEN) - np.maximum(cs, ss), 0, None)
    return jnp.asarray(ov / NSA_CMP_LEN, dtype=BF16)


def nsa_cmp_select(att, q_blk, kc, vc, gate, gate_blk, name):
    bsz, seq, _ = att.shape
    blk = ATTN_BLOCK
    nb = seq // blk
    ncp = seq // NSA_CMP_STRIDE
    ns = seq // NSA_SEL_LEN
    top_n = min(NSA_TOP_N, ns)
    vct = v_transposed(vc, 0, name + "_vt")
    return pl.pallas_call(
        functools.partial(_nsa_cmp_body, top_n=top_n),
        grid=(bsz, nb),
        in_specs=[pl.BlockSpec((1, blk, ATT_Q_WIDTH), lambda b, n: (b, n, q_blk)),
                  pl.BlockSpec((1, ncp, LANES), lambda b, n: (b, 0, 0)),
                  pl.BlockSpec((1, ATT_KV_HEADS, LANES, ncp), lambda b, n: (b, 0, 0, 0)),
                  pl.BlockSpec((ns, ncp), lambda b, n: (0, 0)),
                  pl.BlockSpec((1, blk, LANES), lambda b, n: (b, n, gate_blk))],
        out_specs=[pl.BlockSpec((1, blk, 512), lambda b, n: (b, n, 0)),
                   pl.BlockSpec((1, ATT_KV_HEADS, ns, blk), lambda b, n: (b, 0, 0, n)),
                   pl.BlockSpec((1, 1, ATT_KV_HEADS * 8, ns), lambda b, n: (b, n, 0, 0))],
        out_shape=[jax.ShapeDtypeStruct((bsz, seq, 512), F32),
                   jax.ShapeDtypeStruct((bsz, ATT_KV_HEADS, ns, seq), F32),
                   jax.ShapeDtypeStruct((bsz, nb, ATT_KV_HEADS * 8, ns), F32)],
        scratch_shapes=[pltpu.VMEM((512, blk), F32)],
        compiler_params=_cparams("arbitrary", "arbitrary"),
        name=name,
    )(att, kc, vct, _overlap_matrix_t(seq), gate)


def _nsa_sel_body(tiles_ref, cnt_ref, q_ref, k_ref, vt_ref, m_ref, gate_ref, o_ref, ot_ref, acc_ref,
                  mx_ref, dq_ref, *, ntl):
    blk = ATTN_BLOCK
    hd = ATT_HEAD_DIM
    tk = SEL_KEY_TILE
    sl = NSA_SEL_LEN
    per = tk // sl
    b = pl.program_id(0)
    n = pl.program_id(1)
    nb = pl.num_programs(1)
    krow = lax.broadcasted_iota(jnp.int32, (tk, blk), 0)
    qcol = lax.broadcasted_iota(jnp.int32, (tk, blk), 1)
    dq_ref[...] = (qcol - krow).astype(F32)
    gate_t = jnp.transpose(_sigmoid(gate_ref[0]))
    for g in range(ATT_KV_HEADS):
        one = _ones_row(g)
        lrow = (b * nb + n) * ATT_KV_HEADS + g
        qs = _q_stack(q_ref, g)
        acc_ref[...] = jnp.zeros_like(acc_ref)
        mx_ref[...] = jnp.full_like(mx_ref, NEG_INF)

        def step(j, carry):
            t = tiles_ref[lrow * ntl + j]
            base = pl.multiple_of(t * tk, tk)
            st4 = _dot_nt(k_ref[0, pl.ds(base, tk), :], qs)
            sel8 = m_ref[0, g, pl.ds(pl.multiple_of(t * per, per), per), :]
            off = (n * blk - base).astype(F32)
            negs = []
            for i in range(per):
                relf = dq_ref[i * sl:(i + 1) * sl, :] + off
                ok = (sel8[i:i + 1] + jnp.minimum(relf, 0.0)) > 0.5
                negs.append(jnp.where(ok, -relf, NEG_INF))
            negrel = jnp.concatenate(negs, axis=0)
            m_old = mx_ref[...]
            sts, mns = [], []
            for r in range(ATT_REP):
                st = _head(st4, r) + ALIBI_SLOPES[g * ATT_REP + r] * negrel
                sts.append(st)
                mns.append(jnp.maximum(_head(m_old, r), jnp.max(st, axis=0, keepdims=True)))
            m_new = jnp.concatenate(mns, axis=1)
            p4 = jnp.concatenate([jnp.exp(sts[r] - mns[r]).astype(BF16) for r in range(ATT_REP)],
                                 axis=1)
            acc_ref[...] = (jnp.exp(m_old - m_new) * acc_ref[...]
                            + jnp.dot(vt_ref[0, g, :, pl.ds(base, tk)], p4, preferred_element_type=F32))
            mx_ref[...] = m_new
            return carry

        lax.fori_loop(0, cnt_ref[lrow], step, 0)
        acc4 = acc_ref[...]
        for r in range(ATT_REP):
            h = g * ATT_REP + r
            acc = _head(acc4, r)
            ot = acc[g * hd:(g + 1) * hd] / acc[one:one + 1]
            ot_ref[h * hd:(h + 1) * hd, :] = ot * gate_t[3 * h + 1:3 * h + 2]
    o_ref[0] = jnp.transpose(ot_ref[...])


def _tile_lists(act, seq):
    bsz, nb = act.shape[:2]
    ns = act.shape[-1]
    tk = SEL_KEY_TILE
    ntl = seq // tk
    per = tk // NSA_SEL_LEN
    cnt_blk = act.reshape(bsz, nb, ATT_KV_HEADS, 8, ns)[:, :, :, 0, :]
    hit = cnt_blk.reshape(bsz, nb, ATT_KV_HEADS, ntl, per).sum(-1) > 0.5
    tidx = jnp.arange(ntl, dtype=jnp.int32)
    diag = (jnp.arange(nb, dtype=jnp.int32) * ATTN_BLOCK + ATTN_BLOCK - 1) // tk
    active = hit & (tidx[None, None, None, :] <= diag[None, :, None, None])
    order = -jnp.sort(-jnp.where(active, tidx, -1), axis=-1)
    tiles = jnp.maximum(order, 0).reshape(-1).astype(jnp.int32)
    cnt = active.sum(-1).reshape(-1).astype(jnp.int32)
    return tiles, cnt, ntl


def nsa_selected(att, q_blk, k_blk, v_blk, mask, act, gate, gate_blk, name):
    bsz, seq, _ = att.shape
    blk = ATTN_BLOCK
    ns = seq // NSA_SEL_LEN
    tiles, cnt, ntl = _tile_lists(act, seq)
    vt = v_transposed(att, v_blk, name + "_vt")
    grid_spec = pltpu.PrefetchScalarGridSpec(
        num_scalar_prefetch=2,
        grid=(bsz, seq // blk),
        in_specs=[pl.BlockSpec((1, blk, ATT_Q_WIDTH), lambda b, n, *_: (b, n, q_blk)),
                  pl.BlockSpec((1, seq, LANES), lambda b, n, *_: (b, 0, k_blk)),
                  pl.BlockSpec((1, ATT_KV_HEADS, LANES, seq), lambda b, n, *_: (b, 0, 0, 0)),
                  pl.BlockSpec((1, ATT_KV_HEADS, ns, blk), lambda b, n, *_: (b, 0, 0, n)),
                  pl.BlockSpec((1, blk, LANES), lambda b, n, *_: (b, n, gate_blk))],
        out_specs=pl.BlockSpec((1, blk, 512), lambda b, n, *_: (b, n, 0)),
        scratch_shapes=[pltpu.VMEM((512, blk), F32),
                        pltpu.VMEM((LANES, ATT_REP * blk), F32),
                        pltpu.VMEM((1, ATT_REP * blk), F32),
                        pltpu.VMEM((SEL_KEY_TILE, blk), F32)],
    )
    return pl.pallas_call(
        functools.partial(_nsa_sel_body, ntl=ntl),
        grid_spec=grid_spec,
        out_shape=jax.ShapeDtypeStruct((bsz, seq, 512), F32),
        compiler_params=_cparams("arbitrary", "arbitrary"),
        name=name,
    )(tiles, cnt, att, att, vt, mask, gate)


def _hgrn_body(q_ref, f_ref, i_ref, g_ref, lbp_ref, ng_ref, o_ref, st_ref, b_ref, k_ref, *, layer):
    blk = HGRN_BLOCK
    ch = HGRN_CHUNK
    dk = HGRN_DIM

    @pl.when(pl.program_id(1) == 0)
    def _():
        st_ref[...] = jnp.zeros_like(st_ref)

    lbp = lbp_ref[...]
    e = jnp.exp(lbp - jnp.max(lbp, axis=0, keepdims=True))
    sm = e / jnp.sum(e, axis=0, keepdims=True)
    lb = jnp.zeros((1, lbp.shape[1]), F32)
    for d in range(1, layer + 1):
        lb = lb + sm[d:d + 1]
    z = f_ref[0]
    f = lb + (1.0 - lb) * _sigmoid(z)
    logf = jnp.log(jnp.maximum(f, HGRN_MIN_F))
    k_ref[...] = (1.0 - lb) * _sigmoid(-z)
    tr = lax.broadcasted_iota(jnp.int32, (blk, blk), 0)
    tc = lax.broadcasted_iota(jnp.int32, (blk, blk), 1)
    tri = jnp.where((tr // ch == tc // ch) & (tc <= tr), 1.0, 0.0).astype(F32)
    b_ref[...] = jnp.dot(tri, logf, precision=HIGHEST, preferred_element_type=F32)
    cr = lax.broadcasted_iota(jnp.int32, (ch, ch), 0)
    cc = lax.broadcasted_iota(jnp.int32, (ch, ch), 1)
    causal = cc <= cr
    ng = ng_ref[...]

    def chunk(c, carry):
        r0 = pl.multiple_of(c * ch, ch)
        bc = b_ref[pl.ds(r0, ch), :]
        qc = q_ref[0, pl.ds(r0, ch), :]
        kc = k_ref[pl.ds(r0, ch), :]
        vc = i_ref[0, pl.ds(r0, ch), :]
        gc = g_ref[0, pl.ds(r0, ch), :]
        b_mid = bc[ch // 2:ch // 2 + 1]
        b_last = bc[ch - 1:ch]
        qa = qc * jnp.exp(bc - b_mid)
        ka = kc * jnp.exp(b_mid - bc)
        qe = qc * jnp.exp(bc)
        kl = kc * jnp.exp(b_last - bc)
        dec = jnp.exp(b_last)
        for h in range(HGRN_HEADS):
            sl = slice(h * dk, (h + 1) * dk)
            a = jnp.where(causal, _dot_nt(qa[:, sl], ka[:, sl]), 0.0)
            st = st_ref[h]
            o = _dot(a, vc[:, sl]) + _dot_nt(qe[:, sl], st)
            st_ref[h] = st * dec[:, sl] + _dot_tn(vc[:, sl], kl[:, sl])
            o = _rms(o, ng[:, sl]) * _silu(gc[:, sl])
            o_ref[0, pl.ds(r0, ch), sl] = o
        return carry

    lax.fori_loop(0, blk // ch, chunk, 0)


def hgrn2(hproj, lower_bounds, norm_g, layer, name):
    bsz, seq, _ = hproj.shape
    blk = HGRN_BLOCK
    wid = HGRN_HEADS * HGRN_DIM
    depth = lower_bounds.shape[0]

    def col(j):
        return pl.BlockSpec((1, blk, wid), lambda b, n: (b, n, j))

    return pl.pallas_call(
        functools.partial(_hgrn_body, layer=layer),
        grid=(bsz, seq // blk),
        in_specs=[col(0), col(1), col(2), col(3),
                  pl.BlockSpec((depth, wid), lambda b, n: (0, 0)),
                  pl.BlockSpec((1, wid), lambda b, n: (0, 0))],
        out_specs=pl.BlockSpec((1, blk, wid), lambda b, n: (b, n, 0)),
        out_shape=jax.ShapeDtypeStruct((bsz, seq, wid), F32),
        scratch_shapes=[pltpu.VMEM((HGRN_HEADS, HGRN_DIM, HGRN_DIM), F32),
                        pltpu.VMEM((blk, wid), F32),
                        pltpu.VMEM((blk, wid), F32)],
        compiler_params=_cparams("arbitrary", "arbitrary"),
        name=name,
    )(hproj, hproj, hproj, hproj, lower_bounds, norm_g.reshape(1, wid))


def _ssd_body(xbc_ref, dt_ref, z_ref, cw_ref, cb_ref, dtb_ref, alog_ref, dskip_ref, ng_ref,
              o_ref, xp_ref, act_ref, st_ref, y_ref):
    blk = SSM_BLOCK
    ch = SSM_CHUNK
    hp = SSM_HEAD_DIM
    ns = SSM_STATE
    rep = SSM_HEADS // SSM_GROUPS
    pad = 8

    @pl.when(pl.program_id(1) == 0)
    def _():
        xp_ref[0:pad, :] = jnp.zeros((pad, SSM_CONV_DIM), F32)
        st_ref[...] = jnp.zeros_like(st_ref)

    xin = xbc_ref[0]
    xp_ref[pad:pad + blk, :] = xin
    cw = cw_ref[...]
    conv = cb_ref[...] + cw[SSM_CONV - 1:SSM_CONV] * xin
    for j in range(SSM_CONV - 1):
        shift = SSM_CONV - 1 - j
        conv = conv + cw[j:j + 1] * xp_ref[pl.ds(pad - shift, blk), :]
    xp_ref[0:pad, :] = xin[blk - pad:blk]
    act_ref[...] = _silu(conv)

    dt_raw = dt_ref[0] + dtb_ref[...]
    dt = jnp.maximum(dt_raw, 0.0) + jnp.log(1.0 + jnp.exp(-jnp.abs(dt_raw)))
    a_all = dt * (-jnp.exp(alog_ref[...]))
    cr = lax.broadcasted_iota(jnp.int32, (ch, ch), 0)
    cc = lax.broadcasted_iota(jnp.int32, (ch, ch), 1)
    causal = cc <= cr
    tri = jnp.where(causal, 1.0, 0.0).astype(F32)
    strict = jnp.where(cr > cc, 1.0, 0.0).astype(F32)

    for c in range(blk // ch):
        r0 = c * ch
        dtc = dt[r0:r0 + ch]
        ac = a_all[r0:r0 + ch]
        acs = jnp.dot(tri, ac, precision=HIGHEST, preferred_element_type=F32)
        for g in range(SSM_GROUPS):
            bm = act_ref[r0:r0 + ch, SSM_INNER + g * ns:SSM_INNER + (g + 1) * ns]
            cm = act_ref[r0:r0 + ch, SSM_INNER + SSM_GROUPS * ns + g * ns:
                         SSM_INNER + SSM_GROUPS * ns + (g + 1) * ns]
            cb = _dot_nt(cm, bm)
            for r in range(rep):
                h = g * rep + r
                xh = act_ref[r0:r0 + ch, h * hp:(h + 1) * hp]
                dth = dtc[:, h:h + 1]
                seg = jnp.dot(tri, ac[:, h:h + 1] * strict, precision=HIGHEST,
                              preferred_element_type=F32)
                lmat = jnp.where(causal, jnp.exp(seg), 0.0)
                acs_h = acs[:, h:h + 1]
                acs_last = acs[ch - 1:ch, h:h + 1]
                st = st_ref[h]
                y = _dot(cb * lmat, dth * xh)
                y = y + jnp.exp(acs_h) * _dot(cm, st)
                y = y + dskip_ref[h] * xh
                bw = bm * (jnp.exp(acs_last - acs_h) * dth)
                st_ref[h] = jnp.exp(acs_last) * st + _dot_tn(bw, xh)
                y_ref[r0:r0 + ch, h * hp:(h + 1) * hp] = y
    yz = y_ref[...] * _silu(z_ref[0])
    o_ref[0] = _rms(yz, ng_ref[...])


def mamba2(sproj, conv_w, conv_b, dt_bias, a_log, d_skip, norm_g, name):
    bsz, seq, _ = sproj.shape
    blk = SSM_BLOCK
    padh = LANES - SSM_HEADS
    dtb = jnp.pad(dt_bias, (0, padh)).reshape(1, LANES)
    alog = jnp.pad(a_log, (0, padh)).reshape(1, LANES)
    return pl.pallas_call(
        _ssd_body,
        grid=(bsz, seq // blk),
        in_specs=[pl.BlockSpec((1, blk, SSM_CONV_DIM), lambda b, n: (b, n, 0)),
                  pl.BlockSpec((1, blk, LANES), lambda b, n: (b, n, 6)),
                  pl.BlockSpec((1, blk, SSM_INNER), lambda b, n: (b, n, 2)),
                  pl.BlockSpec((SSM_CONV, SSM_CONV_DIM), lambda b, n: (0, 0)),
                  pl.BlockSpec((1, SSM_CONV_DIM), lambda b, n: (0, 0)),
                  pl.BlockSpec((1, LANES), lambda b, n: (0, 0)),
                  pl.BlockSpec((1, LANES), lambda b, n: (0, 0)),
                  pl.BlockSpec(memory_space=pltpu.SMEM),
                  pl.BlockSpec((1, SSM_INNER), lambda b, n: (0, 0))],
        out_specs=pl.BlockSpec((1, blk, SSM_INNER), lambda b, n: (b, n, 0)),
        out_shape=jax.ShapeDtypeStruct((bsz, seq, SSM_INNER), F32),
        scratch_shapes=[pltpu.VMEM((blk + 8, SSM_CONV_DIM), F32),
                        pltpu.VMEM((blk, SSM_CONV_DIM), F32),
                        pltpu.VMEM((SSM_HEADS, SSM_STATE, SSM_HEAD_DIM), F32),
                        pltpu.VMEM((blk, SSM_INNER), F32)],
        compiler_params=_cparams("arbitrary", "arbitrary"),
        name=name,
    )(sproj, sproj, sproj, conv_w, conv_b.reshape(1, -1), dtb, alog, d_skip,
      norm_g.reshape(1, -1))


def _merge_body(ya_ref, yb_ref, yc1_ref, yc2_ref, yc3_ref, yd_ref, gate_ref, x_ref, wbr_ref,
                wout_ref, o_ref):
    ys = (ya_ref[...], yb_ref[...], yc1_ref[...] + yc2_ref[...] + yc3_ref[...], yd_ref[...])
    u = None
    for nbr in range(N_BRANCH):
        yp = _dot(ys[nbr], wbr_ref[nbr])
        t = _sigmoid(gate_ref[:, nbr * D_MODEL:(nbr + 1) * D_MODEL]) * yp
        u = t if u is None else u + t
    o_ref[...] = x_ref[...] + _dot(u, wout_ref[...])


def merge(ya, yb, yc1, yc2, yc3, yd, gate, x2d, w_branch, w_out, tm, name):
    m = x2d.shape[0]

    def rows(w):
        return pl.BlockSpec((tm, w), lambda i: (i, 0))

    return pl.pallas_call(
        _merge_body,
        grid=(m // tm,),
        in_specs=[rows(512)] * 6 + [rows(N_BRANCH * D_MODEL), rows(D_MODEL),
                                    pl.BlockSpec((N_BRANCH, BRANCH_WIDTH, D_MODEL), lambda i: (0, 0, 0)),
                                    pl.BlockSpec((D_MODEL, D_MODEL), lambda i: (0, 0))],
        out_specs=rows(D_MODEL),
        out_shape=jax.ShapeDtypeStruct((m, D_MODEL), F32),
        compiler_params=_cparams("arbitrary"),
        name=name,
    )(ya, yb, yc1, yc2, yc3, yd, gate, x2d, w_branch.astype(BF16), w_out.astype(BF16))


def _moe_body(x_ref, g_ref, wr_ref, wgu_ref, wd_ref, o_ref, hn_ref, comb_ref):
    e = pl.program_id(1)
    tm = x_ref.shape[0]
    lane = lax.broadcasted_iota(jnp.int32, (tm, LANES), 1)
    lanef = lane.astype(F32)

    @pl.when(e == 0)
    def _():
        x = x_ref[...]
        hn = _rms(x, g_ref[...])
        hn_ref[...] = hn.astype(BF16)
        o_ref[...] = x
        logits = jnp.dot(hn, wr_ref[...], precision=HIGHEST, preferred_element_type=F32)
        is_grp = (lane >= MOE_EXPERTS) & (lane < MOE_EXPERTS + MOE_GROUPS)
        lg = jnp.where(is_grp, logits, -jnp.inf)
        mg = jnp.max(lg, axis=-1, keepdims=True)
        sg = jnp.sum(jnp.where(is_grp, jnp.exp(lg - mg), 0.0), axis=-1, keepdims=True)
        g_w = 1.0 / sg
        gi = jnp.min(jnp.where(lg == mg, lanef, 1e9), axis=-1, keepdims=True) - MOE_EXPERTS
        in_grp = (lane < MOE_EXPERTS) & ((lane // MOE_EPG).astype(F32) == gi)
        le = jnp.where(in_grp, logits, -jnp.inf)
        m1 = jnp.max(le, axis=-1, keepdims=True)
        i1 = jnp.min(jnp.where(le == m1, lanef, 1e9), axis=-1, keepdims=True)
        le2 = jnp.where(lanef == i1, -jnp.inf, le)
        m2 = jnp.max(le2, axis=-1, keepdims=True)
        i2 = jnp.min(jnp.where(le2 == m2, lanef, 1e9), axis=-1, keepdims=True)
        e2 = jnp.exp(m2 - m1)
        den = 1.0 + e2
        comb_ref[...] = (jnp.where(lanef == i1, g_w / den, 0.0)
                         + jnp.where(lanef == i2, g_w * e2 / den, 0.0))

    a = jnp.dot(hn_ref[...], wgu_ref[0], preferred_element_type=F32)
    ce = jnp.sum(jnp.where(lane == e, comb_ref[...], 0.0), axis=-1, keepdims=True)
    act = _silu(a[:, :MOE_FF]) * a[:, MOE_FF:] * ce
    o_ref[...] += _dot(act, wd_ref[0])


def moe(x2d, g_ffn, w_grp, w_exp, w_gate, w_up, w_down, tm, name):
    m = x2d.shape[0]
    wr = jnp.concatenate([w_exp, w_grp,
                          jnp.zeros((D_MODEL, LANES - MOE_EXPERTS - MOE_GROUPS), F32)], axis=1)
    wgu = jnp.concatenate([w_gate, w_up], axis=2).astype(BF16)
    return pl.pallas_call(
        _moe_body,
        grid=(m // tm, MOE_EXPERTS),
        in_specs=[pl.BlockSpec((tm, D_MODEL), lambda i, e: (i, 0)),
                  pl.BlockSpec((1, D_MODEL), lambda i, e: (0, 0)),
                  pl.BlockSpec((D_MODEL, LANES), lambda i, e: (0, 0)),
                  pl.BlockSpec((1, D_MODEL, 2 * MOE_FF), lambda i, e: (e, 0, 0)),
                  pl.BlockSpec((1, MOE_FF, D_MODEL), lambda i, e: (e, 0, 0))],
        out_specs=pl.BlockSpec((tm, D_MODEL), lambda i, e: (i, 0)),
        out_shape=jax.ShapeDtypeStruct((m, D_MODEL), F32),
        scratch_shapes=[pltpu.VMEM((tm, D_MODEL), BF16), pltpu.VMEM((tm, LANES), F32)],
        compiler_params=_cparams("arbitrary", "arbitrary"),
        name=name,
    )(x2d, g_ffn.reshape(1, -1), wr, wgu, w_down.astype(BF16))


def _ple_body(x_ref, p_ref, g_ref, wg_ref, wp_ref, gf_ref, o_ref, *, final):
    x = x_ref[...]
    gate = _sigmoid(_dot(_rms(x, g_ref[...]), wg_ref[...]))
    xn = x + _dot(p_ref[...], wp_ref[...]) * gate
    if final:
        xn = _rms(xn, gf_ref[...])
    o_ref[...] = xn


def ple(x2d, p2d, g_ple, w_gate, w_proj, g_final, final, tm, name):
    m = x2d.shape[0]
    return pl.pallas_call(
        functools.partial(_ple_body, final=final),
        grid=(m // tm,),
        in_specs=[pl.BlockSpec((tm, D_MODEL), lambda i: (i, 0)),
                  pl.BlockSpec((tm, PLE_DIM), lambda i: (i, 0)),
                  pl.BlockSpec((1, D_MODEL), lambda i: (0, 0)),
                  pl.BlockSpec((D_MODEL, D_MODEL), lambda i: (0, 0)),
                  pl.BlockSpec((PLE_DIM, D_MODEL), lambda i: (0, 0)),
                  pl.BlockSpec((1, D_MODEL), lambda i: (0, 0))],
        out_specs=pl.BlockSpec((tm, D_MODEL), lambda i: (i, 0)),
        out_shape=jax.ShapeDtypeStruct((m, D_MODEL), F32),
        compiler_params=_cparams("arbitrary"),
        name=name,
    )(x2d, p2d, g_ple.reshape(1, -1), w_gate.astype(BF16), w_proj.astype(BF16),
      g_final.reshape(1, -1))


def _cols(w, *names):
    return [w[:, _OFF[n][0]:_OFF[n][0] + _OFF[n][1]] for n in names]


def _padcols(w, width):
    return jnp.pad(w, ((0, 0), (0, width - w.shape[1])))


def _pad_q_heads(wq):
    hd = ATT_HEAD_DIM
    zero = jnp.zeros((wq.shape[0], hd), wq.dtype)
    cols = []
    for h in range(ATT_HEADS):
        blk = wq[:, h * hd:(h + 1) * hd] * (hd ** -0.5)
        cols += [blk, zero] if h // ATT_REP == 0 else [zero, blk]
    return jnp.concatenate(cols, axis=1)


def _split_w_in(w):
    swa_q, nsa_q = _cols(w, 'swa_q', 'nsa_q')
    w_att = jnp.concatenate([_pad_q_heads(swa_q), _pad_q_heads(nsa_q)] + _cols(w, *KV_BLK), axis=1)
    w_hgrn = jnp.concatenate(_cols(w, 'hgrn_q', 'hgrn_f', 'hgrn_i', 'hgrn_g'), axis=1)
    (xbc, dt, ngate, z) = _cols(w, 'ssm_xbc', 'ssm_dt', 'nsa_gate', 'ssm_z')
    w_ssm = jnp.concatenate([xbc, _padcols(dt, LANES), _padcols(ngate, LANES), z], axis=1)
    (w_mg,) = _cols(w, 'merge_gate')
    return [a.astype(BF16) for a in (w_att, w_hgrn, w_ssm, w_mg)]


def _mixers(i, att, hproj, sproj, attn_sinks, hgrn_lower_bounds, hgrn_norm_g, nsa_pos_k, nsa_pos_v,
            nsa_cmp_w1_k, nsa_cmp_w2_k, nsa_cmp_w1_v, nsa_cmp_w2_v, ssm_conv_w, ssm_conv_b,
            ssm_dt_bias, ssm_A_log, ssm_D, ssm_norm_g):
    bsz, seq, _ = att.shape
    kv = KV_BLK
    gate_blk = 7
    y_a = banded_attention(att, 0, kv['swa_k'], kv['swa_v'], window=SWA_WINDOW, sinks=attn_sinks[i],
                           name=f"swa{i}")
    y_b = hgrn2(hproj, hgrn_lower_bounds, hgrn_norm_g[i], i, name=f"hgrn{i}")
    ncp = seq // NSA_CMP_STRIDE

    def cmp_in(name):
        c0 = kv[name] * LANES
        return att[:, :, c0:c0 + LANES].reshape(bsz, ncp, NSA_CMP_STRIDE * LANES)

    kc = nsa_compress(cmp_in('nsa_k_cmp'), nsa_pos_k[i], nsa_cmp_w1_k[i], nsa_cmp_w2_k[i], name=f"cmpk{i}")
    vc = nsa_compress(cmp_in('nsa_v_cmp'), nsa_pos_v[i], nsa_cmp_w1_v[i], nsa_cmp_w2_v[i], name=f"cmpv{i}")
    y_c1, mask, act = nsa_cmp_select(att, 1, kc, vc, sproj, gate_blk, name=f"nsacmp{i}")
    y_c2 = nsa_selected(att, 1, kv['nsa_k_slc'], kv['nsa_v_slc'], mask, act, sproj, gate_blk,
                        name=f"nsasel{i}")
    y_c3 = banded_attention(att, 1, kv['nsa_k_win'], kv['nsa_v_win'], window=NSA_WINDOW, gate=sproj,
                            gate_blk=gate_blk, gate_col=2, name=f"nsawin{i}")
    y_d = mamba2(sproj, ssm_conv_w[i], ssm_conv_b[i], ssm_dt_bias[i], ssm_A_log[i], ssm_D[i],
                 ssm_norm_g[i], name=f"ssd{i}")
    return y_a, y_b, y_c1, y_c2, y_c3, y_d


def kernel(x, p, w_in, g_mix, attn_sinks, hgrn_lower_bounds, hgrn_norm_g, nsa_pos_k, nsa_pos_v,
           nsa_cmp_w1_k, nsa_cmp_w2_k, nsa_cmp_w1_v, nsa_cmp_w2_v, ssm_conv_w, ssm_conv_b,
           ssm_dt_bias, ssm_A_log, ssm_D, ssm_norm_g, w_branch, w_out, g_ffn, w_router_grp,
           w_router_exp, w_exp_gate, w_exp_up, w_exp_down, g_ple, w_ple_gate, w_ple_proj, g_final):
    bsz, seq, d = x.shape
    depth = w_in.shape[0]
    t = bsz * seq
    x2 = x.reshape(t, d)
    tm_proj = min(1024, t)
    tm_row = min(256, t)
    tm_moe = min(1024, t)
    for i in range(depth):
        w_att, w_hgrn, w_ssm, w_mg = _split_w_in(w_in[i])
        att = norm_mm(x2, g_mix[i], w_att, BF16, tm_proj, 512, f"proj_att{i}").reshape(bsz, seq, -1)
        hproj = norm_mm(x2, g_mix[i], w_hgrn, F32, tm_proj, 512, f"proj_hgrn{i}").reshape(bsz, seq, -1)
        sproj = norm_mm(x2, g_mix[i], w_ssm, F32, tm_proj, 512, f"proj_ssm{i}").reshape(bsz, seq, -1)
        mgate = norm_mm(x2, g_mix[i], w_mg, F32, tm_proj, 512, f"proj_gate{i}")
        ys = _mixers(i, att, hproj, sproj, attn_sinks, hgrn_lower_bounds, hgrn_norm_g, nsa_pos_k,
                     nsa_pos_v, nsa_cmp_w1_k, nsa_cmp_w2_k, nsa_cmp_w1_v, nsa_cmp_w2_v, ssm_conv_w,
                     ssm_conv_b, ssm_dt_bias, ssm_A_log, ssm_D, ssm_norm_g)
        ys = [y.reshape(t, -1) for y in ys]
        x2 = merge(*ys, mgate, x2, w_branch[i], w_out[i], tm_row, f"merge{i}")
        x2 = moe(x2, g_ffn[i], w_router_grp[i], w_router_exp[i], w_exp_gate[i], w_exp_up[i],
                 w_exp_down[i], tm_moe, f"moe{i}")
        x2 = ple(x2, p[i].reshape(t, -1), g_ple[i], w_ple_gate[i], w_ple_proj[i], g_final,
                 i == depth - 1, tm_row, f"ple{i}")
    return x2.reshape(bsz, seq, d)
```

```python
import functools

import numpy as np
import jax
import jax.numpy as jnp
from jax import lax
from jax.experimental import pallas as pl
from jax.experimental.pallas import tpu as pltpu

F32 = jnp.float32
BF16 = jnp.bfloat16
HIGHEST = lax.Precision.HIGHEST

D_MODEL = 1024
PLE_DIM = 256
NORM_EPS = 1e-6
NEG_INF = -1e30
REMOVED = -3e38
N_BRANCH = 4
BRANCH_WIDTH = 512
ATTN_BLOCK = 128

ATT_HEADS = 8
ATT_KV_HEADS = 2
ATT_HEAD_DIM = 64
ATT_REP = ATT_HEADS // ATT_KV_HEADS
SWA_WINDOW = 128
NSA_WINDOW = 512
NSA_CMP_LEN = 32
NSA_CMP_STRIDE = 16
NSA_CMP_HIDDEN = 256
NSA_SEL_LEN = 64
NSA_TOP_N = 16
NSA_FORCE_SCORE = 1e6
SEL_KEY_TILE = 512
SEL_FEATS = SEL_KEY_TILE // NSA_SEL_LEN
MASK_BIG = 1e30

HGRN_HEADS = 4
HGRN_DIM = 128
HGRN_CHUNK = 32
HGRN_MIN_F = 1e-6
HGRN_BLOCK = 256

SSM_HEADS = 8
SSM_HEAD_DIM = 64
SSM_GROUPS = 2
SSM_STATE = 64
SSM_CONV = 4
SSM_CHUNK = 64
SSM_INNER = 512
SSM_CONV_DIM = 768
SSM_BLOCK = 256

MOE_GROUPS = 4
MOE_EPG = 8
MOE_EXPERTS = 32
MOE_FF = 256

LANES = 128
VMEM_LIMIT = 56 * 1024 * 1024

ALIBI_SLOPES = tuple(2.0 ** (-8.0 * (h + 1) / ATT_HEADS) for h in range(ATT_HEADS))

ATT_Q_WIDTH = ATT_HEADS * LANES
KV_BLK = {name: 2 * ATT_HEADS + j for j, name in enumerate(
    ('swa_k', 'swa_v', 'nsa_k_cmp', 'nsa_v_cmp', 'nsa_k_slc', 'nsa_v_slc', 'nsa_k_win', 'nsa_v_win'))}

_SPLITS = (
    ('swa_q', 512), ('swa_k', 128), ('swa_v', 128),
    ('hgrn_q', 512), ('hgrn_f', 512), ('hgrn_i', 512), ('hgrn_g', 512),
    ('nsa_q', 512), ('nsa_k_cmp', 128), ('nsa_v_cmp', 128), ('nsa_k_slc', 128),
    ('nsa_v_slc', 128), ('nsa_k_win', 128), ('nsa_v_win', 128), ('nsa_gate', 24),
    ('ssm_z', 512), ('ssm_xbc', 768), ('ssm_dt', 8), ('merge_gate', 4096),
)
_OFF = {}
_o = 0
for _n, _w in _SPLITS:
    _OFF[_n] = (_o, _w)
    _o += _w


def _cparams(*sem):
    return pltpu.CompilerParams(dimension_semantics=sem, vmem_limit_bytes=VMEM_LIMIT)


def _sigmoid(x):
    return 1.0 / (1.0 + jnp.exp(-x))


def _silu(x):
    return x * _sigmoid(x)


def _dot(a, b):
    return jnp.dot(a.astype(BF16), b.astype(BF16), preferred_element_type=F32)


def _dot_nt(a, b):
    return lax.dot_general(a.astype(BF16), b.astype(BF16), (((1,), (1,)), ((), ())),
                           preferred_element_type=F32)


def _dot_tn(a, b):
    return lax.dot_general(a.astype(BF16), b.astype(BF16), (((0,), (0,)), ((), ())),
                           preferred_element_type=F32)


def _rms(x, g):
    ms = jnp.mean(x * x, axis=-1, keepdims=True)
    return x * lax.rsqrt(ms + NORM_EPS) * g


def _norm_mm_body(x_ref, g_ref, w_ref, o_ref, hn_ref):
    @pl.when(pl.program_id(1) == 0)
    def _():
        hn_ref[...] = _rms(x_ref[...], g_ref[...]).astype(BF16)

    o_ref[...] = jnp.dot(hn_ref[...], w_ref[...], preferred_element_type=F32).astype(o_ref.dtype)


def norm_mm(x2d, g, w, out_dtype, tm, tn, name):
    m, k = x2d.shape
    n = w.shape[1]
    return pl.pallas_call(
        _norm_mm_body,
        grid=(m // tm, n // tn),
        in_specs=[pl.BlockSpec((tm, k), lambda i, j: (i, 0)),
                  pl.BlockSpec((1, k), lambda i, j: (0, 0)),
                  pl.BlockSpec((k, tn), lambda i, j: (0, j))],
        out_specs=pl.BlockSpec((tm, tn), lambda i, j: (i, j)),
        out_shape=jax.ShapeDtypeStruct((m, n), out_dtype),
        scratch_shapes=[pltpu.VMEM((tm, k), BF16)],
        compiler_params=_cparams("arbitrary", "arbitrary"),
        name=name,
    )(x2d, g.reshape(1, k), w)


def _ones_row(g):
    return ATT_HEAD_DIM if g == 0 else 0


def _vt_body(v_ref, o_ref):
    vt = jnp.transpose(v_ref[0].astype(F32))
    rowid = lax.broadcasted_iota(jnp.int32, vt.shape, 0)
    for g in range(ATT_KV_HEADS):
        aug = jnp.where(rowid // ATT_HEAD_DIM == g, vt, jnp.where(rowid == _ones_row(g), 1.0, 0.0))
        o_ref[0, g] = aug.astype(BF16)


def v_transposed(arr, v_blk, name):
    bsz, seq, _ = arr.shape
    rows = min(2048, seq)
    return pl.pallas_call(
        _vt_body,
        grid=(bsz, seq // rows),
        in_specs=[pl.BlockSpec((1, rows, LANES), lambda b, n: (b, n, v_blk))],
        out_specs=pl.BlockSpec((1, ATT_KV_HEADS, LANES, rows), lambda b, n: (b, 0, 0, n)),
        out_shape=jax.ShapeDtypeStruct((bsz, ATT_KV_HEADS, LANES, seq), BF16),
        compiler_params=_cparams("arbitrary", "arbitrary"),
        name=name,
    )(arr)


def _q_stack(q_ref, g):
    return jnp.concatenate([q_ref[0, :, (g * ATT_REP + r) * LANES:(g * ATT_REP + r + 1) * LANES]
                            for r in range(ATT_REP)], axis=0)


def _head(x, r):
    return x[:, r * ATTN_BLOCK:(r + 1) * ATTN_BLOCK]


def _banded_body(*refs, window, nprev, use_sink, gate_col):
    if use_sink:
        q_ref, k_ref, vt_ref, sink_ref, o_ref, ot_ref = refs
    else:
        q_ref, k_ref, vt_ref, gate_ref, o_ref, ot_ref = refs
    blk = ATTN_BLOCK
    hd = ATT_HEAD_DIM
    n = pl.program_id(1)
    nk = (nprev + 1) * blk
    start = pl.multiple_of(jnp.maximum(n - nprev, 0) * blk, blk)
    k128 = k_ref[0, pl.ds(start, nk), :]
    krow = lax.broadcasted_iota(jnp.int32, (nk, blk), 0)
    qcol = lax.broadcasted_iota(jnp.int32, (nk, blk), 1)
    rel = (n * blk + qcol) - (start + krow)
    negrel = jnp.where((rel >= 0) & (rel < window), -rel.astype(F32), NEG_INF)
    if not use_sink:
        gate_t = jnp.transpose(_sigmoid(gate_ref[0]))
    for g in range(ATT_KV_HEADS):
        one = _ones_row(g)
        st4 = _dot_nt(k128, _q_stack(q_ref, g))
        ps, ms = [], []
        for r in range(ATT_REP):
            h = g * ATT_REP + r
            st = _head(st4, r) + ALIBI_SLOPES[h] * negrel
            m = jnp.max(st, axis=0, keepdims=True)
            if use_sink:
                m = jnp.maximum(m, sink_ref[h])
            ps.append(jnp.exp(st - m).astype(BF16))
            ms.append(m)
        acc4 = jnp.dot(vt_ref[0, g, :, pl.ds(start, nk)], jnp.concatenate(ps, axis=1),
                       preferred_element_type=F32)
        for r in range(ATT_REP):
            h = g * ATT_REP + r
            acc = _head(acc4, r)
            l = acc[one:one + 1]
            if use_sink:
                l = l + jnp.exp(sink_ref[h] - ms[r])
            ot = acc[g * hd:(g + 1) * hd] / l
            if not use_sink:
                c = 3 * h + gate_col
                ot = ot * gate_t[c:c + 1]
            ot_ref[h * hd:(h + 1) * hd, :] = ot
    o_ref[0] = jnp.transpose(ot_ref[...])


def banded_attention(att, q_blk, k_blk, v_blk, *, window, sinks=None, gate=None, gate_blk=None,
                     gate_col=0, name):
    bsz, seq, _ = att.shape
    blk = ATTN_BLOCK
    nprev = (window - 1 + blk - 1) // blk
    use_sink = sinks is not None
    vt = v_transposed(att, v_blk, name + "_vt")
    in_specs = [pl.BlockSpec((1, blk, ATT_Q_WIDTH), lambda b, n: (b, n, q_blk)),
                pl.BlockSpec((1, seq, LANES), lambda b, n: (b, 0, k_blk)),
                pl.BlockSpec((1, ATT_KV_HEADS, LANES, seq), lambda b, n: (b, 0, 0, 0))]
    if use_sink:
        in_specs.append(pl.BlockSpec(memory_space=pltpu.SMEM))
        extra = sinks
    else:
        in_specs.append(pl.BlockSpec((1, blk, LANES), lambda b, n: (b, n, gate_blk)))
        extra = gate
    return pl.pallas_call(
        functools.partial(_banded_body, window=window, nprev=nprev, use_sink=use_sink,
                          gate_col=gate_col),
        grid=(bsz, seq // blk),
        in_specs=in_specs,
        out_specs=pl.BlockSpec((1, blk, 512), lambda b, n: (b, n, 0)),
        out_shape=jax.ShapeDtypeStruct((bsz, seq, 512), F32),
        scratch_shapes=[pltpu.VMEM((512, blk), F32)],
        compiler_params=_cparams("arbitrary", "arbitrary"),
        name=name,
    )(att, att, vt, extra)


def _compress_body(x_ref, w1a_ref, w1b_ref, w1_ref, pos_ref, w2_ref, o_ref):
    x = x_ref[0]
    p = jnp.dot(x, w1a_ref[...], preferred_element_type=F32)
    q = jnp.dot(x, w1b_ref[...], preferred_element_type=F32)
    ncp = x.shape[0]
    q = pltpu.roll(q, shift=ncp - 1, axis=0)
    posb = jnp.broadcast_to(pos_ref[...], (8, pos_ref.shape[1]))
    bias = _dot(posb, w1_ref[...])[0:1]
    hid = NSA_CMP_HIDDEN
    outs = []
    for g in range(ATT_KV_HEADS):
        pre = p[:, g * hid:(g + 1) * hid] + q[:, g * hid:(g + 1) * hid] + bias
        outs.append(_dot(jax.nn.gelu(pre), w2_ref[...]))
    o_ref[0] = jnp.concatenate(outs, axis=1).astype(o_ref.dtype)


def _expand_w1(w1, half):
    hd, hid, ng = ATT_HEAD_DIM, NSA_CMP_HIDDEN, ATT_KV_HEADS
    w = w1.reshape(NSA_CMP_LEN, hd, hid)[half * 16:(half + 1) * 16]
    eye = jnp.eye(ng, dtype=w1.dtype)
    out = jnp.einsum('ldj,gh->lgdhj', w, eye)
    return out.reshape(16 * ng * hd, ng * hid)


def nsa_compress(x16, pos, w1, w2, name):
    bsz, ncp, wid = x16.shape
    hid = NSA_CMP_HIDDEN
    w1a = _expand_w1(w1, 0).astype(BF16)
    w1b = _expand_w1(w1, 1).astype(BF16)
    return pl.pallas_call(
        _compress_body,
        grid=(bsz,),
        in_specs=[pl.BlockSpec((1, ncp, wid), lambda b: (b, 0, 0)),
                  pl.BlockSpec((wid, 2 * hid), lambda b: (0, 0)),
                  pl.BlockSpec((wid, 2 * hid), lambda b: (0, 0)),
                  pl.BlockSpec((NSA_CMP_LEN * ATT_HEAD_DIM, hid), lambda b: (0, 0)),
                  pl.BlockSpec((1, NSA_CMP_LEN * ATT_HEAD_DIM), lambda b: (0, 0)),
                  pl.BlockSpec((hid, ATT_HEAD_DIM), lambda b: (0, 0))],
        out_specs=pl.BlockSpec((1, ncp, LANES), lambda b: (b, 0, 0)),
        out_shape=jax.ShapeDtypeStruct((bsz, ncp, LANES), BF16),
        compiler_params=_cparams("arbitrary"),
        name=name,
    )(x16, w1a, w1b, w1.astype(BF16), pos.reshape(1, -1), w2.astype(BF16))


def _nsa_cmp_body(q_ref, kc_ref, vct_ref, ovt_ref, gate_ref, o_ref, m_ref, act_ref, ot_ref, *, top_n):
    blk = ATTN_BLOCK
    hd = ATT_HEAD_DIM
    n = pl.program_id(1)
    kc = kc_ref[0]
    ncp = kc.shape[0]
    ns = ovt_ref.shape[0]
    crow = lax.broadcasted_iota(jnp.int32, (ncp, blk), 0)
    qcol = lax.broadcasted_iota(jnp.int32, (ncp, blk), 1)
    dist = (n * blk + qcol) - (crow * NSA_CMP_STRIDE + (NSA_CMP_LEN - 1))
    valid = dist >= 0
    negd = jnp.where(valid, -dist.astype(F32), NEG_INF)
    jrow = lax.broadcasted_iota(jnp.int32, (ns, blk), 0)
    qpos = n * blk + lax.broadcasted_iota(jnp.int32, (ns, blk), 1)
    cur = qpos // NSA_SEL_LEN
    forced = (jrow == 0) | (jrow == cur) | (jrow == cur - 1)
    causal_blk = jrow * NSA_SEL_LEN <= qpos
    jrowf = jrow.astype(F32)
    gate_t = jnp.transpose(_sigmoid(gate_ref[0]))
    ones8 = jnp.ones((8, blk), BF16)
    for g in range(ATT_KV_HEADS):
        one = _ones_row(g)
        st4 = _dot_nt(kc, _q_stack(q_ref, g))
        es = []
        for r in range(ATT_REP):
            st = _head(st4, r) + ALIBI_SLOPES[g * ATT_REP + r] * negd
            m = jnp.max(st, axis=0, keepdims=True)
            es.append(jnp.where(valid, jnp.exp(st - m), 0.0))
        acc4 = jnp.dot(vct_ref[0, g], jnp.concatenate([e.astype(BF16) for e in es], axis=1),
                       preferred_element_type=F32)
        psum = jnp.zeros((ncp, blk), F32)
        for r in range(ATT_REP):
            h = g * ATT_REP + r
            acc = _head(acc4, r)
            l = acc[one:one + 1]
            inv = jnp.where(l > 0.0, 1.0 / jnp.where(l > 0.0, l, 1.0), 0.0)
            ot_ref[h * hd:(h + 1) * hd, :] = acc[g * hd:(g + 1) * hd] * inv * gate_t[3 * h:3 * h + 1]
            psum = psum + es[r] * inv
        imp = _dot(ovt_ref[...], psum)
        score = jnp.where(forced, NSA_FORCE_SCORE, jnp.where(causal_blk, imp, NEG_INF))
        sel = jnp.zeros((ns, blk), F32)
        for _ in range(top_n):
            mx = jnp.max(score, axis=0, keepdims=True)
            idx = jnp.min(jnp.where(score == mx, jrowf, float(ns)), axis=0, keepdims=True)
            hit = jrowf == idx
            sel = jnp.where(hit, 1.0, sel)
            score = jnp.where(hit, REMOVED, score)
        m_ref[0, g] = sel
        act_ref[0, 0, g * 8:(g + 1) * 8, :] = _dot_nt(ones8, sel)
    o_ref[0] = jnp.transpose(ot_ref[...])


def _overlap_matrix_t(seq):
    n_cmp = seq // NSA_CMP_STRIDE
    n_sel = seq // NSA_SEL_LEN
    cs = np.arange(n_cmp)[None, :] * NSA_CMP_STRIDE
    ss = np.arange(n_sel)[:, None] * NSA_SEL_LEN
    ov = np.clip(np.minimum(cs + NSA_CMP_LEN, ss + NSA_SEL_LEN) - np.maximum(cs, ss), 0, None)
    return jnp.asarray(ov / NSA_CMP_LEN, dtype=BF16)


def nsa_cmp_select(att, q_blk, kc, vc, gate, gate_blk, name):
    bsz, seq, _ = att.shape
    blk = ATTN_BLOCK
    nb = seq // blk
    ncp = seq // NSA_CMP_STRIDE
    ns = seq // NSA_SEL_LEN
    top_n = min(NSA_TOP_N, ns)
    vct = v_transposed(vc, 0, name + "_vt")
    return pl.pallas_call(
        functools.partial(_nsa_cmp_body, top_n=top_n),
        grid=(bsz, nb),
        in_specs=[pl.BlockSpec((1, blk, ATT_Q_WIDTH), lambda b, n: (b, n, q_blk)),
                  pl.BlockSpec((1, ncp, LANES), lambda b, n: (b, 0, 0)),
                  pl.BlockSpec((1, ATT_KV_HEADS, LANES, ncp), lambda b, n: (b, 0, 0, 0)),
                  pl.BlockSpec((ns, ncp), lambda b, n: (0, 0)),
                  pl.BlockSpec((1, blk, LANES), lambda b, n: (b, n, gate_blk))],
        out_specs=[pl.BlockSpec((1, blk, 512), lambda b, n: (b, n, 0)),
                   pl.BlockSpec((1, ATT_KV_HEADS, ns, blk), lambda b, n: (b, 0, 0, n)),
                   pl.BlockSpec((1, 1, ATT_KV_HEADS * 8, ns), lambda b, n: (b, n, 0, 0))],
        out_shape=[jax.ShapeDtypeStruct((bsz, seq, 512), F32),
                   jax.ShapeDtypeStruct((bsz, ATT_KV_HEADS, ns, seq), F32),
                   jax.ShapeDtypeStruct((bsz, nb, ATT_KV_HEADS * 8, ns), F32)],
        scratch_shapes=[pltpu.VMEM((512, blk), F32)],
        compiler_params=_cparams("arbitrary", "arbitrary"),
        name=name,
    )(att, kc, vct, _overlap_matrix_t(seq), gate)


def _k_aug_body(k_ref, o_ref):
    k = k_ref[0].astype(F32)
    rows = k.shape[0]
    kin = lax.broadcasted_iota(jnp.int32, (rows, LANES), 0) % SEL_KEY_TILE
    lane = lax.broadcasted_iota(jnp.int32, (rows, LANES), 1)
    for g in range(ATT_KV_HEADS):
        f = lane - (ATT_HEAD_DIM if g == 0 else 0)
        feat = jnp.where(f == kin // NSA_SEL_LEN, 1.0, 0.0)
        feat = jnp.where(f == SEL_FEATS, (kin % 256).astype(F32), feat)
        feat = jnp.where(f == SEL_FEATS + 1, (kin // 256 * 256).astype(F32), feat)
        feat = jnp.where((f == SEL_FEATS + 2) | (f == SEL_FEATS + 3), 1.0, feat)
        o_ref[0, g] = jnp.where(lane // ATT_HEAD_DIM == g, k, feat).astype(BF16)


def k_augmented(arr, k_blk, name):
    bsz, seq, _ = arr.shape
    rows = min(2048, seq)
    return pl.pallas_call(
        _k_aug_body,
        grid=(bsz, seq // rows),
        in_specs=[pl.BlockSpec((1, rows, LANES), lambda b, n: (b, n, k_blk))],
        out_specs=pl.BlockSpec((1, ATT_KV_HEADS, rows, LANES), lambda b, n: (b, 0, n, 0)),
        out_shape=jax.ShapeDtypeStruct((bsz, ATT_KV_HEADS, seq, LANES), BF16),
        compiler_params=_cparams("arbitrary", "arbitrary"),
        name=name,
    )(arr)


def _query_feature_rows():
    out = np.zeros((ATT_KV_HEADS, 2 * SEL_FEATS, ATT_REP * ATTN_BLOCK), np.float32)
    qin = np.arange(ATTN_BLOCK, dtype=np.float32)
    for g in range(ATT_KV_HEADS):
        for r in range(ATT_REP):
            slope = ALIBI_SLOPES[g * ATT_REP + r]
            cols = slice(r * ATTN_BLOCK, (r + 1) * ATTN_BLOCK)
            out[g, 0, cols] = slope
            out[g, 1, cols] = slope
            out[g, 3, cols] = -slope * qin
            out[g, SEL_FEATS + 2, cols] = -slope
    return jnp.asarray(out)


def _nsa_sel_body(tiles_ref, cnt_ref, q_ref, k_ref, vt_ref, m_ref, gate_ref, fq_ref, o_ref, ot_ref,
                  acc_ref, mx_ref, dq_ref, *, ntl):
    blk = ATTN_BLOCK
    hd = ATT_HEAD_DIM
    tk = SEL_KEY_TILE
    sl = NSA_SEL_LEN
    per = tk // sl
    b = pl.program_id(0)
    n = pl.program_id(1)
    nb = pl.num_programs(1)
    krow = lax.broadcasted_iota(jnp.int32, (tk, blk), 0)
    qcol = lax.broadcasted_iota(jnp.int32, (tk, blk), 1)
    dq_ref[...] = (qcol - krow).astype(F32)
    gate_t = jnp.transpose(_sigmoid(gate_ref[0]))
    pad_rows = jnp.zeros((hd - 2 * SEL_FEATS, ATT_REP * blk), BF16)
    for g in range(ATT_KV_HEADS):
        one = _ones_row(g)
        lrow = (b * nb + n) * ATT_KV_HEADS + g
        qt = jnp.concatenate(
            [jnp.transpose(q_ref[0, :, (g * ATT_REP + r) * LANES:(g * ATT_REP + r + 1) * LANES]
                           .astype(F32))[g * hd:(g + 1) * hd] for r in range(ATT_REP)],
            axis=1).astype(BF16)
        acc_ref[...] = jnp.zeros_like(acc_ref)
        mx_ref[...] = jnp.full_like(mx_ref, NEG_INF)

        def tile_scores(j, live):
            t = tiles_ref[lrow * ntl + j]
            base = pl.multiple_of(t * tk, tk)
            sel8 = m_ref[0, g, pl.ds(pl.multiple_of(t * per, per), per), :]
            if live is not None:
                sel8 = jnp.where(live, sel8, 0.0)
            off = (n * blk - base).astype(F32)
            mask_rows = jnp.concatenate([(sel8 - 1.0) * MASK_BIG] * ATT_REP, axis=1)
            alibi_rows = fq_ref[g, 0:SEL_FEATS, :] + off * fq_ref[g, SEL_FEATS:2 * SEL_FEATS, :]
            feats = jnp.concatenate([mask_rows, alibi_rows], axis=0).astype(BF16)
            rhs = jnp.concatenate([qt, feats, pad_rows] if g == 0 else [feats, pad_rows, qt], axis=0)
            st4 = jnp.dot(k_ref[0, g, pl.ds(base, tk), :], rhs,
                          preferred_element_type=F32)
            return st4, base, off

        def tile_finish(st4, base, off, diag):
            if diag:
                causal = (dq_ref[...] + off) >= 0.0
            ps, ms = [], []
            for r in range(ATT_REP):
                st = _head(st4, r)
                if diag:
                    st = jnp.where(causal, st, NEG_INF)
                m = jnp.max(st, axis=0, keepdims=True)
                ps.append(jnp.exp(st - m).astype(BF16))
                ms.append(m)
            acc = jnp.dot(vt_ref[0, g, :, pl.ds(base, tk)], jnp.concatenate(ps, axis=1),
                          preferred_element_type=F32)
            return jnp.concatenate(ms, axis=1), acc

        def pair(j2, diag):
            cnt = cnt_ref[lrow]
            sc_a = tile_scores(2 * j2, None)
            sc_b = tile_scores(jnp.minimum(2 * j2 + 1, ntl - 1), 2 * j2 + 1 < cnt)
            m_a, acc_a = tile_finish(*sc_a, diag)
            m_b, acc_b = tile_finish(*sc_b, False)
            m_old = mx_ref[...]
            m_new = jnp.maximum(m_old, jnp.maximum(m_a, m_b))
            acc_ref[...] = (jnp.exp(m_old - m_new) * acc_ref[...] + jnp.exp(m_a - m_new) * acc_a
                            + jnp.exp(m_b - m_new) * acc_b)
            mx_ref[...] = m_new

        pair(0, True)

        def later(j2, carry):
            pair(j2, False)
            return carry

        lax.fori_loop(1, (cnt_ref[lrow] + 1) // 2, later, 0)
        acc4 = acc_ref[...]
        for r in range(ATT_REP):
            h = g * ATT_REP + r
            acc = _head(acc4, r)
            ot = acc[g * hd:(g + 1) * hd] / acc[one:one + 1]
            ot_ref[h * hd:(h + 1) * hd, :] = ot * gate_t[3 * h + 1:3 * h + 2]
    o_ref[0] = jnp.transpose(ot_ref[...])


def _tile_lists(act, seq):
    bsz, nb = act.shape[:2]
    ns = act.shape[-1]
    tk = SEL_KEY_TILE
    ntl = seq // tk
    per = tk // NSA_SEL_LEN
    cnt_blk = act.reshape(bsz, nb, ATT_KV_HEADS, 8, ns)[:, :, :, 0, :]
    hit = cnt_blk.reshape(bsz, nb, ATT_KV_HEADS, ntl, per).sum(-1) > 0.5
    tidx = jnp.arange(ntl, dtype=jnp.int32)
    diag = (jnp.arange(nb, dtype=jnp.int32) * ATTN_BLOCK + ATTN_BLOCK - 1) // tk
    active = hit & (tidx[None, None, None, :] <= diag[None, :, None, None])
    act_i = active.astype(jnp.int32)
    rank = jnp.cumsum(act_i[..., ::-1], axis=-1)[..., ::-1] - 1
    slot = (active[..., :, None] & (rank[..., :, None] == tidx)).astype(jnp.int32)
    tiles = (slot * tidx[:, None]).sum(-2).reshape(-1)
    cnt = act_i.sum(-1).reshape(-1)
    return tiles, cnt, ntl


def nsa_selected(att, q_blk, k_blk, v_blk, mask, act, gate, gate_blk, name):
    bsz, seq, _ = att.shape
    blk = ATTN_BLOCK
    ns = seq // NSA_SEL_LEN
    tiles, cnt, ntl = _tile_lists(act, seq)
    vt = v_transposed(att, v_blk, name + "_vt")
    kaug = k_augmented(att, k_blk, name + "_k")
    grid_spec = pltpu.PrefetchScalarGridSpec(
        num_scalar_prefetch=2,
        grid=(bsz, seq // blk),
        in_specs=[pl.BlockSpec((1, blk, ATT_Q_WIDTH), lambda b, n, *_: (b, n, q_blk)),
                  pl.BlockSpec((1, ATT_KV_HEADS, seq, LANES), lambda b, n, *_: (b, 0, 0, 0)),
                  pl.BlockSpec((1, ATT_KV_HEADS, LANES, seq), lambda b, n, *_: (b, 0, 0, 0)),
                  pl.BlockSpec((1, ATT_KV_HEADS, ns, blk), lambda b, n, *_: (b, 0, 0, n)),
                  pl.BlockSpec((1, blk, LANES), lambda b, n, *_: (b, n, gate_blk)),
                  pl.BlockSpec((ATT_KV_HEADS, 2 * SEL_FEATS, ATT_REP * blk), lambda b, n, *_: (0, 0, 0))],
        out_specs=pl.BlockSpec((1, blk, 512), lambda b, n, *_: (b, n, 0)),
        scratch_shapes=[pltpu.VMEM((512, blk), F32),
                        pltpu.VMEM((LANES, ATT_REP * blk), F32),
                        pltpu.VMEM((1, ATT_REP * blk), F32),
                        pltpu.VMEM((SEL_KEY_TILE, blk), F32)],
    )
    return pl.pallas_call(
        functools.partial(_nsa_sel_body, ntl=ntl),
        grid_spec=grid_spec,
        out_shape=jax.ShapeDtypeStruct((bsz, seq, 512), F32),
        compiler_params=_cparams("arbitrary", "arbitrary"),
        name=name,
    )(tiles, cnt, att, kaug, vt, mask, gate, _query_feature_rows())


def _hgrn_body(q_ref, f_ref, i_ref, g_ref, lbp_ref, ng_ref, o_ref, st_ref, b_ref, k_ref, *, layer):
    blk = HGRN_BLOCK
    ch = HGRN_CHUNK
    dk = HGRN_DIM

    @pl.when(pl.program_id(1) == 0)
    def _():
        st_ref[...] = jnp.zeros_like(st_ref)

    lbp = lbp_ref[...]
    e = jnp.exp(lbp - jnp.max(lbp, axis=0, keepdims=True))
    sm = e / jnp.sum(e, axis=0, keepdims=True)
    lb = jnp.zeros((1, lbp.shape[1]), F32)
    for d in range(1, layer + 1):
        lb = lb + sm[d:d + 1]
    z = f_ref[0]
    f = lb + (1.0 - lb) * _sigmoid(z)
    logf = jnp.log(jnp.maximum(f, HGRN_MIN_F))
    k_ref[...] = (1.0 - lb) * _sigmoid(-z)
    tr = lax.broadcasted_iota(jnp.int32, (blk, blk), 0)
    tc = lax.broadcasted_iota(jnp.int32, (blk, blk), 1)
    tri = jnp.where((tr // ch == tc // ch) & (tc <= tr), 1.0, 0.0).astype(F32)
    b_ref[...] = jnp.dot(tri, logf, precision=HIGHEST, preferred_element_type=F32)
    cr = lax.broadcasted_iota(jnp.int32, (ch, ch), 0)
    cc = lax.broadcasted_iota(jnp.int32, (ch, ch), 1)
    causal = cc <= cr
    ng = ng_ref[...]

    for c in range(blk // ch):
        r0 = c * ch
        bc = b_ref[pl.ds(r0, ch), :]
        qc = q_ref[0, pl.ds(r0, ch), :]
        kc = k_ref[pl.ds(r0, ch), :]
        vc = i_ref[0, pl.ds(r0, ch), :]
        gc = g_ref[0, pl.ds(r0, ch), :]
        b_mid = bc[ch // 2:ch // 2 + 1]
        b_last = bc[ch - 1:ch]
        qa = qc * jnp.exp(bc - b_mid)
        ka = kc * jnp.exp(b_mid - bc)
        qe = qc * jnp.exp(bc)
        kl = kc * jnp.exp(b_last - bc)
        dec = jnp.exp(b_last)
        for h in range(HGRN_HEADS):
            sl = slice(h * dk, (h + 1) * dk)
            a = jnp.where(causal, _dot_nt(qa[:, sl], ka[:, sl]), 0.0)
            st = st_ref[h]
            o = _dot(a, vc[:, sl]) + _dot_nt(qe[:, sl], st)
            st_ref[h] = st * dec[:, sl] + _dot_tn(vc[:, sl], kl[:, sl])
            o = _rms(o, ng[:, sl]) * _silu(gc[:, sl])
            o_ref[0, pl.ds(r0, ch), sl] = o


def hgrn2(hproj, lower_bounds, norm_g, layer, name):
    bsz, seq, _ = hproj.shape
    blk = HGRN_BLOCK
    wid = HGRN_HEADS * HGRN_DIM
    depth = lower_bounds.shape[0]

    def col(j):
        return pl.BlockSpec((1, blk, wid), lambda b, n: (b, n, j))

    return pl.pallas_call(
        functools.partial(_hgrn_body, layer=layer),
        grid=(bsz, seq // blk),
        in_specs=[col(0), col(1), col(2), col(3),
                  pl.BlockSpec((depth, wid), lambda b, n: (0, 0)),
                  pl.BlockSpec((1, wid), lambda b, n: (0, 0))],
        out_specs=pl.BlockSpec((1, blk, wid), lambda b, n: (b, n, 0)),
        out_shape=jax.ShapeDtypeStruct((bsz, seq, wid), F32),
        scratch_shapes=[pltpu.VMEM((HGRN_HEADS, HGRN_DIM, HGRN_DIM), F32),
                        pltpu.VMEM((blk, wid), F32),
                        pltpu.VMEM((blk, wid), F32)],
        compiler_params=_cparams("arbitrary", "arbitrary"),
        name=name,
    )(hproj, hproj, hproj, hproj, lower_bounds, norm_g.reshape(1, wid))


def _ssd_body(xbc_ref, dt_ref, z_ref, cw_ref, cb_ref, dtb_ref, alog_ref, dskip_ref, ng_ref,
              o_ref, xp_ref, act_ref, st_ref, y_ref):
    blk = SSM_BLOCK
    ch = SSM_CHUNK
    hp = SSM_HEAD_DIM
    ns = SSM_STATE
    rep = SSM_HEADS // SSM_GROUPS
    pad = 8

    @pl.when(pl.program_id(1) == 0)
    def _():
        xp_ref[0:pad, :] = jnp.zeros((pad, SSM_CONV_DIM), F32)
        st_ref[...] = jnp.zeros_like(st_ref)

    xin = xbc_ref[0]
    xp_ref[pad:pad + blk, :] = xin
    cw = cw_ref[...]
    conv = cb_ref[...] + cw[SSM_CONV - 1:SSM_CONV] * xin
    for j in range(SSM_CONV - 1):
        shift = SSM_CONV - 1 - j
        conv = conv + cw[j:j + 1] * xp_ref[pl.ds(pad - shift, blk), :]
    xp_ref[0:pad, :] = xin[blk - pad:blk]
    act_ref[...] = _silu(conv)

    dt_raw = dt_ref[0] + dtb_ref[...]
    dt = jnp.maximum(dt_raw, 0.0) + jnp.log(1.0 + jnp.exp(-jnp.abs(dt_raw)))
    a_all = dt * (-jnp.exp(alog_ref[...]))
    cr = lax.broadcasted_iota(jnp.int32, (ch, ch), 0)
    cc = lax.broadcasted_iota(jnp.int32, (ch, ch), 1)
    causal = cc <= cr
    tri = jnp.where(causal, 1.0, 0.0).astype(F32)
    tri_t = jnp.where(cr <= cc, 1.0, 0.0).astype(F32)

    for c in range(blk // ch):
        r0 = c * ch
        dtc = dt[r0:r0 + ch]
        ac = a_all[r0:r0 + ch]
        acs = jnp.dot(tri, ac, precision=HIGHEST, preferred_element_type=F32)
        acs_t = lax.dot_general(ac, tri_t, (((0,), (0,)), ((), ())), precision=HIGHEST,
                                preferred_element_type=F32)
        for g in range(SSM_GROUPS):
            bm = act_ref[r0:r0 + ch, SSM_INNER + g * ns:SSM_INNER + (g + 1) * ns]
            cm = act_ref[r0:r0 + ch, SSM_INNER + SSM_GROUPS * ns + g * ns:
                         SSM_INNER + SSM_GROUPS * ns + (g + 1) * ns]
            cb = _dot_nt(cm, bm)
            for r in range(rep):
                h = g * rep + r
                xh = act_ref[r0:r0 + ch, h * hp:(h + 1) * hp]
                dth = dtc[:, h:h + 1]
                acs_h = acs[:, h:h + 1]
                lmat = jnp.where(causal, jnp.exp(acs_h - acs_t[h:h + 1, :]), 0.0)
                acs_last = acs[ch - 1:ch, h:h + 1]
                st = st_ref[h]
                y = _dot(cb * lmat, dth * xh)
                y = y + jnp.exp(acs_h) * _dot(cm, st)
                y = y + dskip_ref[h] * xh
                bw = bm * (jnp.exp(acs_last - acs_h) * dth)
                st_ref[h] = jnp.exp(acs_last) * st + _dot_tn(bw, xh)
                y_ref[r0:r0 + ch, h * hp:(h + 1) * hp] = y
    yz = y_ref[...] * _silu(z_ref[0])
    o_ref[0] = _rms(yz, ng_ref[...])


def mamba2(sproj, conv_w, conv_b, dt_bias, a_log, d_skip, norm_g, name):
    bsz, seq, _ = sproj.shape
    blk = SSM_BLOCK
    padh = LANES - SSM_HEADS
    dtb = jnp.pad(dt_bias, (0, padh)).reshape(1, LANES)
    alog = jnp.pad(a_log, (0, padh)).reshape(1, LANES)
    return pl.pallas_call(
        _ssd_body,
        grid=(bsz, seq // blk),
        in_specs=[pl.BlockSpec((1, blk, SSM_CONV_DIM), lambda b, n: (b, n, 0)),
                  pl.BlockSpec((1, blk, LANES), lambda b, n: (b, n, 6)),
                  pl.BlockSpec((1, blk, SSM_INNER), lambda b, n: (b, n, 2)),
                  pl.BlockSpec((SSM_CONV, SSM_CONV_DIM), lambda b, n: (0, 0)),
                  pl.BlockSpec((1, SSM_CONV_DIM), lambda b, n: (0, 0)),
                  pl.BlockSpec((1, LANES), lambda b, n: (0, 0)),
                  pl.BlockSpec((1, LANES), lambda b, n: (0, 0)),
                  pl.BlockSpec(memory_space=pltpu.SMEM),
                  pl.BlockSpec((1, SSM_INNER), lambda b, n: (0, 0))],
        out_specs=pl.BlockSpec((1, blk, SSM_INNER), lambda b, n: (b, n, 0)),
        out_shape=jax.ShapeDtypeStruct((bsz, seq, SSM_INNER), F32),
        scratch_shapes=[pltpu.VMEM((blk + 8, SSM_CONV_DIM), F32),
                        pltpu.VMEM((blk, SSM_CONV_DIM), F32),
                        pltpu.VMEM((SSM_HEADS, SSM_STATE, SSM_HEAD_DIM), F32),
                        pltpu.VMEM((blk, SSM_INNER), F32)],
        compiler_params=_cparams("arbitrary", "arbitrary"),
        name=name,
    )(sproj, sproj, sproj, conv_w, conv_b.reshape(1, -1), dtb, alog, d_skip,
      norm_g.reshape(1, -1))


def _merge_body(ya_ref, yb_ref, yc1_ref, yc2_ref, yc3_ref, yd_ref, gate_ref, x_ref, wbr_ref,
                wout_ref, o_ref):
    ys = (ya_ref[...], yb_ref[...], yc1_ref[...] + yc2_ref[...] + yc3_ref[...], yd_ref[...])
    u = None
    for nbr in range(N_BRANCH):
        yp = _dot(ys[nbr], wbr_ref[nbr])
        t = _sigmoid(gate_ref[:, nbr * D_MODEL:(nbr + 1) * D_MODEL]) * yp
        u = t if u is None else u + t
    o_ref[...] = x_ref[...] + _dot(u, wout_ref[...])


def merge(ya, yb, yc1, yc2, yc3, yd, gate, x2d, w_branch, w_out, tm, name):
    m = x2d.shape[0]

    def rows(w):
        return pl.BlockSpec((tm, w), lambda i: (i, 0))

    return pl.pallas_call(
        _merge_body,
        grid=(m // tm,),
        in_specs=[rows(512)] * 6 + [rows(N_BRANCH * D_MODEL), rows(D_MODEL),
                                    pl.BlockSpec((N_BRANCH, BRANCH_WIDTH, D_MODEL), lambda i: (0, 0, 0)),
                                    pl.BlockSpec((D_MODEL, D_MODEL), lambda i: (0, 0))],
        out_specs=rows(D_MODEL),
        out_shape=jax.ShapeDtypeStruct((m, D_MODEL), F32),
        compiler_params=_cparams("arbitrary"),
        name=name,
    )(ya, yb, yc1, yc2, yc3, yd, gate, x2d, w_branch.astype(BF16), w_out.astype(BF16))


def _moe_body(x_ref, g_ref, wr_ref, wgu_ref, wd_ref, o_ref, hn_ref, comb_ref):
    e = pl.program_id(1)
    tm = x_ref.shape[0]
    lane = lax.broadcasted_iota(jnp.int32, (tm, LANES), 1)
    lanef = lane.astype(F32)

    @pl.when(e == 0)
    def _():
        x = x_ref[...]
        hn = _rms(x, g_ref[...])
        hn_ref[...] = hn.astype(BF16)
        o_ref[...] = x
        logits = jnp.dot(hn, wr_ref[...], precision=HIGHEST, preferred_element_type=F32)
        is_grp = (lane >= MOE_EXPERTS) & (lane < MOE_EXPERTS + MOE_GROUPS)
        lg = jnp.where(is_grp, logits, -jnp.inf)
        mg = jnp.max(lg, axis=-1, keepdims=True)
        sg = jnp.sum(jnp.where(is_grp, jnp.exp(lg - mg), 0.0), axis=-1, keepdims=True)
        g_w = 1.0 / sg
        gi = jnp.min(jnp.where(lg == mg, lanef, 1e9), axis=-1, keepdims=True) - MOE_EXPERTS
        in_grp = (lane < MOE_EXPERTS) & ((lane // MOE_EPG).astype(F32) == gi)
        le = jnp.where(in_grp, logits, -jnp.inf)
        m1 = jnp.max(le, axis=-1, keepdims=True)
        i1 = jnp.min(jnp.where(le == m1, lanef, 1e9), axis=-1, keepdims=True)
        le2 = jnp.where(lanef == i1, -jnp.inf, le)
        m2 = jnp.max(le2, axis=-1, keepdims=True)
        i2 = jnp.min(jnp.where(le2 == m2, lanef, 1e9), axis=-1, keepdims=True)
        e2 = jnp.exp(m2 - m1)
        den = 1.0 + e2
        comb_ref[...] = (jnp.where(lanef == i1, g_w / den, 0.0)
                         + jnp.where(lanef == i2, g_w * e2 / den, 0.0))

    a = jnp.dot(hn_ref[...], wgu_ref[0], preferred_element_type=F32)
    ce = jnp.sum(jnp.where(lane == e, comb_ref[...], 0.0), axis=-1, keepdims=True)
    act = _silu(a[:, :MOE_FF]) * a[:, MOE_FF:] * ce
    o_ref[...] += _dot(act, wd_ref[0])


def moe(x2d, g_ffn, w_grp, w_exp, w_gate, w_up, w_down, tm, name):
    m = x2d.shape[0]
    wr = jnp.concatenate([w_exp, w_grp,
                          jnp.zeros((D_MODEL, LANES - MOE_EXPERTS - MOE_GROUPS), F32)], axis=1)
    wgu = jnp.concatenate([w_gate, w_up], axis=2).astype(BF16)
    return pl.pallas_call(
        _moe_body,
        grid=(m // tm, MOE_EXPERTS),
        in_specs=[pl.BlockSpec((tm, D_MODEL), lambda i, e: (i, 0)),
                  pl.BlockSpec((1, D_MODEL), lambda i, e: (0, 0)),
                  pl.BlockSpec((D_MODEL, LANES), lambda i, e: (0, 0)),
                  pl.BlockSpec((1, D_MODEL, 2 * MOE_FF), lambda i, e: (e, 0, 0)),
                  pl.BlockSpec((1, MOE_FF, D_MODEL), lambda i, e: (e, 0, 0))],
        out_specs=pl.BlockSpec((tm, D_MODEL), lambda i, e: (i, 0)),
        out_shape=jax.ShapeDtypeStruct((m, D_MODEL), F32),
        scratch_shapes=[pltpu.VMEM((tm, D_MODEL), BF16), pltpu.VMEM((tm, LANES), F32)],
        compiler_params=_cparams("arbitrary", "arbitrary"),
        name=name,
    )(x2d, g_ffn.reshape(1, -1), wr, wgu, w_down.astype(BF16))


def _ple_body(x_ref, p_ref, g_ref, wg_ref, wp_ref, gf_ref, o_ref, *, final):
    x = x_ref[...]
    gate = _sigmoid(_dot(_rms(x, g_ref[...]), wg_ref[...]))
    xn = x + _dot(p_ref[...], wp_ref[...]) * gate
    if final:
        xn = _rms(xn, gf_ref[...])
    o_ref[...] = xn


def ple(x2d, p2d, g_ple, w_gate, w_proj, g_final, final, tm, name):
    m = x2d.shape[0]
    return pl.pallas_call(
        functools.partial(_ple_body, final=final),
        grid=(m // tm,),
        in_specs=[pl.BlockSpec((tm, D_MODEL), lambda i: (i, 0)),
                  pl.BlockSpec((tm, PLE_DIM), lambda i: (i, 0)),
                  pl.BlockSpec((1, D_MODEL), lambda i: (0, 0)),
                  pl.BlockSpec((D_MODEL, D_MODEL), lambda i: (0, 0)),
                  pl.BlockSpec((PLE_DIM, D_MODEL), lambda i: (0, 0)),
                  pl.BlockSpec((1, D_MODEL), lambda i: (0, 0))],
        out_specs=pl.BlockSpec((tm, D_MODEL), lambda i: (i, 0)),
        out_shape=jax.ShapeDtypeStruct((m, D_MODEL), F32),
        compiler_params=_cparams("arbitrary"),
        name=name,
    )(x2d, p2d, g_ple.reshape(1, -1), w_gate.astype(BF16), w_proj.astype(BF16),
      g_final.reshape(1, -1))


def _cols(w, *names):
    return [w[:, _OFF[n][0]:_OFF[n][0] + _OFF[n][1]] for n in names]


def _padcols(w, width):
    return jnp.pad(w, ((0, 0), (0, width - w.shape[1])))


def _pad_q_heads(wq):
    hd = ATT_HEAD_DIM
    zero = jnp.zeros((wq.shape[0], hd), wq.dtype)
    cols = []
    for h in range(ATT_HEADS):
        blk = wq[:, h * hd:(h + 1) * hd] * (hd ** -0.5)
        cols += [blk, zero] if h // ATT_REP == 0 else [zero, blk]
    return jnp.concatenate(cols, axis=1)


def _split_w_in(w):
    swa_q, nsa_q = _cols(w, 'swa_q', 'nsa_q')
    w_att = jnp.concatenate([_pad_q_heads(swa_q), _pad_q_heads(nsa_q)] + _cols(w, *KV_BLK), axis=1)
    w_hgrn = jnp.concatenate(_cols(w, 'hgrn_q', 'hgrn_f', 'hgrn_i', 'hgrn_g'), axis=1)
    (xbc, dt, ngate, z) = _cols(w, 'ssm_xbc', 'ssm_dt', 'nsa_gate', 'ssm_z')
    w_ssm = jnp.concatenate([xbc, _padcols(dt, LANES), _padcols(ngate, LANES), z], axis=1)
    (w_mg,) = _cols(w, 'merge_gate')
    return [a.astype(BF16) for a in (w_att, w_hgrn, w_ssm, w_mg)]


def _mixers(i, att, hproj, sproj, attn_sinks, hgrn_lower_bounds, hgrn_norm_g, nsa_pos_k, nsa_pos_v,
            nsa_cmp_w1_k, nsa_cmp_w2_k, nsa_cmp_w1_v, nsa_cmp_w2_v, ssm_conv_w, ssm_conv_b,
            ssm_dt_bias, ssm_A_log, ssm_D, ssm_norm_g):
    bsz, seq, _ = att.shape
    kv = KV_BLK
    gate_blk = 7
    y_a = banded_attention(att, 0, kv['swa_k'], kv['swa_v'], window=SWA_WINDOW, sinks=attn_sinks[i],
                           name=f"swa{i}")
    y_b = hgrn2(hproj, hgrn_lower_bounds, hgrn_norm_g[i], i, name=f"hgrn{i}")
    ncp = seq // NSA_CMP_STRIDE

    def cmp_in(name):
        c0 = kv[name] * LANES
        return att[:, :, c0:c0 + LANES].reshape(bsz, ncp, NSA_CMP_STRIDE * LANES)

    kc = nsa_compress(cmp_in('nsa_k_cmp'), nsa_pos_k[i], nsa_cmp_w1_k[i], nsa_cmp_w2_k[i], name=f"cmpk{i}")
    vc = nsa_compress(cmp_in('nsa_v_cmp'), nsa_pos_v[i], nsa_cmp_w1_v[i], nsa_cmp_w2_v[i], name=f"cmpv{i}")
    y_c1, mask, act = nsa_cmp_select(att, 1, kc, vc, sproj, gate_blk, name=f"nsacmp{i}")
    y_c2 = nsa_selected(att, 1, kv['nsa_k_slc'], kv['nsa_v_slc'], mask, act, sproj, gate_blk,
                        name=f"nsasel{i}")
    y_c3 = banded_attention(att, 1, kv['nsa_k_win'], kv['nsa_v_win'], window=NSA_WINDOW, gate=sproj,
                            gate_blk=gate_blk, gate_col=2, name=f"nsawin{i}")
    y_d = mamba2(sproj, ssm_conv_w[i], ssm_conv_b[i], ssm_dt_bias[i], ssm_A_log[i], ssm_D[i],
                 ssm_norm_g[i], name=f"ssd{i}")
    return y_a, y_b, y_c1, y_c2, y_c3, y_d


def kernel(x, p, w_in, g_mix, attn_sinks, hgrn_lower_bounds, hgrn_norm_g, nsa_pos_k, nsa_pos_v,
           nsa_cmp_w1_k, nsa_cmp_w2_k, nsa_cmp_w1_v, nsa_cmp_w2_v, ssm_conv_w, ssm_conv_b,
           ssm_dt_bias, ssm_A_log, ssm_D, ssm_norm_g, w_branch, w_out, g_ffn, w_router_grp,
           w_router_exp, w_exp_gate, w_exp_up, w_exp_down, g_ple, w_ple_gate, w_ple_proj, g_final):
    bsz, seq, d = x.shape
    depth = w_in.shape[0]
    t = bsz * seq
    x2 = x.reshape(t, d)
    tm_proj = min(1024, t)
    tm_row = min(256, t)
    tm_moe = min(1024, t)
    for i in range(depth):
        w_att, w_hgrn, w_ssm, w_mg = _split_w_in(w_in[i])
        att = norm_mm(x2, g_mix[i], w_att, BF16, tm_proj, 512, f"proj_att{i}").reshape(bsz, seq, -1)
        hproj = norm_mm(x2, g_mix[i], w_hgrn, F32, tm_proj, 512, f"proj_hgrn{i}").reshape(bsz, seq, -1)
        sproj = norm_mm(x2, g_mix[i], w_ssm, F32, tm_proj, 512, f"proj_ssm{i}").reshape(bsz, seq, -1)
        mgate = norm_mm(x2, g_mix[i], w_mg, F32, tm_proj, 512, f"proj_gate{i}")
        ys = _mixers(i, att, hproj, sproj, attn_sinks, hgrn_lower_bounds, hgrn_norm_g, nsa_pos_k,
                     nsa_pos_v, nsa_cmp_w1_k, nsa_cmp_w2_k, nsa_cmp_w1_v, nsa_cmp_w2_v, ssm_conv_w,
                     ssm_conv_b, ssm_dt_bias, ssm_A_log, ssm_D, ssm_norm_g)
        ys = [y.reshape(t, -1) for y in ys]
        x2 = merge(*ys, mgate, x2, w_branch[i], w_out[i], tm_row, f"merge{i}")
        x2 = moe(x2, g_ffn[i], w_router_grp[i], w_router_exp[i], w_exp_gate[i], w_exp_up[i],
                 w_exp_down[i], tm_moe, f"moe{i}")
        x2 = ple(x2, p[i].reshape(t, -1), g_ple[i], w_ple_gate[i], w_ple_proj[i], g_final,
                 i == depth - 1, tm_row, f"ple{i}")
    return x2.reshape(bsz, seq, d)
```

```python
import functools

import numpy as np
import jax
import jax.numpy as jnp
from jax import lax
from jax.experimental import pallas as pl
from jax.experimental.pallas import tpu as pltpu

F32 = jnp.float32
BF16 = jnp.bfloat16
HIGHEST = lax.Precision.HIGHEST

D_MODEL = 1024
PLE_DIM = 256
NORM_EPS = 1e-6
NEG_INF = -1e30
REMOVED = -3e38
N_BRANCH = 4
BRANCH_WIDTH = 512
ATTN_BLOCK = 128

ATT_HEADS = 8
ATT_KV_HEADS = 2
ATT_HEAD_DIM = 64
ATT_REP = ATT_HEADS // ATT_KV_HEADS
SWA_WINDOW = 128
NSA_WINDOW = 512
NSA_CMP_LEN = 32
NSA_CMP_STRIDE = 16
NSA_CMP_HIDDEN = 256
NSA_SEL_LEN = 64
NSA_TOP_N = 16
NSA_FORCE_SCORE = 1e6
SEL_KEY_TILE = 512
SEL_FEATS = SEL_KEY_TILE // NSA_SEL_LEN
MASK_BIG = 1e30

HGRN_HEADS = 4
HGRN_DIM = 128
HGRN_CHUNK = 32
HGRN_MIN_F = 1e-6
HGRN_BLOCK = 256

SSM_HEADS = 8
SSM_HEAD_DIM = 64
SSM_GROUPS = 2
SSM_STATE = 64
SSM_CONV = 4
SSM_CHUNK = 64
SSM_INNER = 512
SSM_CONV_DIM = 768
SSM_BLOCK = 256

MOE_GROUPS = 4
MOE_EPG = 8
MOE_EXPERTS = 32
MOE_FF = 256

LANES = 128
VMEM_LIMIT = 56 * 1024 * 1024

ALIBI_SLOPES = tuple(2.0 ** (-8.0 * (h + 1) / ATT_HEADS) for h in range(ATT_HEADS))

ATT_Q_WIDTH = ATT_HEADS * LANES
KV_BLK = {name: 2 * ATT_HEADS + j for j, name in enumerate(
    ('swa_k', 'swa_v', 'nsa_k_cmp', 'nsa_v_cmp', 'nsa_k_slc', 'nsa_v_slc', 'nsa_k_win', 'nsa_v_win'))}

_SPLITS = (
    ('swa_q', 512), ('swa_k', 128), ('swa_v', 128),
    ('hgrn_q', 512), ('hgrn_f', 512), ('hgrn_i', 512), ('hgrn_g', 512),
    ('nsa_q', 512), ('nsa_k_cmp', 128), ('nsa_v_cmp', 128), ('nsa_k_slc', 128),
    ('nsa_v_slc', 128), ('nsa_k_win', 128), ('nsa_v_win', 128), ('nsa_gate', 24),
    ('ssm_z', 512), ('ssm_xbc', 768), ('ssm_dt', 8), ('merge_gate', 4096),
)
_OFF = {}
_o = 0
for _n, _w in _SPLITS:
    _OFF[_n] = (_o, _w)
    _o += _w


def _cparams(*sem):
    return pltpu.CompilerParams(dimension_semantics=sem, vmem_limit_bytes=VMEM_LIMIT)


def _sigmoid(x):
    return 1.0 / (1.0 + jnp.exp(-x))


def _silu(x):
    return x * _sigmoid(x)


def _dot(a, b):
    return jnp.dot(a.astype(BF16), b.astype(BF16), preferred_element_type=F32)


def _dot_nt(a, b):
    return lax.dot_general(a.astype(BF16), b.astype(BF16), (((1,), (1,)), ((), ())),
                           preferred_element_type=F32)


def _dot_tn(a, b):
    return lax.dot_general(a.astype(BF16), b.astype(BF16), (((0,), (0,)), ((), ())),
                           preferred_element_type=F32)


def _rms(x, g):
    ms = jnp.mean(x * x, axis=-1, keepdims=True)
    return x * lax.rsqrt(ms + NORM_EPS) * g


def _norm_mm_body(x_ref, g_ref, w_ref, o_ref, hn_ref):
    @pl.when(pl.program_id(1) == 0)
    def _():
        hn_ref[...] = _rms(x_ref[...], g_ref[...]).astype(BF16)

    o_ref[...] = jnp.dot(hn_ref[...], w_ref[...], preferred_element_type=F32).astype(o_ref.dtype)


def norm_mm(x2d, g, w, out_dtype, tm, tn, name):
    m, k = x2d.shape
    n = w.shape[1]
    return pl.pallas_call(
        _norm_mm_body,
        grid=(m // tm, n // tn),
        in_specs=[pl.BlockSpec((tm, k), lambda i, j: (i, 0)),
                  pl.BlockSpec((1, k), lambda i, j: (0, 0)),
                  pl.BlockSpec((k, tn), lambda i, j: (0, j))],
        out_specs=pl.BlockSpec((tm, tn), lambda i, j: (i, j)),
        out_shape=jax.ShapeDtypeStruct((m, n), out_dtype),
        scratch_shapes=[pltpu.VMEM((tm, k), BF16)],
        compiler_params=_cparams("arbitrary", "arbitrary"),
        name=name,
    )(x2d, g.reshape(1, k), w)


def _ones_row(g):
    return ATT_HEAD_DIM if g == 0 else 0


def _vt_body(v_ref, o_ref):
    vt = jnp.transpose(v_ref[0].astype(F32))
    rowid = lax.broadcasted_iota(jnp.int32, vt.shape, 0)
    for g in range(ATT_KV_HEADS):
        aug = jnp.where(rowid // ATT_HEAD_DIM == g, vt, jnp.where(rowid == _ones_row(g), 1.0, 0.0))
        o_ref[0, g] = aug.astype(BF16)


def v_transposed(arr, v_blk, name):
    bsz, seq, _ = arr.shape
    rows = min(2048, seq)
    return pl.pallas_call(
        _vt_body,
        grid=(bsz, seq // rows),
        in_specs=[pl.BlockSpec((1, rows, LANES), lambda b, n: (b, n, v_blk))],
        out_specs=pl.BlockSpec((1, ATT_KV_HEADS, LANES, rows), lambda b, n: (b, 0, 0, n)),
        out_shape=jax.ShapeDtypeStruct((bsz, ATT_KV_HEADS, LANES, seq), BF16),
        compiler_params=_cparams("arbitrary", "arbitrary"),
        name=name,
    )(arr)


def _q_stack(q_ref, g):
    return jnp.concatenate([q_ref[0, :, (g * ATT_REP + r) * LANES:(g * ATT_REP + r + 1) * LANES]
                            for r in range(ATT_REP)], axis=0)


def _head(x, r):
    return x[:, r * ATTN_BLOCK:(r + 1) * ATTN_BLOCK]


def _banded_body(*refs, window, nprev, use_sink, gate_col):
    if use_sink:
        q_ref, k_ref, vt_ref, sink_ref, o_ref, ot_ref = refs
    else:
        q_ref, k_ref, vt_ref, gate_ref, o_ref, ot_ref = refs
    blk = ATTN_BLOCK
    hd = ATT_HEAD_DIM
    n = pl.program_id(1)
    nk = (nprev + 1) * blk
    start = pl.multiple_of(jnp.maximum(n - nprev, 0) * blk, blk)
    k128 = k_ref[0, pl.ds(start, nk), :]
    krow = lax.broadcasted_iota(jnp.int32, (nk, blk), 0)
    qcol = lax.broadcasted_iota(jnp.int32, (nk, blk), 1)
    rel = (n * blk + qcol) - (start + krow)
    negrel = jnp.where((rel >= 0) & (rel < window), -rel.astype(F32), NEG_INF)
    if not use_sink:
        gate_t = jnp.transpose(_sigmoid(gate_ref[0]))
    for g in range(ATT_KV_HEADS):
        one = _ones_row(g)
        st4 = _dot_nt(k128, _q_stack(q_ref, g))
        ps, ms = [], []
        for r in range(ATT_REP):
            h = g * ATT_REP + r
            st = _head(st4, r) + ALIBI_SLOPES[h] * negrel
            m = jnp.max(st, axis=0, keepdims=True)
            if use_sink:
                m = jnp.maximum(m, sink_ref[h])
            ps.append(jnp.exp(st - m).astype(BF16))
            ms.append(m)
        acc4 = jnp.dot(vt_ref[0, g, :, pl.ds(start, nk)], jnp.concatenate(ps, axis=1),
                       preferred_element_type=F32)
        for r in range(ATT_REP):
            h = g * ATT_REP + r
            acc = _head(acc4, r)
            l = acc[one:one + 1]
            if use_sink:
                l = l + jnp.exp(sink_ref[h] - ms[r])
            ot = acc[g * hd:(g + 1) * hd] / l
            if not use_sink:
                c = 3 * h + gate_col
                ot = ot * gate_t[c:c + 1]
            ot_ref[h * hd:(h + 1) * hd, :] = ot
    o_ref[0] = jnp.transpose(ot_ref[...])


def banded_attention(att, q_blk, k_blk, v_blk, *, window, sinks=None, gate=None, gate_blk=None,
                     gate_col=0, name):
    bsz, seq, _ = att.shape
    blk = ATTN_BLOCK
    nprev = (window - 1 + blk - 1) // blk
    use_sink = sinks is not None
    vt = v_transposed(att, v_blk, name + "_vt")
    in_specs = [pl.BlockSpec((1, blk, ATT_Q_WIDTH), lambda b, n: (b, n, q_blk)),
                pl.BlockSpec((1, seq, LANES), lambda b, n: (b, 0, k_blk)),
                pl.BlockSpec((1, ATT_KV_HEADS, LANES, seq), lambda b, n: (b, 0, 0, 0))]
    if use_sink:
        in_specs.append(pl.BlockSpec(memory_space=pltpu.SMEM))
        extra = sinks
    else:
        in_specs.append(pl.BlockSpec((1, blk, LANES), lambda b, n: (b, n, gate_blk)))
        extra = gate
    return pl.pallas_call(
        functools.partial(_banded_body, window=window, nprev=nprev, use_sink=use_sink,
                          gate_col=gate_col),
        grid=(bsz, seq // blk),
        in_specs=in_specs,
        out_specs=pl.BlockSpec((1, blk, 512), lambda b, n: (b, n, 0)),
        out_shape=jax.ShapeDtypeStruct((bsz, seq, 512), F32),
        scratch_shapes=[pltpu.VMEM((512, blk), F32)],
        compiler_params=_cparams("arbitrary", "arbitrary"),
        name=name,
    )(att, att, vt, extra)


def _compress_body(x_ref, w1a_ref, w1b_ref, w1_ref, pos_ref, w2_ref, o_ref):
    x = x_ref[0]
    p = jnp.dot(x, w1a_ref[...], preferred_element_type=F32)
    q = jnp.dot(x, w1b_ref[...], preferred_element_type=F32)
    ncp = x.shape[0]
    q = pltpu.roll(q, shift=ncp - 1, axis=0)
    posb = jnp.broadcast_to(pos_ref[...], (8, pos_ref.shape[1]))
    bias = _dot(posb, w1_ref[...])[0:1]
    hid = NSA_CMP_HIDDEN
    outs = []
    for g in range(ATT_KV_HEADS):
        pre = p[:, g * hid:(g + 1) * hid] + q[:, g * hid:(g + 1) * hid] + bias
        outs.append(_dot(jax.nn.gelu(pre), w2_ref[...]))
    o_ref[0] = jnp.concatenate(outs, axis=1).astype(o_ref.dtype)


def _expand_w1(w1, half):
    hd, hid, ng = ATT_HEAD_DIM, NSA_CMP_HIDDEN, ATT_KV_HEADS
    w = w1.reshape(NSA_CMP_LEN, hd, hid)[half * 16:(half + 1) * 16]
    eye = jnp.eye(ng, dtype=w1.dtype)
    out = jnp.einsum('ldj,gh->lgdhj', w, eye)
    return out.reshape(16 * ng * hd, ng * hid)


def nsa_compress(x16, pos, w1, w2, name):
    bsz, ncp, wid = x16.shape
    hid = NSA_CMP_HIDDEN
    w1a = _expand_w1(w1, 0).astype(BF16)
    w1b = _expand_w1(w1, 1).astype(BF16)
    return pl.pallas_call(
        _compress_body,
        grid=(bsz,),
        in_specs=[pl.BlockSpec((1, ncp, wid), lambda b: (b, 0, 0)),
                  pl.BlockSpec((wid, 2 * hid), lambda b: (0, 0)),
                  pl.BlockSpec((wid, 2 * hid), lambda b: (0, 0)),
                  pl.BlockSpec((NSA_CMP_LEN * ATT_HEAD_DIM, hid), lambda b: (0, 0)),
                  pl.BlockSpec((1, NSA_CMP_LEN * ATT_HEAD_DIM), lambda b: (0, 0)),
                  pl.BlockSpec((hid, ATT_HEAD_DIM), lambda b: (0, 0))],
        out_specs=pl.BlockSpec((1, ncp, LANES), lambda b: (b, 0, 0)),
        out_shape=jax.ShapeDtypeStruct((bsz, ncp, LANES), BF16),
        compiler_params=_cparams("arbitrary"),
        name=name,
    )(x16, w1a, w1b, w1.astype(BF16), pos.reshape(1, -1), w2.astype(BF16))


def _nsa_cmp_body(q_ref, kc_ref, vct_ref, ovt_ref, gate_ref, o_ref, m_ref, act_ref, ot_ref, *, top_n):
    blk = ATTN_BLOCK
    hd = ATT_HEAD_DIM
    n = pl.program_id(1)
    kc = kc_ref[0]
    ncp = kc.shape[0]
    ns = ovt_ref.shape[0]
    crow = lax.broadcasted_iota(jnp.int32, (ncp, blk), 0)
    qcol = lax.broadcasted_iota(jnp.int32, (ncp, blk), 1)
    dist = (n * blk + qcol) - (crow * NSA_CMP_STRIDE + (NSA_CMP_LEN - 1))
    valid = dist >= 0
    negd = jnp.where(valid, -dist.astype(F32), NEG_INF)
    jrow = lax.broadcasted_iota(jnp.int32, (ns, blk), 0)
    qpos = n * blk + lax.broadcasted_iota(jnp.int32, (ns, blk), 1)
    cur = qpos // NSA_SEL_LEN
    forced = (jrow == 0) | (jrow == cur) | (jrow == cur - 1)
    causal_blk = jrow * NSA_SEL_LEN <= qpos
    jrowf = jrow.astype(F32)
    gate_t = jnp.transpose(_sigmoid(gate_ref[0]))
    ones8 = jnp.ones((8, blk), BF16)
    for g in range(ATT_KV_HEADS):
        one = _ones_row(g)
        st4 = _dot_nt(kc, _q_stack(q_ref, g))
        es = []
        for r in range(ATT_REP):
            st = _head(st4, r) + ALIBI_SLOPES[g * ATT_REP + r] * negd
            m = jnp.max(st, axis=0, keepdims=True)
            es.append(jnp.where(valid, jnp.exp(st - m), 0.0))
        acc4 = jnp.dot(vct_ref[0, g], jnp.concatenate([e.astype(BF16) for e in es], axis=1),
                       preferred_element_type=F32)
        psum = jnp.zeros((ncp, blk), F32)
        for r in range(ATT_REP):
            h = g * ATT_REP + r
            acc = _head(acc4, r)
            l = acc[one:one + 1]
            inv = jnp.where(l > 0.0, 1.0 / jnp.where(l > 0.0, l, 1.0), 0.0)
            ot_ref[h * hd:(h + 1) * hd, :] = acc[g * hd:(g + 1) * hd] * inv * gate_t[3 * h:3 * h + 1]
            psum = psum + es[r] * inv
        imp = _dot(ovt_ref[...], psum)
        score = jnp.where(forced, NSA_FORCE_SCORE, jnp.where(causal_blk, imp, NEG_INF))
        sel = jnp.zeros((ns, blk), F32)
        for _ in range(top_n):
            mx = jnp.max(score, axis=0, keepdims=True)
            idx = jnp.min(jnp.where(score == mx, jrowf, float(ns)), axis=0, keepdims=True)
            hit = jrowf == idx
            sel = jnp.where(hit, 1.0, sel)
            score = jnp.where(hit, REMOVED, score)
        m_ref[0, g] = sel
        act_ref[0, 0, g * 8:(g + 1) * 8, :] = _dot_nt(ones8, sel)
    o_ref[0] = jnp.transpose(ot_ref[...])


def _overlap_matrix_t(seq):
    n_cmp = seq // NSA_CMP_STRIDE
    n_sel = seq // NSA_SEL_LEN
    cs = np.arange(n_cmp)[None, :] * NSA_CMP_STRIDE
    ss = np.arange(n_sel)[:, None] * NSA_SEL_LEN
    ov = np.clip(np.minimum(cs + NSA_CMP_LEN, ss + NSA_SEL_LEN) - np.maximum(cs, ss), 0, None)
    return jnp.asarray(ov / NSA_CMP_LEN, dtype=BF16)


def nsa_cmp_select(att, q_blk, kc, vc, gate, gate_blk, name):
    bsz, seq, _ = att.shape
    blk = ATTN_BLOCK
    nb = seq // blk
    ncp = seq // NSA_CMP_STRIDE
    ns = seq // NSA_SEL_LEN
    top_n = min(NSA_TOP_N, ns)
    vct = v_transposed(vc, 0, name + "_vt")
    return pl.pallas_call(
        functools.partial(_nsa_cmp_body, top_n=top_n),
        grid=(bsz, nb),
        in_specs=[pl.BlockSpec((1, blk, ATT_Q_WIDTH), lambda b, n: (b, n, q_blk)),
                  pl.BlockSpec((1, ncp, LANES), lambda b, n: (b, 0, 0)),
                  pl.BlockSpec((1, ATT_KV_HEADS, LANES, ncp), lambda b, n: (b, 0, 0, 0)),
                  pl.BlockSpec((ns, ncp), lambda b, n: (0, 0)),
                  pl.BlockSpec((1, blk, LANES), lambda b, n: (b, n, gate_blk))],
        out_specs=[pl.BlockSpec((1, blk, 512), lambda b, n: (b, n, 0)),
                   pl.BlockSpec((1, ATT_KV_HEADS, ns, blk), lambda b, n: (b, 0, 0, n)),
                   pl.BlockSpec((1, 1, ATT_KV_HEADS * 8, ns), lambda b, n: (b, n, 0, 0))],
        out_shape=[jax.ShapeDtypeStruct((bsz, seq, 512), F32),
                   jax.ShapeDtypeStruct((bsz, ATT_KV_HEADS, ns, seq), F32),
                   jax.ShapeDtypeStruct((bsz, nb, ATT_KV_HEADS * 8, ns), F32)],
        scratch_shapes=[pltpu.VMEM((512, blk), F32)],
        compiler_params=_cparams("arbitrary", "arbitrary"),
        name=name,
    )(att, kc, vct, _overlap_matrix_t(seq), gate)


def _k_aug_body(k_ref, o_ref):
    k = k_ref[0].astype(F32)
    rows = k.shape[0]
    kin = lax.broadcasted_iota(jnp.int32, (rows, LANES), 0) % SEL_KEY_TILE
    lane = lax.broadcasted_iota(jnp.int32, (rows, LANES), 1)
    for g in range(ATT_KV_HEADS):
        f = lane - (ATT_HEAD_DIM if g == 0 else 0)
        feat = jnp.where(f == kin // NSA_SEL_LEN, 1.0, 0.0)
        feat = jnp.where(f == SEL_FEATS, (kin % 256).astype(F32), feat)
        feat = jnp.where(f == SEL_FEATS + 1, (kin // 256 * 256).astype(F32), feat)
        feat = jnp.where((f == SEL_FEATS + 2) | (f == SEL_FEATS + 3), 1.0, feat)
        o_ref[0, g] = jnp.where(lane // ATT_HEAD_DIM == g, k, feat).astype(BF16)


def k_augmented(arr, k_blk, name):
    bsz, seq, _ = arr.shape
    rows = min(2048, seq)
    return pl.pallas_call(
        _k_aug_body,
        grid=(bsz, seq // rows),
        in_specs=[pl.BlockSpec((1, rows, LANES), lambda b, n: (b, n, k_blk))],
        out_specs=pl.BlockSpec((1, ATT_KV_HEADS, rows, LANES), lambda b, n: (b, 0, n, 0)),
        out_shape=jax.ShapeDtypeStruct((bsz, ATT_KV_HEADS, seq, LANES), BF16),
        compiler_params=_cparams("arbitrary", "arbitrary"),
        name=name,
    )(arr)


def _query_feature_rows():
    out = np.zeros((ATT_KV_HEADS, 2 * SEL_FEATS, ATT_REP * ATTN_BLOCK), np.float32)
    qin = np.arange(ATTN_BLOCK, dtype=np.float32)
    for g in range(ATT_KV_HEADS):
        for r in range(ATT_REP):
            slope = ALIBI_SLOPES[g * ATT_REP + r]
            cols = slice(r * ATTN_BLOCK, (r + 1) * ATTN_BLOCK)
            out[g, 0, cols] = slope
            out[g, 1, cols] = slope
            out[g, 3, cols] = -slope * qin
            out[g, SEL_FEATS + 2, cols] = -slope
    return jnp.asarray(out)


def _nsa_sel_body(tiles_ref, cnt_ref, q_ref, k_ref, vt_ref, m_ref, gate_ref, fq_ref, o_ref, ot_ref,
                  acc_ref, mx_ref, dq_ref, *, ntl):
    blk = ATTN_BLOCK
    hd = ATT_HEAD_DIM
    tk = SEL_KEY_TILE
    sl = NSA_SEL_LEN
    per = tk // sl
    b = pl.program_id(0)
    n = pl.program_id(1)
    nb = pl.num_programs(1)
    krow = lax.broadcasted_iota(jnp.int32, (tk, blk), 0)
    qcol = lax.broadcasted_iota(jnp.int32, (tk, blk), 1)
    dq_ref[...] = (qcol - krow).astype(F32)
    gate_t = jnp.transpose(_sigmoid(gate_ref[0]))
    pad_rows = jnp.zeros((hd - 2 * SEL_FEATS, ATT_REP * blk), BF16)
    for g in range(ATT_KV_HEADS):
        one = _ones_row(g)
        lrow = (b * nb + n) * ATT_KV_HEADS + g
        qt = jnp.concatenate(
            [jnp.transpose(q_ref[0, :, (g * ATT_REP + r) * LANES:(g * ATT_REP + r + 1) * LANES]
                           .astype(F32))[g * hd:(g + 1) * hd] for r in range(ATT_REP)],
            axis=1).astype(BF16)
        acc_ref[...] = jnp.zeros_like(acc_ref)
        mx_ref[...] = jnp.full_like(mx_ref, NEG_INF)

        def tile_scores(j, live):
            t = tiles_ref[lrow * ntl + j]
            base = pl.multiple_of(t * tk, tk)
            sel8 = m_ref[0, g, pl.ds(pl.multiple_of(t * per, per), per), :]
            if live is not None:
                sel8 = jnp.where(live, sel8, 0.0)
            off = (n * blk - base).astype(F32)
            mask_rows = jnp.concatenate([(sel8 - 1.0) * MASK_BIG] * ATT_REP, axis=1)
            alibi_rows = fq_ref[g, 0:SEL_FEATS, :] + off * fq_ref[g, SEL_FEATS:2 * SEL_FEATS, :]
            feats = jnp.concatenate([mask_rows, alibi_rows], axis=0).astype(BF16)
            rhs = jnp.concatenate([qt, feats, pad_rows] if g == 0 else [feats, pad_rows, qt], axis=0)
            st4 = jnp.dot(k_ref[0, g, pl.ds(base, tk), :], rhs,
                          preferred_element_type=F32)
            return st4, base, off

        def tile_finish(st4, base, off, diag):
            if diag:
                causal = (dq_ref[...] + off) >= 0.0
            ps, ms = [], []
            for r in range(ATT_REP):
                st = _head(st4, r)
                if diag:
                    st = jnp.where(causal, st, NEG_INF)
                m = jnp.max(st, axis=0, keepdims=True)
                ps.append(jnp.exp(st - m).astype(BF16))
                ms.append(m)
            acc = jnp.dot(vt_ref[0, g, :, pl.ds(base, tk)], jnp.concatenate(ps, axis=1),
                          preferred_element_type=F32)
            return jnp.concatenate(ms, axis=1), acc

        def pair(j2, diag):
            cnt = cnt_ref[lrow]
            sc_a = tile_scores(2 * j2, None)
            sc_b = tile_scores(jnp.minimum(2 * j2 + 1, ntl - 1), 2 * j2 + 1 < cnt)
            m_a, acc_a = tile_finish(*sc_a, diag)
            m_b, acc_b = tile_finish(*sc_b, False)
            m_old = mx_ref[...]
            m_new = jnp.maximum(m_old, jnp.maximum(m_a, m_b))
            acc_ref[...] = (jnp.exp(m_old - m_new) * acc_ref[...] + jnp.exp(m_a - m_new) * acc_a
                            + jnp.exp(m_b - m_new) * acc_b)
            mx_ref[...] = m_new

        pair(0, True)

        def later(j2, carry):
            pair(j2, False)
            return carry

        lax.fori_loop(1, (cnt_ref[lrow] + 1) // 2, later, 0)
        acc4 = acc_ref[...]
        for r in range(ATT_REP):
            h = g * ATT_REP + r
            acc = _head(acc4, r)
            ot = acc[g * hd:(g + 1) * hd] / acc[one:one + 1]
            ot_ref[h * hd:(h + 1) * hd, :] = ot * gate_t[3 * h + 1:3 * h + 2]
    o_ref[0] = jnp.transpose(ot_ref[...])


def _tile_lists(act, seq):
    bsz, nb = act.shape[:2]
    ns = act.shape[-1]
    tk = SEL_KEY_TILE
    ntl = seq // tk
    per = tk // NSA_SEL_LEN
    cnt_blk = act.reshape(bsz, nb, ATT_KV_HEADS, 8, ns)[:, :, :, 0, :]
    hit = cnt_blk.reshape(bsz, nb, ATT_KV_HEADS, ntl, per).sum(-1) > 0.5
    tidx = jnp.arange(ntl, dtype=jnp.int32)
    diag = (jnp.arange(nb, dtype=jnp.int32) * ATTN_BLOCK + ATTN_BLOCK - 1) // tk
    active = hit & (tidx[None, None, None, :] <= diag[None, :, None, None])
    act_i = active.astype(jnp.int32)
    rank = jnp.cumsum(act_i[..., ::-1], axis=-1)[..., ::-1] - 1
    slot = (active[..., :, None] & (rank[..., :, None] == tidx)).astype(jnp.int32)
    tiles = (slot * tidx[:, None]).sum(-2).reshape(-1)
    cnt = act_i.sum(-1).reshape(-1)
    return tiles, cnt, ntl


def nsa_selected(att, q_blk, k_blk, v_blk, mask, act, gate, gate_blk, name):
    bsz, seq, _ = att.shape
    blk = ATTN_BLOCK
    ns = seq // NSA_SEL_LEN
    tiles, cnt, ntl = _tile_lists(act, seq)
    vt = v_transposed(att, v_blk, name + "_vt")
    kaug = k_augmented(att, k_blk, name + "_k")
    grid_spec = pltpu.PrefetchScalarGridSpec(
        num_scalar_prefetch=2,
        grid=(bsz, seq // blk),
        in_specs=[pl.BlockSpec((1, blk, ATT_Q_WIDTH), lambda b, n, *_: (b, n, q_blk)),
                  pl.BlockSpec((1, ATT_KV_HEADS, seq, LANES), lambda b, n, *_: (b, 0, 0, 0)),
                  pl.BlockSpec((1, ATT_KV_HEADS, LANES, seq), lambda b, n, *_: (b, 0, 0, 0)),
                  pl.BlockSpec((1, ATT_KV_HEADS, ns, blk), lambda b, n, *_: (b, 0, 0, n)),
                  pl.BlockSpec((1, blk, LANES), lambda b, n, *_: (b, n, gate_blk)),
                  pl.BlockSpec((ATT_KV_HEADS, 2 * SEL_FEATS, ATT_REP * blk), lambda b, n, *_: (0, 0, 0))],
        out_specs=pl.BlockSpec((1, blk, 512), lambda b, n, *_: (b, n, 0)),
        scratch_shapes=[pltpu.VMEM((512, blk), F32),
                        pltpu.VMEM((LANES, ATT_REP * blk), F32),
                        pltpu.VMEM((1, ATT_REP * blk), F32),
                        pltpu.VMEM((SEL_KEY_TILE, blk), F32)],
    )
    return pl.pallas_call(
        functools.partial(_nsa_sel_body, ntl=ntl),
        grid_spec=grid_spec,
        out_shape=jax.ShapeDtypeStruct((bsz, seq, 512), F32),
        compiler_params=_cparams("arbitrary", "arbitrary"),
        name=name,
    )(tiles, cnt, att, kaug, vt, mask, gate, _query_feature_rows())


def _hgrn_body(q_ref, f_ref, i_ref, g_ref, lbp_ref, ng_ref, o_ref, st_ref, b_ref, k_ref, *, layer):
    blk = HGRN_BLOCK
    ch = HGRN_CHUNK
    dk = HGRN_DIM

    @pl.when(pl.program_id(1) == 0)
    def _():
        st_ref[...] = jnp.zeros_like(st_ref)

    lbp = lbp_ref[...]
    e = jnp.exp(lbp - jnp.max(lbp, axis=0, keepdims=True))
    sm = e / jnp.sum(e, axis=0, keepdims=True)
    lb = jnp.zeros((1, lbp.shape[1]), F32)
    for d in range(1, layer + 1):
        lb = lb + sm[d:d + 1]
    z = f_ref[0]
    f = lb + (1.0 - lb) * _sigmoid(z)
    logf = jnp.log(jnp.maximum(f, HGRN_MIN_F))
    k_ref[...] = (1.0 - lb) * _sigmoid(-z)
    tr = lax.broadcasted_iota(jnp.int32, (blk, blk), 0)
    tc = lax.broadcasted_iota(jnp.int32, (blk, blk), 1)
    tri = jnp.where((tr // ch == tc // ch) & (tc <= tr), 1.0, 0.0).astype(F32)
    b_ref[...] = jnp.dot(tri, logf, precision=HIGHEST, preferred_element_type=F32)
    cr = lax.broadcasted_iota(jnp.int32, (ch, ch), 0)
    cc = lax.broadcasted_iota(jnp.int32, (ch, ch), 1)
    causal = cc <= cr
    ng = ng_ref[...]

    for c in range(blk // ch):
        r0 = c * ch
        bc = b_ref[pl.ds(r0, ch), :]
        qc = q_ref[0, pl.ds(r0, ch), :]
        kc = k_ref[pl.ds(r0, ch), :]
        vc = i_ref[0, pl.ds(r0, ch), :]
        gc = g_ref[0, pl.ds(r0, ch), :]
        b_mid = bc[ch // 2:ch // 2 + 1]
        b_last = bc[ch - 1:ch]
        qa = qc * jnp.exp(bc - b_mid)
        ka = kc * jnp.exp(b_mid - bc)
        qe = qc * jnp.exp(bc)
        kl = kc * jnp.exp(b_last - bc)
        dec = jnp.exp(b_last)
        for h in range(HGRN_HEADS):
            sl = slice(h * dk, (h + 1) * dk)
            a = jnp.where(causal, _dot_nt(qa[:, sl], ka[:, sl]), 0.0)
            st = st_ref[h]
            o = _dot(a, vc[:, sl]) + _dot_nt(qe[:, sl], st)
            st_ref[h] = st * dec[:, sl] + _dot_tn(vc[:, sl], kl[:, sl])
            o = _rms(o, ng[:, sl]) * _silu(gc[:, sl])
            o_ref[0, pl.ds(r0, ch), sl] = o


def hgrn2(hproj, lower_bounds, norm_g, layer, name):
    bsz, seq, _ = hproj.shape
    blk = HGRN_BLOCK
    wid = HGRN_HEADS * HGRN_DIM
    depth = lower_bounds.shape[0]

    def col(j):
        return pl.BlockSpec((1, blk, wid), lambda b, n: (b, n, j))

    return pl.pallas_call(
        functools.partial(_hgrn_body, layer=layer),
        grid=(bsz, seq // blk),
        in_specs=[col(0), col(1), col(2), col(3),
                  pl.BlockSpec((depth, wid), lambda b, n: (0, 0)),
                  pl.BlockSpec((1, wid), lambda b, n: (0, 0))],
        out_specs=pl.BlockSpec((1, blk, wid), lambda b, n: (b, n, 0)),
        out_shape=jax.ShapeDtypeStruct((bsz, seq, wid), F32),
        scratch_shapes=[pltpu.VMEM((HGRN_HEADS, HGRN_DIM, HGRN_DIM), F32),
                        pltpu.VMEM((blk, wid), F32),
                        pltpu.VMEM((blk, wid), F32)],
        compiler_params=_cparams("arbitrary", "arbitrary"),
        name=name,
    )(hproj, hproj, hproj, hproj, lower_bounds, norm_g.reshape(1, wid))


def _ssd_body(xbc_ref, dt_ref, z_ref, cw_ref, cb_ref, dtb_ref, alog_ref, dskip_ref, ng_ref,
              o_ref, xp_ref, act_ref, st_ref, y_ref):
    blk = SSM_BLOCK
    ch = SSM_CHUNK
    hp = SSM_HEAD_DIM
    ns = SSM_STATE
    rep = SSM_HEADS // SSM_GROUPS
    pad = 8

    @pl.when(pl.program_id(1) == 0)
    def _():
        xp_ref[0:pad, :] = jnp.zeros((pad, SSM_CONV_DIM), F32)
        st_ref[...] = jnp.zeros_like(st_ref)

    xin = xbc_ref[0]
    xp_ref[pad:pad + blk, :] = xin
    cw = cw_ref[...]
    conv = cb_ref[...] + cw[SSM_CONV - 1:SSM_CONV] * xin
    for j in range(SSM_CONV - 1):
        shift = SSM_CONV - 1 - j
        conv = conv + cw[j:j + 1] * xp_ref[pl.ds(pad - shift, blk), :]
    xp_ref[0:pad, :] = xin[blk - pad:blk]
    act_ref[...] = _silu(conv)

    dt_raw = dt_ref[0] + dtb_ref[...]
    dt = jnp.maximum(dt_raw, 0.0) + jnp.log(1.0 + jnp.exp(-jnp.abs(dt_raw)))
    a_all = dt * (-jnp.exp(alog_ref[...]))
    cr = lax.broadcasted_iota(jnp.int32, (ch, ch), 0)
    cc = lax.broadcasted_iota(jnp.int32, (ch, ch), 1)
    causal = cc <= cr
    tri = jnp.where(causal, 1.0, 0.0).astype(F32)
    tri_t = jnp.where(cr <= cc, 1.0, 0.0).astype(F32)

    for c in range(blk // ch):
        r0 = c * ch
        dtc = dt[r0:r0 + ch]
        ac = a_all[r0:r0 + ch]
        acs = jnp.dot(tri, ac, precision=HIGHEST, preferred_element_type=F32)
        acs_t = lax.dot_general(ac, tri_t, (((0,), (0,)), ((), ())), precision=HIGHEST,
                                preferred_element_type=F32)
        for g in range(SSM_GROUPS):
            bm = act_ref[r0:r0 + ch, SSM_INNER + g * ns:SSM_INNER + (g + 1) * ns]
            cm = act_ref[r0:r0 + ch, SSM_INNER + SSM_GROUPS * ns + g * ns:
                         SSM_INNER + SSM_GROUPS * ns + (g + 1) * ns]
            cb = _dot_nt(cm, bm)
            for r in range(rep):
                h = g * rep + r
                xh = act_ref[r0:r0 + ch, h * hp:(h + 1) * hp]
                dth = dtc[:, h:h + 1]
                acs_h = acs[:, h:h + 1]
                lmat = jnp.where(causal, jnp.exp(acs_h - acs_t[h:h + 1, :]), 0.0)
                acs_last = acs[ch - 1:ch, h:h + 1]
                st = st_ref[h]
                y = _dot(cb * lmat, dth * xh)
                y = y + jnp.exp(acs_h) * _dot(cm, st)
                y = y + dskip_ref[h] * xh
                bw = bm * (jnp.exp(acs_last - acs_h) * dth)
                st_ref[h] = jnp.exp(acs_last) * st + _dot_tn(bw, xh)
                y_ref[r0:r0 + ch, h * hp:(h + 1) * hp] = y
    yz = y_ref[...] * _silu(z_ref[0])
    o_ref[0] = _rms(yz, ng_ref[...])


def mamba2(sproj, conv_w, conv_b, dt_bias, a_log, d_skip, norm_g, name):
    bsz, seq, _ = sproj.shape
    blk = SSM_BLOCK
    padh = LANES - SSM_HEADS
    dtb = jnp.pad(dt_bias, (0, padh)).reshape(1, LANES)
    alog = jnp.pad(a_log, (0, padh)).reshape(1, LANES)
    return pl.pallas_call(
        _ssd_body,
        grid=(bsz, seq // blk),
        in_specs=[pl.BlockSpec((1, blk, SSM_CONV_DIM), lambda b, n: (b, n, 0)),
                  pl.BlockSpec((1, blk, LANES), lambda b, n: (b, n, 6)),
                  pl.BlockSpec((1, blk, SSM_INNER), lambda b, n: (b, n, 2)),
                  pl.BlockSpec((SSM_CONV, SSM_CONV_DIM), lambda b, n: (0, 0)),
                  pl.BlockSpec((1, SSM_CONV_DIM), lambda b, n: (0, 0)),
                  pl.BlockSpec((1, LANES), lambda b, n: (0, 0)),
                  pl.BlockSpec((1, LANES), lambda b, n: (0, 0)),
                  pl.BlockSpec(memory_space=pltpu.SMEM),
                  pl.BlockSpec((1, SSM_INNER), lambda b, n: (0, 0))],
        out_specs=pl.BlockSpec((1, blk, SSM_INNER), lambda b, n: (b, n, 0)),
        out_shape=jax.ShapeDtypeStruct((bsz, seq, SSM_INNER), F32),
        scratch_shapes=[pltpu.VMEM((blk + 8, SSM_CONV_DIM), F32),
                        pltpu.VMEM((blk, SSM_CONV_DIM), F32),
                        pltpu.VMEM((SSM_HEADS, SSM_STATE, SSM_HEAD_DIM), F32),
                        pltpu.VMEM((blk, SSM_INNER), F32)],
        compiler_params=_cparams("arbitrary", "arbitrary"),
        name=name,
    )(sproj, sproj, sproj, conv_w, conv_b.reshape(1, -1), dtb, alog, d_skip,
      norm_g.reshape(1, -1))


def _merge_body(ya_ref, yb_ref, yc1_ref, yc2_ref, yc3_ref, yd_ref, gate_ref, x_ref, wbr_ref,
                wout_ref, o_ref):
    ys = (ya_ref[...], yb_ref[...], yc1_ref[...] + yc2_ref[...] + yc3_ref[...], yd_ref[...])
    u = None
    for nbr in range(N_BRANCH):
        yp = _dot(ys[nbr], wbr_ref[nbr])
        t = _sigmoid(gate_ref[:, nbr * D_MODEL:(nbr + 1) * D_MODEL]) * yp
        u = t if u is None else u + t
    o_ref[...] = x_ref[...] + _dot(u, wout_ref[...])


def merge(ya, yb, yc1, yc2, yc3, yd, gate, x2d, w_branch, w_out, tm, name):
    m = x2d.shape[0]

    def rows(w):
        return pl.BlockSpec((tm, w), lambda i: (i, 0))

    return pl.pallas_call(
        _merge_body,
        grid=(m // tm,),
        in_specs=[rows(512)] * 6 + [rows(N_BRANCH * D_MODEL), rows(D_MODEL),
                                    pl.BlockSpec((N_BRANCH, BRANCH_WIDTH, D_MODEL), lambda i: (0, 0, 0)),
                                    pl.BlockSpec((D_MODEL, D_MODEL), lambda i: (0, 0))],
        out_specs=rows(D_MODEL),
        out_shape=jax.ShapeDtypeStruct((m, D_MODEL), F32),
        compiler_params=_cparams("arbitrary"),
        name=name,
    )(ya, yb, yc1, yc2, yc3, yd, gate, x2d, w_branch.astype(BF16), w_out.astype(BF16))


def _router_logits(hn, wr_ref):
    return jnp.dot(hn, wr_ref[...], precision=HIGHEST, preferred_element_type=F32)


def _group_lanes(lane):
    return (lane >= MOE_EXPERTS) & (lane < MOE_EXPERTS + MOE_GROUPS)


def _route_body(x_ref, g_ref, wr_ref, o_ref):
    tm = x_ref.shape[0]
    lane = lax.broadcasted_iota(jnp.int32, (tm, LANES), 1)
    logits = _router_logits(_rms(x_ref[...], g_ref[...]), wr_ref)
    lg = jnp.where(_group_lanes(lane), logits, -jnp.inf)
    mg = jnp.max(lg, axis=-1, keepdims=True)
    gi = jnp.min(jnp.where(lg == mg, lane.astype(F32), 1e9), axis=-1, keepdims=True) - MOE_EXPERTS
    o_ref[...] = jnp.broadcast_to(gi, (tm, LANES)).astype(jnp.int32)


def _gather_body(idx_ref, src_ref, o_ref, sem):
    rows = o_ref.shape[0]
    base = pl.program_id(0) * rows

    def issue(r, carry):
        pltpu.make_async_copy(src_ref.at[pl.ds(idx_ref[base + r], 1)], o_ref.at[pl.ds(r, 1)],
                              sem).start()
        return carry

    lax.fori_loop(0, rows, issue, 0, unroll=8)
    pltpu.make_async_copy(src_ref.at[pl.ds(0, rows)], o_ref, sem).wait()


def gather_rows(src, idx, rows, name):
    n = idx.shape[0]
    d = src.shape[1]
    grid_spec = pltpu.PrefetchScalarGridSpec(
        num_scalar_prefetch=1,
        grid=(n // rows,),
        in_specs=[pl.BlockSpec(memory_space=pl.ANY)],
        out_specs=pl.BlockSpec((rows, d), lambda i, idx_ref: (i, 0)),
        scratch_shapes=[pltpu.SemaphoreType.DMA(())],
    )
    return pl.pallas_call(
        _gather_body,
        grid_spec=grid_spec,
        out_shape=jax.ShapeDtypeStruct((n, d), src.dtype),
        compiler_params=_cparams("arbitrary"),
        name=name,
    )(idx, src)


def _moe_body(tg_ref, x_ref, g_ref, wr_ref, wgu_ref, wd_ref, o_ref, hn_ref, comb_ref):
    e = pl.program_id(1)
    grp = tg_ref[pl.program_id(0)]
    tm = x_ref.shape[0]
    lane = lax.broadcasted_iota(jnp.int32, (tm, LANES), 1)
    lanef = lane.astype(F32)

    @pl.when(e == 0)
    def _():
        x = x_ref[...]
        hn = _rms(x, g_ref[...])
        hn_ref[...] = hn.astype(BF16)
        o_ref[...] = x
        logits = _router_logits(hn, wr_ref)
        is_grp = _group_lanes(lane)
        lg = jnp.where(is_grp, logits, -jnp.inf)
        mg = jnp.max(lg, axis=-1, keepdims=True)
        sg = jnp.sum(jnp.where(is_grp, jnp.exp(lg - mg), 0.0), axis=-1, keepdims=True)
        lt = jnp.sum(jnp.where(lane == MOE_EXPERTS + grp, logits, 0.0), axis=-1, keepdims=True)
        g_w = jnp.exp(lt - mg) / sg
        in_grp = (lane < MOE_EXPERTS) & (lane // MOE_EPG == grp)
        le = jnp.where(in_grp, logits, -jnp.inf)
        m1 = jnp.max(le, axis=-1, keepdims=True)
        i1 = jnp.min(jnp.where(le == m1, lanef, 1e9), axis=-1, keepdims=True)
        le2 = jnp.where(lanef == i1, -jnp.inf, le)
        m2 = jnp.max(le2, axis=-1, keepdims=True)
        i2 = jnp.min(jnp.where(le2 == m2, lanef, 1e9), axis=-1, keepdims=True)
        e2 = jnp.exp(m2 - m1)
        den = 1.0 + e2
        comb_ref[...] = (jnp.where(lanef == i1, g_w / den, 0.0)
                         + jnp.where(lanef == i2, g_w * e2 / den, 0.0))

    a = jnp.dot(hn_ref[...], wgu_ref[0], preferred_element_type=F32)
    ce = jnp.sum(jnp.where(lane == grp * MOE_EPG + e, comb_ref[...], 0.0), axis=-1, keepdims=True)
    act = _silu(a[:, :MOE_FF]) * a[:, MOE_FF:] * ce
    o_ref[...] += _dot(act, wd_ref[0])


def _moe_plan(gid, tm):
    t = gid.shape[0]
    onehot = (gid[:, None] == jnp.arange(MOE_GROUPS, dtype=jnp.int32)).astype(jnp.int32)
    csum = jnp.cumsum(onehot, axis=0)
    counts = csum[-1]
    rank = (csum * onehot).sum(axis=1) - 1
    padded = (counts + tm - 1) // tm * tm
    pend = jnp.cumsum(padded)
    dest = (pend - padded)[gid] + rank
    r_pad = t + MOE_GROUPS * tm
    row_token = jnp.zeros((r_pad,), jnp.int32).at[dest].set(jnp.arange(t, dtype=jnp.int32))
    tile_start = jnp.arange(r_pad // tm, dtype=jnp.int32) * tm
    tile_group = jnp.minimum((tile_start[:, None] >= pend[None, :]).sum(axis=1), MOE_GROUPS - 1)
    return row_token, dest.astype(jnp.int32), tile_group.astype(jnp.int32)


def moe(x2d, g_ffn, w_grp, w_exp, w_gate, w_up, w_down, tm, name):
    m = x2d.shape[0]
    gf = g_ffn.reshape(1, -1)
    wr = jnp.concatenate([w_exp, w_grp,
                          jnp.zeros((D_MODEL, LANES - MOE_EXPERTS - MOE_GROUPS), F32)], axis=1)
    wgu = jnp.concatenate([w_gate, w_up], axis=2).astype(BF16)
    tr = min(1024, m)
    gid = pl.pallas_call(
        _route_body,
        grid=(m // tr,),
        in_specs=[pl.BlockSpec((tr, D_MODEL), lambda i: (i, 0)),
                  pl.BlockSpec((1, D_MODEL), lambda i: (0, 0)),
                  pl.BlockSpec((D_MODEL, LANES), lambda i: (0, 0))],
        out_specs=pl.BlockSpec((tr, LANES), lambda i: (i, 0)),
        out_shape=jax.ShapeDtypeStruct((m, LANES), jnp.int32),
        compiler_params=_cparams("arbitrary"),
        name=name + "_route",
    )(x2d, gf, wr)[:, 0]
    row_token, dest, tile_group = _moe_plan(gid, tm)
    xs = gather_rows(x2d, row_token, tm, name + "_gather")
    grid_spec = pltpu.PrefetchScalarGridSpec(
        num_scalar_prefetch=1,
        grid=(xs.shape[0] // tm, MOE_EPG),
        in_specs=[pl.BlockSpec((tm, D_MODEL), lambda i, e, tg: (i, 0)),
                  pl.BlockSpec((1, D_MODEL), lambda i, e, tg: (0, 0)),
                  pl.BlockSpec((D_MODEL, LANES), lambda i, e, tg: (0, 0)),
                  pl.BlockSpec((1, D_MODEL, 2 * MOE_FF), lambda i, e, tg: (tg[i] * MOE_EPG + e, 0, 0)),
                  pl.BlockSpec((1, MOE_FF, D_MODEL), lambda i, e, tg: (tg[i] * MOE_EPG + e, 0, 0))],
        out_specs=pl.BlockSpec((tm, D_MODEL), lambda i, e, tg: (i, 0)),
        scratch_shapes=[pltpu.VMEM((tm, D_MODEL), BF16), pltpu.VMEM((tm, LANES), F32)],
    )
    ys = pl.pallas_call(
        _moe_body,
        grid_spec=grid_spec,
        out_shape=jax.ShapeDtypeStruct(xs.shape, F32),
        compiler_params=_cparams("arbitrary", "arbitrary"),
        name=name,
    )(tile_group, xs, gf, wr, wgu, w_down.astype(BF16))
    return gather_rows(ys, dest, tm, name + "_scatter")


def _ple_body(x_ref, p_ref, g_ref, wg_ref, wp_ref, gf_ref, o_ref, *, final):
    x = x_ref[...]
    gate = _sigmoid(_dot(_rms(x, g_ref[...]), wg_ref[...]))
    xn = x + _dot(p_ref[...], wp_ref[...]) * gate
    if final:
        xn = _rms(xn, gf_ref[...])
    o_ref[...] = xn


def ple(x2d, p2d, g_ple, w_gate, w_proj, g_final, final, tm, name):
    m = x2d.shape[0]
    return pl.pallas_call(
        functools.partial(_ple_body, final=final),
        grid=(m // tm,),
        in_specs=[pl.BlockSpec((tm, D_MODEL), lambda i: (i, 0)),
                  pl.BlockSpec((tm, PLE_DIM), lambda i: (i, 0)),
                  pl.BlockSpec((1, D_MODEL), lambda i: (0, 0)),
                  pl.BlockSpec((D_MODEL, D_MODEL), lambda i: (0, 0)),
                  pl.BlockSpec((PLE_DIM, D_MODEL), lambda i: (0, 0)),
                  pl.BlockSpec((1, D_MODEL), lambda i: (0, 0))],
        out_specs=pl.BlockSpec((tm, D_MODEL), lambda i: (i, 0)),
        out_shape=jax.ShapeDtypeStruct((m, D_MODEL), F32),
        compiler_params=_cparams("arbitrary"),
        name=name,
    )(x2d, p2d, g_ple.reshape(1, -1), w_gate.astype(BF16), w_proj.astype(BF16),
      g_final.reshape(1, -1))


def _cols(w, *names):
    return [w[:, _OFF[n][0]:_OFF[n][0] + _OFF[n][1]] for n in names]


def _padcols(w, width):
    return jnp.pad(w, ((0, 0), (0, width - w.shape[1])))


def _pad_q_heads(wq):
    hd = ATT_HEAD_DIM
    zero = jnp.zeros((wq.shape[0], hd), wq.dtype)
    cols = []
    for h in range(ATT_HEADS):
        blk = wq[:, h * hd:(h + 1) * hd] * (hd ** -0.5)
        cols += [blk, zero] if h // ATT_REP == 0 else [zero, blk]
    return jnp.concatenate(cols, axis=1)


def _split_w_in(w):
    swa_q, nsa_q = _cols(w, 'swa_q', 'nsa_q')
    w_att = jnp.concatenate([_pad_q_heads(swa_q), _pad_q_heads(nsa_q)] + _cols(w, *KV_BLK), axis=1)
    w_hgrn = jnp.concatenate(_cols(w, 'hgrn_q', 'hgrn_f', 'hgrn_i', 'hgrn_g'), axis=1)
    (xbc, dt, ngate, z) = _cols(w, 'ssm_xbc', 'ssm_dt', 'nsa_gate', 'ssm_z')
    w_ssm = jnp.concatenate([xbc, _padcols(dt, LANES), _padcols(ngate, LANES), z], axis=1)
    (w_mg,) = _cols(w, 'merge_gate')
    return [a.astype(BF16) for a in (w_att, w_hgrn, w_ssm, w_mg)]


def _mixers(i, att, hproj, sproj, attn_sinks, hgrn_lower_bounds, hgrn_norm_g, nsa_pos_k, nsa_pos_v,
            nsa_cmp_w1_k, nsa_cmp_w2_k, nsa_cmp_w1_v, nsa_cmp_w2_v, ssm_conv_w, ssm_conv_b,
            ssm_dt_bias, ssm_A_log, ssm_D, ssm_norm_g):
    bsz, seq, _ = att.shape
    kv = KV_BLK
    gate_blk = 7
    y_a = banded_attention(att, 0, kv['swa_k'], kv['swa_v'], window=SWA_WINDOW, sinks=attn_sinks[i],
                           name=f"swa{i}")
    y_b = hgrn2(hproj, hgrn_lower_bounds, hgrn_norm_g[i], i, name=f"hgrn{i}")
    ncp = seq // NSA_CMP_STRIDE

    def cmp_in(name):
        c0 = kv[name] * LANES
        return att[:, :, c0:c0 + LANES].reshape(bsz, ncp, NSA_CMP_STRIDE * LANES)

    kc = nsa_compress(cmp_in('nsa_k_cmp'), nsa_pos_k[i], nsa_cmp_w1_k[i], nsa_cmp_w2_k[i], name=f"cmpk{i}")
    vc = nsa_compress(cmp_in('nsa_v_cmp'), nsa_pos_v[i], nsa_cmp_w1_v[i], nsa_cmp_w2_v[i], name=f"cmpv{i}")
    y_c1, mask, act = nsa_cmp_select(att, 1, kc, vc, sproj, gate_blk, name=f"nsacmp{i}")
    y_c2 = nsa_selected(att, 1, kv['nsa_k_slc'], kv['nsa_v_slc'], mask, act, sproj, gate_blk,
                        name=f"nsasel{i}")
    y_c3 = banded_attention(att, 1, kv['nsa_k_win'], kv['nsa_v_win'], window=NSA_WINDOW, gate=sproj,
                            gate_blk=gate_blk, gate_col=2, name=f"nsawin{i}")
    y_d = mamba2(sproj, ssm_conv_w[i], ssm_conv_b[i], ssm_dt_bias[i], ssm_A_log[i], ssm_D[i],
                 ssm_norm_g[i], name=f"ssd{i}")
    return y_a, y_b, y_c1, y_c2, y_c3, y_d


def kernel(x, p, w_in, g_mix, attn_sinks, hgrn_lower_bounds, hgrn_norm_g, nsa_pos_k, nsa_pos_v,
           nsa_cmp_w1_k, nsa_cmp_w2_k, nsa_cmp_w1_v, nsa_cmp_w2_v, ssm_conv_w, ssm_conv_b,
           ssm_dt_bias, ssm_A_log, ssm_D, ssm_norm_g, w_branch, w_out, g_ffn, w_router_grp,
           w_router_exp, w_exp_gate, w_exp_up, w_exp_down, g_ple, w_ple_gate, w_ple_proj, g_final):
    bsz, seq, d = x.shape
    depth = w_in.shape[0]
    t = bsz * seq
    x2 = x.reshape(t, d)
    tm_proj = min(1024, t)
    tm_row = min(256, t)
    tm_moe = min(512, t)
    for i in range(depth):
        w_att, w_hgrn, w_ssm, w_mg = _split_w_in(w_in[i])
        att = norm_mm(x2, g_mix[i], w_att, BF16, tm_proj, 512, f"proj_att{i}").reshape(bsz, seq, -1)
        hproj = norm_mm(x2, g_mix[i], w_hgrn, F32, tm_proj, 512, f"proj_hgrn{i}").reshape(bsz, seq, -1)
        sproj = norm_mm(x2, g_mix[i], w_ssm, F32, tm_proj, 512, f"proj_ssm{i}").reshape(bsz, seq, -1)
        mgate = norm_mm(x2, g_mix[i], w_mg, F32, tm_proj, 512, f"proj_gate{i}")
        ys = _mixers(i, att, hproj, sproj, attn_sinks, hgrn_lower_bounds, hgrn_norm_g, nsa_pos_k,
                     nsa_pos_v, nsa_cmp_w1_k, nsa_cmp_w2_k, nsa_cmp_w1_v, nsa_cmp_w2_v, ssm_conv_w,
                     ssm_conv_b, ssm_dt_bias, ssm_A_log, ssm_D, ssm_norm_g)
        ys = [y.reshape(t, -1) for y in ys]
        x2 = merge(*ys, mgate, x2, w_branch[i], w_out[i], tm_row, f"merge{i}")
        x2 = moe(x2, g_ffn[i], w_router_grp[i], w_router_exp[i], w_exp_gate[i], w_exp_up[i],
                 w_exp_down[i], tm_moe, f"moe{i}")
        x2 = ple(x2, p[i].reshape(t, -1), g_ple[i], w_ple_gate[i], w_ple_proj[i], g_final,
                 i == depth - 1, tm_row, f"ple{i}")
    return x2.reshape(bsz, seq, d)
```

```python
import functools

import numpy as np
import jax
import jax.numpy as jnp
from jax import lax
from jax.experimental import pallas as pl
from jax.experimental.pallas import tpu as pltpu

F32 = jnp.float32
BF16 = jnp.bfloat16
HIGHEST = lax.Precision.HIGHEST

D_MODEL = 1024
PLE_DIM = 256
NORM_EPS = 1e-6
NEG_INF = -1e30
REMOVED = -3e38
N_BRANCH = 4
BRANCH_WIDTH = 512
ATTN_BLOCK = 128

ATT_HEADS = 8
ATT_KV_HEADS = 2
ATT_HEAD_DIM = 64
ATT_REP = ATT_HEADS // ATT_KV_HEADS
SWA_WINDOW = 128
NSA_WINDOW = 512
NSA_CMP_LEN = 32
NSA_CMP_STRIDE = 16
NSA_CMP_HIDDEN = 256
NSA_SEL_LEN = 64
NSA_TOP_N = 16
NSA_FORCE_SCORE = 1e6
SEL_KEY_TILE = 512
SEL_FEATS = SEL_KEY_TILE // NSA_SEL_LEN
MASK_BIG = 1e30

HGRN_HEADS = 4
HGRN_DIM = 128
HGRN_CHUNK = 32
HGRN_MIN_F = 1e-6
HGRN_BLOCK = 256

SSM_HEADS = 8
SSM_HEAD_DIM = 64
SSM_GROUPS = 2
SSM_STATE = 64
SSM_CONV = 4
SSM_CHUNK = 64
SSM_INNER = 512
SSM_CONV_DIM = 768
SSM_BLOCK = 256

MOE_GROUPS = 4
MOE_EPG = 8
MOE_EXPERTS = 32
MOE_FF = 256
MOE_GATHER_ROWS = 512

LANES = 128
VMEM_LIMIT = 56 * 1024 * 1024

ALIBI_SLOPES = tuple(2.0 ** (-8.0 * (h + 1) / ATT_HEADS) for h in range(ATT_HEADS))

ATT_Q_WIDTH = ATT_HEADS * LANES
KV_BLK = {name: 2 * ATT_HEADS + j for j, name in enumerate(
    ('swa_k', 'swa_v', 'nsa_k_cmp', 'nsa_v_cmp', 'nsa_k_slc', 'nsa_v_slc', 'nsa_k_win', 'nsa_v_win'))}

_SPLITS = (
    ('swa_q', 512), ('swa_k', 128), ('swa_v', 128),
    ('hgrn_q', 512), ('hgrn_f', 512), ('hgrn_i', 512), ('hgrn_g', 512),
    ('nsa_q', 512), ('nsa_k_cmp', 128), ('nsa_v_cmp', 128), ('nsa_k_slc', 128),
    ('nsa_v_slc', 128), ('nsa_k_win', 128), ('nsa_v_win', 128), ('nsa_gate', 24),
    ('ssm_z', 512), ('ssm_xbc', 768), ('ssm_dt', 8), ('merge_gate', 4096),
)
_OFF = {}
_o = 0
for _n, _w in _SPLITS:
    _OFF[_n] = (_o, _w)
    _o += _w


def _cparams(*sem):
    return pltpu.CompilerParams(dimension_semantics=sem, vmem_limit_bytes=VMEM_LIMIT)


def _sigmoid(x):
    return 1.0 / (1.0 + jnp.exp(-x))


def _silu(x):
    return x * _sigmoid(x)


def _dot(a, b):
    return jnp.dot(a.astype(BF16), b.astype(BF16), preferred_element_type=F32)


def _dot_nt(a, b):
    return lax.dot_general(a.astype(BF16), b.astype(BF16), (((1,), (1,)), ((), ())),
                           preferred_element_type=F32)


def _dot_tn(a, b):
    return lax.dot_general(a.astype(BF16), b.astype(BF16), (((0,), (0,)), ((), ())),
                           preferred_element_type=F32)


def _rms(x, g):
    ms = jnp.mean(x * x, axis=-1, keepdims=True)
    return x * lax.rsqrt(ms + NORM_EPS) * g


def _norm_mm_body(x_ref, g_ref, w_ref, o_ref, hn_ref):
    @pl.when(pl.program_id(1) == 0)
    def _():
        hn_ref[...] = _rms(x_ref[...], g_ref[...]).astype(BF16)

    o_ref[...] = jnp.dot(hn_ref[...], w_ref[...], preferred_element_type=F32).astype(o_ref.dtype)


def norm_mm(x2d, g, w, out_dtype, tm, tn, name):
    m, k = x2d.shape
    n = w.shape[1]
    return pl.pallas_call(
        _norm_mm_body,
        grid=(m // tm, n // tn),
        in_specs=[pl.BlockSpec((tm, k), lambda i, j: (i, 0)),
                  pl.BlockSpec((1, k), lambda i, j: (0, 0)),
                  pl.BlockSpec((k, tn), lambda i, j: (0, j))],
        out_specs=pl.BlockSpec((tm, tn), lambda i, j: (i, j)),
        out_shape=jax.ShapeDtypeStruct((m, n), out_dtype),
        scratch_shapes=[pltpu.VMEM((tm, k), BF16)],
        compiler_params=_cparams("arbitrary", "arbitrary"),
        name=name,
    )(x2d, g.reshape(1, k), w)


def _ones_row(g):
    return ATT_HEAD_DIM if g == 0 else 0


def _vt_body(v_ref, o_ref):
    vt = jnp.transpose(v_ref[0].astype(F32))
    rowid = lax.broadcasted_iota(jnp.int32, vt.shape, 0)
    for g in range(ATT_KV_HEADS):
        aug = jnp.where(rowid // ATT_HEAD_DIM == g, vt, jnp.where(rowid == _ones_row(g), 1.0, 0.0))
        o_ref[0, g] = aug.astype(BF16)


def v_transposed(arr, v_blk, name):
    bsz, seq, _ = arr.shape
    rows = min(2048, seq)
    return pl.pallas_call(
        _vt_body,
        grid=(bsz, seq // rows),
        in_specs=[pl.BlockSpec((1, rows, LANES), lambda b, n: (b, n, v_blk))],
        out_specs=pl.BlockSpec((1, ATT_KV_HEADS, LANES, rows), lambda b, n: (b, 0, 0, n)),
        out_shape=jax.ShapeDtypeStruct((bsz, ATT_KV_HEADS, LANES, seq), BF16),
        compiler_params=_cparams("arbitrary", "arbitrary"),
        name=name,
    )(arr)


def _q_stack(q_ref, g):
    return jnp.concatenate([q_ref[0, :, (g * ATT_REP + r) * LANES:(g * ATT_REP + r + 1) * LANES]
                            for r in range(ATT_REP)], axis=0)


def _head(x, r):
    return x[:, r * ATTN_BLOCK:(r + 1) * ATTN_BLOCK]


def _banded_body(*refs, window, nprev, use_sink, gate_col):
    if use_sink:
        q_ref, k_ref, vt_ref, sink_ref, o_ref, ot_ref = refs
    else:
        q_ref, k_ref, vt_ref, gate_ref, o_ref, ot_ref = refs
    blk = ATTN_BLOCK
    hd = ATT_HEAD_DIM
    n = pl.program_id(1)
    nk = (nprev + 1) * blk
    start = pl.multiple_of(jnp.maximum(n - nprev, 0) * blk, blk)
    k128 = k_ref[0, pl.ds(start, nk), :]
    krow = lax.broadcasted_iota(jnp.int32, (nk, blk), 0)
    qcol = lax.broadcasted_iota(jnp.int32, (nk, blk), 1)
    rel = (n * blk + qcol) - (start + krow)
    negrel = jnp.where((rel >= 0) & (rel < window), -rel.astype(F32), NEG_INF)
    if not use_sink:
        gate_t = jnp.transpose(_sigmoid(gate_ref[0]))
    st4s = [_dot_nt(k128, _q_stack(q_ref, g)) for g in range(ATT_KV_HEADS)]
    for g in range(ATT_KV_HEADS):
        one = _ones_row(g)
        st4 = st4s[g]
        ps, ms = [], []
        for r in range(ATT_REP):
            h = g * ATT_REP + r
            st = _head(st4, r) + ALIBI_SLOPES[h] * negrel
            m = jnp.max(st, axis=0, keepdims=True)
            if use_sink:
                m = jnp.maximum(m, sink_ref[h])
            ps.append(jnp.exp(st - m).astype(BF16))
            ms.append(m)
        acc4 = jnp.dot(vt_ref[0, g, :, pl.ds(start, nk)], jnp.concatenate(ps, axis=1),
                       preferred_element_type=F32)
        for r in range(ATT_REP):
            h = g * ATT_REP + r
            acc = _head(acc4, r)
            l = acc[one:one + 1]
            if use_sink:
                l = l + jnp.exp(sink_ref[h] - ms[r])
            ot = acc[g * hd:(g + 1) * hd] / l
            if not use_sink:
                c = 3 * h + gate_col
                ot = ot * gate_t[c:c + 1]
            ot_ref[h * hd:(h + 1) * hd, :] = ot
    o_ref[0] = jnp.transpose(ot_ref[...])


def banded_attention(att, q_blk, k_blk, v_blk, *, window, sinks=None, gate=None, gate_blk=None,
                     gate_col=0, name):
    bsz, seq, _ = att.shape
    blk = ATTN_BLOCK
    nprev = (window - 1 + blk - 1) // blk
    use_sink = sinks is not None
    vt = v_transposed(att, v_blk, name + "_vt")
    in_specs = [pl.BlockSpec((1, blk, ATT_Q_WIDTH), lambda b, n: (b, n, q_blk)),
                pl.BlockSpec((1, seq, LANES), lambda b, n: (b, 0, k_blk)),
                pl.BlockSpec((1, ATT_KV_HEADS, LANES, seq), lambda b, n: (b, 0, 0, 0))]
    if use_sink:
        in_specs.append(pl.BlockSpec(memory_space=pltpu.SMEM))
        extra = sinks
    else:
        in_specs.append(pl.BlockSpec((1, blk, LANES), lambda b, n: (b, n, gate_blk)))
        extra = gate
    return pl.pallas_call(
        functools.partial(_banded_body, window=window, nprev=nprev, use_sink=use_sink,
                          gate_col=gate_col),
        grid=(bsz, seq // blk),
        in_specs=in_specs,
        out_specs=pl.BlockSpec((1, blk, 512), lambda b, n: (b, n, 0)),
        out_shape=jax.ShapeDtypeStruct((bsz, seq, 512), F32),
        scratch_shapes=[pltpu.VMEM((512, blk), F32)],
        compiler_params=_cparams("arbitrary", "arbitrary"),
        name=name,
    )(att, att, vt, extra)


def _compress_body(x_ref, w1a_ref, w1b_ref, w1_ref, pos_ref, w2_ref, o_ref):
    x = x_ref[0]
    p = jnp.dot(x, w1a_ref[...], preferred_element_type=F32)
    q = jnp.dot(x, w1b_ref[...], preferred_element_type=F32)
    ncp = x.shape[0]
    q = pltpu.roll(q, shift=ncp - 1, axis=0)
    posb = jnp.broadcast_to(pos_ref[...], (8, pos_ref.shape[1]))
    bias = _dot(posb, w1_ref[...])[0:1]
    hid = NSA_CMP_HIDDEN
    outs = []
    for g in range(ATT_KV_HEADS):
        pre = p[:, g * hid:(g + 1) * hid] + q[:, g * hid:(g + 1) * hid] + bias
        outs.append(_dot(jax.nn.gelu(pre), w2_ref[...]))
    o_ref[0] = jnp.concatenate(outs, axis=1).astype(o_ref.dtype)


def _expand_w1(w1, half):
    hd, hid, ng = ATT_HEAD_DIM, NSA_CMP_HIDDEN, ATT_KV_HEADS
    w = w1.reshape(NSA_CMP_LEN, hd, hid)[half * 16:(half + 1) * 16]
    eye = jnp.eye(ng, dtype=w1.dtype)
    out = jnp.einsum('ldj,gh->lgdhj', w, eye)
    return out.reshape(16 * ng * hd, ng * hid)


def nsa_compress(x16, pos, w1, w2, name):
    bsz, ncp, wid = x16.shape
    hid = NSA_CMP_HIDDEN
    w1a = _expand_w1(w1, 0).astype(BF16)
    w1b = _expand_w1(w1, 1).astype(BF16)
    return pl.pallas_call(
        _compress_body,
        grid=(bsz,),
        in_specs=[pl.BlockSpec((1, ncp, wid), lambda b: (b, 0, 0)),
                  pl.BlockSpec((wid, 2 * hid), lambda b: (0, 0)),
                  pl.BlockSpec((wid, 2 * hid), lambda b: (0, 0)),
                  pl.BlockSpec((NSA_CMP_LEN * ATT_HEAD_DIM, hid), lambda b: (0, 0)),
                  pl.BlockSpec((1, NSA_CMP_LEN * ATT_HEAD_DIM), lambda b: (0, 0)),
                  pl.BlockSpec((hid, ATT_HEAD_DIM), lambda b: (0, 0))],
        out_specs=pl.BlockSpec((1, ncp, LANES), lambda b: (b, 0, 0)),
        out_shape=jax.ShapeDtypeStruct((bsz, ncp, LANES), BF16),
        compiler_params=_cparams("arbitrary"),
        name=name,
    )(x16, w1a, w1b, w1.astype(BF16), pos.reshape(1, -1), w2.astype(BF16))


def _nsa_cmp_body(q_ref, kc_ref, vct_ref, ovt_ref, gate_ref, o_ref, m_ref, act_ref, ot_ref, *, top_n):
    blk = ATTN_BLOCK
    hd = ATT_HEAD_DIM
    n = pl.program_id(1)
    kc = kc_ref[0]
    ncp = kc.shape[0]
    ns = ovt_ref.shape[0]
    crow = lax.broadcasted_iota(jnp.int32, (ncp, blk), 0)
    qcol = lax.broadcasted_iota(jnp.int32, (ncp, blk), 1)
    dist = (n * blk + qcol) - (crow * NSA_CMP_STRIDE + (NSA_CMP_LEN - 1))
    valid = dist >= 0
    negd = jnp.where(valid, -dist.astype(F32), NEG_INF)
    jrow = lax.broadcasted_iota(jnp.int32, (ns, blk), 0)
    qpos = n * blk + lax.broadcasted_iota(jnp.int32, (ns, blk), 1)
    cur = qpos // NSA_SEL_LEN
    forced = (jrow == 0) | (jrow == cur) | (jrow == cur - 1)
    causal_blk = jrow * NSA_SEL_LEN <= qpos
    jrowf = jrow.astype(F32)
    gate_t = jnp.transpose(_sigmoid(gate_ref[0]))
    ones8 = jnp.ones((8, blk), BF16)
    st4s = [_dot_nt(kc, _q_stack(q_ref, g)) for g in range(ATT_KV_HEADS)]
    for g in range(ATT_KV_HEADS):
        one = _ones_row(g)
        st4 = st4s[g]
        es = []
        for r in range(ATT_REP):
            st = _head(st4, r) + ALIBI_SLOPES[g * ATT_REP + r] * negd
            m = jnp.max(st, axis=0, keepdims=True)
            es.append(jnp.where(valid, jnp.exp(st - m), 0.0))
        acc4 = jnp.dot(vct_ref[0, g], jnp.concatenate([e.astype(BF16) for e in es], axis=1),
                       preferred_element_type=F32)
        psum = jnp.zeros((ncp, blk), F32)
        for r in range(ATT_REP):
            h = g * ATT_REP + r
            acc = _head(acc4, r)
            l = acc[one:one + 1]
            inv = jnp.where(l > 0.0, 1.0 / jnp.where(l > 0.0, l, 1.0), 0.0)
            ot_ref[h * hd:(h + 1) * hd, :] = acc[g * hd:(g + 1) * hd] * inv * gate_t[3 * h:3 * h + 1]
            psum = psum + es[r] * inv
        imp = _dot(ovt_ref[...], psum)
        score = jnp.where(forced, NSA_FORCE_SCORE, jnp.where(causal_blk, imp, NEG_INF))
        sel = jnp.zeros((ns, blk), F32)
        for _ in range(top_n):
            mx = jnp.max(score, axis=0, keepdims=True)
            idx = jnp.min(jnp.where(score == mx, jrowf, float(ns)), axis=0, keepdims=True)
            hit = jrowf == idx
            sel = jnp.where(hit, 1.0, sel)
            score = jnp.where(hit, REMOVED, score)
        m_ref[0, g] = sel
        act_ref[0, 0, g * 8:(g + 1) * 8, :] = _dot_nt(ones8, sel)
    o_ref[0] = jnp.transpose(ot_ref[...])


def _overlap_matrix_t(seq):
    n_cmp = seq // NSA_CMP_STRIDE
    n_sel = seq // NSA_SEL_LEN
    cs = np.arange(n_cmp)[None, :] * NSA_CMP_STRIDE
    ss = np.arange(n_sel)[:, None] * NSA_SEL_LEN
    ov = np.clip(np.minimum(cs + NSA_CMP_LEN, ss + NSA_SEL_LEN) - np.maximum(cs, ss), 0, None)
    return jnp.asarray(ov / NSA_CMP_LEN, dtype=BF16)


def nsa_cmp_select(att, q_blk, kc, vc, gate, gate_blk, name):
    bsz, seq, _ = att.shape
    blk = ATTN_BLOCK
    nb = seq // blk
    ncp = seq // NSA_CMP_STRIDE
    ns = seq // NSA_SEL_LEN
    top_n = min(NSA_TOP_N, ns)
    vct = v_transposed(vc, 0, name + "_vt")
    return pl.pallas_call(
        functools.partial(_nsa_cmp_body, top_n=top_n),
        grid=(bsz, nb),
        in_specs=[pl.BlockSpec((1, blk, ATT_Q_WIDTH), lambda b, n: (b, n, q_blk)),
                  pl.BlockSpec((1, ncp, LANES), lambda b, n: (b, 0, 0)),
                  pl.BlockSpec((1, ATT_KV_HEADS, LANES, ncp), lambda b, n: (b, 0, 0, 0)),
                  pl.BlockSpec((ns, ncp), lambda b, n: (0, 0)),
                  pl.BlockSpec((1, blk, LANES), lambda b, n: (b, n, gate_blk))],
        out_specs=[pl.BlockSpec((1, blk, 512), lambda b, n: (b, n, 0)),
                   pl.BlockSpec((1, ATT_KV_HEADS, ns, blk), lambda b, n: (b, 0, 0, n)),
                   pl.BlockSpec((1, 1, ATT_KV_HEADS * 8, ns), lambda b, n: (b, n, 0, 0))],
        out_shape=[jax.ShapeDtypeStruct((bsz, seq, 512), F32),
                   jax.ShapeDtypeStruct((bsz, ATT_KV_HEADS, ns, seq), F32),
                   jax.ShapeDtypeStruct((bsz, nb, ATT_KV_HEADS * 8, ns), F32)],
        scratch_shapes=[pltpu.VMEM((512, blk), F32)],
        compiler_params=_cparams("arbitrary", "arbitrary"),
        name=name,
    )(att, kc, vct, _overlap_matrix_t(seq), gate)


def _k_aug_body(k_ref, o_ref):
    k = k_ref[0].astype(F32)
    rows = k.shape[0]
    kin = lax.broadcasted_iota(jnp.int32, (rows, LANES), 0) % SEL_KEY_TILE
    lane = lax.broadcasted_iota(jnp.int32, (rows, LANES), 1)
    for g in range(ATT_KV_HEADS):
        f = lane - (ATT_HEAD_DIM if g == 0 else 0)
        feat = jnp.where(f == kin // NSA_SEL_LEN, 1.0, 0.0)
        feat = jnp.where(f == SEL_FEATS, (kin % 256).astype(F32), feat)
        feat = jnp.where(f == SEL_FEATS + 1, (kin // 256 * 256).astype(F32), feat)
        feat = jnp.where((f == SEL_FEATS + 2) | (f == SEL_FEATS + 3), 1.0, feat)
        o_ref[0, g] = jnp.where(lane // ATT_HEAD_DIM == g, k, feat).astype(BF16)


def k_augmented(arr, k_blk, name):
    bsz, seq, _ = arr.shape
    rows = min(2048, seq)
    return pl.pallas_call(
        _k_aug_body,
        grid=(bsz, seq // rows),
        in_specs=[pl.BlockSpec((1, rows, LANES), lambda b, n: (b, n, k_blk))],
        out_specs=pl.BlockSpec((1, ATT_KV_HEADS, rows, LANES), lambda b, n: (b, 0, n, 0)),
        out_shape=jax.ShapeDtypeStruct((bsz, ATT_KV_HEADS, seq, LANES), BF16),
        compiler_params=_cparams("arbitrary", "arbitrary"),
        name=name,
    )(arr)


def _query_feature_rows():
    out = np.zeros((ATT_KV_HEADS, 2 * SEL_FEATS, ATT_REP * ATTN_BLOCK), np.float32)
    qin = np.arange(ATTN_BLOCK, dtype=np.float32)
    for g in range(ATT_KV_HEADS):
        for r in range(ATT_REP):
            slope = ALIBI_SLOPES[g * ATT_REP + r]
            cols = slice(r * ATTN_BLOCK, (r + 1) * ATTN_BLOCK)
            out[g, 0, cols] = slope
            out[g, 1, cols] = slope
            out[g, 3, cols] = -slope * qin
            out[g, SEL_FEATS + 2, cols] = -slope
    return jnp.asarray(out)


def _nsa_sel_body(tiles_ref, cnt_ref, q_ref, k_ref, vt_ref, m_ref, gate_ref, fq_ref, o_ref, ot_ref,
                  acc_ref, mx_ref, dq_ref, *, ntl):
    blk = ATTN_BLOCK
    hd = ATT_HEAD_DIM
    tk = SEL_KEY_TILE
    sl = NSA_SEL_LEN
    per = tk // sl
    b = pl.program_id(0)
    n = pl.program_id(1)
    nb = pl.num_programs(1)
    krow = lax.broadcasted_iota(jnp.int32, (tk, blk), 0)
    qcol = lax.broadcasted_iota(jnp.int32, (tk, blk), 1)
    dq_ref[...] = (qcol - krow).astype(F32)
    gate_t = jnp.transpose(_sigmoid(gate_ref[0]))
    pad_rows = jnp.zeros((hd - 2 * SEL_FEATS, ATT_REP * blk), BF16)
    for g in range(ATT_KV_HEADS):
        one = _ones_row(g)
        lrow = (b * nb + n) * ATT_KV_HEADS + g
        qt = jnp.concatenate(
            [jnp.transpose(q_ref[0, :, (g * ATT_REP + r) * LANES:(g * ATT_REP + r + 1) * LANES]
                           .astype(F32))[g * hd:(g + 1) * hd] for r in range(ATT_REP)],
            axis=1).astype(BF16)
        acc_ref[...] = jnp.zeros_like(acc_ref)
        mx_ref[...] = jnp.full_like(mx_ref, NEG_INF)

        def tile_scores(j, live):
            t = tiles_ref[lrow * ntl + j]
            base = pl.multiple_of(t * tk, tk)
            sel8 = m_ref[0, g, pl.ds(pl.multiple_of(t * per, per), per), :]
            if live is not None:
                sel8 = jnp.where(live, sel8, 0.0)
            off = (n * blk - base).astype(F32)
            mask_rows = jnp.concatenate([(sel8 - 1.0) * MASK_BIG] * ATT_REP, axis=1)
            alibi_rows = fq_ref[g, 0:SEL_FEATS, :] + off * fq_ref[g, SEL_FEATS:2 * SEL_FEATS, :]
            feats = jnp.concatenate([mask_rows, alibi_rows], axis=0).astype(BF16)
            rhs = jnp.concatenate([qt, feats, pad_rows] if g == 0 else [feats, pad_rows, qt], axis=0)
            st4 = jnp.dot(k_ref[0, g, pl.ds(base, tk), :], rhs,
                          preferred_element_type=F32)
            return st4, base, off

        def tile_finish(st4, base, off, diag):
            if diag:
                causal = (dq_ref[...] + off) >= 0.0
            ps, ms = [], []
            for r in range(ATT_REP):
                st = _head(st4, r)
                if diag:
                    st = jnp.where(causal, st, NEG_INF)
                m = jnp.max(st, axis=0, keepdims=True)
                ps.append(jnp.exp(st - m).astype(BF16))
                ms.append(m)
            acc = jnp.dot(vt_ref[0, g, :, pl.ds(base, tk)], jnp.concatenate(ps, axis=1),
                          preferred_element_type=F32)
            return jnp.concatenate(ms, axis=1), acc

        def pair(j2, diag):
            cnt = cnt_ref[lrow]
            sc_a = tile_scores(2 * j2, None)
            sc_b = tile_scores(jnp.minimum(2 * j2 + 1, ntl - 1), 2 * j2 + 1 < cnt)
            m_a, acc_a = tile_finish(*sc_a, diag)
            m_b, acc_b = tile_finish(*sc_b, False)
            m_old = mx_ref[...]
            m_new = jnp.maximum(m_old, jnp.maximum(m_a, m_b))
            acc_ref[...] = (jnp.exp(m_old - m_new) * acc_ref[...] + jnp.exp(m_a - m_new) * acc_a
                            + jnp.exp(m_b - m_new) * acc_b)
            mx_ref[...] = m_new

        pair(0, True)

        def later(j2, carry):
            pair(j2, False)
            return carry

        lax.fori_loop(1, (cnt_ref[lrow] + 1) // 2, later, 0)
        acc4 = acc_ref[...]
        for r in range(ATT_REP):
            h = g * ATT_REP + r
            acc = _head(acc4, r)
            ot = acc[g * hd:(g + 1) * hd] / acc[one:one + 1]
            ot_ref[h * hd:(h + 1) * hd, :] = ot * gate_t[3 * h + 1:3 * h + 2]
    o_ref[0] = jnp.transpose(ot_ref[...])


def _tile_lists(act, seq):
    bsz, nb = act.shape[:2]
    ns = act.shape[-1]
    tk = SEL_KEY_TILE
    ntl = seq // tk
    per = tk // NSA_SEL_LEN
    cnt_blk = act.reshape(bsz, nb, ATT_KV_HEADS, 8, ns)[:, :, :, 0, :]
    hit = cnt_blk.reshape(bsz, nb, ATT_KV_HEADS, ntl, per).sum(-1) > 0.5
    tidx = jnp.arange(ntl, dtype=jnp.int32)
    diag = (jnp.arange(nb, dtype=jnp.int32) * ATTN_BLOCK + ATTN_BLOCK - 1) // tk
    active = hit & (tidx[None, None, None, :] <= diag[None, :, None, None])
    act_i = active.astype(jnp.int32)
    rank = jnp.cumsum(act_i[..., ::-1], axis=-1)[..., ::-1] - 1
    slot = (active[..., :, None] & (rank[..., :, None] == tidx)).astype(jnp.int32)
    tiles = (slot * tidx[:, None]).sum(-2).reshape(-1)
    cnt = act_i.sum(-1).reshape(-1)
    return tiles, cnt, ntl


def nsa_selected(att, q_blk, k_blk, v_blk, mask, act, gate, gate_blk, name):
    bsz, seq, _ = att.shape
    blk = ATTN_BLOCK
    ns = seq // NSA_SEL_LEN
    tiles, cnt, ntl = _tile_lists(act, seq)
    vt = v_transposed(att, v_blk, name + "_vt")
    kaug = k_augmented(att, k_blk, name + "_k")
    grid_spec = pltpu.PrefetchScalarGridSpec(
        num_scalar_prefetch=2,
        grid=(bsz, seq // blk),
        in_specs=[pl.BlockSpec((1, blk, ATT_Q_WIDTH), lambda b, n, *_: (b, n, q_blk)),
                  pl.BlockSpec((1, ATT_KV_HEADS, seq, LANES), lambda b, n, *_: (b, 0, 0, 0)),
                  pl.BlockSpec((1, ATT_KV_HEADS, LANES, seq), lambda b, n, *_: (b, 0, 0, 0)),
                  pl.BlockSpec((1, ATT_KV_HEADS, ns, blk), lambda b, n, *_: (b, 0, 0, n)),
                  pl.BlockSpec((1, blk, LANES), lambda b, n, *_: (b, n, gate_blk)),
                  pl.BlockSpec((ATT_KV_HEADS, 2 * SEL_FEATS, ATT_REP * blk), lambda b, n, *_: (0, 0, 0))],
        out_specs=pl.BlockSpec((1, blk, 512), lambda b, n, *_: (b, n, 0)),
        scratch_shapes=[pltpu.VMEM((512, blk), F32),
                        pltpu.VMEM((LANES, ATT_REP * blk), F32),
                        pltpu.VMEM((1, ATT_REP * blk), F32),
                        pltpu.VMEM((SEL_KEY_TILE, blk), F32)],
    )
    return pl.pallas_call(
        functools.partial(_nsa_sel_body, ntl=ntl),
        grid_spec=grid_spec,
        out_shape=jax.ShapeDtypeStruct((bsz, seq, 512), F32),
        compiler_params=_cparams("arbitrary", "arbitrary"),
        name=name,
    )(tiles, cnt, att, kaug, vt, mask, gate, _query_feature_rows())


def _hgrn_body(q_ref, f_ref, i_ref, g_ref, lbp_ref, ng_ref, o_ref, st_ref, b_ref, k_ref, *, layer):
    blk = HGRN_BLOCK
    ch = HGRN_CHUNK
    dk = HGRN_DIM

    @pl.when(pl.program_id(1) == 0)
    def _():
        st_ref[...] = jnp.zeros_like(st_ref)

    lbp = lbp_ref[...]
    e = jnp.exp(lbp - jnp.max(lbp, axis=0, keepdims=True))
    sm = e / jnp.sum(e, axis=0, keepdims=True)
    lb = jnp.zeros((1, lbp.shape[1]), F32)
    for d in range(1, layer + 1):
        lb = lb + sm[d:d + 1]
    z = f_ref[0]
    f = lb + (1.0 - lb) * _sigmoid(z)
    logf = jnp.log(jnp.maximum(f, HGRN_MIN_F))
    k_ref[...] = (1.0 - lb) * _sigmoid(-z)
    tr = lax.broadcasted_iota(jnp.int32, (blk, blk), 0)
    tc = lax.broadcasted_iota(jnp.int32, (blk, blk), 1)
    tri = jnp.where((tr // ch == tc // ch) & (tc <= tr), 1.0, 0.0).astype(F32)
    b_ref[...] = jnp.dot(tri, logf, precision=HIGHEST, preferred_element_type=F32)
    cr = lax.broadcasted_iota(jnp.int32, (ch, ch), 0)
    cc = lax.broadcasted_iota(jnp.int32, (ch, ch), 1)
    causal = cc <= cr
    ng = ng_ref[...]

    for c in range(blk // ch):
        r0 = c * ch
        bc = b_ref[pl.ds(r0, ch), :]
        qc = q_ref[0, pl.ds(r0, ch), :]
        kc = k_ref[pl.ds(r0, ch), :]
        vc = i_ref[0, pl.ds(r0, ch), :]
        gc = g_ref[0, pl.ds(r0, ch), :]
        b_mid = bc[ch // 2:ch // 2 + 1]
        b_last = bc[ch - 1:ch]
        qa = qc * jnp.exp(bc - b_mid)
        ka = kc * jnp.exp(b_mid - bc)
        qe = qc * jnp.exp(bc)
        kl = kc * jnp.exp(b_last - bc)
        dec = jnp.exp(b_last)
        for h in range(HGRN_HEADS):
            sl = slice(h * dk, (h + 1) * dk)
            a = jnp.where(causal, _dot_nt(qa[:, sl], ka[:, sl]), 0.0)
            st = st_ref[h]
            o = _dot(a, vc[:, sl]) + _dot_nt(qe[:, sl], st)
            st_ref[h] = st * dec[:, sl] + _dot_tn(vc[:, sl], kl[:, sl])
            o = _rms(o, ng[:, sl]) * _silu(gc[:, sl])
            o_ref[0, pl.ds(r0, ch), sl] = o


def hgrn2(hproj, lower_bounds, norm_g, layer, name):
    bsz, seq, _ = hproj.shape
    blk = HGRN_BLOCK
    wid = HGRN_HEADS * HGRN_DIM
    depth = lower_bounds.shape[0]

    def col(j):
        return pl.BlockSpec((1, blk, wid), lambda b, n: (b, n, j))

    return pl.pallas_call(
        functools.partial(_hgrn_body, layer=layer),
        grid=(bsz, seq // blk),
        in_specs=[col(0), col(1), col(2), col(3),
                  pl.BlockSpec((depth, wid), lambda b, n: (0, 0)),
                  pl.BlockSpec((1, wid), lambda b, n: (0, 0))],
        out_specs=pl.BlockSpec((1, blk, wid), lambda b, n: (b, n, 0)),
        out_shape=jax.ShapeDtypeStruct((bsz, seq, wid), F32),
        scratch_shapes=[pltpu.VMEM((HGRN_HEADS, HGRN_DIM, HGRN_DIM), F32),
                        pltpu.VMEM((blk, wid), F32),
                        pltpu.VMEM((blk, wid), F32)],
        compiler_params=_cparams("arbitrary", "arbitrary"),
        name=name,
    )(hproj, hproj, hproj, hproj, lower_bounds, norm_g.reshape(1, wid))


def _ssd_body(xbc_ref, dt_ref, z_ref, cw_ref, cb_ref, dtb_ref, alog_ref, dskip_ref, ng_ref,
              o_ref, xp_ref, act_ref, st_ref, y_ref):
    blk = SSM_BLOCK
    ch = SSM_CHUNK
    hp = SSM_HEAD_DIM
    ns = SSM_STATE
    rep = SSM_HEADS // SSM_GROUPS
    pad = 8

    @pl.when(pl.program_id(1) == 0)
    def _():
        xp_ref[0:pad, :] = jnp.zeros((pad, SSM_CONV_DIM), F32)
        st_ref[...] = jnp.zeros_like(st_ref)

    xin = xbc_ref[0]
    xp_ref[pad:pad + blk, :] = xin
    cw = cw_ref[...]
    conv = cb_ref[...] + cw[SSM_CONV - 1:SSM_CONV] * xin
    for j in range(SSM_CONV - 1):
        shift = SSM_CONV - 1 - j
        conv = conv + cw[j:j + 1] * xp_ref[pl.ds(pad - shift, blk), :]
    xp_ref[0:pad, :] = xin[blk - pad:blk]
    act_ref[...] = _silu(conv)

    dt_raw = dt_ref[0] + dtb_ref[...]
    dt = jnp.maximum(dt_raw, 0.0) + jnp.log(1.0 + jnp.exp(-jnp.abs(dt_raw)))
    a_all = dt * (-jnp.exp(alog_ref[...]))
    cr = lax.broadcasted_iota(jnp.int32, (ch, ch), 0)
    cc = lax.broadcasted_iota(jnp.int32, (ch, ch), 1)
    causal = cc <= cr
    tri = jnp.where(causal, 1.0, 0.0).astype(F32)
    tri_t = jnp.where(cr <= cc, 1.0, 0.0).astype(F32)

    for c in range(blk // ch):
        r0 = c * ch
        dtc = dt[r0:r0 + ch]
        ac = a_all[r0:r0 + ch]
        acs = jnp.dot(tri, ac, precision=HIGHEST, preferred_element_type=F32)
        acs_t = lax.dot_general(ac, tri_t, (((0,), (0,)), ((), ())), precision=HIGHEST,
                                preferred_element_type=F32)
        for g in range(SSM_GROUPS):
            bm = act_ref[r0:r0 + ch, SSM_INNER + g * ns:SSM_INNER + (g + 1) * ns]
            cm = act_ref[r0:r0 + ch, SSM_INNER + SSM_GROUPS * ns + g * ns:
                         SSM_INNER + SSM_GROUPS * ns + (g + 1) * ns]
            cb = _dot_nt(cm, bm)
            for r in range(rep):
                h = g * rep + r
                xh = act_ref[r0:r0 + ch, h * hp:(h + 1) * hp]
                dth = dtc[:, h:h + 1]
                acs_h = acs[:, h:h + 1]
                lmat = jnp.where(causal, jnp.exp(acs_h - acs_t[h:h + 1, :]), 0.0)
                acs_last = acs[ch - 1:ch, h:h + 1]
                st = st_ref[h]
                y = _dot(cb * lmat, dth * xh)
                y = y + jnp.exp(acs_h) * _dot(cm, st)
                y = y + dskip_ref[h] * xh
                bw = bm * (jnp.exp(acs_last - acs_h) * dth)
                st_ref[h] = jnp.exp(acs_last) * st + _dot_tn(bw, xh)
                y_ref[r0:r0 + ch, h * hp:(h + 1) * hp] = y
    yz = y_ref[...] * _silu(z_ref[0])
    o_ref[0] = _rms(yz, ng_ref[...])


def mamba2(sproj, conv_w, conv_b, dt_bias, a_log, d_skip, norm_g, name):
    bsz, seq, _ = sproj.shape
    blk = SSM_BLOCK
    padh = LANES - SSM_HEADS
    dtb = jnp.pad(dt_bias, (0, padh)).reshape(1, LANES)
    alog = jnp.pad(a_log, (0, padh)).reshape(1, LANES)
    return pl.pallas_call(
        _ssd_body,
        grid=(bsz, seq // blk),
        in_specs=[pl.BlockSpec((1, blk, SSM_CONV_DIM), lambda b, n: (b, n, 0)),
                  pl.BlockSpec((1, blk, LANES), lambda b, n: (b, n, 6)),
                  pl.BlockSpec((1, blk, SSM_INNER), lambda b, n: (b, n, 2)),
                  pl.BlockSpec((SSM_CONV, SSM_CONV_DIM), lambda b, n: (0, 0)),
                  pl.BlockSpec((1, SSM_CONV_DIM), lambda b, n: (0, 0)),
                  pl.BlockSpec((1, LANES), lambda b, n: (0, 0)),
                  pl.BlockSpec((1, LANES), lambda b, n: (0, 0)),
                  pl.BlockSpec(memory_space=pltpu.SMEM),
                  pl.BlockSpec((1, SSM_INNER), lambda b, n: (0, 0))],
        out_specs=pl.BlockSpec((1, blk, SSM_INNER), lambda b, n: (b, n, 0)),
        out_shape=jax.ShapeDtypeStruct((bsz, seq, SSM_INNER), F32),
        scratch_shapes=[pltpu.VMEM((blk + 8, SSM_CONV_DIM), F32),
                        pltpu.VMEM((blk, SSM_CONV_DIM), F32),
                        pltpu.VMEM((SSM_HEADS, SSM_STATE, SSM_HEAD_DIM), F32),
                        pltpu.VMEM((blk, SSM_INNER), F32)],
        compiler_params=_cparams("arbitrary", "arbitrary"),
        name=name,
    )(sproj, sproj, sproj, conv_w, conv_b.reshape(1, -1), dtb, alog, d_skip,
      norm_g.reshape(1, -1))


def _merge_body(ya_ref, yb_ref, yc1_ref, yc2_ref, yc3_ref, yd_ref, x_ref, gm_ref, wmg_ref, wbr_ref,
                wout_ref, o_ref):
    ys = (ya_ref[...], yb_ref[...], yc1_ref[...] + yc2_ref[...] + yc3_ref[...], yd_ref[...])
    x = x_ref[...]
    hn = _rms(x, gm_ref[...]).astype(BF16)
    u = None
    for nbr in range(N_BRANCH):
        gate = jnp.dot(hn, wmg_ref[:, nbr * D_MODEL:(nbr + 1) * D_MODEL], preferred_element_type=F32)
        t = _sigmoid(gate) * _dot(ys[nbr], wbr_ref[nbr])
        u = t if u is None else u + t
    o_ref[...] = x + _dot(u, wout_ref[...])


def merge(ya, yb, yc1, yc2, yc3, yd, x2d, g_mix, w_mgate, w_branch, w_out, tm, name):
    m = x2d.shape[0]

    def rows(w):
        return pl.BlockSpec((tm, w), lambda i: (i, 0))

    return pl.pallas_call(
        _merge_body,
        grid=(m // tm,),
        in_specs=[rows(512)] * 6 + [rows(D_MODEL),
                                    pl.BlockSpec((1, D_MODEL), lambda i: (0, 0)),
                                    pl.BlockSpec((D_MODEL, N_BRANCH * D_MODEL), lambda i: (0, 0)),
                                    pl.BlockSpec((N_BRANCH, BRANCH_WIDTH, D_MODEL), lambda i: (0, 0, 0)),
                                    pl.BlockSpec((D_MODEL, D_MODEL), lambda i: (0, 0))],
        out_specs=rows(D_MODEL),
        out_shape=jax.ShapeDtypeStruct((m, D_MODEL), F32),
        compiler_params=_cparams("arbitrary"),
        name=name,
    )(ya, yb, yc1, yc2, yc3, yd, x2d, g_mix.reshape(1, -1), w_mgate, w_branch.astype(BF16),
      w_out.astype(BF16))


def _router_logits(hn, wr_ref):
    return jnp.dot(hn, wr_ref[...], precision=HIGHEST, preferred_element_type=F32)


def _group_lanes(lane):
    return (lane >= MOE_EXPERTS) & (lane < MOE_EXPERTS + MOE_GROUPS)


def _route_body(x_ref, g_ref, wr_ref, o_ref):
    tm = x_ref.shape[0]
    lane = lax.broadcasted_iota(jnp.int32, (tm, LANES), 1)
    logits = _router_logits(_rms(x_ref[...], g_ref[...]), wr_ref)
    lg = jnp.where(_group_lanes(lane), logits, -jnp.inf)
    mg = jnp.max(lg, axis=-1, keepdims=True)
    gi = jnp.min(jnp.where(lg == mg, lane.astype(F32), 1e9), axis=-1, keepdims=True) - MOE_EXPERTS
    o_ref[...] = jnp.broadcast_to(gi, (tm, LANES)).astype(jnp.int32)


def _gather_body(idx_ref, src_ref, o_ref, sem):
    rows = o_ref.shape[0]
    base = pl.program_id(0) * rows

    def issue(r, carry):
        pltpu.make_async_copy(src_ref.at[pl.ds(idx_ref[base + r], 1)], o_ref.at[pl.ds(r, 1)],
                              sem).start()
        return carry

    lax.fori_loop(0, rows, issue, 0, unroll=8)
    pltpu.make_async_copy(src_ref.at[pl.ds(0, rows)], o_ref, sem).wait()


def gather_rows(src, idx, rows, name):
    n = idx.shape[0]
    d = src.shape[1]
    grid_spec = pltpu.PrefetchScalarGridSpec(
        num_scalar_prefetch=1,
        grid=(n // rows,),
        in_specs=[pl.BlockSpec(memory_space=pl.ANY)],
        out_specs=pl.BlockSpec((rows, d), lambda i, idx_ref: (i, 0)),
        scratch_shapes=[pltpu.SemaphoreType.DMA(())],
    )
    return pl.pallas_call(
        _gather_body,
        grid_spec=grid_spec,
        out_shape=jax.ShapeDtypeStruct((n, d), src.dtype),
        compiler_params=_cparams("arbitrary"),
        name=name,
    )(idx, src)


def _moe_body(tg_ref, x_ref, g_ref, wr_ref, wgu_ref, wd_ref, o_ref, hn_ref, comb_ref):
    e = pl.program_id(1)
    grp = tg_ref[pl.program_id(0)]
    tm = x_ref.shape[0]
    lane = lax.broadcasted_iota(jnp.int32, (tm, LANES), 1)
    lanef = lane.astype(F32)
    live = grp < MOE_GROUPS

    @pl.when(e == 0)
    def _():
        o_ref[...] = x_ref[...]

    @pl.when((e == 0) & live)
    def _():
        hn = _rms(x_ref[...], g_ref[...])
        hn_ref[...] = hn.astype(BF16)
        logits = _router_logits(hn, wr_ref)
        is_grp = _group_lanes(lane)
        lg = jnp.where(is_grp, logits, -jnp.inf)
        mg = jnp.max(lg, axis=-1, keepdims=True)
        sg = jnp.sum(jnp.where(is_grp, jnp.exp(lg - mg), 0.0), axis=-1, keepdims=True)
        lt = jnp.sum(jnp.where(lane == MOE_EXPERTS + grp, logits, 0.0), axis=-1, keepdims=True)
        g_w = jnp.exp(lt - mg) / sg
        in_grp = (lane < MOE_EXPERTS) & (lane // MOE_EPG == grp)
        le = jnp.where(in_grp, logits, -jnp.inf)
        m1 = jnp.max(le, axis=-1, keepdims=True)
        i1 = jnp.min(jnp.where(le == m1, lanef, 1e9), axis=-1, keepdims=True)
        le2 = jnp.where(lanef == i1, -jnp.inf, le)
        m2 = jnp.max(le2, axis=-1, keepdims=True)
        i2 = jnp.min(jnp.where(le2 == m2, lanef, 1e9), axis=-1, keepdims=True)
        e2 = jnp.exp(m2 - m1)
        den = 1.0 + e2
        comb_ref[...] = (jnp.where(lanef == i1, g_w / den, 0.0)
                         + jnp.where(lanef == i2, g_w * e2 / den, 0.0))

    @pl.when(live)
    def _():
        a = jnp.dot(hn_ref[...], wgu_ref[0], preferred_element_type=F32)
        ce = jnp.sum(jnp.where(lane == grp * MOE_EPG + e, comb_ref[...], 0.0), axis=-1, keepdims=True)
        act = _silu(a[:, :MOE_FF]) * a[:, MOE_FF:] * ce
        o_ref[...] += _dot(act, wd_ref[0])


def _moe_plan(gid, tm):
    t = gid.shape[0]
    onehot = (gid[:, None] == jnp.arange(MOE_GROUPS, dtype=jnp.int32)).astype(jnp.int32)
    csum = jnp.cumsum(onehot, axis=0)
    counts = csum[-1]
    rank = (csum * onehot).sum(axis=1) - 1
    padded = (counts + tm - 1) // tm * tm
    pend = jnp.cumsum(padded)
    dest = (pend - padded)[gid] + rank
    r_pad = t + MOE_GROUPS * tm
    row_token = jnp.zeros((r_pad,), jnp.int32).at[dest].set(jnp.arange(t, dtype=jnp.int32))
    tile_start = jnp.arange(r_pad // tm, dtype=jnp.int32) * tm
    tile_group = (tile_start[:, None] >= pend[None, :]).sum(axis=1)
    return row_token, dest.astype(jnp.int32), tile_group.astype(jnp.int32)


def moe(x2d, g_ffn, w_grp, w_exp, w_gate, w_up, w_down, tm, name):
    m = x2d.shape[0]
    gf = g_ffn.reshape(1, -1)
    wr = jnp.concatenate([w_exp, w_grp,
                          jnp.zeros((D_MODEL, LANES - MOE_EXPERTS - MOE_GROUPS), F32)], axis=1)
    wgu = jnp.concatenate([w_gate, w_up], axis=2).astype(BF16)
    tr = min(1024, m)
    gid = pl.pallas_call(
        _route_body,
        grid=(m // tr,),
        in_specs=[pl.BlockSpec((tr, D_MODEL), lambda i: (i, 0)),
                  pl.BlockSpec((1, D_MODEL), lambda i: (0, 0)),
                  pl.BlockSpec((D_MODEL, LANES), lambda i: (0, 0))],
        out_specs=pl.BlockSpec((tr, LANES), lambda i: (i, 0)),
        out_shape=jax.ShapeDtypeStruct((m, LANES), jnp.int32),
        compiler_params=_cparams("arbitrary"),
        name=name + "_route",
    )(x2d, gf, wr)[:, 0]
    row_token, dest, tile_group = _moe_plan(gid, tm)
    rows_dma = min(MOE_GATHER_ROWS, tm)
    xs = gather_rows(x2d, row_token, rows_dma, name + "_gather")

    def expert(i, e, tg):
        return (jnp.minimum(tg[i], MOE_GROUPS - 1) * MOE_EPG + e, 0, 0)

    grid_spec = pltpu.PrefetchScalarGridSpec(
        num_scalar_prefetch=1,
        grid=(xs.shape[0] // tm, MOE_EPG),
        in_specs=[pl.BlockSpec((tm, D_MODEL), lambda i, e, tg: (i, 0)),
                  pl.BlockSpec((1, D_MODEL), lambda i, e, tg: (0, 0)),
                  pl.BlockSpec((D_MODEL, LANES), lambda i, e, tg: (0, 0)),
                  pl.BlockSpec((1, D_MODEL, 2 * MOE_FF), expert),
                  pl.BlockSpec((1, MOE_FF, D_MODEL), expert)],
        out_specs=pl.BlockSpec((tm, D_MODEL), lambda i, e, tg: (i, 0)),
        scratch_shapes=[pltpu.VMEM((tm, D_MODEL), BF16), pltpu.VMEM((tm, LANES), F32)],
    )
    ys = pl.pallas_call(
        _moe_body,
        grid_spec=grid_spec,
        out_shape=jax.ShapeDtypeStruct(xs.shape, F32),
        compiler_params=_cparams("arbitrary", "arbitrary"),
        name=name,
    )(tile_group, xs, gf, wr, wgu, w_down.astype(BF16))
    return gather_rows(ys, dest, rows_dma, name + "_scatter")


def _ple_body(x_ref, p_ref, g_ref, wg_ref, wp_ref, gf_ref, o_ref, *, final):
    x = x_ref[...]
    gate = _sigmoid(_dot(_rms(x, g_ref[...]), wg_ref[...]))
    xn = x + _dot(p_ref[...], wp_ref[...]) * gate
    if final:
        xn = _rms(xn, gf_ref[...])
    o_ref[...] = xn


def ple(x2d, p2d, g_ple, w_gate, w_proj, g_final, final, tm, name):
    m = x2d.shape[0]
    return pl.pallas_call(
        functools.partial(_ple_body, final=final),
        grid=(m // tm,),
        in_specs=[pl.BlockSpec((tm, D_MODEL), lambda i: (i, 0)),
                  pl.BlockSpec((tm, PLE_DIM), lambda i: (i, 0)),
                  pl.BlockSpec((1, D_MODEL), lambda i: (0, 0)),
                  pl.BlockSpec((D_MODEL, D_MODEL), lambda i: (0, 0)),
                  pl.BlockSpec((PLE_DIM, D_MODEL), lambda i: (0, 0)),
                  pl.BlockSpec((1, D_MODEL), lambda i: (0, 0))],
        out_specs=pl.BlockSpec((tm, D_MODEL), lambda i: (i, 0)),
        out_shape=jax.ShapeDtypeStruct((m, D_MODEL), F32),
        compiler_params=_cparams("arbitrary"),
        name=name,
    )(x2d, p2d, g_ple.reshape(1, -1), w_gate.astype(BF16), w_proj.astype(BF16),
      g_final.reshape(1, -1))


def _cols(w, *names):
    return [w[:, _OFF[n][0]:_OFF[n][0] + _OFF[n][1]] for n in names]


def _padcols(w, width):
    return jnp.pad(w, ((0, 0), (0, width - w.shape[1])))


def _pad_q_heads(wq):
    hd = ATT_HEAD_DIM
    zero = jnp.zeros((wq.shape[0], hd), wq.dtype)
    cols = []
    for h in range(ATT_HEADS):
        blk = wq[:, h * hd:(h + 1) * hd] * (hd ** -0.5)
        cols += [blk, zero] if h // ATT_REP == 0 else [zero, blk]
    return jnp.concatenate(cols, axis=1)


def _split_w_in(w):
    swa_q, nsa_q = _cols(w, 'swa_q', 'nsa_q')
    w_att = jnp.concatenate([_pad_q_heads(swa_q), _pad_q_heads(nsa_q)] + _cols(w, *KV_BLK), axis=1)
    w_hgrn = jnp.concatenate(_cols(w, 'hgrn_q', 'hgrn_f', 'hgrn_i', 'hgrn_g'), axis=1)
    (xbc, dt, ngate, z) = _cols(w, 'ssm_xbc', 'ssm_dt', 'nsa_gate', 'ssm_z')
    w_ssm = jnp.concatenate([xbc, _padcols(dt, LANES), _padcols(ngate, LANES), z], axis=1)
    (w_mg,) = _cols(w, 'merge_gate')
    return [a.astype(BF16) for a in (w_att, w_hgrn, w_ssm, w_mg)]


def _mixers(i, att, hproj, sproj, attn_sinks, hgrn_lower_bounds, hgrn_norm_g, nsa_pos_k, nsa_pos_v,
            nsa_cmp_w1_k, nsa_cmp_w2_k, nsa_cmp_w1_v, nsa_cmp_w2_v, ssm_conv_w, ssm_conv_b,
            ssm_dt_bias, ssm_A_log, ssm_D, ssm_norm_g):
    bsz, seq, _ = att.shape
    kv = KV_BLK
    gate_blk = 7
    y_a = banded_attention(att, 0, kv['swa_k'], kv['swa_v'], window=SWA_WINDOW, sinks=attn_sinks[i],
                           name=f"swa{i}")
    y_b = hgrn2(hproj, hgrn_lower_bounds, hgrn_norm_g[i], i, name=f"hgrn{i}")
    ncp = seq // NSA_CMP_STRIDE

    def cmp_in(name):
        c0 = kv[name] * LANES
        return att[:, :, c0:c0 + LANES].reshape(bsz, ncp, NSA_CMP_STRIDE * LANES)

    kc = nsa_compress(cmp_in('nsa_k_cmp'), nsa_pos_k[i], nsa_cmp_w1_k[i], nsa_cmp_w2_k[i], name=f"cmpk{i}")
    vc = nsa_compress(cmp_in('nsa_v_cmp'), nsa_pos_v[i], nsa_cmp_w1_v[i], nsa_cmp_w2_v[i], name=f"cmpv{i}")
    y_c1, mask, act = nsa_cmp_select(att, 1, kc, vc, sproj, gate_blk, name=f"nsacmp{i}")
    y_c2 = nsa_selected(att, 1, kv['nsa_k_slc'], kv['nsa_v_slc'], mask, act, sproj, gate_blk,
                        name=f"nsasel{i}")
    y_c3 = banded_attention(att, 1, kv['nsa_k_win'], kv['nsa_v_win'], window=NSA_WINDOW, gate=sproj,
                            gate_blk=gate_blk, gate_col=2, name=f"nsawin{i}")
    y_d = mamba2(sproj, ssm_conv_w[i], ssm_conv_b[i], ssm_dt_bias[i], ssm_A_log[i], ssm_D[i],
                 ssm_norm_g[i], name=f"ssd{i}")
    return y_a, y_b, y_c1, y_c2, y_c3, y_d


def kernel(x, p, w_in, g_mix, attn_sinks, hgrn_lower_bounds, hgrn_norm_g, nsa_pos_k, nsa_pos_v,
           nsa_cmp_w1_k, nsa_cmp_w2_k, nsa_cmp_w1_v, nsa_cmp_w2_v, ssm_conv_w, ssm_conv_b,
           ssm_dt_bias, ssm_A_log, ssm_D, ssm_norm_g, w_branch, w_out, g_ffn, w_router_grp,
           w_router_exp, w_exp_gate, w_exp_up, w_exp_down, g_ple, w_ple_gate, w_ple_proj, g_final):
    bsz, seq, d = x.shape
    depth = w_in.shape[0]
    t = bsz * seq
    x2 = x.reshape(t, d)
    tm_proj = min(1024, t)
    tm_row = min(256, t)
    tm_moe = min(1024, t)
    for i in range(depth):
        w_att, w_hgrn, w_ssm, w_mg = _split_w_in(w_in[i])
        att = norm_mm(x2, g_mix[i], w_att, BF16, tm_proj, 512, f"proj_att{i}").reshape(bsz, seq, -1)
        hproj = norm_mm(x2, g_mix[i], w_hgrn, F32, tm_proj, 512, f"proj_hgrn{i}").reshape(bsz, seq, -1)
        sproj = norm_mm(x2, g_mix[i], w_ssm, F32, tm_proj, 512, f"proj_ssm{i}").reshape(bsz, seq, -1)
        ys = _mixers(i, att, hproj, sproj, attn_sinks, hgrn_lower_bounds, hgrn_norm_g, nsa_pos_k,
                     nsa_pos_v, nsa_cmp_w1_k, nsa_cmp_w2_k, nsa_cmp_w1_v, nsa_cmp_w2_v, ssm_conv_w,
                     ssm_conv_b, ssm_dt_bias, ssm_A_log, ssm_D, ssm_norm_g)
        ys = [y.reshape(t, -1) for y in ys]
        x2 = merge(*ys, x2, g_mix[i], w_mg, w_branch[i], w_out[i], tm_row, f"merge{i}")
        x2 = moe(x2, g_ffn[i], w_router_grp[i], w_router_exp[i], w_exp_gate[i], w_exp_up[i],
                 w_exp_down[i], tm_moe, f"moe{i}")
        x2 = ple(x2, p[i].reshape(t, -1), g_ple[i], w_ple_gate[i], w_ple_proj[i], g_final,
                 i == depth - 1, tm_row, f"ple{i}")
    return x2.reshape(bsz, seq, d)
```

```python
import functools

import numpy as np
import jax
import jax.numpy as jnp
from jax import lax
from jax.experimental import pallas as pl
from jax.experimental.pallas import tpu as pltpu

F32 = jnp.float32
BF16 = jnp.bfloat16
HIGHEST = lax.Precision.HIGHEST

D_MODEL = 1024
PLE_DIM = 256
NORM_EPS = 1e-6
NEG_INF = -1e30
REMOVED = -3e38
N_BRANCH = 4
BRANCH_WIDTH = 512
ATTN_BLOCK = 128

ATT_HEADS = 8
ATT_KV_HEADS = 2
ATT_HEAD_DIM = 64
ATT_REP = ATT_HEADS // ATT_KV_HEADS
SWA_WINDOW = 128
NSA_WINDOW = 512
NSA_CMP_LEN = 32
NSA_CMP_STRIDE = 16
NSA_CMP_HIDDEN = 256
NSA_SEL_LEN = 64
NSA_TOP_N = 16
NSA_FORCE_SCORE = 1e6
SEL_KEY_TILE = 512
SEL_FEATS = SEL_KEY_TILE // NSA_SEL_LEN
MASK_BIG = 1e30

HGRN_HEADS = 4
HGRN_DIM = 128
HGRN_CHUNK = 32
HGRN_MIN_F = 1e-6
HGRN_BLOCK = 256

SSM_HEADS = 8
SSM_HEAD_DIM = 64
SSM_GROUPS = 2
SSM_STATE = 64
SSM_CONV = 4
SSM_CHUNK = 64
SSM_INNER = 512
SSM_CONV_DIM = 768
SSM_BLOCK = 256

MOE_GROUPS = 4
MOE_EPG = 8
MOE_EXPERTS = 32
MOE_FF = 256
MOE_GATHER_ROWS = 512

LANES = 128
VMEM_LIMIT = 56 * 1024 * 1024

ALIBI_SLOPES = tuple(2.0 ** (-8.0 * (h + 1) / ATT_HEADS) for h in range(ATT_HEADS))

ATT_Q_WIDTH = ATT_HEADS * LANES
KV_BLK = {name: 2 * ATT_HEADS + j for j, name in enumerate(
    ('swa_k', 'swa_v', 'nsa_k_cmp', 'nsa_v_cmp', 'nsa_k_slc', 'nsa_v_slc', 'nsa_k_win', 'nsa_v_win'))}

_SPLITS = (
    ('swa_q', 512), ('swa_k', 128), ('swa_v', 128),
    ('hgrn_q', 512), ('hgrn_f', 512), ('hgrn_i', 512), ('hgrn_g', 512),
    ('nsa_q', 512), ('nsa_k_cmp', 128), ('nsa_v_cmp', 128), ('nsa_k_slc', 128),
    ('nsa_v_slc', 128), ('nsa_k_win', 128), ('nsa_v_win', 128), ('nsa_gate', 24),
    ('ssm_z', 512), ('ssm_xbc', 768), ('ssm_dt', 8), ('merge_gate', 4096),
)
_OFF = {}
_o = 0
for _n, _w in _SPLITS:
    _OFF[_n] = (_o, _w)
    _o += _w


def _cparams(*sem):
    return pltpu.CompilerParams(dimension_semantics=sem, vmem_limit_bytes=VMEM_LIMIT)


def _sigmoid(x):
    return 1.0 / (1.0 + jnp.exp(-x))


def _silu(x):
    return x * _sigmoid(x)


def _dot(a, b):
    return jnp.dot(a.astype(BF16), b.astype(BF16), preferred_element_type=F32)


def _dot_nt(a, b):
    return lax.dot_general(a.astype(BF16), b.astype(BF16), (((1,), (1,)), ((), ())),
                           preferred_element_type=F32)


def _dot_tn(a, b):
    return lax.dot_general(a.astype(BF16), b.astype(BF16), (((0,), (0,)), ((), ())),
                           preferred_element_type=F32)


def _rms(x, g):
    ms = jnp.mean(x * x, axis=-1, keepdims=True)
    return x * lax.rsqrt(ms + NORM_EPS) * g


def _norm_mm_body(x_ref, g_ref, w_ref, o_ref, hn_ref):
    @pl.when(pl.program_id(1) == 0)
    def _():
        hn_ref[...] = _rms(x_ref[...], g_ref[...]).astype(BF16)

    o_ref[...] = jnp.dot(hn_ref[...], w_ref[...], preferred_element_type=F32).astype(o_ref.dtype)


def norm_mm(x2d, g, w, out_dtype, tm, tn, name):
    m, k = x2d.shape
    n = w.shape[1]
    return pl.pallas_call(
        _norm_mm_body,
        grid=(m // tm, n // tn),
        in_specs=[pl.BlockSpec((tm, k), lambda i, j: (i, 0)),
                  pl.BlockSpec((1, k), lambda i, j: (0, 0)),
                  pl.BlockSpec((k, tn), lambda i, j: (0, j))],
        out_specs=pl.BlockSpec((tm, tn), lambda i, j: (i, j)),
        out_shape=jax.ShapeDtypeStruct((m, n), out_dtype),
        scratch_shapes=[pltpu.VMEM((tm, k), BF16)],
        compiler_params=_cparams("arbitrary", "arbitrary"),
        name=name,
    )(x2d, g.reshape(1, k), w)


def _ones_row(g):
    return ATT_HEAD_DIM if g == 0 else 0


def _vt_body(v_ref, o_ref):
    vt = jnp.transpose(v_ref[0].astype(F32))
    rowid = lax.broadcasted_iota(jnp.int32, vt.shape, 0)
    for g in range(ATT_KV_HEADS):
        aug = jnp.where(rowid // ATT_HEAD_DIM == g, vt, jnp.where(rowid == _ones_row(g), 1.0, 0.0))
        o_ref[0, g] = aug.astype(BF16)


def v_transposed(arr, v_blk, name):
    bsz, seq, _ = arr.shape
    rows = min(2048, seq)
    return pl.pallas_call(
        _vt_body,
        grid=(bsz, seq // rows),
        in_specs=[pl.BlockSpec((1, rows, LANES), lambda b, n: (b, n, v_blk))],
        out_specs=pl.BlockSpec((1, ATT_KV_HEADS, LANES, rows), lambda b, n: (b, 0, 0, n)),
        out_shape=jax.ShapeDtypeStruct((bsz, ATT_KV_HEADS, LANES, seq), BF16),
        compiler_params=_cparams("arbitrary", "arbitrary"),
        name=name,
    )(arr)


def _q_stack(q_ref, g):
    return jnp.concatenate([q_ref[0, :, (g * ATT_REP + r) * LANES:(g * ATT_REP + r + 1) * LANES]
                            for r in range(ATT_REP)], axis=0)


def _head(x, r):
    return x[:, r * ATTN_BLOCK:(r + 1) * ATTN_BLOCK]


def _banded_body(*refs, window, nprev, use_sink, gate_col):
    if use_sink:
        q_ref, k_ref, vt_ref, sink_ref, o_ref, ot_ref = refs
    else:
        q_ref, k_ref, vt_ref, gate_ref, o_ref, ot_ref = refs
    blk = ATTN_BLOCK
    hd = ATT_HEAD_DIM
    n = pl.program_id(1)
    nk = (nprev + 1) * blk
    start = pl.multiple_of(jnp.maximum(n - nprev, 0) * blk, blk)
    k128 = k_ref[0, pl.ds(start, nk), :]
    krow = lax.broadcasted_iota(jnp.int32, (nk, blk), 0)
    qcol = lax.broadcasted_iota(jnp.int32, (nk, blk), 1)
    rel = (n * blk + qcol) - (start + krow)
    negrel = jnp.where((rel >= 0) & (rel < window), -rel.astype(F32), NEG_INF)
    if not use_sink:
        gate_t = jnp.transpose(_sigmoid(gate_ref[0]))
    st4s = [_dot_nt(k128, _q_stack(q_ref, g)) for g in range(ATT_KV_HEADS)]
    for g in range(ATT_KV_HEADS):
        one = _ones_row(g)
        st4 = st4s[g]
        ps, ms = [], []
        for r in range(ATT_REP):
            h = g * ATT_REP + r
            st = _head(st4, r) + ALIBI_SLOPES[h] * negrel
            m = jnp.max(st, axis=0, keepdims=True)
            if use_sink:
                m = jnp.maximum(m, sink_ref[h])
            ps.append(jnp.exp(st - m).astype(BF16))
            ms.append(m)
        acc4 = jnp.dot(vt_ref[0, g, :, pl.ds(start, nk)], jnp.concatenate(ps, axis=1),
                       preferred_element_type=F32)
        for r in range(ATT_REP):
            h = g * ATT_REP + r
            acc = _head(acc4, r)
            l = acc[one:one + 1]
            if use_sink:
                l = l + jnp.exp(sink_ref[h] - ms[r])
            ot = acc[g * hd:(g + 1) * hd] / l
            if not use_sink:
                c = 3 * h + gate_col
                ot = ot * gate_t[c:c + 1]
            ot_ref[h * hd:(h + 1) * hd, :] = ot
    o_ref[0] = jnp.transpose(ot_ref[...])


def banded_attention(att, q_blk, k_blk, v_blk, *, window, sinks=None, gate=None, gate_blk=None,
                     gate_col=0, name):
    bsz, seq, _ = att.shape
    blk = ATTN_BLOCK
    nprev = (window - 1 + blk - 1) // blk
    use_sink = sinks is not None
    vt = v_transposed(att, v_blk, name + "_vt")
    in_specs = [pl.BlockSpec((1, blk, ATT_Q_WIDTH), lambda b, n: (b, n, q_blk)),
                pl.BlockSpec((1, seq, LANES), lambda b, n: (b, 0, k_blk)),
                pl.BlockSpec((1, ATT_KV_HEADS, LANES, seq), lambda b, n: (b, 0, 0, 0))]
    if use_sink:
        in_specs.append(pl.BlockSpec(memory_space=pltpu.SMEM))
        extra = sinks
    else:
        in_specs.append(pl.BlockSpec((1, blk, LANES), lambda b, n: (b, n, gate_blk)))
        extra = gate
    return pl.pallas_call(
        functools.partial(_banded_body, window=window, nprev=nprev, use_sink=use_sink,
                          gate_col=gate_col),
        grid=(bsz, seq // blk),
        in_specs=in_specs,
        out_specs=pl.BlockSpec((1, blk, 512), lambda b, n: (b, n, 0)),
        out_shape=jax.ShapeDtypeStruct((bsz, seq, 512), F32),
        scratch_shapes=[pltpu.VMEM((512, blk), F32)],
        compiler_params=_cparams("arbitrary", "arbitrary"),
        name=name,
    )(att, att, vt, extra)


def _compress_body(x_ref, w1a_ref, w1b_ref, w1_ref, pos_ref, w2_ref, o_ref):
    x = x_ref[0]
    p = jnp.dot(x, w1a_ref[...], preferred_element_type=F32)
    q = jnp.dot(x, w1b_ref[...], preferred_element_type=F32)
    ncp = x.shape[0]
    q = pltpu.roll(q, shift=ncp - 1, axis=0)
    posb = jnp.broadcast_to(pos_ref[...], (8, pos_ref.shape[1]))
    bias = _dot(posb, w1_ref[...])[0:1]
    hid = NSA_CMP_HIDDEN
    outs = []
    for g in range(ATT_KV_HEADS):
        pre = p[:, g * hid:(g + 1) * hid] + q[:, g * hid:(g + 1) * hid] + bias
        outs.append(_dot(jax.nn.gelu(pre), w2_ref[...]))
    o_ref[0] = jnp.concatenate(outs, axis=1).astype(o_ref.dtype)


def _expand_w1(w1, half):
    hd, hid, ng = ATT_HEAD_DIM, NSA_CMP_HIDDEN, ATT_KV_HEADS
    w = w1.reshape(NSA_CMP_LEN, hd, hid)[half * 16:(half + 1) * 16]
    eye = jnp.eye(ng, dtype=w1.dtype)
    out = jnp.einsum('ldj,gh->lgdhj', w, eye)
    return out.reshape(16 * ng * hd, ng * hid)


def nsa_compress(x16, pos, w1, w2, name):
    bsz, ncp, wid = x16.shape
    hid = NSA_CMP_HIDDEN
    w1a = _expand_w1(w1, 0).astype(BF16)
    w1b = _expand_w1(w1, 1).astype(BF16)
    return pl.pallas_call(
        _compress_body,
        grid=(bsz,),
        in_specs=[pl.BlockSpec((1, ncp, wid), lambda b: (b, 0, 0)),
                  pl.BlockSpec((wid, 2 * hid), lambda b: (0, 0)),
                  pl.BlockSpec((wid, 2 * hid), lambda b: (0, 0)),
                  pl.BlockSpec((NSA_CMP_LEN * ATT_HEAD_DIM, hid), lambda b: (0, 0)),
                  pl.BlockSpec((1, NSA_CMP_LEN * ATT_HEAD_DIM), lambda b: (0, 0)),
                  pl.BlockSpec((hid, ATT_HEAD_DIM), lambda b: (0, 0))],
        out_specs=pl.BlockSpec((1, ncp, LANES), lambda b: (b, 0, 0)),
        out_shape=jax.ShapeDtypeStruct((bsz, ncp, LANES), BF16),
        compiler_params=_cparams("arbitrary"),
        name=name,
    )(x16, w1a, w1b, w1.astype(BF16), pos.reshape(1, -1), w2.astype(BF16))


def _nsa_cmp_body(q_ref, kc_ref, vct_ref, ovt_ref, gate_ref, o_ref, m_ref, act_ref, ot_ref, *, top_n):
    blk = ATTN_BLOCK
    hd = ATT_HEAD_DIM
    n = pl.program_id(1)
    kc = kc_ref[0]
    ncp = kc.shape[0]
    ns = ovt_ref.shape[0]
    crow = lax.broadcasted_iota(jnp.int32, (ncp, blk), 0)
    qcol = lax.broadcasted_iota(jnp.int32, (ncp, blk), 1)
    dist = (n * blk + qcol) - (crow * NSA_CMP_STRIDE + (NSA_CMP_LEN - 1))
    valid = dist >= 0
    negd = jnp.where(valid, -dist.astype(F32), NEG_INF)
    jrow = lax.broadcasted_iota(jnp.int32, (ns, blk), 0)
    qpos = n * blk + lax.broadcasted_iota(jnp.int32, (ns, blk), 1)
    cur = qpos // NSA_SEL_LEN
    forced = (jrow == 0) | (jrow == cur) | (jrow == cur - 1)
    causal_blk = jrow * NSA_SEL_LEN <= qpos
    jrowf = jrow.astype(F32)
    gate_t = jnp.transpose(_sigmoid(gate_ref[0]))
    ones8 = jnp.ones((8, blk), BF16)
    st4s = [_dot_nt(kc, _q_stack(q_ref, g)) for g in range(ATT_KV_HEADS)]
    for g in range(ATT_KV_HEADS):
        one = _ones_row(g)
        st4 = st4s[g]
        es = []
        for r in range(ATT_REP):
            st = _head(st4, r) + ALIBI_SLOPES[g * ATT_REP + r] * negd
            m = jnp.max(st, axis=0, keepdims=True)
            es.append(jnp.where(valid, jnp.exp(st - m), 0.0))
        acc4 = jnp.dot(vct_ref[0, g], jnp.concatenate([e.astype(BF16) for e in es], axis=1),
                       preferred_element_type=F32)
        psum = jnp.zeros((ncp, blk), F32)
        for r in range(ATT_REP):
            h = g * ATT_REP + r
            acc = _head(acc4, r)
            l = acc[one:one + 1]
            inv = jnp.where(l > 0.0, 1.0 / jnp.where(l > 0.0, l, 1.0), 0.0)
            ot_ref[h * hd:(h + 1) * hd, :] = acc[g * hd:(g + 1) * hd] * inv * gate_t[3 * h:3 * h + 1]
            psum = psum + es[r] * inv
        imp = _dot(ovt_ref[...], psum)
        score = jnp.where(forced, NSA_FORCE_SCORE, jnp.where(causal_blk, imp, NEG_INF))
        sel = jnp.zeros((ns, blk), F32)
        for _ in range(top_n):
            mx = jnp.max(score, axis=0, keepdims=True)
            idx = jnp.min(jnp.where(score == mx, jrowf, float(ns)), axis=0, keepdims=True)
            hit = jrowf == idx
            sel = jnp.where(hit, 1.0, sel)
            score = jnp.where(hit, REMOVED, score)
        m_ref[0, g] = sel
        act_ref[0, 0, g * 8:(g + 1) * 8, :] = _dot_nt(ones8, sel)
    o_ref[0] = jnp.transpose(ot_ref[...])


def _overlap_matrix_t(seq):
    n_cmp = seq // NSA_CMP_STRIDE
    n_sel = seq // NSA_SEL_LEN
    cs = np.arange(n_cmp)[None, :] * NSA_CMP_STRIDE
    ss = np.arange(n_sel)[:, None] * NSA_SEL_LEN
    ov = np.clip(np.minimum(cs + NSA_CMP_LEN, ss + NSA_SEL_LEN) - np.maximum(cs, ss), 0, None)
    return jnp.asarray(ov / NSA_CMP_LEN, dtype=BF16)


def nsa_cmp_select(att, q_blk, kc, vc, gate, gate_blk, name):
    bsz, seq, _ = att.shape
    blk = ATTN_BLOCK
    nb = seq // blk
    ncp = seq // NSA_CMP_STRIDE
    ns = seq // NSA_SEL_LEN
    top_n = min(NSA_TOP_N, ns)
    vct = v_transposed(vc, 0, name + "_vt")
    return pl.pallas_call(
        functools.partial(_nsa_cmp_body, top_n=top_n),
        grid=(bsz, nb),
        in_specs=[pl.BlockSpec((1, blk, ATT_Q_WIDTH), lambda b, n: (b, n, q_blk)),
                  pl.BlockSpec((1, ncp, LANES), lambda b, n: (b, 0, 0)),
                  pl.BlockSpec((1, ATT_KV_HEADS, LANES, ncp), lambda b, n: (b, 0, 0, 0)),
                  pl.BlockSpec((ns, ncp), lambda b, n: (0, 0)),
                  pl.BlockSpec((1, blk, LANES), lambda b, n: (b, n, gate_blk))],
        out_specs=[pl.BlockSpec((1, blk, 512), lambda b, n: (b, n, 0)),
                   pl.BlockSpec((1, ATT_KV_HEADS, ns, blk), lambda b, n: (b, 0, 0, n)),
                   pl.BlockSpec((1, 1, ATT_KV_HEADS * 8, ns), lambda b, n: (b, n, 0, 0))],
        out_shape=[jax.ShapeDtypeStruct((bsz, seq, 512), F32),
                   jax.ShapeDtypeStruct((bsz, ATT_KV_HEADS, ns, seq), F32),
                   jax.ShapeDtypeStruct((bsz, nb, ATT_KV_HEADS * 8, ns), F32)],
        scratch_shapes=[pltpu.VMEM((512, blk), F32)],
        compiler_params=_cparams("arbitrary", "arbitrary"),
        name=name,
    )(att, kc, vct, _overlap_matrix_t(seq), gate)


def _k_aug_body(k_ref, o_ref):
    k = k_ref[0].astype(F32)
    rows = k.shape[0]
    kin = lax.broadcasted_iota(jnp.int32, (rows, LANES), 0) % SEL_KEY_TILE
    lane = lax.broadcasted_iota(jnp.int32, (rows, LANES), 1)
    for g in range(ATT_KV_HEADS):
        f = lane - (ATT_HEAD_DIM if g == 0 else 0)
        feat = jnp.where(f == kin // NSA_SEL_LEN, 1.0, 0.0)
        feat = jnp.where(f == SEL_FEATS, (kin % 256).astype(F32), feat)
        feat = jnp.where(f == SEL_FEATS + 1, (kin // 256 * 256).astype(F32), feat)
        feat = jnp.where((f == SEL_FEATS + 2) | (f == SEL_FEATS + 3), 1.0, feat)
        o_ref[0, g] = jnp.where(lane // ATT_HEAD_DIM == g, k, feat).astype(BF16)


def k_augmented(arr, k_blk, name):
    bsz, seq, _ = arr.shape
    rows = min(2048, seq)
    return pl.pallas_call(
        _k_aug_body,
        grid=(bsz, seq // rows),
        in_specs=[pl.BlockSpec((1, rows, LANES), lambda b, n: (b, n, k_blk))],
        out_specs=pl.BlockSpec((1, ATT_KV_HEADS, rows, LANES), lambda b, n: (b, 0, n, 0)),
        out_shape=jax.ShapeDtypeStruct((bsz, ATT_KV_HEADS, seq, LANES), BF16),
        compiler_params=_cparams("arbitrary", "arbitrary"),
        name=name,
    )(arr)


def _query_feature_rows():
    out = np.zeros((ATT_KV_HEADS, 2 * SEL_FEATS, ATT_REP * ATTN_BLOCK), np.float32)
    qin = np.arange(ATTN_BLOCK, dtype=np.float32)
    for g in range(ATT_KV_HEADS):
        for r in range(ATT_REP):
            slope = ALIBI_SLOPES[g * ATT_REP + r]
            cols = slice(r * ATTN_BLOCK, (r + 1) * ATTN_BLOCK)
            out[g, 0, cols] = slope
            out[g, 1, cols] = slope
            out[g, 3, cols] = -slope * qin
            out[g, SEL_FEATS + 2, cols] = -slope
    return jnp.asarray(out)


def _nsa_sel_body(tiles_ref, cnt_ref, q_ref, k_ref, vt_ref, m_ref, gate_ref, fq_ref, o_ref, ot_ref,
                  acc_ref, mx_ref, dq_ref, *, ntl):
    blk = ATTN_BLOCK
    hd = ATT_HEAD_DIM
    tk = SEL_KEY_TILE
    sl = NSA_SEL_LEN
    per = tk // sl
    b = pl.program_id(0)
    n = pl.program_id(1)
    nb = pl.num_programs(1)
    krow = lax.broadcasted_iota(jnp.int32, (tk, blk), 0)
    qcol = lax.broadcasted_iota(jnp.int32, (tk, blk), 1)
    dq_ref[...] = (qcol - krow).astype(F32)
    gate_t = jnp.transpose(_sigmoid(gate_ref[0]))
    pad_rows = jnp.zeros((hd - 2 * SEL_FEATS, ATT_REP * blk), BF16)
    for g in range(ATT_KV_HEADS):
        one = _ones_row(g)
        lrow = (b * nb + n) * ATT_KV_HEADS + g
        qt = jnp.concatenate(
            [jnp.transpose(q_ref[0, :, (g * ATT_REP + r) * LANES:(g * ATT_REP + r + 1) * LANES]
                           .astype(F32))[g * hd:(g + 1) * hd] for r in range(ATT_REP)],
            axis=1).astype(BF16)
        acc_ref[...] = jnp.zeros_like(acc_ref)
        mx_ref[...] = jnp.full_like(mx_ref, NEG_INF)

        def tile_scores(j, live):
            t = tiles_ref[lrow * ntl + j]
            base = pl.multiple_of(t * tk, tk)
            sel8 = m_ref[0, g, pl.ds(pl.multiple_of(t * per, per), per), :]
            if live is not None:
                sel8 = jnp.where(live, sel8, 0.0)
            off = (n * blk - base).astype(F32)
            mask_rows = jnp.concatenate([(sel8 - 1.0) * MASK_BIG] * ATT_REP, axis=1)
            alibi_rows = fq_ref[g, 0:SEL_FEATS, :] + off * fq_ref[g, SEL_FEATS:2 * SEL_FEATS, :]
            feats = jnp.concatenate([mask_rows, alibi_rows], axis=0).astype(BF16)
            rhs = jnp.concatenate([qt, feats, pad_rows] if g == 0 else [feats, pad_rows, qt], axis=0)
            st4 = jnp.dot(k_ref[0, g, pl.ds(base, tk), :], rhs,
                          preferred_element_type=F32)
            return st4, base, off

        def tile_finish(st4, base, off, diag):
            if diag:
                causal = (dq_ref[...] + off) >= 0.0
            ps, ms = [], []
            for r in range(ATT_REP):
                st = _head(st4, r)
                if diag:
                    st = jnp.where(causal, st, NEG_INF)
                m = jnp.max(st, axis=0, keepdims=True)
                ps.append(jnp.exp(st - m).astype(BF16))
                ms.append(m)
            acc = jnp.dot(vt_ref[0, g, :, pl.ds(base, tk)], jnp.concatenate(ps, axis=1),
                          preferred_element_type=F32)
            return jnp.concatenate(ms, axis=1), acc

        def pair(j2, diag):
            cnt = cnt_ref[lrow]
            sc_a = tile_scores(2 * j2, None)
            sc_b = tile_scores(jnp.minimum(2 * j2 + 1, ntl - 1), 2 * j2 + 1 < cnt)
            m_a, acc_a = tile_finish(*sc_a, diag)
            m_b, acc_b = tile_finish(*sc_b, False)
            m_old = mx_ref[...]
            m_new = jnp.maximum(m_old, jnp.maximum(m_a, m_b))
            acc_ref[...] = (jnp.exp(m_old - m_new) * acc_ref[...] + jnp.exp(m_a - m_new) * acc_a
                            + jnp.exp(m_b - m_new) * acc_b)
            mx_ref[...] = m_new

        pair(0, True)

        def later(j2, carry):
            pair(j2, False)
            return carry

        lax.fori_loop(1, (cnt_ref[lrow] + 1) // 2, later, 0)
        acc4 = acc_ref[...]
        for r in range(ATT_REP):
            h = g * ATT_REP + r
            acc = _head(acc4, r)
            ot = acc[g * hd:(g + 1) * hd] / acc[one:one + 1]
            ot_ref[h * hd:(h + 1) * hd, :] = ot * gate_t[3 * h + 1:3 * h + 2]
    o_ref[0] = jnp.transpose(ot_ref[...])


def _tile_lists(act, seq):
    bsz, nb = act.shape[:2]
    ns = act.shape[-1]
    tk = SEL_KEY_TILE
    ntl = seq // tk
    per = tk // NSA_SEL_LEN
    cnt_blk = act.reshape(bsz, nb, ATT_KV_HEADS, 8, ns)[:, :, :, 0, :]
    hit = cnt_blk.reshape(bsz, nb, ATT_KV_HEADS, ntl, per).sum(-1) > 0.5
    tidx = jnp.arange(ntl, dtype=jnp.int32)
    diag = (jnp.arange(nb, dtype=jnp.int32) * ATTN_BLOCK + ATTN_BLOCK - 1) // tk
    active = hit & (tidx[None, None, None, :] <= diag[None, :, None, None])
    act_i = active.astype(jnp.int32)
    rank = jnp.cumsum(act_i[..., ::-1], axis=-1)[..., ::-1] - 1
    slot = (active[..., :, None] & (rank[..., :, None] == tidx)).astype(jnp.int32)
    tiles = (slot * tidx[:, None]).sum(-2).reshape(-1)
    cnt = act_i.sum(-1).reshape(-1)
    return tiles, cnt, ntl


def nsa_selected(att, q_blk, k_blk, v_blk, mask, act, gate, gate_blk, name):
    bsz, seq, _ = att.shape
    blk = ATTN_BLOCK
    ns = seq // NSA_SEL_LEN
    tiles, cnt, ntl = _tile_lists(act, seq)
    vt = v_transposed(att, v_blk, name + "_vt")
    kaug = k_augmented(att, k_blk, name + "_k")
    grid_spec = pltpu.PrefetchScalarGridSpec(
        num_scalar_prefetch=2,
        grid=(bsz, seq // blk),
        in_specs=[pl.BlockSpec((1, blk, ATT_Q_WIDTH), lambda b, n, *_: (b, n, q_blk)),
                  pl.BlockSpec((1, ATT_KV_HEADS, seq, LANES), lambda b, n, *_: (b, 0, 0, 0)),
                  pl.BlockSpec((1, ATT_KV_HEADS, LANES, seq), lambda b, n, *_: (b, 0, 0, 0)),
                  pl.BlockSpec((1, ATT_KV_HEADS, ns, blk), lambda b, n, *_: (b, 0, 0, n)),
                  pl.BlockSpec((1, blk, LANES), lambda b, n, *_: (b, n, gate_blk)),
                  pl.BlockSpec((ATT_KV_HEADS, 2 * SEL_FEATS, ATT_REP * blk), lambda b, n, *_: (0, 0, 0))],
        out_specs=pl.BlockSpec((1, blk, 512), lambda b, n, *_: (b, n, 0)),
        scratch_shapes=[pltpu.VMEM((512, blk), F32),
                        pltpu.VMEM((LANES, ATT_REP * blk), F32),
                        pltpu.VMEM((1, ATT_REP * blk), F32),
                        pltpu.VMEM((SEL_KEY_TILE, blk), F32)],
    )
    return pl.pallas_call(
        functools.partial(_nsa_sel_body, ntl=ntl),
        grid_spec=grid_spec,
        out_shape=jax.ShapeDtypeStruct((bsz, seq, 512), F32),
        compiler_params=_cparams("arbitrary", "arbitrary"),
        name=name,
    )(tiles, cnt, att, kaug, vt, mask, gate, _query_feature_rows())


def _hgrn_body(q_ref, f_ref, i_ref, g_ref, lbp_ref, ng_ref, o_ref, st_ref, b_ref, k_ref, *, layer):
    blk = HGRN_BLOCK
    ch = HGRN_CHUNK
    dk = HGRN_DIM

    @pl.when(pl.program_id(1) == 0)
    def _():
        st_ref[...] = jnp.zeros_like(st_ref)

    lbp = lbp_ref[...]
    e = jnp.exp(lbp - jnp.max(lbp, axis=0, keepdims=True))
    sm = e / jnp.sum(e, axis=0, keepdims=True)
    lb = jnp.zeros((1, lbp.shape[1]), F32)
    for d in range(1, layer + 1):
        lb = lb + sm[d:d + 1]
    z = f_ref[0]
    f = lb + (1.0 - lb) * _sigmoid(z)
    logf = jnp.log(jnp.maximum(f, HGRN_MIN_F))
    k_ref[...] = (1.0 - lb) * _sigmoid(-z)
    tr = lax.broadcasted_iota(jnp.int32, (blk, blk), 0)
    tc = lax.broadcasted_iota(jnp.int32, (blk, blk), 1)
    tri = jnp.where((tr // ch == tc // ch) & (tc <= tr), 1.0, 0.0).astype(F32)
    b_ref[...] = jnp.dot(tri, logf, precision=HIGHEST, preferred_element_type=F32)
    cr = lax.broadcasted_iota(jnp.int32, (ch, ch), 0)
    cc = lax.broadcasted_iota(jnp.int32, (ch, ch), 1)
    causal = cc <= cr
    ng = ng_ref[...]

    for c in range(blk // ch):
        r0 = c * ch
        bc = b_ref[pl.ds(r0, ch), :]
        qc = q_ref[0, pl.ds(r0, ch), :]
        kc = k_ref[pl.ds(r0, ch), :]
        vc = i_ref[0, pl.ds(r0, ch), :]
        gc = g_ref[0, pl.ds(r0, ch), :]
        b_mid = bc[ch // 2:ch // 2 + 1]
        b_last = bc[ch - 1:ch]
        qa = qc * jnp.exp(bc - b_mid)
        ka = kc * jnp.exp(b_mid - bc)
        qe = qc * jnp.exp(bc)
        kl = kc * jnp.exp(b_last - bc)
        dec = jnp.exp(b_last)
        for h in range(HGRN_HEADS):
            sl = slice(h * dk, (h + 1) * dk)
            a = jnp.where(causal, _dot_nt(qa[:, sl], ka[:, sl]), 0.0)
            st = st_ref[h]
            o = _dot(a, vc[:, sl]) + _dot_nt(qe[:, sl], st)
            st_ref[h] = st * dec[:, sl] + _dot_tn(vc[:, sl], kl[:, sl])
            o = _rms(o, ng[:, sl]) * _silu(gc[:, sl])
            o_ref[0, pl.ds(r0, ch), sl] = o


def hgrn2(hproj, lower_bounds, norm_g, layer, name):
    bsz, seq, _ = hproj.shape
    blk = HGRN_BLOCK
    wid = HGRN_HEADS * HGRN_DIM
    depth = lower_bounds.shape[0]

    def col(j):
        return pl.BlockSpec((1, blk, wid), lambda b, n: (b, n, j))

    return pl.pallas_call(
        functools.partial(_hgrn_body, layer=layer),
        grid=(bsz, seq // blk),
        in_specs=[col(0), col(1), col(2), col(3),
                  pl.BlockSpec((depth, wid), lambda b, n: (0, 0)),
                  pl.BlockSpec((1, wid), lambda b, n: (0, 0))],
        out_specs=pl.BlockSpec((1, blk, wid), lambda b, n: (b, n, 0)),
        out_shape=jax.ShapeDtypeStruct((bsz, seq, wid), F32),
        scratch_shapes=[pltpu.VMEM((HGRN_HEADS, HGRN_DIM, HGRN_DIM), F32),
                        pltpu.VMEM((blk, wid), F32),
                        pltpu.VMEM((blk, wid), F32)],
        compiler_params=_cparams("arbitrary", "arbitrary"),
        name=name,
    )(hproj, hproj, hproj, hproj, lower_bounds, norm_g.reshape(1, wid))


def _ssd_body(xbc_ref, dt_ref, z_ref, cw_ref, cb_ref, dtb_ref, alog_ref, dskip_ref, ng_ref,
              o_ref, xp_ref, act_ref, st_ref, y_ref):
    blk = SSM_BLOCK
    ch = SSM_CHUNK
    hp = SSM_HEAD_DIM
    ns = SSM_STATE
    rep = SSM_HEADS // SSM_GROUPS
    pad = 8

    @pl.when(pl.program_id(1) == 0)
    def _():
        xp_ref[0:pad, :] = jnp.zeros((pad, SSM_CONV_DIM), F32)
        st_ref[...] = jnp.zeros_like(st_ref)

    xin = xbc_ref[0]
    xp_ref[pad:pad + blk, :] = xin
    cw = cw_ref[...]
    conv = cb_ref[...] + cw[SSM_CONV - 1:SSM_CONV] * xin
    for j in range(SSM_CONV - 1):
        shift = SSM_CONV - 1 - j
        conv = conv + cw[j:j + 1] * xp_ref[pl.ds(pad - shift, blk), :]
    xp_ref[0:pad, :] = xin[blk - pad:blk]
    act_ref[...] = _silu(conv)

    dt_raw = dt_ref[0] + dtb_ref[...]
    dt = jnp.maximum(dt_raw, 0.0) + jnp.log(1.0 + jnp.exp(-jnp.abs(dt_raw)))
    a_all = dt * (-jnp.exp(alog_ref[...]))
    cr = lax.broadcasted_iota(jnp.int32, (ch, ch), 0)
    cc = lax.broadcasted_iota(jnp.int32, (ch, ch), 1)
    causal = cc <= cr
    tri = jnp.where(causal, 1.0, 0.0).astype(F32)
    tri_t = jnp.where(cr <= cc, 1.0, 0.0).astype(F32)

    for c in range(blk // ch):
        r0 = c * ch
        dtc = dt[r0:r0 + ch]
        ac = a_all[r0:r0 + ch]
        acs = jnp.dot(tri, ac, precision=HIGHEST, preferred_element_type=F32)
        acs_t = lax.dot_general(ac, tri_t, (((0,), (0,)), ((), ())), precision=HIGHEST,
                                preferred_element_type=F32)
        for g in range(SSM_GROUPS):
            bm = act_ref[r0:r0 + ch, SSM_INNER + g * ns:SSM_INNER + (g + 1) * ns]
            cm = act_ref[r0:r0 + ch, SSM_INNER + SSM_GROUPS * ns + g * ns:
                         SSM_INNER + SSM_GROUPS * ns + (g + 1) * ns]
            cb = _dot_nt(cm, bm)
            for r in range(rep):
                h = g * rep + r
                xh = act_ref[r0:r0 + ch, h * hp:(h + 1) * hp]
                dth = dtc[:, h:h + 1]
                acs_h = acs[:, h:h + 1]
                lmat = jnp.where(causal, jnp.exp(acs_h - acs_t[h:h + 1, :]), 0.0)
                acs_last = acs[ch - 1:ch, h:h + 1]
                st = st_ref[h]
                y = _dot(cb * lmat, dth * xh)
                y = y + jnp.exp(acs_h) * _dot(cm, st)
                y = y + dskip_ref[h] * xh
                bw = bm * (jnp.exp(acs_last - acs_h) * dth)
                st_ref[h] = jnp.exp(acs_last) * st + _dot_tn(bw, xh)
                y_ref[r0:r0 + ch, h * hp:(h + 1) * hp] = y
    yz = y_ref[...] * _silu(z_ref[0])
    o_ref[0] = _rms(yz, ng_ref[...])


def mamba2(sproj, conv_w, conv_b, dt_bias, a_log, d_skip, norm_g, name):
    bsz, seq, _ = sproj.shape
    blk = SSM_BLOCK
    padh = LANES - SSM_HEADS
    dtb = jnp.pad(dt_bias, (0, padh)).reshape(1, LANES)
    alog = jnp.pad(a_log, (0, padh)).reshape(1, LANES)
    return pl.pallas_call(
        _ssd_body,
        grid=(bsz, seq // blk),
        in_specs=[pl.BlockSpec((1, blk, SSM_CONV_DIM), lambda b, n: (b, n, 0)),
                  pl.BlockSpec((1, blk, LANES), lambda b, n: (b, n, 6)),
                  pl.BlockSpec((1, blk, SSM_INNER), lambda b, n: (b, n, 2)),
                  pl.BlockSpec((SSM_CONV, SSM_CONV_DIM), lambda b, n: (0, 0)),
                  pl.BlockSpec((1, SSM_CONV_DIM), lambda b, n: (0, 0)),
                  pl.BlockSpec((1, LANES), lambda b, n: (0, 0)),
                  pl.BlockSpec((1, LANES), lambda b, n: (0, 0)),
                  pl.BlockSpec(memory_space=pltpu.SMEM),
                  pl.BlockSpec((1, SSM_INNER), lambda b, n: (0, 0))],
        out_specs=pl.BlockSpec((1, blk, SSM_INNER), lambda b, n: (b, n, 0)),
        out_shape=jax.ShapeDtypeStruct((bsz, seq, SSM_INNER), F32),
        scratch_shapes=[pltpu.VMEM((blk + 8, SSM_CONV_DIM), F32),
                        pltpu.VMEM((blk, SSM_CONV_DIM), F32),
                        pltpu.VMEM((SSM_HEADS, SSM_STATE, SSM_HEAD_DIM), F32),
                        pltpu.VMEM((blk, SSM_INNER), F32)],
        compiler_params=_cparams("arbitrary", "arbitrary"),
        name=name,
    )(sproj, sproj, sproj, conv_w, conv_b.reshape(1, -1), dtb, alog, d_skip,
      norm_g.reshape(1, -1))


def _merge_body(ya_ref, yb_ref, yc1_ref, yc2_ref, yc3_ref, yd_ref, x_ref, gm_ref, wmg_ref, wbr_ref,
                wout_ref, o_ref):
    ys = (ya_ref[...], yb_ref[...], yc1_ref[...] + yc2_ref[...] + yc3_ref[...], yd_ref[...])
    x = x_ref[...]
    hn = _rms(x, gm_ref[...]).astype(BF16)
    u = None
    for nbr in range(N_BRANCH):
        gate = jnp.dot(hn, wmg_ref[:, nbr * D_MODEL:(nbr + 1) * D_MODEL], preferred_element_type=F32)
        t = _sigmoid(gate) * _dot(ys[nbr], wbr_ref[nbr])
        u = t if u is None else u + t
    o_ref[...] = x + _dot(u, wout_ref[...])


def merge(ya, yb, yc1, yc2, yc3, yd, x2d, g_mix, w_mgate, w_branch, w_out, tm, name):
    m = x2d.shape[0]

    def rows(w):
        return pl.BlockSpec((tm, w), lambda i: (i, 0))

    return pl.pallas_call(
        _merge_body,
        grid=(m // tm,),
        in_specs=[rows(512)] * 6 + [rows(D_MODEL),
                                    pl.BlockSpec((1, D_MODEL), lambda i: (0, 0)),
                                    pl.BlockSpec((D_MODEL, N_BRANCH * D_MODEL), lambda i: (0, 0)),
                                    pl.BlockSpec((N_BRANCH, BRANCH_WIDTH, D_MODEL), lambda i: (0, 0, 0)),
                                    pl.BlockSpec((D_MODEL, D_MODEL), lambda i: (0, 0))],
        out_specs=rows(D_MODEL),
        out_shape=jax.ShapeDtypeStruct((m, D_MODEL), F32),
        compiler_params=_cparams("arbitrary"),
        name=name,
    )(ya, yb, yc1, yc2, yc3, yd, x2d, g_mix.reshape(1, -1), w_mgate, w_branch.astype(BF16),
      w_out.astype(BF16))


def _router_logits(hn, wr_ref):
    return jnp.dot(hn, wr_ref[...], precision=HIGHEST, preferred_element_type=F32)


def _group_lanes(lane):
    return (lane >= MOE_EXPERTS) & (lane < MOE_EXPERTS + MOE_GROUPS)


def _route_body(x_ref, g_ref, wr_ref, o_ref):
    tm = x_ref.shape[0]
    lane = lax.broadcasted_iota(jnp.int32, (tm, LANES), 1)
    logits = _router_logits(_rms(x_ref[...], g_ref[...]), wr_ref)
    lg = jnp.where(_group_lanes(lane), logits, -jnp.inf)
    mg = jnp.max(lg, axis=-1, keepdims=True)
    gi = jnp.min(jnp.where(lg == mg, lane.astype(F32), 1e9), axis=-1, keepdims=True) - MOE_EXPERTS
    o_ref[...] = jnp.broadcast_to(gi, (tm, LANES)).astype(jnp.int32)


def _gather_body(idx_ref, nlive_ref, src_ref, o_ref, buf, sem):
    rows = o_ref.shape[0]
    i = pl.program_id(0)
    nlive = nlive_ref[0]

    def live(tile):
        return tile * rows < nlive

    def issue_tile(tile):
        base = tile * rows
        slot = tile % 2

        def issue(r, carry):
            pltpu.make_async_copy(src_ref.at[pl.ds(idx_ref[base + r], 1)], buf.at[slot, pl.ds(r, 1)],
                                  sem.at[slot]).start()
            return carry

        lax.fori_loop(0, rows, issue, 0, unroll=8)

    @pl.when((i == 0) & live(0))
    def _():
        issue_tile(0)

    @pl.when((i + 1 < pl.num_programs(0)) & live(i + 1))
    def _():
        issue_tile(i + 1)

    @pl.when(live(i))
    def _():
        slot = i % 2
        pltpu.make_async_copy(src_ref.at[pl.ds(0, rows)], buf.at[slot], sem.at[slot]).wait()
        o_ref[...] = buf[slot]

    @pl.when(jnp.logical_not(live(i)))
    def _():
        o_ref[...] = jnp.zeros_like(o_ref)


def gather_rows(src, idx, nlive, rows, name):
    n = idx.shape[0]
    d = src.shape[1]
    grid_spec = pltpu.PrefetchScalarGridSpec(
        num_scalar_prefetch=2,
        grid=(n // rows,),
        in_specs=[pl.BlockSpec(memory_space=pl.ANY)],
        out_specs=pl.BlockSpec((rows, d), lambda i, *_: (i, 0)),
        scratch_shapes=[pltpu.VMEM((2, rows, d), src.dtype), pltpu.SemaphoreType.DMA((2,))],
    )
    return pl.pallas_call(
        _gather_body,
        grid_spec=grid_spec,
        out_shape=jax.ShapeDtypeStruct((n, d), src.dtype),
        compiler_params=_cparams("arbitrary"),
        name=name,
    )(idx, jnp.reshape(nlive, (1,)).astype(jnp.int32), src)


def _moe_body(tg_ref, x_ref, g_ref, wr_ref, wgu_ref, wd_ref, o_ref, hn_ref, comb_ref):
    e = pl.program_id(1)
    grp = tg_ref[pl.program_id(0)]
    tm = x_ref.shape[0]
    lane = lax.broadcasted_iota(jnp.int32, (tm, LANES), 1)
    lanef = lane.astype(F32)
    live = grp < MOE_GROUPS

    @pl.when(e == 0)
    def _():
        o_ref[...] = x_ref[...]

    @pl.when((e == 0) & live)
    def _():
        hn = _rms(x_ref[...], g_ref[...])
        hn_ref[...] = hn.astype(BF16)
        logits = _router_logits(hn, wr_ref)
        is_grp = _group_lanes(lane)
        lg = jnp.where(is_grp, logits, -jnp.inf)
        mg = jnp.max(lg, axis=-1, keepdims=True)
        sg = jnp.sum(jnp.where(is_grp, jnp.exp(lg - mg), 0.0), axis=-1, keepdims=True)
        lt = jnp.sum(jnp.where(lane == MOE_EXPERTS + grp, logits, 0.0), axis=-1, keepdims=True)
        g_w = jnp.exp(lt - mg) / sg
        in_grp = (lane < MOE_EXPERTS) & (lane // MOE_EPG == grp)
        le = jnp.where(in_grp, logits, -jnp.inf)
        m1 = jnp.max(le, axis=-1, keepdims=True)
        i1 = jnp.min(jnp.where(le == m1, lanef, 1e9), axis=-1, keepdims=True)
        le2 = jnp.where(lanef == i1, -jnp.inf, le)
        m2 = jnp.max(le2, axis=-1, keepdims=True)
        i2 = jnp.min(jnp.where(le2 == m2, lanef, 1e9), axis=-1, keepdims=True)
        e2 = jnp.exp(m2 - m1)
        den = 1.0 + e2
        comb_ref[...] = (jnp.where(lanef == i1, g_w / den, 0.0)
                         + jnp.where(lanef == i2, g_w * e2 / den, 0.0))

    @pl.when(live)
    def _():
        a = jnp.dot(hn_ref[...], wgu_ref[0], preferred_element_type=F32)
        ce = jnp.sum(jnp.where(lane == grp * MOE_EPG + e, comb_ref[...], 0.0), axis=-1, keepdims=True)
        act = _silu(a[:, :MOE_FF]) * a[:, MOE_FF:] * ce
        o_ref[...] += _dot(act, wd_ref[0])


def _moe_plan(gid, tm):
    t = gid.shape[0]
    onehot = (gid[:, None] == jnp.arange(MOE_GROUPS, dtype=jnp.int32)).astype(jnp.int32)
    csum = jnp.cumsum(onehot, axis=0)
    counts = csum[-1]
    rank = (csum * onehot).sum(axis=1) - 1
    padded = (counts + tm - 1) // tm * tm
    pend = jnp.cumsum(padded)
    dest = (pend - padded)[gid] + rank
    r_pad = t + MOE_GROUPS * tm
    row_token = (jnp.arange(r_pad, dtype=jnp.int32) % t).at[dest].set(jnp.arange(t, dtype=jnp.int32))
    tile_start = jnp.arange(r_pad // tm, dtype=jnp.int32) * tm
    tile_group = (tile_start[:, None] >= pend[None, :]).sum(axis=1)
    return row_token, dest.astype(jnp.int32), tile_group.astype(jnp.int32), pend[-1]


def moe(x2d, g_ffn, w_grp, w_exp, w_gate, w_up, w_down, tm, name):
    m = x2d.shape[0]
    gf = g_ffn.reshape(1, -1)
    wr = jnp.concatenate([w_exp, w_grp,
                          jnp.zeros((D_MODEL, LANES - MOE_EXPERTS - MOE_GROUPS), F32)], axis=1)
    wgu = jnp.concatenate([w_gate, w_up], axis=2).astype(BF16)
    tr = min(1024, m)
    gid = pl.pallas_call(
        _route_body,
        grid=(m // tr,),
        in_specs=[pl.BlockSpec((tr, D_MODEL), lambda i: (i, 0)),
                  pl.BlockSpec((1, D_MODEL), lambda i: (0, 0)),
                  pl.BlockSpec((D_MODEL, LANES), lambda i: (0, 0))],
        out_specs=pl.BlockSpec((tr, LANES), lambda i: (i, 0)),
        out_shape=jax.ShapeDtypeStruct((m, LANES), jnp.int32),
        compiler_params=_cparams("arbitrary"),
        name=name + "_route",
    )(x2d, gf, wr)[:, 0]
    row_token, dest, tile_group, n_sorted = _moe_plan(gid, tm)
    rows_dma = min(MOE_GATHER_ROWS, tm)
    xs = gather_rows(x2d, row_token, n_sorted, rows_dma, name + "_gather")

    def expert(i, e, tg):
        return (jnp.minimum(tg[i], MOE_GROUPS - 1) * MOE_EPG + e, 0, 0)

    grid_spec = pltpu.PrefetchScalarGridSpec(
        num_scalar_prefetch=1,
        grid=(xs.shape[0] // tm, MOE_EPG),
        in_specs=[pl.BlockSpec((tm, D_MODEL), lambda i, e, tg: (i, 0)),
                  pl.BlockSpec((1, D_MODEL), lambda i, e, tg: (0, 0)),
                  pl.BlockSpec((D_MODEL, LANES), lambda i, e, tg: (0, 0)),
                  pl.BlockSpec((1, D_MODEL, 2 * MOE_FF), expert),
                  pl.BlockSpec((1, MOE_FF, D_MODEL), expert)],
        out_specs=pl.BlockSpec((tm, D_MODEL), lambda i, e, tg: (i, 0)),
        scratch_shapes=[pltpu.VMEM((tm, D_MODEL), BF16), pltpu.VMEM((tm, LANES), F32)],
    )
    ys = pl.pallas_call(
        _moe_body,
        grid_spec=grid_spec,
        out_shape=jax.ShapeDtypeStruct(xs.shape, F32),
        compiler_params=_cparams("arbitrary", "arbitrary"),
        name=name,
    )(tile_group, xs, gf, wr, wgu, w_down.astype(BF16))
    return gather_rows(ys, dest, jnp.int32(m), rows_dma, name + "_scatter")


def _ple_body(x_ref, p_ref, g_ref, wg_ref, wp_ref, gf_ref, o_ref, *, final):
    x = x_ref[...]
    gate = _sigmoid(_dot(_rms(x, g_ref[...]), wg_ref[...]))
    xn = x + _dot(p_ref[...], wp_ref[...]) * gate
    if final:
        xn = _rms(xn, gf_ref[...])
    o_ref[...] = xn


def ple(x2d, p2d, g_ple, w_gate, w_proj, g_final, final, tm, name):
    m = x2d.shape[0]
    return pl.pallas_call(
        functools.partial(_ple_body, final=final),
        grid=(m // tm,),
        in_specs=[pl.BlockSpec((tm, D_MODEL), lambda i: (i, 0)),
                  pl.BlockSpec((tm, PLE_DIM), lambda i: (i, 0)),
                  pl.BlockSpec((1, D_MODEL), lambda i: (0, 0)),
                  pl.BlockSpec((D_MODEL, D_MODEL), lambda i: (0, 0)),
                  pl.BlockSpec((PLE_DIM, D_MODEL), lambda i: (0, 0)),
                  pl.BlockSpec((1, D_MODEL), lambda i: (0, 0))],
        out_specs=pl.BlockSpec((tm, D_MODEL), lambda i: (i, 0)),
        out_shape=jax.ShapeDtypeStruct((m, D_MODEL), F32),
        compiler_params=_cparams("arbitrary"),
        name=name,
    )(x2d, p2d, g_ple.reshape(1, -1), w_gate.astype(BF16), w_proj.astype(BF16),
      g_final.reshape(1, -1))


def _cols(w, *names):
    return [w[:, _OFF[n][0]:_OFF[n][0] + _OFF[n][1]] for n in names]


def _padcols(w, width):
    return jnp.pad(w, ((0, 0), (0, width - w.shape[1])))


def _pad_q_heads(wq):
    hd = ATT_HEAD_DIM
    zero = jnp.zeros((wq.shape[0], hd), wq.dtype)
    cols = []
    for h in range(ATT_HEADS):
        blk = wq[:, h * hd:(h + 1) * hd] * (hd ** -0.5)
        cols += [blk, zero] if h // ATT_REP == 0 else [zero, blk]
    return jnp.concatenate(cols, axis=1)


def _split_w_in(w):
    swa_q, nsa_q = _cols(w, 'swa_q', 'nsa_q')
    w_att = jnp.concatenate([_pad_q_heads(swa_q), _pad_q_heads(nsa_q)] + _cols(w, *KV_BLK), axis=1)
    w_hgrn = jnp.concatenate(_cols(w, 'hgrn_q', 'hgrn_f', 'hgrn_i', 'hgrn_g'), axis=1)
    (xbc, dt, ngate, z) = _cols(w, 'ssm_xbc', 'ssm_dt', 'nsa_gate', 'ssm_z')
    w_ssm = jnp.concatenate([xbc, _padcols(dt, LANES), _padcols(ngate, LANES), z], axis=1)
    (w_mg,) = _cols(w, 'merge_gate')
    return [a.astype(BF16) for a in (w_att, w_hgrn, w_ssm, w_mg)]


def _mixers(i, att, hproj, sproj, attn_sinks, hgrn_lower_bounds, hgrn_norm_g, nsa_pos_k, nsa_pos_v,
            nsa_cmp_w1_k, nsa_cmp_w2_k, nsa_cmp_w1_v, nsa_cmp_w2_v, ssm_conv_w, ssm_conv_b,
            ssm_dt_bias, ssm_A_log, ssm_D, ssm_norm_g):
    bsz, seq, _ = att.shape
    kv = KV_BLK
    gate_blk = 7
    y_a = banded_attention(att, 0, kv['swa_k'], kv['swa_v'], window=SWA_WINDOW, sinks=attn_sinks[i],
                           name=f"swa{i}")
    y_b = hgrn2(hproj, hgrn_lower_bounds, hgrn_norm_g[i], i, name=f"hgrn{i}")
    ncp = seq // NSA_CMP_STRIDE

    def cmp_in(name):
        c0 = kv[name] * LANES
        return att[:, :, c0:c0 + LANES].reshape(bsz, ncp, NSA_CMP_STRIDE * LANES)

    kc = nsa_compress(cmp_in('nsa_k_cmp'), nsa_pos_k[i], nsa_cmp_w1_k[i], nsa_cmp_w2_k[i], name=f"cmpk{i}")
    vc = nsa_compress(cmp_in('nsa_v_cmp'), nsa_pos_v[i], nsa_cmp_w1_v[i], nsa_cmp_w2_v[i], name=f"cmpv{i}")
    y_c1, mask, act = nsa_cmp_select(att, 1, kc, vc, sproj, gate_blk, name=f"nsacmp{i}")
    y_c2 = nsa_selected(att, 1, kv['nsa_k_slc'], kv['nsa_v_slc'], mask, act, sproj, gate_blk,
                        name=f"nsasel{i}")
    y_c3 = banded_attention(att, 1, kv['nsa_k_win'], kv['nsa_v_win'], window=NSA_WINDOW, gate=sproj,
                            gate_blk=gate_blk, gate_col=2, name=f"nsawin{i}")
    y_d = mamba2(sproj, ssm_conv_w[i], ssm_conv_b[i], ssm_dt_bias[i], ssm_A_log[i], ssm_D[i],
                 ssm_norm_g[i], name=f"ssd{i}")
    return y_a, y_b, y_c1, y_c2, y_c3, y_d


def kernel(x, p, w_in, g_mix, attn_sinks, hgrn_lower_bounds, hgrn_norm_g, nsa_pos_k, nsa_pos_v,
           nsa_cmp_w1_k, nsa_cmp_w2_k, nsa_cmp_w1_v, nsa_cmp_w2_v, ssm_conv_w, ssm_conv_b,
           ssm_dt_bias, ssm_A_log, ssm_D, ssm_norm_g, w_branch, w_out, g_ffn, w_router_grp,
           w_router_exp, w_exp_gate, w_exp_up, w_exp_down, g_ple, w_ple_gate, w_ple_proj, g_final):
    bsz, seq, d = x.shape
    depth = w_in.shape[0]
    t = bsz * seq
    x2 = x.reshape(t, d)
    tm_proj = min(1024, t)
    tm_row = min(256, t)
    tm_moe = min(1024, t)
    for i in range(depth):
        w_att, w_hgrn, w_ssm, w_mg = _split_w_in(w_in[i])
        att = norm_mm(x2, g_mix[i], w_att, BF16, tm_proj, 512, f"proj_att{i}").reshape(bsz, seq, -1)
        hproj = norm_mm(x2, g_mix[i], w_hgrn, F32, tm_proj, 512, f"proj_hgrn{i}").reshape(bsz, seq, -1)
        sproj = norm_mm(x2, g_mix[i], w_ssm, F32, tm_proj, 512, f"proj_ssm{i}").reshape(bsz, seq, -1)
        ys = _mixers(i, att, hproj, sproj, attn_sinks, hgrn_lower_bounds, hgrn_norm_g, nsa_pos_k,
                     nsa_pos_v, nsa_cmp_w1_k, nsa_cmp_w2_k, nsa_cmp_w1_v, nsa_cmp_w2_v, ssm_conv_w,
                     ssm_conv_b, ssm_dt_bias, ssm_A_log, ssm_D, ssm_norm_g)
        ys = [y.reshape(t, -1) for y in ys]
        x2 = merge(*ys, x2, g_mix[i], w_mg, w_branch[i], w_out[i], tm_row, f"merge{i}")
        x2 = moe(x2, g_ffn[i], w_router_grp[i], w_router_exp[i], w_exp_gate[i], w_exp_up[i],
                 w_exp_down[i], tm_moe, f"moe{i}")
        x2 = ple(x2, p[i].reshape(t, -1), g_ple[i], w_ple_gate[i], w_ple_proj[i], g_final,
                 i == depth - 1, tm_row, f"ple{i}")
    return x2.reshape(bsz, seq, d)
```

```python
import functools

import numpy as np
import jax
import jax.numpy as jnp
from jax import lax
from jax.experimental import pallas as pl
from jax.experimental.pallas import tpu as pltpu

F32 = jnp.float32
BF16 = jnp.bfloat16
HIGHEST = lax.Precision.HIGHEST

D_MODEL = 1024
PLE_DIM = 256
NORM_EPS = 1e-6
NEG_INF = -1e30
REMOVED = -3e38
N_BRANCH = 4
BRANCH_WIDTH = 512
ATTN_BLOCK = 128

ATT_HEADS = 8
ATT_KV_HEADS = 2
ATT_HEAD_DIM = 64
ATT_REP = ATT_HEADS // ATT_KV_HEADS
SWA_WINDOW = 128
NSA_WINDOW = 512
NSA_CMP_LEN = 32
NSA_CMP_STRIDE = 16
NSA_CMP_HIDDEN = 256
NSA_SEL_LEN = 64
NSA_TOP_N = 16
NSA_FORCE_SCORE = 1e6
SEL_KEY_TILE = 512
SEL_FEATS = SEL_KEY_TILE // NSA_SEL_LEN
MASK_BIG = 1e30

HGRN_HEADS = 4
HGRN_DIM = 128
HGRN_CHUNK = 32
HGRN_MIN_F = 1e-6
HGRN_BLOCK = 256

SSM_HEADS = 8
SSM_HEAD_DIM = 64
SSM_GROUPS = 2
SSM_STATE = 64
SSM_CONV = 4
SSM_CHUNK = 64
SSM_INNER = 512
SSM_CONV_DIM = 768
SSM_BLOCK = 256

MOE_GROUPS = 4
MOE_EPG = 8
MOE_EXPERTS = 32
MOE_FF = 256
MOE_GATHER_ROWS = 512

LANES = 128
VMEM_LIMIT = 56 * 1024 * 1024

ALIBI_SLOPES = tuple(2.0 ** (-8.0 * (h + 1) / ATT_HEADS) for h in range(ATT_HEADS))

ATT_Q_WIDTH = ATT_HEADS * LANES
KV_BLK = {name: 2 * ATT_HEADS + j for j, name in enumerate(
    ('swa_k', 'swa_v', 'nsa_k_cmp', 'nsa_v_cmp', 'nsa_k_slc', 'nsa_v_slc', 'nsa_k_win', 'nsa_v_win'))}

_SPLITS = (
    ('swa_q', 512), ('swa_k', 128), ('swa_v', 128),
    ('hgrn_q', 512), ('hgrn_f', 512), ('hgrn_i', 512), ('hgrn_g', 512),
    ('nsa_q', 512), ('nsa_k_cmp', 128), ('nsa_v_cmp', 128), ('nsa_k_slc', 128),
    ('nsa_v_slc', 128), ('nsa_k_win', 128), ('nsa_v_win', 128), ('nsa_gate', 24),
    ('ssm_z', 512), ('ssm_xbc', 768), ('ssm_dt', 8), ('merge_gate', 4096),
)
_OFF = {}
_o = 0
for _n, _w in _SPLITS:
    _OFF[_n] = (_o, _w)
    _o += _w


def _cparams(*sem):
    return pltpu.CompilerParams(dimension_semantics=sem, vmem_limit_bytes=VMEM_LIMIT)


def _sigmoid(x):
    return 1.0 / (1.0 + jnp.exp(-x))


def _silu(x):
    return x * _sigmoid(x)


def _dot(a, b):
    return jnp.dot(a.astype(BF16), b.astype(BF16), preferred_element_type=F32)


def _dot_nt(a, b):
    return lax.dot_general(a.astype(BF16), b.astype(BF16), (((1,), (1,)), ((), ())),
                           preferred_element_type=F32)


def _dot_tn(a, b):
    return lax.dot_general(a.astype(BF16), b.astype(BF16), (((0,), (0,)), ((), ())),
                           preferred_element_type=F32)


def _rms(x, g):
    ms = jnp.mean(x * x, axis=-1, keepdims=True)
    return x * lax.rsqrt(ms + NORM_EPS) * g


def _norm_mm_body(x_ref, g_ref, w_ref, o_ref, hn_ref):
    @pl.when(pl.program_id(1) == 0)
    def _():
        hn_ref[...] = _rms(x_ref[...], g_ref[...]).astype(BF16)

    o_ref[...] = jnp.dot(hn_ref[...], w_ref[...], preferred_element_type=F32).astype(o_ref.dtype)


def norm_mm(x2d, g, w, out_dtype, tm, tn, name):
    m, k = x2d.shape
    n = w.shape[1]
    return pl.pallas_call(
        _norm_mm_body,
        grid=(m // tm, n // tn),
        in_specs=[pl.BlockSpec((tm, k), lambda i, j: (i, 0)),
                  pl.BlockSpec((1, k), lambda i, j: (0, 0)),
                  pl.BlockSpec((k, tn), lambda i, j: (0, j))],
        out_specs=pl.BlockSpec((tm, tn), lambda i, j: (i, j)),
        out_shape=jax.ShapeDtypeStruct((m, n), out_dtype),
        scratch_shapes=[pltpu.VMEM((tm, k), BF16)],
        compiler_params=_cparams("arbitrary", "arbitrary"),
        name=name,
    )(x2d, g.reshape(1, k), w)


def _ones_row(g):
    return ATT_HEAD_DIM if g == 0 else 0


def _vt_body(v_ref, o_ref):
    vt = jnp.transpose(v_ref[0].astype(F32))
    rowid = lax.broadcasted_iota(jnp.int32, vt.shape, 0)
    for g in range(ATT_KV_HEADS):
        aug = jnp.where(rowid // ATT_HEAD_DIM == g, vt, jnp.where(rowid == _ones_row(g), 1.0, 0.0))
        o_ref[0, g] = aug.astype(BF16)


def v_transposed(arr, v_blk, name):
    bsz, seq, _ = arr.shape
    rows = min(2048, seq)
    return pl.pallas_call(
        _vt_body,
        grid=(bsz, seq // rows),
        in_specs=[pl.BlockSpec((1, rows, LANES), lambda b, n: (b, n, v_blk))],
        out_specs=pl.BlockSpec((1, ATT_KV_HEADS, LANES, rows), lambda b, n: (b, 0, 0, n)),
        out_shape=jax.ShapeDtypeStruct((bsz, ATT_KV_HEADS, LANES, seq), BF16),
        compiler_params=_cparams("arbitrary", "arbitrary"),
        name=name,
    )(arr)


def _q_stack(q_ref, g):
    return jnp.concatenate([q_ref[0, :, (g * ATT_REP + r) * LANES:(g * ATT_REP + r + 1) * LANES]
                            for r in range(ATT_REP)], axis=0)


def _head(x, r):
    return x[:, r * ATTN_BLOCK:(r + 1) * ATTN_BLOCK]


def _banded_body(*refs, window, nprev, use_sink, gate_col):
    if use_sink:
        q_ref, k_ref, vt_ref, sink_ref, o_ref, ot_ref = refs
    else:
        q_ref, k_ref, vt_ref, gate_ref, o_ref, ot_ref = refs
    blk = ATTN_BLOCK
    hd = ATT_HEAD_DIM
    n = pl.program_id(1)
    nk = (nprev + 1) * blk
    start = pl.multiple_of(jnp.maximum(n - nprev, 0) * blk, blk)
    k128 = k_ref[0, pl.ds(start, nk), :]
    krow = lax.broadcasted_iota(jnp.int32, (nk, blk), 0)
    qcol = lax.broadcasted_iota(jnp.int32, (nk, blk), 1)
    rel = (n * blk + qcol) - (start + krow)
    negrel = jnp.where((rel >= 0) & (rel < window), -rel.astype(F32), NEG_INF)
    if not use_sink:
        gate_t = jnp.transpose(_sigmoid(gate_ref[0]))
    st4s = [_dot_nt(k128, _q_stack(q_ref, g)) for g in range(ATT_KV_HEADS)]
    for g in range(ATT_KV_HEADS):
        one = _ones_row(g)
        st4 = st4s[g]
        ps, ms = [], []
        for r in range(ATT_REP):
            h = g * ATT_REP + r
            st = _head(st4, r) + ALIBI_SLOPES[h] * negrel
            m = jnp.max(st, axis=0, keepdims=True)
            if use_sink:
                m = jnp.maximum(m, sink_ref[h])
            ps.append(jnp.exp(st - m).astype(BF16))
            ms.append(m)
        acc4 = jnp.dot(vt_ref[0, g, :, pl.ds(start, nk)], jnp.concatenate(ps, axis=1),
                       preferred_element_type=F32)
        for r in range(ATT_REP):
            h = g * ATT_REP + r
            acc = _head(acc4, r)
            l = acc[one:one + 1]
            if use_sink:
                l = l + jnp.exp(sink_ref[h] - ms[r])
            ot = acc[g * hd:(g + 1) * hd] / l
            if not use_sink:
                c = 3 * h + gate_col
                ot = ot * gate_t[c:c + 1]
            ot_ref[h * hd:(h + 1) * hd, :] = ot
    o_ref[0] = jnp.transpose(ot_ref[...])


def banded_attention(att, q_blk, k_blk, v_blk, *, window, sinks=None, gate=None, gate_blk=None,
                     gate_col=0, name):
    bsz, seq, _ = att.shape
    blk = ATTN_BLOCK
    nprev = (window - 1 + blk - 1) // blk
    use_sink = sinks is not None
    vt = v_transposed(att, v_blk, name + "_vt")
    in_specs = [pl.BlockSpec((1, blk, ATT_Q_WIDTH), lambda b, n: (b, n, q_blk)),
                pl.BlockSpec((1, seq, LANES), lambda b, n: (b, 0, k_blk)),
                pl.BlockSpec((1, ATT_KV_HEADS, LANES, seq), lambda b, n: (b, 0, 0, 0))]
    if use_sink:
        in_specs.append(pl.BlockSpec(memory_space=pltpu.SMEM))
        extra = sinks
    else:
        in_specs.append(pl.BlockSpec((1, blk, LANES), lambda b, n: (b, n, gate_blk)))
        extra = gate
    return pl.pallas_call(
        functools.partial(_banded_body, window=window, nprev=nprev, use_sink=use_sink,
                          gate_col=gate_col),
        grid=(bsz, seq // blk),
        in_specs=in_specs,
        out_specs=pl.BlockSpec((1, blk, 512), lambda b, n: (b, n, 0)),
        out_shape=jax.ShapeDtypeStruct((bsz, seq, 512), F32),
        scratch_shapes=[pltpu.VMEM((512, blk), F32)],
        compiler_params=_cparams("arbitrary", "arbitrary"),
        name=name,
    )(att, att, vt, extra)


def _compress_body(x_ref, w1a_ref, w1b_ref, w1_ref, pos_ref, w2_ref, o_ref):
    x = x_ref[0]
    p = jnp.dot(x, w1a_ref[...], preferred_element_type=F32)
    q = jnp.dot(x, w1b_ref[...], preferred_element_type=F32)
    ncp = x.shape[0]
    q = pltpu.roll(q, shift=ncp - 1, axis=0)
    posb = jnp.broadcast_to(pos_ref[...], (8, pos_ref.shape[1]))
    bias = _dot(posb, w1_ref[...])[0:1]
    hid = NSA_CMP_HIDDEN
    outs = []
    for g in range(ATT_KV_HEADS):
        pre = p[:, g * hid:(g + 1) * hid] + q[:, g * hid:(g + 1) * hid] + bias
        outs.append(_dot(jax.nn.gelu(pre), w2_ref[...]))
    o_ref[0] = jnp.concatenate(outs, axis=1).astype(o_ref.dtype)


def _expand_w1(w1, half):
    hd, hid, ng = ATT_HEAD_DIM, NSA_CMP_HIDDEN, ATT_KV_HEADS
    w = w1.reshape(NSA_CMP_LEN, hd, hid)[half * 16:(half + 1) * 16]
    eye = jnp.eye(ng, dtype=w1.dtype)
    out = jnp.einsum('ldj,gh->lgdhj', w, eye)
    return out.reshape(16 * ng * hd, ng * hid)


def nsa_compress(x16, pos, w1, w2, name):
    bsz, ncp, wid = x16.shape
    hid = NSA_CMP_HIDDEN
    w1a = _expand_w1(w1, 0).astype(BF16)
    w1b = _expand_w1(w1, 1).astype(BF16)
    return pl.pallas_call(
        _compress_body,
        grid=(bsz,),
        in_specs=[pl.BlockSpec((1, ncp, wid), lambda b: (b, 0, 0)),
                  pl.BlockSpec((wid, 2 * hid), lambda b: (0, 0)),
                  pl.BlockSpec((wid, 2 * hid), lambda b: (0, 0)),
                  pl.BlockSpec((NSA_CMP_LEN * ATT_HEAD_DIM, hid), lambda b: (0, 0)),
                  pl.BlockSpec((1, NSA_CMP_LEN * ATT_HEAD_DIM), lambda b: (0, 0)),
                  pl.BlockSpec((hid, ATT_HEAD_DIM), lambda b: (0, 0))],
        out_specs=pl.BlockSpec((1, ncp, LANES), lambda b: (b, 0, 0)),
        out_shape=jax.ShapeDtypeStruct((bsz, ncp, LANES), BF16),
        compiler_params=_cparams("arbitrary"),
        name=name,
    )(x16, w1a, w1b, w1.astype(BF16), pos.reshape(1, -1), w2.astype(BF16))


def _nsa_cmp_body(q_ref, kc_ref, vct_ref, ovt_ref, gate_ref, o_ref, m_ref, act_ref, ot_ref, *, top_n):
    blk = ATTN_BLOCK
    hd = ATT_HEAD_DIM
    n = pl.program_id(1)
    kc = kc_ref[0]
    ncp = kc.shape[0]
    ns = ovt_ref.shape[0]
    crow = lax.broadcasted_iota(jnp.int32, (ncp, blk), 0)
    qcol = lax.broadcasted_iota(jnp.int32, (ncp, blk), 1)
    dist = (n * blk + qcol) - (crow * NSA_CMP_STRIDE + (NSA_CMP_LEN - 1))
    valid = dist >= 0
    negd = jnp.where(valid, -dist.astype(F32), NEG_INF)
    jrow = lax.broadcasted_iota(jnp.int32, (ns, blk), 0)
    qpos = n * blk + lax.broadcasted_iota(jnp.int32, (ns, blk), 1)
    cur = qpos // NSA_SEL_LEN
    forced = (jrow == 0) | (jrow == cur) | (jrow == cur - 1)
    causal_blk = jrow * NSA_SEL_LEN <= qpos
    jrowf = jrow.astype(F32)
    gate_t = jnp.transpose(_sigmoid(gate_ref[0]))
    ones8 = jnp.ones((8, blk), BF16)
    st4s = [_dot_nt(kc, _q_stack(q_ref, g)) for g in range(ATT_KV_HEADS)]
    for g in range(ATT_KV_HEADS):
        one = _ones_row(g)
        st4 = st4s[g]
        es = []
        for r in range(ATT_REP):
            st = _head(st4, r) + ALIBI_SLOPES[g * ATT_REP + r] * negd
            m = jnp.max(st, axis=0, keepdims=True)
            es.append(jnp.where(valid, jnp.exp(st - m), 0.0))
        acc4 = jnp.dot(vct_ref[0, g], jnp.concatenate([e.astype(BF16) for e in es], axis=1),
                       preferred_element_type=F32)
        psum = jnp.zeros((ncp, blk), F32)
        for r in range(ATT_REP):
            h = g * ATT_REP + r
            acc = _head(acc4, r)
            l = acc[one:one + 1]
            inv = jnp.where(l > 0.0, 1.0 / jnp.where(l > 0.0, l, 1.0), 0.0)
            ot_ref[h * hd:(h + 1) * hd, :] = acc[g * hd:(g + 1) * hd] * inv * gate_t[3 * h:3 * h + 1]
            psum = psum + es[r] * inv
        imp = _dot(ovt_ref[...], psum)
        score = jnp.where(forced, NSA_FORCE_SCORE, jnp.where(causal_blk, imp, NEG_INF))
        sel = jnp.zeros((ns, blk), F32)
        for _ in range(top_n):
            mx = jnp.max(score, axis=0, keepdims=True)
            idx = jnp.min(jnp.where(score == mx, jrowf, float(ns)), axis=0, keepdims=True)
            hit = jrowf == idx
            sel = jnp.where(hit, 1.0, sel)
            score = jnp.where(hit, REMOVED, score)
        m_ref[0, g] = sel
        act_ref[0, 0, g * 8:(g + 1) * 8, :] = _dot_nt(ones8, sel)
    o_ref[0] = jnp.transpose(ot_ref[...])


def _overlap_matrix_t(seq):
    n_cmp = seq // NSA_CMP_STRIDE
    n_sel = seq // NSA_SEL_LEN
    cs = np.arange(n_cmp)[None, :] * NSA_CMP_STRIDE
    ss = np.arange(n_sel)[:, None] * NSA_SEL_LEN
    ov = np.clip(np.minimum(cs + NSA_CMP_LEN, ss + NSA_SEL_LEN) - np.maximum(cs, ss), 0, None)
    return jnp.asarray(ov / NSA_CMP_LEN, dtype=BF16)


def nsa_cmp_select(att, q_blk, kc, vc, gate, gate_blk, name):
    bsz, seq, _ = att.shape
    blk = ATTN_BLOCK
    nb = seq // blk
    ncp = seq // NSA_CMP_STRIDE
    ns = seq // NSA_SEL_LEN
    top_n = min(NSA_TOP_N, ns)
    vct = v_transposed(vc, 0, name + "_vt")
    return pl.pallas_call(
        functools.partial(_nsa_cmp_body, top_n=top_n),
        grid=(bsz, nb),
        in_specs=[pl.BlockSpec((1, blk, ATT_Q_WIDTH), lambda b, n: (b, n, q_blk)),
                  pl.BlockSpec((1, ncp, LANES), lambda b, n: (b, 0, 0)),
                  pl.BlockSpec((1, ATT_KV_HEADS, LANES, ncp), lambda b, n: (b, 0, 0, 0)),
                  pl.BlockSpec((ns, ncp), lambda b, n: (0, 0)),
                  pl.BlockSpec((1, blk, LANES), lambda b, n: (b, n, gate_blk))],
        out_specs=[pl.BlockSpec((1, blk, 512), lambda b, n: (b, n, 0)),
                   pl.BlockSpec((1, ATT_KV_HEADS, ns, blk), lambda b, n: (b, 0, 0, n)),
                   pl.BlockSpec((1, 1, ATT_KV_HEADS * 8, ns), lambda b, n: (b, n, 0, 0))],
        out_shape=[jax.ShapeDtypeStruct((bsz, seq, 512), F32),
                   jax.ShapeDtypeStruct((bsz, ATT_KV_HEADS, ns, seq), F32),
                   jax.ShapeDtypeStruct((bsz, nb, ATT_KV_HEADS * 8, ns), F32)],
        scratch_shapes=[pltpu.VMEM((512, blk), F32)],
        compiler_params=_cparams("arbitrary", "arbitrary"),
        name=name,
    )(att, kc, vct, _overlap_matrix_t(seq), gate)


def _k_aug_body(k_ref, o_ref):
    k = k_ref[0].astype(F32)
    rows = k.shape[0]
    kin = lax.broadcasted_iota(jnp.int32, (rows, LANES), 0) % SEL_KEY_TILE
    lane = lax.broadcasted_iota(jnp.int32, (rows, LANES), 1)
    for g in range(ATT_KV_HEADS):
        f = lane - (ATT_HEAD_DIM if g == 0 else 0)
        feat = jnp.where(f == kin // NSA_SEL_LEN, 1.0, 0.0)
        feat = jnp.where(f == SEL_FEATS, (kin % 256).astype(F32), feat)
        feat = jnp.where(f == SEL_FEATS + 1, (kin // 256 * 256).astype(F32), feat)
        feat = jnp.where((f == SEL_FEATS + 2) | (f == SEL_FEATS + 3), 1.0, feat)
        o_ref[0, g] = jnp.where(lane // ATT_HEAD_DIM == g, k, feat).astype(BF16)


def k_augmented(arr, k_blk, name):
    bsz, seq, _ = arr.shape
    rows = min(2048, seq)
    return pl.pallas_call(
        _k_aug_body,
        grid=(bsz, seq // rows),
        in_specs=[pl.BlockSpec((1, rows, LANES), lambda b, n: (b, n, k_blk))],
        out_specs=pl.BlockSpec((1, ATT_KV_HEADS, rows, LANES), lambda b, n: (b, 0, n, 0)),
        out_shape=jax.ShapeDtypeStruct((bsz, ATT_KV_HEADS, seq, LANES), BF16),
        compiler_params=_cparams("arbitrary", "arbitrary"),
        name=name,
    )(arr)


def _query_feature_rows():
    out = np.zeros((ATT_KV_HEADS, 2 * SEL_FEATS, ATT_REP * ATTN_BLOCK), np.float32)
    qin = np.arange(ATTN_BLOCK, dtype=np.float32)
    for g in range(ATT_KV_HEADS):
        for r in range(ATT_REP):
            slope = ALIBI_SLOPES[g * ATT_REP + r]
            cols = slice(r * ATTN_BLOCK, (r + 1) * ATTN_BLOCK)
            out[g, 0, cols] = slope
            out[g, 1, cols] = slope
            out[g, 3, cols] = -slope * qin
            out[g, SEL_FEATS + 2, cols] = -slope
    return jnp.asarray(out)


def _nsa_sel_body(tiles_ref, cnt_ref, q_ref, k_ref, vt_ref, m_ref, gate_ref, fq_ref, o_ref, ot_ref,
                  acc_ref, mx_ref, dq_ref, sa_ref, sb_ref, *, ntl):
    blk = ATTN_BLOCK
    hd = ATT_HEAD_DIM
    tk = SEL_KEY_TILE
    sl = NSA_SEL_LEN
    per = tk // sl
    b = pl.program_id(0)
    n = pl.program_id(1)
    nb = pl.num_programs(1)
    krow = lax.broadcasted_iota(jnp.int32, (tk, blk), 0)
    qcol = lax.broadcasted_iota(jnp.int32, (tk, blk), 1)
    dq_ref[...] = (qcol - krow).astype(F32)
    gate_t = jnp.transpose(_sigmoid(gate_ref[0]))
    pad_rows = jnp.zeros((hd - 2 * SEL_FEATS, ATT_REP * blk), BF16)
    for g in range(ATT_KV_HEADS):
        one = _ones_row(g)
        lrow = (b * nb + n) * ATT_KV_HEADS + g
        qt = jnp.concatenate(
            [jnp.transpose(q_ref[0, :, (g * ATT_REP + r) * LANES:(g * ATT_REP + r + 1) * LANES]
                           .astype(F32))[g * hd:(g + 1) * hd] for r in range(ATT_REP)],
            axis=1).astype(BF16)
        acc_ref[...] = jnp.zeros_like(acc_ref)
        mx_ref[...] = jnp.full_like(mx_ref, NEG_INF)

        def tile_scores(j, live):
            t = tiles_ref[lrow * ntl + j]
            base = pl.multiple_of(t * tk, tk)
            sel8 = m_ref[0, g, pl.ds(pl.multiple_of(t * per, per), per), :]
            if live is not None:
                sel8 = jnp.where(live, sel8, 0.0)
            off = (n * blk - base).astype(F32)
            mask_rows = jnp.concatenate([(sel8 - 1.0) * MASK_BIG] * ATT_REP, axis=1)
            alibi_rows = fq_ref[g, 0:SEL_FEATS, :] + off * fq_ref[g, SEL_FEATS:2 * SEL_FEATS, :]
            feats = jnp.concatenate([mask_rows, alibi_rows], axis=0).astype(BF16)
            rhs = jnp.concatenate([qt, feats, pad_rows] if g == 0 else [feats, pad_rows, qt], axis=0)
            return jnp.dot(k_ref[0, g, pl.ds(base, tk), :], rhs,
                           preferred_element_type=F32)

        def tile_finish(j, st4, diag):
            base = pl.multiple_of(tiles_ref[lrow * ntl + j] * tk, tk)
            off = (n * blk - base).astype(F32)
            if diag:
                causal = (dq_ref[...] + off) >= 0.0
            ps, ms = [], []
            for r in range(ATT_REP):
                st = _head(st4, r)
                if diag:
                    st = jnp.where(causal, st, NEG_INF)
                m = jnp.max(st, axis=0, keepdims=True)
                ps.append(jnp.exp(st - m).astype(BF16))
                ms.append(m)
            acc = jnp.dot(vt_ref[0, g, :, pl.ds(base, tk)], jnp.concatenate(ps, axis=1),
                          preferred_element_type=F32)
            return jnp.concatenate(ms, axis=1), acc

        cnt = cnt_ref[lrow]

        def scores_into(s_ref, j):
            s_ref[...] = tile_scores(jnp.minimum(j, ntl - 1), j < cnt)

        def finish_from(s_ref, j, diag):
            m_t, acc_t = tile_finish(jnp.minimum(j, ntl - 1), s_ref[...], diag)
            m_old = mx_ref[...]
            m_new = jnp.maximum(m_old, m_t)
            acc_ref[...] = jnp.exp(m_old - m_new) * acc_ref[...] + jnp.exp(m_t - m_new) * acc_t
            mx_ref[...] = m_new

        def two_tiles(k, diag):
            scores_into(sb_ref, 2 * k + 1)
            finish_from(sa_ref, 2 * k, diag)
            scores_into(sa_ref, 2 * k + 2)
            finish_from(sb_ref, 2 * k + 1, False)

        scores_into(sa_ref, 0)
        two_tiles(0, True)

        def later(k, carry):
            two_tiles(k, False)
            return carry

        lax.fori_loop(1, (cnt + 1) // 2, later, 0)
        acc4 = acc_ref[...]
        for r in range(ATT_REP):
            h = g * ATT_REP + r
            acc = _head(acc4, r)
            ot = acc[g * hd:(g + 1) * hd] / acc[one:one + 1]
            ot_ref[h * hd:(h + 1) * hd, :] = ot * gate_t[3 * h + 1:3 * h + 2]
    o_ref[0] = jnp.transpose(ot_ref[...])


def _tile_lists(act, seq):
    bsz, nb = act.shape[:2]
    ns = act.shape[-1]
    tk = SEL_KEY_TILE
    ntl = seq // tk
    per = tk // NSA_SEL_LEN
    cnt_blk = act.reshape(bsz, nb, ATT_KV_HEADS, 8, ns)[:, :, :, 0, :]
    hit = cnt_blk.reshape(bsz, nb, ATT_KV_HEADS, ntl, per).sum(-1) > 0.5
    tidx = jnp.arange(ntl, dtype=jnp.int32)
    diag = (jnp.arange(nb, dtype=jnp.int32) * ATTN_BLOCK + ATTN_BLOCK - 1) // tk
    active = hit & (tidx[None, None, None, :] <= diag[None, :, None, None])
    act_i = active.astype(jnp.int32)
    rank = jnp.cumsum(act_i[..., ::-1], axis=-1)[..., ::-1] - 1
    slot = (active[..., :, None] & (rank[..., :, None] == tidx)).astype(jnp.int32)
    tiles = (slot * tidx[:, None]).sum(-2).reshape(-1)
    cnt = act_i.sum(-1).reshape(-1)
    return tiles, cnt, ntl


def nsa_selected(att, q_blk, k_blk, v_blk, mask, act, gate, gate_blk, name):
    bsz, seq, _ = att.shape
    blk = ATTN_BLOCK
    ns = seq // NSA_SEL_LEN
    tiles, cnt, ntl = _tile_lists(act, seq)
    vt = v_transposed(att, v_blk, name + "_vt")
    kaug = k_augmented(att, k_blk, name + "_k")
    grid_spec = pltpu.PrefetchScalarGridSpec(
        num_scalar_prefetch=2,
        grid=(bsz, seq // blk),
        in_specs=[pl.BlockSpec((1, blk, ATT_Q_WIDTH), lambda b, n, *_: (b, n, q_blk)),
                  pl.BlockSpec((1, ATT_KV_HEADS, seq, LANES), lambda b, n, *_: (b, 0, 0, 0)),
                  pl.BlockSpec((1, ATT_KV_HEADS, LANES, seq), lambda b, n, *_: (b, 0, 0, 0)),
                  pl.BlockSpec((1, ATT_KV_HEADS, ns, blk), lambda b, n, *_: (b, 0, 0, n)),
                  pl.BlockSpec((1, blk, LANES), lambda b, n, *_: (b, n, gate_blk)),
                  pl.BlockSpec((ATT_KV_HEADS, 2 * SEL_FEATS, ATT_REP * blk), lambda b, n, *_: (0, 0, 0))],
        out_specs=pl.BlockSpec((1, blk, 512), lambda b, n, *_: (b, n, 0)),
        scratch_shapes=[pltpu.VMEM((512, blk), F32),
                        pltpu.VMEM((LANES, ATT_REP * blk), F32),
                        pltpu.VMEM((1, ATT_REP * blk), F32),
                        pltpu.VMEM((SEL_KEY_TILE, blk), F32),
                        pltpu.VMEM((SEL_KEY_TILE, ATT_REP * blk), F32),
                        pltpu.VMEM((SEL_KEY_TILE, ATT_REP * blk), F32)],
    )
    return pl.pallas_call(
        functools.partial(_nsa_sel_body, ntl=ntl),
        grid_spec=grid_spec,
        out_shape=jax.ShapeDtypeStruct((bsz, seq, 512), F32),
        compiler_params=_cparams("arbitrary", "arbitrary"),
        name=name,
    )(tiles, cnt, att, kaug, vt, mask, gate, _query_feature_rows())


def _hgrn_body(q_ref, f_ref, i_ref, g_ref, lbp_ref, ng_ref, o_ref, st_ref, b_ref, k_ref, *, layer):
    blk = HGRN_BLOCK
    ch = HGRN_CHUNK
    dk = HGRN_DIM

    @pl.when(pl.program_id(1) == 0)
    def _():
        st_ref[...] = jnp.zeros_like(st_ref)

    lbp = lbp_ref[...]
    e = jnp.exp(lbp - jnp.max(lbp, axis=0, keepdims=True))
    sm = e / jnp.sum(e, axis=0, keepdims=True)
    lb = jnp.zeros((1, lbp.shape[1]), F32)
    for d in range(1, layer + 1):
        lb = lb + sm[d:d + 1]
    z = f_ref[0]
    f = lb + (1.0 - lb) * _sigmoid(z)
    logf = jnp.log(jnp.maximum(f, HGRN_MIN_F))
    k_ref[...] = (1.0 - lb) * _sigmoid(-z)
    tr = lax.broadcasted_iota(jnp.int32, (blk, blk), 0)
    tc = lax.broadcasted_iota(jnp.int32, (blk, blk), 1)
    tri = jnp.where((tr // ch == tc // ch) & (tc <= tr), 1.0, 0.0).astype(F32)
    b_ref[...] = jnp.dot(tri, logf, precision=HIGHEST, preferred_element_type=F32)
    cr = lax.broadcasted_iota(jnp.int32, (ch, ch), 0)
    cc = lax.broadcasted_iota(jnp.int32, (ch, ch), 1)
    causal = cc <= cr
    ng = ng_ref[...]

    for c in range(blk // ch):
        r0 = c * ch
        bc = b_ref[pl.ds(r0, ch), :]
        qc = q_ref[0, pl.ds(r0, ch), :]
        kc = k_ref[pl.ds(r0, ch), :]
        vc = i_ref[0, pl.ds(r0, ch), :]
        gc = g_ref[0, pl.ds(r0, ch), :]
        b_mid = bc[ch // 2:ch // 2 + 1]
        b_last = bc[ch - 1:ch]
        qa = qc * jnp.exp(bc - b_mid)
        ka = kc * jnp.exp(b_mid - bc)
        qe = qc * jnp.exp(bc)
        kl = kc * jnp.exp(b_last - bc)
        dec = jnp.exp(b_last)
        for h in range(HGRN_HEADS):
            sl = slice(h * dk, (h + 1) * dk)
            a = jnp.where(causal, _dot_nt(qa[:, sl], ka[:, sl]), 0.0)
            st = st_ref[h]
            o = _dot(a, vc[:, sl]) + _dot_nt(qe[:, sl], st)
            st_ref[h] = st * dec[:, sl] + _dot_tn(vc[:, sl], kl[:, sl])
            o = _rms(o, ng[:, sl]) * _silu(gc[:, sl])
            o_ref[0, pl.ds(r0, ch), sl] = o


def hgrn2(hproj, lower_bounds, norm_g, layer, name):
    bsz, seq, _ = hproj.shape
    blk = HGRN_BLOCK
    wid = HGRN_HEADS * HGRN_DIM
    depth = lower_bounds.shape[0]

    def col(j):
        return pl.BlockSpec((1, blk, wid), lambda b, n: (b, n, j))

    return pl.pallas_call(
        functools.partial(_hgrn_body, layer=layer),
        grid=(bsz, seq // blk),
        in_specs=[col(0), col(1), col(2), col(3),
                  pl.BlockSpec((depth, wid), lambda b, n: (0, 0)),
                  pl.BlockSpec((1, wid), lambda b, n: (0, 0))],
        out_specs=pl.BlockSpec((1, blk, wid), lambda b, n: (b, n, 0)),
        out_shape=jax.ShapeDtypeStruct((bsz, seq, wid), F32),
        scratch_shapes=[pltpu.VMEM((HGRN_HEADS, HGRN_DIM, HGRN_DIM), F32),
                        pltpu.VMEM((blk, wid), F32),
                        pltpu.VMEM((blk, wid), F32)],
        compiler_params=_cparams("arbitrary", "arbitrary"),
        name=name,
    )(hproj, hproj, hproj, hproj, lower_bounds, norm_g.reshape(1, wid))


def _ssd_body(xbc_ref, dt_ref, z_ref, cw_ref, cb_ref, dtb_ref, alog_ref, dskip_ref, ng_ref,
              o_ref, xp_ref, act_ref, st_ref, y_ref):
    blk = SSM_BLOCK
    ch = SSM_CHUNK
    hp = SSM_HEAD_DIM
    ns = SSM_STATE
    rep = SSM_HEADS // SSM_GROUPS
    pad = 8

    @pl.when(pl.program_id(1) == 0)
    def _():
        xp_ref[0:pad, :] = jnp.zeros((pad, SSM_CONV_DIM), F32)
        st_ref[...] = jnp.zeros_like(st_ref)

    xin = xbc_ref[0]
    xp_ref[pad:pad + blk, :] = xin
    cw = cw_ref[...]
    conv = cb_ref[...] + cw[SSM_CONV - 1:SSM_CONV] * xin
    for j in range(SSM_CONV - 1):
        shift = SSM_CONV - 1 - j
        conv = conv + cw[j:j + 1] * xp_ref[pl.ds(pad - shift, blk), :]
    xp_ref[0:pad, :] = xin[blk - pad:blk]
    act_ref[...] = _silu(conv)

    dt_raw = dt_ref[0] + dtb_ref[...]
    dt = jnp.maximum(dt_raw, 0.0) + jnp.log(1.0 + jnp.exp(-jnp.abs(dt_raw)))
    a_all = dt * (-jnp.exp(alog_ref[...]))
    cr = lax.broadcasted_iota(jnp.int32, (ch, ch), 0)
    cc = lax.broadcasted_iota(jnp.int32, (ch, ch), 1)
    causal = cc <= cr
    tri = jnp.where(causal, 1.0, 0.0).astype(F32)
    tri_t = jnp.where(cr <= cc, 1.0, 0.0).astype(F32)

    for c in range(blk // ch):
        r0 = c * ch
        dtc = dt[r0:r0 + ch]
        ac = a_all[r0:r0 + ch]
        acs = jnp.dot(tri, ac, precision=HIGHEST, preferred_element_type=F32)
        acs_t = lax.dot_general(ac, tri_t, (((0,), (0,)), ((), ())), precision=HIGHEST,
                                preferred_element_type=F32)
        for g in range(SSM_GROUPS):
            bm = act_ref[r0:r0 + ch, SSM_INNER + g * ns:SSM_INNER + (g + 1) * ns]
            cm = act_ref[r0:r0 + ch, SSM_INNER + SSM_GROUPS * ns + g * ns:
                         SSM_INNER + SSM_GROUPS * ns + (g + 1) * ns]
            cb = _dot_nt(cm, bm)
            for r in range(rep):
                h = g * rep + r
                xh = act_ref[r0:r0 + ch, h * hp:(h + 1) * hp]
                dth = dtc[:, h:h + 1]
                acs_h = acs[:, h:h + 1]
                lmat = jnp.where(causal, jnp.exp(acs_h - acs_t[h:h + 1, :]), 0.0)
                acs_last = acs[ch - 1:ch, h:h + 1]
                st = st_ref[h]
                y = _dot(cb * lmat, dth * xh)
                y = y + jnp.exp(acs_h) * _dot(cm, st)
                y = y + dskip_ref[h] * xh
                bw = bm * (jnp.exp(acs_last - acs_h) * dth)
                st_ref[h] = jnp.exp(acs_last) * st + _dot_tn(bw, xh)
                y_ref[r0:r0 + ch, h * hp:(h + 1) * hp] = y
    yz = y_ref[...] * _silu(z_ref[0])
    o_ref[0] = _rms(yz, ng_ref[...])


def mamba2(sproj, conv_w, conv_b, dt_bias, a_log, d_skip, norm_g, name):
    bsz, seq, _ = sproj.shape
    blk = SSM_BLOCK
    padh = LANES - SSM_HEADS
    dtb = jnp.pad(dt_bias, (0, padh)).reshape(1, LANES)
    alog = jnp.pad(a_log, (0, padh)).reshape(1, LANES)
    return pl.pallas_call(
        _ssd_body,
        grid=(bsz, seq // blk),
        in_specs=[pl.BlockSpec((1, blk, SSM_CONV_DIM), lambda b, n: (b, n, 0)),
                  pl.BlockSpec((1, blk, LANES), lambda b, n: (b, n, 6)),
                  pl.BlockSpec((1, blk, SSM_INNER), lambda b, n: (b, n, 2)),
                  pl.BlockSpec((SSM_CONV, SSM_CONV_DIM), lambda b, n: (0, 0)),
                  pl.BlockSpec((1, SSM_CONV_DIM), lambda b, n: (0, 0)),
                  pl.BlockSpec((1, LANES), lambda b, n: (0, 0)),
                  pl.BlockSpec((1, LANES), lambda b, n: (0, 0)),
                  pl.BlockSpec(memory_space=pltpu.SMEM),
                  pl.BlockSpec((1, SSM_INNER), lambda b, n: (0, 0))],
        out_specs=pl.BlockSpec((1, blk, SSM_INNER), lambda b, n: (b, n, 0)),
        out_shape=jax.ShapeDtypeStruct((bsz, seq, SSM_INNER), F32),
        scratch_shapes=[pltpu.VMEM((blk + 8, SSM_CONV_DIM), F32),
                        pltpu.VMEM((blk, SSM_CONV_DIM), F32),
                        pltpu.VMEM((SSM_HEADS, SSM_STATE, SSM_HEAD_DIM), F32),
                        pltpu.VMEM((blk, SSM_INNER), F32)],
        compiler_params=_cparams("arbitrary", "arbitrary"),
        name=name,
    )(sproj, sproj, sproj, conv_w, conv_b.reshape(1, -1), dtb, alog, d_skip,
      norm_g.reshape(1, -1))


def _merge_body(ya_ref, yb_ref, yc1_ref, yc2_ref, yc3_ref, yd_ref, x_ref, gm_ref, wmg_ref, wbr_ref,
                wout_ref, o_ref):
    ys = (ya_ref[...], yb_ref[...], yc1_ref[...] + yc2_ref[...] + yc3_ref[...], yd_ref[...])
    x = x_ref[...]
    hn = _rms(x, gm_ref[...]).astype(BF16)
    u = None
    for nbr in range(N_BRANCH):
        gate = jnp.dot(hn, wmg_ref[:, nbr * D_MODEL:(nbr + 1) * D_MODEL], preferred_element_type=F32)
        t = _sigmoid(gate) * _dot(ys[nbr], wbr_ref[nbr])
        u = t if u is None else u + t
    o_ref[...] = x + _dot(u, wout_ref[...])


def merge(ya, yb, yc1, yc2, yc3, yd, x2d, g_mix, w_mgate, w_branch, w_out, tm, name):
    m = x2d.shape[0]

    def rows(w):
        return pl.BlockSpec((tm, w), lambda i: (i, 0))

    return pl.pallas_call(
        _merge_body,
        grid=(m // tm,),
        in_specs=[rows(512)] * 6 + [rows(D_MODEL),
                                    pl.BlockSpec((1, D_MODEL), lambda i: (0, 0)),
                                    pl.BlockSpec((D_MODEL, N_BRANCH * D_MODEL), lambda i: (0, 0)),
                                    pl.BlockSpec((N_BRANCH, BRANCH_WIDTH, D_MODEL), lambda i: (0, 0, 0)),
                                    pl.BlockSpec((D_MODEL, D_MODEL), lambda i: (0, 0))],
        out_specs=rows(D_MODEL),
        out_shape=jax.ShapeDtypeStruct((m, D_MODEL), F32),
        compiler_params=_cparams("arbitrary"),
        name=name,
    )(ya, yb, yc1, yc2, yc3, yd, x2d, g_mix.reshape(1, -1), w_mgate, w_branch.astype(BF16),
      w_out.astype(BF16))


def _router_logits(hn, wr_ref):
    h_hi = hn.astype(BF16)
    h_lo = (hn - h_hi.astype(F32)).astype(BF16)
    return (jnp.dot(h_hi, wr_ref[0], preferred_element_type=F32)
            + jnp.dot(h_hi, wr_ref[1], preferred_element_type=F32)
            + jnp.dot(h_lo, wr_ref[0], preferred_element_type=F32))


def _group_lanes(lane):
    return (lane >= MOE_EXPERTS) & (lane < MOE_EXPERTS + MOE_GROUPS)


def _route_body(x_ref, g_ref, wr_ref, o_ref):
    tm = x_ref.shape[0]
    lane = lax.broadcasted_iota(jnp.int32, (tm, LANES), 1)
    logits = _router_logits(_rms(x_ref[...], g_ref[...]), wr_ref)
    lg = jnp.where(_group_lanes(lane), logits, -jnp.inf)
    mg = jnp.max(lg, axis=-1, keepdims=True)
    gi = jnp.min(jnp.where(lg == mg, lane.astype(F32), 1e9), axis=-1, keepdims=True) - MOE_EXPERTS
    o_ref[...] = jnp.broadcast_to(gi, (tm, LANES)).astype(jnp.int32)


def _gather_body(idx_ref, nlive_ref, src_ref, o_ref, buf, sem):
    rows = o_ref.shape[0]
    i = pl.program_id(0)
    nlive = nlive_ref[0]

    def live(tile):
        return tile * rows < nlive

    def issue_tile(tile):
        base = tile * rows
        slot = tile % 2

        def issue(r, carry):
            pltpu.make_async_copy(src_ref.at[pl.ds(idx_ref[base + r], 1)], buf.at[slot, pl.ds(r, 1)],
                                  sem.at[slot]).start()
            return carry

        lax.fori_loop(0, rows, issue, 0, unroll=8)

    @pl.when((i == 0) & live(0))
    def _():
        issue_tile(0)

    @pl.when((i + 1 < pl.num_programs(0)) & live(i + 1))
    def _():
        issue_tile(i + 1)

    @pl.when(live(i))
    def _():
        slot = i % 2
        pltpu.make_async_copy(src_ref.at[pl.ds(0, rows)], buf.at[slot], sem.at[slot]).wait()
        o_ref[...] = buf[slot]

    @pl.when(jnp.logical_not(live(i)))
    def _():
        o_ref[...] = jnp.zeros_like(o_ref)


def gather_rows(src, idx, nlive, rows, name):
    n = idx.shape[0]
    d = src.shape[1]
    grid_spec = pltpu.PrefetchScalarGridSpec(
        num_scalar_prefetch=2,
        grid=(n // rows,),
        in_specs=[pl.BlockSpec(memory_space=pl.ANY)],
        out_specs=pl.BlockSpec((rows, d), lambda i, *_: (i, 0)),
        scratch_shapes=[pltpu.VMEM((2, rows, d), src.dtype), pltpu.SemaphoreType.DMA((2,))],
    )
    return pl.pallas_call(
        _gather_body,
        grid_spec=grid_spec,
        out_shape=jax.ShapeDtypeStruct((n, d), src.dtype),
        compiler_params=_cparams("arbitrary"),
        name=name,
    )(idx, jnp.reshape(nlive, (1,)).astype(jnp.int32), src)


def _moe_body(tg_ref, x_ref, g_ref, wr_ref, wgu_ref, wd_ref, o_ref, hn_ref, comb_ref):
    e = pl.program_id(1)
    grp = tg_ref[pl.program_id(0)]
    tm = x_ref.shape[0]
    lane = lax.broadcasted_iota(jnp.int32, (tm, LANES), 1)
    lanef = lane.astype(F32)
    live = grp < MOE_GROUPS

    @pl.when(e == 0)
    def _():
        o_ref[...] = x_ref[...]

    @pl.when((e == 0) & live)
    def _():
        hn = _rms(x_ref[...], g_ref[...])
        hn_ref[...] = hn.astype(BF16)
        logits = _router_logits(hn, wr_ref)
        is_grp = _group_lanes(lane)
        lg = jnp.where(is_grp, logits, -jnp.inf)
        mg = jnp.max(lg, axis=-1, keepdims=True)
        sg = jnp.sum(jnp.where(is_grp, jnp.exp(lg - mg), 0.0), axis=-1, keepdims=True)
        lt = jnp.sum(jnp.where(lane == MOE_EXPERTS + grp, logits, 0.0), axis=-1, keepdims=True)
        g_w = jnp.exp(lt - mg) / sg
        in_grp = (lane < MOE_EXPERTS) & (lane // MOE_EPG == grp)
        le = jnp.where(in_grp, logits, -jnp.inf)
        m1 = jnp.max(le, axis=-1, keepdims=True)
        i1 = jnp.min(jnp.where(le == m1, lanef, 1e9), axis=-1, keepdims=True)
        le2 = jnp.where(lanef == i1, -jnp.inf, le)
        m2 = jnp.max(le2, axis=-1, keepdims=True)
        i2 = jnp.min(jnp.where(le2 == m2, lanef, 1e9), axis=-1, keepdims=True)
        e2 = jnp.exp(m2 - m1)
        den = 1.0 + e2
        comb_ref[...] = (jnp.where(lanef == i1, g_w / den, 0.0)
                         + jnp.where(lanef == i2, g_w * e2 / den, 0.0))

    @pl.when(live)
    def _():
        a = jnp.dot(hn_ref[...], wgu_ref[0], preferred_element_type=F32)
        ce = jnp.sum(jnp.where(lane == grp * MOE_EPG + e, comb_ref[...], 0.0), axis=-1, keepdims=True)
        act = _silu(a[:, :MOE_FF]) * a[:, MOE_FF:] * ce
        o_ref[...] += _dot(act, wd_ref[0])


def _moe_plan(gid, tm):
    t = gid.shape[0]
    onehot = (gid[:, None] == jnp.arange(MOE_GROUPS, dtype=jnp.int32)).astype(jnp.int32)
    csum = jnp.cumsum(onehot, axis=0)
    counts = csum[-1]
    rank = (csum * onehot).sum(axis=1) - 1
    padded = (counts + tm - 1) // tm * tm
    pend = jnp.cumsum(padded)
    dest = (pend - padded)[gid] + rank
    r_pad = t + MOE_GROUPS * tm
    row_token = (jnp.arange(r_pad, dtype=jnp.int32) % t).at[dest].set(jnp.arange(t, dtype=jnp.int32))
    tile_start = jnp.arange(r_pad // tm, dtype=jnp.int32) * tm
    tile_group = (tile_start[:, None] >= pend[None, :]).sum(axis=1)
    return row_token, dest.astype(jnp.int32), tile_group.astype(jnp.int32), pend[-1]


def moe(x2d, g_ffn, w_grp, w_exp, w_gate, w_up, w_down, tm, name):
    m = x2d.shape[0]
    gf = g_ffn.reshape(1, -1)
    wr32 = jnp.concatenate([w_exp, w_grp,
                            jnp.zeros((D_MODEL, LANES - MOE_EXPERTS - MOE_GROUPS), F32)], axis=1)
    wr_hi = wr32.astype(BF16)
    wr = jnp.stack([wr_hi, (wr32 - wr_hi.astype(F32)).astype(BF16)])
    wgu = jnp.concatenate([w_gate, w_up], axis=2).astype(BF16)
    tr = min(1024, m)
    gid = pl.pallas_call(
        _route_body,
        grid=(m // tr,),
        in_specs=[pl.BlockSpec((tr, D_MODEL), lambda i: (i, 0)),
                  pl.BlockSpec((1, D_MODEL), lambda i: (0, 0)),
                  pl.BlockSpec((2, D_MODEL, LANES), lambda i: (0, 0, 0))],
        out_specs=pl.BlockSpec((tr, LANES), lambda i: (i, 0)),
        out_shape=jax.ShapeDtypeStruct((m, LANES), jnp.int32),
        compiler_params=_cparams("arbitrary"),
        name=name + "_route",
    )(x2d, gf, wr)[:, 0]
    row_token, dest, tile_group, n_sorted = _moe_plan(gid, tm)
    rows_dma = min(MOE_GATHER_ROWS, tm)
    xs = gather_rows(x2d, row_token, n_sorted, rows_dma, name + "_gather")

    def expert(i, e, tg):
        return (jnp.minimum(tg[i], MOE_GROUPS - 1) * MOE_EPG + e, 0, 0)

    grid_spec = pltpu.PrefetchScalarGridSpec(
        num_scalar_prefetch=1,
        grid=(xs.shape[0] // tm, MOE_EPG),
        in_specs=[pl.BlockSpec((tm, D_MODEL), lambda i, e, tg: (i, 0)),
                  pl.BlockSpec((1, D_MODEL), lambda i, e, tg: (0, 0)),
                  pl.BlockSpec((2, D_MODEL, LANES), lambda i, e, tg: (0, 0, 0)),
                  pl.BlockSpec((1, D_MODEL, 2 * MOE_FF), expert),
                  pl.BlockSpec((1, MOE_FF, D_MODEL), expert)],
        out_specs=pl.BlockSpec((tm, D_MODEL), lambda i, e, tg: (i, 0)),
        scratch_shapes=[pltpu.VMEM((tm, D_MODEL), BF16), pltpu.VMEM((tm, LANES), F32)],
    )
    ys = pl.pallas_call(
        _moe_body,
        grid_spec=grid_spec,
        out_shape=jax.ShapeDtypeStruct(xs.shape, F32),
        compiler_params=_cparams("arbitrary", "arbitrary"),
        name=name,
    )(tile_group, xs, gf, wr, wgu, w_down.astype(BF16))
    return gather_rows(ys, dest, jnp.int32(m), rows_dma, name + "_scatter")


def _ple_body(x_ref, p_ref, g_ref, wg_ref, wp_ref, gf_ref, o_ref, *, final):
    x = x_ref[...]
    gate = _sigmoid(_dot(_rms(x, g_ref[...]), wg_ref[...]))
    xn = x + _dot(p_ref[...], wp_ref[...]) * gate
    if final:
        xn = _rms(xn, gf_ref[...])
    o_ref[...] = xn


def ple(x2d, p2d, g_ple, w_gate, w_proj, g_final, final, tm, name):
    m = x2d.shape[0]
    return pl.pallas_call(
        functools.partial(_ple_body, final=final),
        grid=(m // tm,),
        in_specs=[pl.BlockSpec((tm, D_MODEL), lambda i: (i, 0)),
                  pl.BlockSpec((tm, PLE_DIM), lambda i: (i, 0)),
                  pl.BlockSpec((1, D_MODEL), lambda i: (0, 0)),
                  pl.BlockSpec((D_MODEL, D_MODEL), lambda i: (0, 0)),
                  pl.BlockSpec((PLE_DIM, D_MODEL), lambda i: (0, 0)),
                  pl.BlockSpec((1, D_MODEL), lambda i: (0, 0))],
        out_specs=pl.BlockSpec((tm, D_MODEL), lambda i: (i, 0)),
        out_shape=jax.ShapeDtypeStruct((m, D_MODEL), F32),
        compiler_params=_cparams("arbitrary"),
        name=name,
    )(x2d, p2d, g_ple.reshape(1, -1), w_gate.astype(BF16), w_proj.astype(BF16),
      g_final.reshape(1, -1))


def _cols(w, *names):
    return [w[:, _OFF[n][0]:_OFF[n][0] + _OFF[n][1]] for n in names]


def _padcols(w, width):
    return jnp.pad(w, ((0, 0), (0, width - w.shape[1])))


def _pad_q_heads(wq):
    hd = ATT_HEAD_DIM
    zero = jnp.zeros((wq.shape[0], hd), wq.dtype)
    cols = []
    for h in range(ATT_HEADS):
        blk = wq[:, h * hd:(h + 1) * hd] * (hd ** -0.5)
        cols += [blk, zero] if h // ATT_REP == 0 else [zero, blk]
    return jnp.concatenate(cols, axis=1)


def _split_w_in(w):
    swa_q, nsa_q = _cols(w, 'swa_q', 'nsa_q')
    w_att = jnp.concatenate([_pad_q_heads(swa_q), _pad_q_heads(nsa_q)] + _cols(w, *KV_BLK), axis=1)
    w_hgrn = jnp.concatenate(_cols(w, 'hgrn_q', 'hgrn_f', 'hgrn_i', 'hgrn_g'), axis=1)
    (xbc, dt, ngate, z) = _cols(w, 'ssm_xbc', 'ssm_dt', 'nsa_gate', 'ssm_z')
    w_ssm = jnp.concatenate([xbc, _padcols(dt, LANES), _padcols(ngate, LANES), z], axis=1)
    (w_mg,) = _cols(w, 'merge_gate')
    return [a.astype(BF16) for a in (w_att, w_hgrn, w_ssm, w_mg)]


def _mixers(i, att, hproj, sproj, attn_sinks, hgrn_lower_bounds, hgrn_norm_g, nsa_pos_k, nsa_pos_v,
            nsa_cmp_w1_k, nsa_cmp_w2_k, nsa_cmp_w1_v, nsa_cmp_w2_v, ssm_conv_w, ssm_conv_b,
            ssm_dt_bias, ssm_A_log, ssm_D, ssm_norm_g):
    bsz, seq, _ = att.shape
    kv = KV_BLK
    gate_blk = 7
    y_a = banded_attention(att, 0, kv['swa_k'], kv['swa_v'], window=SWA_WINDOW, sinks=attn_sinks[i],
                           name=f"swa{i}")
    y_b = hgrn2(hproj, hgrn_lower_bounds, hgrn_norm_g[i], i, name=f"hgrn{i}")
    ncp = seq // NSA_CMP_STRIDE

    def cmp_in(name):
        c0 = kv[name] * LANES
        return att[:, :, c0:c0 + LANES].reshape(bsz, ncp, NSA_CMP_STRIDE * LANES)

    kc = nsa_compress(cmp_in('nsa_k_cmp'), nsa_pos_k[i], nsa_cmp_w1_k[i], nsa_cmp_w2_k[i], name=f"cmpk{i}")
    vc = nsa_compress(cmp_in('nsa_v_cmp'), nsa_pos_v[i], nsa_cmp_w1_v[i], nsa_cmp_w2_v[i], name=f"cmpv{i}")
    y_c1, mask, act = nsa_cmp_select(att, 1, kc, vc, sproj, gate_blk, name=f"nsacmp{i}")
    y_c2 = nsa_selected(att, 1, kv['nsa_k_slc'], kv['nsa_v_slc'], mask, act, sproj, gate_blk,
                        name=f"nsasel{i}")
    y_c3 = banded_attention(att, 1, kv['nsa_k_win'], kv['nsa_v_win'], window=NSA_WINDOW, gate=sproj,
                            gate_blk=gate_blk, gate_col=2, name=f"nsawin{i}")
    y_d = mamba2(sproj, ssm_conv_w[i], ssm_conv_b[i], ssm_dt_bias[i], ssm_A_log[i], ssm_D[i],
                 ssm_norm_g[i], name=f"ssd{i}")
    return y_a, y_b, y_c1, y_c2, y_c3, y_d


def kernel(x, p, w_in, g_mix, attn_sinks, hgrn_lower_bounds, hgrn_norm_g, nsa_pos_k, nsa_pos_v,
           nsa_cmp_w1_k, nsa_cmp_w2_k, nsa_cmp_w1_v, nsa_cmp_w2_v, ssm_conv_w, ssm_conv_b,
           ssm_dt_bias, ssm_A_log, ssm_D, ssm_norm_g, w_branch, w_out, g_ffn, w_router_grp,
           w_router_exp, w_exp_gate, w_exp_up, w_exp_down, g_ple, w_ple_gate, w_ple_proj, g_final):
    bsz, seq, d = x.shape
    depth = w_in.shape[0]
    t = bsz * seq
    x2 = x.reshape(t, d)
    tm_proj = min(1024, t)
    tm_row = min(256, t)
    tm_moe = min(1024, t)
    for i in range(depth):
        w_att, w_hgrn, w_ssm, w_mg = _split_w_in(w_in[i])
        att = norm_mm(x2, g_mix[i], w_att, BF16, tm_proj, 512, f"proj_att{i}").reshape(bsz, seq, -1)
        hproj = norm_mm(x2, g_mix[i], w_hgrn, F32, tm_proj, 512, f"proj_hgrn{i}").reshape(bsz, seq, -1)
        sproj = norm_mm(x2, g_mix[i], w_ssm, F32, tm_proj, 512, f"proj_ssm{i}").reshape(bsz, seq, -1)
        ys = _mixers(i, att, hproj, sproj, attn_sinks, hgrn_lower_bounds, hgrn_norm_g, nsa_pos_k,
                     nsa_pos_v, nsa_cmp_w1_k, nsa_cmp_w2_k, nsa_cmp_w1_v, nsa_cmp_w2_v, ssm_conv_w,
                     ssm_conv_b, ssm_dt_bias, ssm_A_log, ssm_D, ssm_norm_g)
        ys = [y.reshape(t, -1) for y in ys]
        x2 = merge(*ys, x2, g_mix[i], w_mg, w_branch[i], w_out[i], tm_row, f"merge{i}")
        x2 = moe(x2, g_ffn[i], w_router_grp[i], w_router_exp[i], w_exp_gate[i], w_exp_up[i],
                 w_exp_down[i], tm_moe, f"moe{i}")
        x2 = ple(x2, p[i].reshape(t, -1), g_ple[i], w_ple_gate[i], w_ple_proj[i], g_final,
                 i == depth - 1, tm_row, f"ple{i}")
    return x2.reshape(bsz, seq, d)
```

```python
import functools

import numpy as np
import jax
import jax.numpy as jnp
from jax import lax
from jax.experimental import pallas as pl
from jax.experimental.pallas import tpu as pltpu

F32 = jnp.float32
BF16 = jnp.bfloat16
HIGHEST = lax.Precision.HIGHEST

D_MODEL = 1024
PLE_DIM = 256
NORM_EPS = 1e-6
NEG_INF = -1e30
REMOVED = -3e38
N_BRANCH = 4
BRANCH_WIDTH = 512
ATTN_BLOCK = 128

ATT_HEADS = 8
ATT_KV_HEADS = 2
ATT_HEAD_DIM = 64
ATT_REP = ATT_HEADS // ATT_KV_HEADS
SWA_WINDOW = 128
NSA_WINDOW = 512
NSA_CMP_LEN = 32
NSA_CMP_STRIDE = 16
NSA_CMP_HIDDEN = 256
NSA_SEL_LEN = 64
NSA_TOP_N = 16
NSA_FORCE_SCORE = 1e6
CMP_KEY_CHUNK = 128
SEL_KEY_TILE = 512
SEL_FEATS = SEL_KEY_TILE // NSA_SEL_LEN
MASK_BIG = 1e30

HGRN_HEADS = 4
HGRN_DIM = 128
HGRN_CHUNK = 32
HGRN_MIN_F = 1e-6
HGRN_BLOCK = 256

SSM_HEADS = 8
SSM_HEAD_DIM = 64
SSM_GROUPS = 2
SSM_STATE = 64
SSM_CONV = 4
SSM_CHUNK = 64
SSM_INNER = 512
SSM_CONV_DIM = 768
SSM_BLOCK = 256

MOE_GROUPS = 4
MOE_EPG = 8
MOE_EXPERTS = 32
MOE_FF = 256
MOE_GATHER_ROWS = 512

LANES = 128
VMEM_LIMIT = 56 * 1024 * 1024

ALIBI_SLOPES = tuple(2.0 ** (-8.0 * (h + 1) / ATT_HEADS) for h in range(ATT_HEADS))

ATT_Q_WIDTH = ATT_HEADS * LANES
KV_BLK = {name: 2 * ATT_HEADS + j for j, name in enumerate(
    ('swa_k', 'swa_v', 'nsa_k_cmp', 'nsa_v_cmp', 'nsa_k_slc', 'nsa_v_slc', 'nsa_k_win', 'nsa_v_win'))}

_SPLITS = (
    ('swa_q', 512), ('swa_k', 128), ('swa_v', 128),
    ('hgrn_q', 512), ('hgrn_f', 512), ('hgrn_i', 512), ('hgrn_g', 512),
    ('nsa_q', 512), ('nsa_k_cmp', 128), ('nsa_v_cmp', 128), ('nsa_k_slc', 128),
    ('nsa_v_slc', 128), ('nsa_k_win', 128), ('nsa_v_win', 128), ('nsa_gate', 24),
    ('ssm_z', 512), ('ssm_xbc', 768), ('ssm_dt', 8), ('merge_gate', 4096),
)
_OFF = {}
_o = 0
for _n, _w in _SPLITS:
    _OFF[_n] = (_o, _w)
    _o += _w


def _cparams(*sem):
    return pltpu.CompilerParams(dimension_semantics=sem, vmem_limit_bytes=VMEM_LIMIT)


def _sigmoid(x):
    return 1.0 / (1.0 + jnp.exp(-x))


def _silu(x):
    return x * _sigmoid(x)


def _dot(a, b):
    return jnp.dot(a.astype(BF16), b.astype(BF16), preferred_element_type=F32)


def _dot_nt(a, b):
    return lax.dot_general(a.astype(BF16), b.astype(BF16), (((1,), (1,)), ((), ())),
                           preferred_element_type=F32)


def _dot_tn(a, b):
    return lax.dot_general(a.astype(BF16), b.astype(BF16), (((0,), (0,)), ((), ())),
                           preferred_element_type=F32)


def _rms(x, g):
    ms = jnp.mean(x * x, axis=-1, keepdims=True)
    return x * lax.rsqrt(ms + NORM_EPS) * g


def _norm_mm_body(x_ref, g_ref, w_ref, o_ref, hn_ref):
    @pl.when(pl.program_id(1) == 0)
    def _():
        hn_ref[...] = _rms(x_ref[...], g_ref[...]).astype(BF16)

    o_ref[...] = jnp.dot(hn_ref[...], w_ref[...], preferred_element_type=F32).astype(o_ref.dtype)


def norm_mm(x2d, g, w, out_dtype, tm, tn, name):
    m, k = x2d.shape
    n = w.shape[1]
    return pl.pallas_call(
        _norm_mm_body,
        grid=(m // tm, n // tn),
        in_specs=[pl.BlockSpec((tm, k), lambda i, j: (i, 0)),
                  pl.BlockSpec((1, k), lambda i, j: (0, 0)),
                  pl.BlockSpec((k, tn), lambda i, j: (0, j))],
        out_specs=pl.BlockSpec((tm, tn), lambda i, j: (i, j)),
        out_shape=jax.ShapeDtypeStruct((m, n), out_dtype),
        scratch_shapes=[pltpu.VMEM((tm, k), BF16)],
        compiler_params=_cparams("arbitrary", "arbitrary"),
        name=name,
    )(x2d, g.reshape(1, k), w)


def _ones_row(g):
    return ATT_HEAD_DIM if g == 0 else 0


def _vt_body(v_ref, o_ref):
    vt = jnp.transpose(v_ref[0].astype(F32))
    rowid = lax.broadcasted_iota(jnp.int32, vt.shape, 0)
    for g in range(ATT_KV_HEADS):
        aug = jnp.where(rowid // ATT_HEAD_DIM == g, vt, jnp.where(rowid == _ones_row(g), 1.0, 0.0))
        o_ref[0, g] = aug.astype(BF16)


def v_transposed(arr, v_blk, name):
    bsz, seq, _ = arr.shape
    rows = min(2048, seq)
    return pl.pallas_call(
        _vt_body,
        grid=(bsz, seq // rows),
        in_specs=[pl.BlockSpec((1, rows, LANES), lambda b, n: (b, n, v_blk))],
        out_specs=pl.BlockSpec((1, ATT_KV_HEADS, LANES, rows), lambda b, n: (b, 0, 0, n)),
        out_shape=jax.ShapeDtypeStruct((bsz, ATT_KV_HEADS, LANES, seq), BF16),
        compiler_params=_cparams("arbitrary", "arbitrary"),
        name=name,
    )(arr)


def _q_stack(q_ref, g):
    return jnp.concatenate([q_ref[0, :, (g * ATT_REP + r) * LANES:(g * ATT_REP + r + 1) * LANES]
                            for r in range(ATT_REP)], axis=0)


def _head(x, r):
    return x[:, r * ATTN_BLOCK:(r + 1) * ATTN_BLOCK]


def _banded_body(*refs, window, nprev, use_sink, gate_col):
    if use_sink:
        q_ref, k_ref, vt_ref, sink_ref, o_ref, ot_ref = refs
    else:
        q_ref, k_ref, vt_ref, gate_ref, o_ref, ot_ref = refs
    blk = ATTN_BLOCK
    hd = ATT_HEAD_DIM
    n = pl.program_id(1)
    nk = (nprev + 1) * blk
    start = pl.multiple_of(jnp.maximum(n - nprev, 0) * blk, blk)
    k128 = k_ref[0, pl.ds(start, nk), :]
    krow = lax.broadcasted_iota(jnp.int32, (nk, blk), 0)
    qcol = lax.broadcasted_iota(jnp.int32, (nk, blk), 1)
    rel = (n * blk + qcol) - (start + krow)
    negrel = jnp.where((rel >= 0) & (rel < window), -rel.astype(F32), NEG_INF)
    if not use_sink:
        gate_t = jnp.transpose(_sigmoid(gate_ref[0]))
    st4s = [_dot_nt(k128, _q_stack(q_ref, g)) for g in range(ATT_KV_HEADS)]
    for g in range(ATT_KV_HEADS):
        one = _ones_row(g)
        st4 = st4s[g]
        ps, ms = [], []
        for r in range(ATT_REP):
            h = g * ATT_REP + r
            st = _head(st4, r) + ALIBI_SLOPES[h] * negrel
            m = jnp.max(st, axis=0, keepdims=True)
            if use_sink:
                m = jnp.maximum(m, sink_ref[h])
            ps.append(jnp.exp(st - m).astype(BF16))
            ms.append(m)
        acc4 = jnp.dot(vt_ref[0, g, :, pl.ds(start, nk)], jnp.concatenate(ps, axis=1),
                       preferred_element_type=F32)
        for r in range(ATT_REP):
            h = g * ATT_REP + r
            acc = _head(acc4, r)
            l = acc[one:one + 1]
            if use_sink:
                l = l + jnp.exp(sink_ref[h] - ms[r])
            ot = acc[g * hd:(g + 1) * hd] / l
            if not use_sink:
                c = 3 * h + gate_col
                ot = ot * gate_t[c:c + 1]
            ot_ref[h * hd:(h + 1) * hd, :] = ot
    o_ref[0] = jnp.transpose(ot_ref[...])


def banded_attention(att, q_blk, k_blk, v_blk, *, window, sinks=None, gate=None, gate_blk=None,
                     gate_col=0, name):
    bsz, seq, _ = att.shape
    blk = ATTN_BLOCK
    nprev = (window - 1 + blk - 1) // blk
    use_sink = sinks is not None
    vt = v_transposed(att, v_blk, name + "_vt")
    in_specs = [pl.BlockSpec((1, blk, ATT_Q_WIDTH), lambda b, n: (b, n, q_blk)),
                pl.BlockSpec((1, seq, LANES), lambda b, n: (b, 0, k_blk)),
                pl.BlockSpec((1, ATT_KV_HEADS, LANES, seq), lambda b, n: (b, 0, 0, 0))]
    if use_sink:
        in_specs.append(pl.BlockSpec(memory_space=pltpu.SMEM))
        extra = sinks
    else:
        in_specs.append(pl.BlockSpec((1, blk, LANES), lambda b, n: (b, n, gate_blk)))
        extra = gate
    return pl.pallas_call(
        functools.partial(_banded_body, window=window, nprev=nprev, use_sink=use_sink,
                          gate_col=gate_col),
        grid=(bsz, seq // blk),
        in_specs=in_specs,
        out_specs=pl.BlockSpec((1, blk, 512), lambda b, n: (b, n, 0)),
        out_shape=jax.ShapeDtypeStruct((bsz, seq, 512), F32),
        scratch_shapes=[pltpu.VMEM((512, blk), F32)],
        compiler_params=_cparams("arbitrary", "arbitrary"),
        name=name,
    )(att, att, vt, extra)


def _compress_body(x_ref, w1a_ref, w1b_ref, w1_ref, pos_ref, w2_ref, o_ref):
    x = x_ref[0]
    p = jnp.dot(x, w1a_ref[...], preferred_element_type=F32)
    q = jnp.dot(x, w1b_ref[...], preferred_element_type=F32)
    ncp = x.shape[0]
    q = pltpu.roll(q, shift=ncp - 1, axis=0)
    posb = jnp.broadcast_to(pos_ref[...], (8, pos_ref.shape[1]))
    bias = _dot(posb, w1_ref[...])[0:1]
    hid = NSA_CMP_HIDDEN
    outs = []
    for g in range(ATT_KV_HEADS):
        pre = p[:, g * hid:(g + 1) * hid] + q[:, g * hid:(g + 1) * hid] + bias
        outs.append(_dot(jax.nn.gelu(pre), w2_ref[...]))
    o_ref[0] = jnp.concatenate(outs, axis=1).astype(o_ref.dtype)


def _expand_w1(w1, half):
    hd, hid, ng = ATT_HEAD_DIM, NSA_CMP_HIDDEN, ATT_KV_HEADS
    w = w1.reshape(NSA_CMP_LEN, hd, hid)[half * 16:(half + 1) * 16]
    eye = jnp.eye(ng, dtype=w1.dtype)
    out = jnp.einsum('ldj,gh->lgdhj', w, eye)
    return out.reshape(16 * ng * hd, ng * hid)


def nsa_compress(x16, pos, w1, w2, name):
    bsz, ncp, wid = x16.shape
    hid = NSA_CMP_HIDDEN
    w1a = _expand_w1(w1, 0).astype(BF16)
    w1b = _expand_w1(w1, 1).astype(BF16)
    return pl.pallas_call(
        _compress_body,
        grid=(bsz,),
        in_specs=[pl.BlockSpec((1, ncp, wid), lambda b: (b, 0, 0)),
                  pl.BlockSpec((wid, 2 * hid), lambda b: (0, 0)),
                  pl.BlockSpec((wid, 2 * hid), lambda b: (0, 0)),
                  pl.BlockSpec((NSA_CMP_LEN * ATT_HEAD_DIM, hid), lambda b: (0, 0)),
                  pl.BlockSpec((1, NSA_CMP_LEN * ATT_HEAD_DIM), lambda b: (0, 0)),
                  pl.BlockSpec((hid, ATT_HEAD_DIM), lambda b: (0, 0))],
        out_specs=pl.BlockSpec((1, ncp, LANES), lambda b: (b, 0, 0)),
        out_shape=jax.ShapeDtypeStruct((bsz, ncp, LANES), BF16),
        compiler_params=_cparams("arbitrary"),
        name=name,
    )(x16, w1a, w1b, w1.astype(BF16), pos.reshape(1, -1), w2.astype(BF16))


def _nsa_cmp_body(q_ref, kc_ref, vct_ref, ovt_ref, gate_ref, o_ref, m_ref, act_ref, ot_ref, imp_ref, *,
                  top_n):
    blk = ATTN_BLOCK
    hd = ATT_HEAD_DIM
    n = pl.program_id(1)
    kc = kc_ref[0]
    ncp = kc.shape[0]
    ns = ovt_ref.shape[0]
    crow = lax.broadcasted_iota(jnp.int32, (ncp, blk), 0)
    qcol = lax.broadcasted_iota(jnp.int32, (ncp, blk), 1)
    dist = (n * blk + qcol) - (crow * NSA_CMP_STRIDE + (NSA_CMP_LEN - 1))
    valid = dist >= 0
    negd = jnp.where(valid, -dist.astype(F32), NEG_INF)
    jrow = lax.broadcasted_iota(jnp.int32, (ns, blk), 0)
    qpos = n * blk + lax.broadcasted_iota(jnp.int32, (ns, blk), 1)
    cur = qpos // NSA_SEL_LEN
    forced = (jrow == 0) | (jrow == cur) | (jrow == cur - 1)
    causal_blk = jrow * NSA_SEL_LEN <= qpos
    jrowf = jrow.astype(F32)
    gate_t = jnp.transpose(_sigmoid(gate_ref[0]))
    ones8 = jnp.ones((8, blk), BF16)

    def attend(rows):
        st4s = [_dot_nt(kc[:rows], _q_stack(q_ref, g)) for g in range(ATT_KV_HEADS)]
        for g in range(ATT_KV_HEADS):
            one = _ones_row(g)
            es = []
            for r in range(ATT_REP):
                st = _head(st4s[g], r) + ALIBI_SLOPES[g * ATT_REP + r] * negd[:rows]
                m = jnp.max(st, axis=0, keepdims=True)
                es.append(jnp.where(valid[:rows], jnp.exp(st - m), 0.0))
            acc4 = jnp.dot(vct_ref[0, g, :, :rows], jnp.concatenate([e.astype(BF16) for e in es], axis=1),
                           preferred_element_type=F32)
            psum = jnp.zeros((rows, blk), F32)
            for r in range(ATT_REP):
                h = g * ATT_REP + r
                acc = _head(acc4, r)
                l = acc[one:one + 1]
                inv = jnp.where(l > 0.0, 1.0 / jnp.where(l > 0.0, l, 1.0), 0.0)
                ot_ref[h * hd:(h + 1) * hd, :] = acc[g * hd:(g + 1) * hd] * inv * gate_t[3 * h:3 * h + 1]
                psum = psum + es[r] * inv
            imp_ref[g] = _dot(ovt_ref[:, :rows], psum)

    chunk = min(CMP_KEY_CHUNK, ncp)
    nchunks = ncp // chunk
    need = jnp.minimum((n * (blk // NSA_CMP_STRIDE) + blk // NSA_CMP_STRIDE - 1 + chunk - 1) // chunk,
                       nchunks)
    for c in range(1, nchunks + 1):
        pl.when(need == c)(functools.partial(attend, c * chunk))

    for g in range(ATT_KV_HEADS):
        score = jnp.where(forced, NSA_FORCE_SCORE, jnp.where(causal_blk, imp_ref[g], NEG_INF))
        sel = jnp.zeros((ns, blk), F32)
        for _ in range(top_n):
            mx = jnp.max(score, axis=0, keepdims=True)
            idx = jnp.min(jnp.where(score == mx, jrowf, float(ns)), axis=0, keepdims=True)
            hit = jrowf == idx
            sel = jnp.where(hit, 1.0, sel)
            score = jnp.where(hit, REMOVED, score)
        m_ref[0, g] = sel
        act_ref[0, 0, g * 8:(g + 1) * 8, :] = _dot_nt(ones8, sel)
    o_ref[0] = jnp.transpose(ot_ref[...])


def _overlap_matrix_t(seq):
    n_cmp = seq // NSA_CMP_STRIDE
    n_sel = seq // NSA_SEL_LEN
    cs = np.arange(n_cmp)[None, :] * NSA_CMP_STRIDE
    ss = np.arange(n_sel)[:, None] * NSA_SEL_LEN
    ov = np.clip(np.minimum(cs + NSA_CMP_LEN, ss + NSA_SEL_LEN) - np.maximum(cs, ss), 0, None)
    return jnp.asarray(ov / NSA_CMP_LEN, dtype=BF16)


def nsa_cmp_select(att, q_blk, kc, vc, gate, gate_blk, name):
    bsz, seq, _ = att.shape
    blk = ATTN_BLOCK
    nb = seq // blk
    ncp = seq // NSA_CMP_STRIDE
    ns = seq // NSA_SEL_LEN
    top_n = min(NSA_TOP_N, ns)
    vct = v_transposed(vc, 0, name + "_vt")
    return pl.pallas_call(
        functools.partial(_nsa_cmp_body, top_n=top_n),
        grid=(bsz, nb),
        in_specs=[pl.BlockSpec((1, blk, ATT_Q_WIDTH), lambda b, n: (b, n, q_blk)),
                  pl.BlockSpec((1, ncp, LANES), lambda b, n: (b, 0, 0)),
                  pl.BlockSpec((1, ATT_KV_HEADS, LANES, ncp), lambda b, n: (b, 0, 0, 0)),
                  pl.BlockSpec((ns, ncp), lambda b, n: (0, 0)),
                  pl.BlockSpec((1, blk, LANES), lambda b, n: (b, n, gate_blk))],
        out_specs=[pl.BlockSpec((1, blk, 512), lambda b, n: (b, n, 0)),
                   pl.BlockSpec((1, ATT_KV_HEADS, ns, blk), lambda b, n: (b, 0, 0, n)),
                   pl.BlockSpec((1, 1, ATT_KV_HEADS * 8, ns), lambda b, n: (b, n, 0, 0))],
        out_shape=[jax.ShapeDtypeStruct((bsz, seq, 512), F32),
                   jax.ShapeDtypeStruct((bsz, ATT_KV_HEADS, ns, seq), F32),
                   jax.ShapeDtypeStruct((bsz, nb, ATT_KV_HEADS * 8, ns), F32)],
        scratch_shapes=[pltpu.VMEM((512, blk), F32), pltpu.VMEM((ATT_KV_HEADS, ns, blk), F32)],
        compiler_params=_cparams("arbitrary", "arbitrary"),
        name=name,
    )(att, kc, vct, _overlap_matrix_t(seq), gate)


def _k_aug_body(k_ref, o_ref):
    k = k_ref[0].astype(F32)
    rows = k.shape[0]
    kin = lax.broadcasted_iota(jnp.int32, (rows, LANES), 0) % SEL_KEY_TILE
    lane = lax.broadcasted_iota(jnp.int32, (rows, LANES), 1)
    for g in range(ATT_KV_HEADS):
        f = lane - (ATT_HEAD_DIM if g == 0 else 0)
        feat = jnp.where(f == kin // NSA_SEL_LEN, 1.0, 0.0)
        feat = jnp.where(f == SEL_FEATS, (kin % 256).astype(F32), feat)
        feat = jnp.where(f == SEL_FEATS + 1, (kin // 256 * 256).astype(F32), feat)
        feat = jnp.where((f == SEL_FEATS + 2) | (f == SEL_FEATS + 3), 1.0, feat)
        o_ref[0, g] = jnp.where(lane // ATT_HEAD_DIM == g, k, feat).astype(BF16)


def k_augmented(arr, k_blk, name):
    bsz, seq, _ = arr.shape
    rows = min(2048, seq)
    return pl.pallas_call(
        _k_aug_body,
        grid=(bsz, seq // rows),
        in_specs=[pl.BlockSpec((1, rows, LANES), lambda b, n: (b, n, k_blk))],
        out_specs=pl.BlockSpec((1, ATT_KV_HEADS, rows, LANES), lambda b, n: (b, 0, n, 0)),
        out_shape=jax.ShapeDtypeStruct((bsz, ATT_KV_HEADS, seq, LANES), BF16),
        compiler_params=_cparams("arbitrary", "arbitrary"),
        name=name,
    )(arr)


def _query_feature_rows():
    out = np.zeros((ATT_KV_HEADS, 2 * SEL_FEATS, ATT_REP * ATTN_BLOCK), np.float32)
    qin = np.arange(ATTN_BLOCK, dtype=np.float32)
    for g in range(ATT_KV_HEADS):
        for r in range(ATT_REP):
            slope = ALIBI_SLOPES[g * ATT_REP + r]
            cols = slice(r * ATTN_BLOCK, (r + 1) * ATTN_BLOCK)
            out[g, 0, cols] = slope
            out[g, 1, cols] = slope
            out[g, 3, cols] = -slope * qin
            out[g, SEL_FEATS + 2, cols] = -slope
    return jnp.asarray(out)


def _nsa_sel_body(tiles_ref, cnt_ref, q_ref, k_ref, vt_ref, m_ref, gate_ref, fq_ref, o_ref, ot_ref,
                  acc_ref, mx_ref, dq_ref, sa_ref, sb_ref, *, ntl):
    blk = ATTN_BLOCK
    hd = ATT_HEAD_DIM
    tk = SEL_KEY_TILE
    sl = NSA_SEL_LEN
    per = tk // sl
    b = pl.program_id(0)
    n = pl.program_id(1)
    nb = pl.num_programs(1)
    krow = lax.broadcasted_iota(jnp.int32, (tk, blk), 0)
    qcol = lax.broadcasted_iota(jnp.int32, (tk, blk), 1)
    dq_ref[...] = (qcol - krow).astype(F32)
    gate_t = jnp.transpose(_sigmoid(gate_ref[0]))
    pad_rows = jnp.zeros((hd - 2 * SEL_FEATS, ATT_REP * blk), BF16)
    for g in range(ATT_KV_HEADS):
        one = _ones_row(g)
        lrow = (b * nb + n) * ATT_KV_HEADS + g
        qt = jnp.concatenate(
            [jnp.transpose(q_ref[0, :, (g * ATT_REP + r) * LANES:(g * ATT_REP + r + 1) * LANES]
                           .astype(F32))[g * hd:(g + 1) * hd] for r in range(ATT_REP)],
            axis=1).astype(BF16)
        acc_ref[...] = jnp.zeros_like(acc_ref)
        mx_ref[...] = jnp.full_like(mx_ref, NEG_INF)

        def tile_scores(j, live):
            t = tiles_ref[lrow * ntl + j]
            base = pl.multiple_of(t * tk, tk)
            sel8 = m_ref[0, g, pl.ds(pl.multiple_of(t * per, per), per), :]
            if live is not None:
                sel8 = jnp.where(live, sel8, 0.0)
            off = (n * blk - base).astype(F32)
            mask_rows = jnp.concatenate([(sel8 - 1.0) * MASK_BIG] * ATT_REP, axis=1)
            alibi_rows = fq_ref[g, 0:SEL_FEATS, :] + off * fq_ref[g, SEL_FEATS:2 * SEL_FEATS, :]
            feats = jnp.concatenate([mask_rows, alibi_rows], axis=0).astype(BF16)
            rhs = jnp.concatenate([qt, feats, pad_rows] if g == 0 else [feats, pad_rows, qt], axis=0)
            return jnp.dot(k_ref[0, g, pl.ds(base, tk), :], rhs,
                           preferred_element_type=F32)

        def tile_finish(j, st4, diag):
            base = pl.multiple_of(tiles_ref[lrow * ntl + j] * tk, tk)
            off = (n * blk - base).astype(F32)
            if diag:
                causal = (dq_ref[...] + off) >= 0.0
            ps, ms = [], []
            for r in range(ATT_REP):
                st = _head(st4, r)
                if diag:
                    st = jnp.where(causal, st, NEG_INF)
                m = jnp.max(st, axis=0, keepdims=True)
                ps.append(jnp.exp(st - m).astype(BF16))
                ms.append(m)
            acc = jnp.dot(vt_ref[0, g, :, pl.ds(base, tk)], jnp.concatenate(ps, axis=1),
                          preferred_element_type=F32)
            return jnp.concatenate(ms, axis=1), acc

        cnt = cnt_ref[lrow]

        def scores_into(s_ref, j):
            s_ref[...] = tile_scores(jnp.minimum(j, ntl - 1), j < cnt)

        def finish_from(s_ref, j, diag):
            m_t, acc_t = tile_finish(jnp.minimum(j, ntl - 1), s_ref[...], diag)
            m_old = mx_ref[...]
            m_new = jnp.maximum(m_old, m_t)
            acc_ref[...] = jnp.exp(m_old - m_new) * acc_ref[...] + jnp.exp(m_t - m_new) * acc_t
            mx_ref[...] = m_new

        def two_tiles(k, diag):
            scores_into(sb_ref, 2 * k + 1)
            finish_from(sa_ref, 2 * k, diag)
            scores_into(sa_ref, 2 * k + 2)
            finish_from(sb_ref, 2 * k + 1, False)

        scores_into(sa_ref, 0)
        two_tiles(0, True)

        def later(k, carry):
            two_tiles(k, False)
            return carry

        lax.fori_loop(1, (cnt + 1) // 2, later, 0)
        acc4 = acc_ref[...]
        for r in range(ATT_REP):
            h = g * ATT_REP + r
            acc = _head(acc4, r)
            ot = acc[g * hd:(g + 1) * hd] / acc[one:one + 1]
            ot_ref[h * hd:(h + 1) * hd, :] = ot * gate_t[3 * h + 1:3 * h + 2]
    o_ref[0] = jnp.transpose(ot_ref[...])


def _tile_lists(act, seq):
    bsz, nb = act.shape[:2]
    ns = act.shape[-1]
    tk = SEL_KEY_TILE
    ntl = seq // tk
    per = tk // NSA_SEL_LEN
    cnt_blk = act.reshape(bsz, nb, ATT_KV_HEADS, 8, ns)[:, :, :, 0, :]
    hit = cnt_blk.reshape(bsz, nb, ATT_KV_HEADS, ntl, per).sum(-1) > 0.5
    tidx = jnp.arange(ntl, dtype=jnp.int32)
    diag = (jnp.arange(nb, dtype=jnp.int32) * ATTN_BLOCK + ATTN_BLOCK - 1) // tk
    active = hit & (tidx[None, None, None, :] <= diag[None, :, None, None])
    act_i = active.astype(jnp.int32)
    rank = jnp.cumsum(act_i[..., ::-1], axis=-1)[..., ::-1] - 1
    slot = (active[..., :, None] & (rank[..., :, None] == tidx)).astype(jnp.int32)
    tiles = (slot * tidx[:, None]).sum(-2).reshape(-1)
    cnt = act_i.sum(-1).reshape(-1)
    return tiles, cnt, ntl


def nsa_selected(att, q_blk, k_blk, v_blk, mask, act, gate, gate_blk, name):
    bsz, seq, _ = att.shape
    blk = ATTN_BLOCK
    ns = seq // NSA_SEL_LEN
    tiles, cnt, ntl = _tile_lists(act, seq)
    vt = v_transposed(att, v_blk, name + "_vt")
    kaug = k_augmented(att, k_blk, name + "_k")
    grid_spec = pltpu.PrefetchScalarGridSpec(
        num_scalar_prefetch=2,
        grid=(bsz, seq // blk),
        in_specs=[pl.BlockSpec((1, blk, ATT_Q_WIDTH), lambda b, n, *_: (b, n, q_blk)),
                  pl.BlockSpec((1, ATT_KV_HEADS, seq, LANES), lambda b, n, *_: (b, 0, 0, 0)),
                  pl.BlockSpec((1, ATT_KV_HEADS, LANES, seq), lambda b, n, *_: (b, 0, 0, 0)),
                  pl.BlockSpec((1, ATT_KV_HEADS, ns, blk), lambda b, n, *_: (b, 0, 0, n)),
                  pl.BlockSpec((1, blk, LANES), lambda b, n, *_: (b, n, gate_blk)),
                  pl.BlockSpec((ATT_KV_HEADS, 2 * SEL_FEATS, ATT_REP * blk), lambda b, n, *_: (0, 0, 0))],
        out_specs=pl.BlockSpec((1, blk, 512), lambda b, n, *_: (b, n, 0)),
        scratch_shapes=[pltpu.VMEM((512, blk), F32),
                        pltpu.VMEM((LANES, ATT_REP * blk), F32),
                        pltpu.VMEM((1, ATT_REP * blk), F32),
                        pltpu.VMEM((SEL_KEY_TILE, blk), F32),
                        pltpu.VMEM((SEL_KEY_TILE, ATT_REP * blk), F32),
                        pltpu.VMEM((SEL_KEY_TILE, ATT_REP * blk), F32)],
    )
    return pl.pallas_call(
        functools.partial(_nsa_sel_body, ntl=ntl),
        grid_spec=grid_spec,
        out_shape=jax.ShapeDtypeStruct((bsz, seq, 512), F32),
        compiler_params=_cparams("arbitrary", "arbitrary"),
        name=name,
    )(tiles, cnt, att, kaug, vt, mask, gate, _query_feature_rows())


def _hgrn_body(q_ref, f_ref, i_ref, g_ref, lbp_ref, ng_ref, o_ref, st_ref, b_ref, k_ref, *, layer):
    blk = HGRN_BLOCK
    ch = HGRN_CHUNK
    dk = HGRN_DIM

    @pl.when(pl.program_id(1) == 0)
    def _():
        st_ref[...] = jnp.zeros_like(st_ref)

    lbp = lbp_ref[...]
    e = jnp.exp(lbp - jnp.max(lbp, axis=0, keepdims=True))
    sm = e / jnp.sum(e, axis=0, keepdims=True)
    lb = jnp.zeros((1, lbp.shape[1]), F32)
    for d in range(1, layer + 1):
        lb = lb + sm[d:d + 1]
    z = f_ref[0]
    f = lb + (1.0 - lb) * _sigmoid(z)
    logf = jnp.log(jnp.maximum(f, HGRN_MIN_F))
    k_ref[...] = (1.0 - lb) * _sigmoid(-z)
    tr = lax.broadcasted_iota(jnp.int32, (blk, blk), 0)
    tc = lax.broadcasted_iota(jnp.int32, (blk, blk), 1)
    tri = jnp.where((tr // ch == tc // ch) & (tc <= tr), 1.0, 0.0).astype(F32)
    b_ref[...] = jnp.dot(tri, logf, precision=HIGHEST, preferred_element_type=F32)
    cr = lax.broadcasted_iota(jnp.int32, (ch, ch), 0)
    cc = lax.broadcasted_iota(jnp.int32, (ch, ch), 1)
    causal = cc <= cr
    ng = ng_ref[...]

    for c in range(blk // ch):
        r0 = c * ch
        bc = b_ref[pl.ds(r0, ch), :]
        qc = q_ref[0, pl.ds(r0, ch), :]
        kc = k_ref[pl.ds(r0, ch), :]
        vc = i_ref[0, pl.ds(r0, ch), :]
        gc = g_ref[0, pl.ds(r0, ch), :]
        b_mid = bc[ch // 2:ch // 2 + 1]
        b_last = bc[ch - 1:ch]
        qa = qc * jnp.exp(bc - b_mid)
        ka = kc * jnp.exp(b_mid - bc)
        qe = qc * jnp.exp(bc)
        kl = kc * jnp.exp(b_last - bc)
        dec = jnp.exp(b_last)
        for h in range(HGRN_HEADS):
            sl = slice(h * dk, (h + 1) * dk)
            a = jnp.where(causal, _dot_nt(qa[:, sl], ka[:, sl]), 0.0)
            st = st_ref[h]
            o = _dot(a, vc[:, sl]) + _dot_nt(qe[:, sl], st)
            st_ref[h] = st * dec[:, sl] + _dot_tn(vc[:, sl], kl[:, sl])
            o = _rms(o, ng[:, sl]) * _silu(gc[:, sl])
            o_ref[0, pl.ds(r0, ch), sl] = o


def hgrn2(hproj, lower_bounds, norm_g, layer, name):
    bsz, seq, _ = hproj.shape
    blk = HGRN_BLOCK
    wid = HGRN_HEADS * HGRN_DIM
    depth = lower_bounds.shape[0]

    def col(j):
        return pl.BlockSpec((1, blk, wid), lambda b, n: (b, n, j))

    return pl.pallas_call(
        functools.partial(_hgrn_body, layer=layer),
        grid=(bsz, seq // blk),
        in_specs=[col(0), col(1), col(2), col(3),
                  pl.BlockSpec((depth, wid), lambda b, n: (0, 0)),
                  pl.BlockSpec((1, wid), lambda b, n: (0, 0))],
        out_specs=pl.BlockSpec((1, blk, wid), lambda b, n: (b, n, 0)),
        out_shape=jax.ShapeDtypeStruct((bsz, seq, wid), F32),
        scratch_shapes=[pltpu.VMEM((HGRN_HEADS, HGRN_DIM, HGRN_DIM), F32),
                        pltpu.VMEM((blk, wid), F32),
                        pltpu.VMEM((blk, wid), F32)],
        compiler_params=_cparams("arbitrary", "arbitrary"),
        name=name,
    )(hproj, hproj, hproj, hproj, lower_bounds, norm_g.reshape(1, wid))


def _ssd_body(xbc_ref, dt_ref, dtw_ref, z_ref, cw_ref, cb_ref, dtb_ref, alog_ref, dtbw_ref, alogw_ref,
              dskw_ref, ng_ref, o_ref, xp_ref, act_ref, st_ref, y_ref):
    blk = SSM_BLOCK
    ch = SSM_CHUNK
    hp = SSM_HEAD_DIM
    ns = SSM_STATE
    rep = SSM_HEADS // SSM_GROUPS
    pad = 8

    @pl.when(pl.program_id(1) == 0)
    def _():
        xp_ref[0:pad, :] = jnp.zeros((pad, SSM_CONV_DIM), F32)
        st_ref[...] = jnp.zeros_like(st_ref)

    xin = xbc_ref[0]
    xp_ref[pad:pad + blk, :] = xin
    cw = cw_ref[...]
    conv = cb_ref[...] + cw[SSM_CONV - 1:SSM_CONV] * xin
    for j in range(SSM_CONV - 1):
        shift = SSM_CONV - 1 - j
        conv = conv + cw[j:j + 1] * xp_ref[pl.ds(pad - shift, blk), :]
    xp_ref[0:pad, :] = xin[blk - pad:blk]
    act_ref[...] = _silu(conv)

    def softplus(v):
        return jnp.maximum(v, 0.0) + jnp.log(1.0 + jnp.exp(-jnp.abs(v)))

    a_all = softplus(dt_ref[0] + dtb_ref[...]) * (-jnp.exp(alog_ref[...]))
    dtw = softplus(dtw_ref[0] + dtbw_ref[...])
    aw = dtw * (-jnp.exp(alogw_ref[...]))
    cr = lax.broadcasted_iota(jnp.int32, (ch, ch), 0)
    cc = lax.broadcasted_iota(jnp.int32, (ch, ch), 1)
    causal = cc <= cr
    tri = jnp.where(causal, 1.0, 0.0).astype(F32)
    tri_t = jnp.where(cr <= cc, 1.0, 0.0).astype(F32)

    for c in range(blk // ch):
        r0 = c * ch
        acs_t = lax.dot_general(a_all[r0:r0 + ch], tri_t, (((0,), (0,)), ((), ())), precision=HIGHEST,
                                preferred_element_type=F32)
        acs = jnp.dot(tri, aw[r0:r0 + ch], precision=HIGHEST, preferred_element_type=F32)
        last = acs[ch - 1:ch]
        dtc = dtw[r0:r0 + ch]
        xc = act_ref[r0:r0 + ch, 0:SSM_INNER]
        xdt = dtc * xc
        grow = jnp.exp(acs)
        wdec = jnp.exp(last - acs) * dtc
        keep = jnp.exp(last)
        skip = dskw_ref[...] * xc
        for g in range(SSM_GROUPS):
            bm = act_ref[r0:r0 + ch, SSM_INNER + g * ns:SSM_INNER + (g + 1) * ns]
            cm = act_ref[r0:r0 + ch, SSM_INNER + SSM_GROUPS * ns + g * ns:
                         SSM_INNER + SSM_GROUPS * ns + (g + 1) * ns]
            cb = _dot_nt(cm, bm)
            for r in range(rep):
                h = g * rep + r
                hl = slice(h * hp, (h + 1) * hp)
                lmat = jnp.where(causal, jnp.exp(acs[:, hl] - acs_t[h:h + 1, :]), 0.0)
                st = st_ref[h]
                y = _dot(cb * lmat, xdt[:, hl]) + grow[:, hl] * _dot(cm, st) + skip[:, hl]
                st_ref[h] = keep[:, hl] * st + _dot_tn(bm * wdec[:, hl], xc[:, hl])
                y_ref[r0:r0 + ch, hl] = y
    yz = y_ref[...] * _silu(z_ref[0])
    o_ref[0] = _rms(yz, ng_ref[...])


def mamba2(sproj, conv_w, conv_b, dt_bias, a_log, d_skip, norm_g, name):
    bsz, seq, _ = sproj.shape
    blk = SSM_BLOCK
    padh = LANES - SSM_HEADS
    dtb = jnp.pad(dt_bias, (0, padh)).reshape(1, LANES)
    alog = jnp.pad(a_log, (0, padh)).reshape(1, LANES)

    def wide(v):
        return jnp.repeat(v, SSM_HEAD_DIM).reshape(1, SSM_INNER)

    def row(w):
        return pl.BlockSpec((1, w), lambda b, n: (0, 0))

    return pl.pallas_call(
        _ssd_body,
        grid=(bsz, seq // blk),
        in_specs=[pl.BlockSpec((1, blk, SSM_CONV_DIM), lambda b, n: (b, n, 0)),
                  pl.BlockSpec((1, blk, LANES), lambda b, n: (b, n, 6)),
                  pl.BlockSpec((1, blk, SSM_INNER), lambda b, n: (b, n, 3)),
                  pl.BlockSpec((1, blk, SSM_INNER), lambda b, n: (b, n, 2)),
                  pl.BlockSpec((SSM_CONV, SSM_CONV_DIM), lambda b, n: (0, 0)),
                  row(SSM_CONV_DIM), row(LANES), row(LANES), row(SSM_INNER), row(SSM_INNER),
                  row(SSM_INNER), row(SSM_INNER)],
        out_specs=pl.BlockSpec((1, blk, SSM_INNER), lambda b, n: (b, n, 0)),
        out_shape=jax.ShapeDtypeStruct((bsz, seq, SSM_INNER), F32),
        scratch_shapes=[pltpu.VMEM((blk + 8, SSM_CONV_DIM), F32),
                        pltpu.VMEM((blk, SSM_CONV_DIM), F32),
                        pltpu.VMEM((SSM_HEADS, SSM_STATE, SSM_HEAD_DIM), F32),
                        pltpu.VMEM((blk, SSM_INNER), F32)],
        compiler_params=_cparams("arbitrary", "arbitrary"),
        name=name,
    )(sproj, sproj, sproj, sproj, conv_w, conv_b.reshape(1, -1), dtb, alog, wide(dt_bias), wide(a_log),
      wide(d_skip), norm_g.reshape(1, -1))


def _merge_body(ya_ref, yb_ref, yc1_ref, yc2_ref, yc3_ref, yd_ref, x_ref, gm_ref, wmg_ref, wbr_ref,
                wout_ref, o_ref):
    ys = (ya_ref[...], yb_ref[...], yc1_ref[...] + yc2_ref[...] + yc3_ref[...], yd_ref[...])
    x = x_ref[...]
    hn = _rms(x, gm_ref[...]).astype(BF16)
    u = None
    for nbr in range(N_BRANCH):
        gate = jnp.dot(hn, wmg_ref[:, nbr * D_MODEL:(nbr + 1) * D_MODEL], preferred_element_type=F32)
        t = _sigmoid(gate) * _dot(ys[nbr], wbr_ref[nbr])
        u = t if u is None else u + t
    o_ref[...] = x + _dot(u, wout_ref[...])


def merge(ya, yb, yc1, yc2, yc3, yd, x2d, g_mix, w_mgate, w_branch, w_out, tm, name):
    m = x2d.shape[0]

    def rows(w):
        return pl.BlockSpec((tm, w), lambda i: (i, 0))

    return pl.pallas_call(
        _merge_body,
        grid=(m // tm,),
        in_specs=[rows(512)] * 6 + [rows(D_MODEL),
                                    pl.BlockSpec((1, D_MODEL), lambda i: (0, 0)),
                                    pl.BlockSpec((D_MODEL, N_BRANCH * D_MODEL), lambda i: (0, 0)),
                                    pl.BlockSpec((N_BRANCH, BRANCH_WIDTH, D_MODEL), lambda i: (0, 0, 0)),
                                    pl.BlockSpec((D_MODEL, D_MODEL), lambda i: (0, 0))],
        out_specs=rows(D_MODEL),
        out_shape=jax.ShapeDtypeStruct((m, D_MODEL), F32),
        compiler_params=_cparams("arbitrary"),
        name=name,
    )(ya, yb, yc1, yc2, yc3, yd, x2d, g_mix.reshape(1, -1), w_mgate, w_branch.astype(BF16),
      w_out.astype(BF16))


def _router_logits(hn, wr_ref):
    h_hi = hn.astype(BF16)
    h_lo = (hn - h_hi.astype(F32)).astype(BF16)
    return (jnp.dot(h_hi, wr_ref[0], preferred_element_type=F32)
            + jnp.dot(h_hi, wr_ref[1], preferred_element_type=F32)
            + jnp.dot(h_lo, wr_ref[0], preferred_element_type=F32))


def _group_lanes(lane):
    return (lane >= MOE_EXPERTS) & (lane < MOE_EXPERTS + MOE_GROUPS)


def _route_body(x_ref, g_ref, wr_ref, o_ref):
    tm = x_ref.shape[0]
    lane = lax.broadcasted_iota(jnp.int32, (tm, LANES), 1)
    logits = _router_logits(_rms(x_ref[...], g_ref[...]), wr_ref)
    lg = jnp.where(_group_lanes(lane), logits, -jnp.inf)
    mg = jnp.max(lg, axis=-1, keepdims=True)
    gi = jnp.min(jnp.where(lg == mg, lane.astype(F32), 1e9), axis=-1, keepdims=True) - MOE_EXPERTS
    o_ref[...] = jnp.broadcast_to(gi, (tm, LANES)).astype(jnp.int32)


def _gather_body(idx_ref, nlive_ref, src_ref, o_ref, buf, sem):
    rows = o_ref.shape[0]
    i = pl.program_id(0)
    nlive = nlive_ref[0]

    def live(tile):
        return tile * rows < nlive

    def issue_tile(tile):
        base = tile * rows
        slot = tile % 2

        def issue(r, carry):
            pltpu.make_async_copy(src_ref.at[pl.ds(idx_ref[base + r], 1)], buf.at[slot, pl.ds(r, 1)],
                                  sem.at[slot]).start()
            return carry

        lax.fori_loop(0, rows, issue, 0, unroll=8)

    @pl.when((i == 0) & live(0))
    def _():
        issue_tile(0)

    @pl.when((i + 1 < pl.num_programs(0)) & live(i + 1))
    def _():
        issue_tile(i + 1)

    @pl.when(live(i))
    def _():
        slot = i % 2
        pltpu.make_async_copy(src_ref.at[pl.ds(0, rows)], buf.at[slot], sem.at[slot]).wait()
        o_ref[...] = buf[slot]

    @pl.when(jnp.logical_not(live(i)))
    def _():
        o_ref[...] = jnp.zeros_like(o_ref)


def gather_rows(src, idx, nlive, rows, name):
    n = idx.shape[0]
    d = src.shape[1]
    grid_spec = pltpu.PrefetchScalarGridSpec(
        num_scalar_prefetch=2,
        grid=(n // rows,),
        in_specs=[pl.BlockSpec(memory_space=pl.ANY)],
        out_specs=pl.BlockSpec((rows, d), lambda i, *_: (i, 0)),
        scratch_shapes=[pltpu.VMEM((2, rows, d), src.dtype), pltpu.SemaphoreType.DMA((2,))],
    )
    return pl.pallas_call(
        _gather_body,
        grid_spec=grid_spec,
        out_shape=jax.ShapeDtypeStruct((n, d), src.dtype),
        compiler_params=_cparams("arbitrary"),
        name=name,
    )(idx, jnp.reshape(nlive, (1,)).astype(jnp.int32), src)


def _moe_body(tg_ref, x_ref, g_ref, wr_ref, wgu_ref, wd_ref, o_ref, hn_ref, comb_ref):
    e = pl.program_id(1)
    grp = tg_ref[pl.program_id(0)]
    tm = x_ref.shape[0]
    lane = lax.broadcasted_iota(jnp.int32, (tm, LANES), 1)
    lanef = lane.astype(F32)
    live = grp < MOE_GROUPS

    @pl.when(e == 0)
    def _():
        o_ref[...] = x_ref[...]

    @pl.when((e == 0) & live)
    def _():
        hn = _rms(x_ref[...], g_ref[...])
        hn_ref[...] = hn.astype(BF16)
        logits = _router_logits(hn, wr_ref)
        is_grp = _group_lanes(lane)
        lg = jnp.where(is_grp, logits, -jnp.inf)
        mg = jnp.max(lg, axis=-1, keepdims=True)
        sg = jnp.sum(jnp.where(is_grp, jnp.exp(lg - mg), 0.0), axis=-1, keepdims=True)
        lt = jnp.sum(jnp.where(lane == MOE_EXPERTS + grp, logits, 0.0), axis=-1, keepdims=True)
        g_w = jnp.exp(lt - mg) / sg
        in_grp = (lane < MOE_EXPERTS) & (lane // MOE_EPG == grp)
        le = jnp.where(in_grp, logits, -jnp.inf)
        m1 = jnp.max(le, axis=-1, keepdims=True)
        i1 = jnp.min(jnp.where(le == m1, lanef, 1e9), axis=-1, keepdims=True)
        le2 = jnp.where(lanef == i1, -jnp.inf, le)
        m2 = jnp.max(le2, axis=-1, keepdims=True)
        i2 = jnp.min(jnp.where(le2 == m2, lanef, 1e9), axis=-1, keepdims=True)
        e2 = jnp.exp(m2 - m1)
        den = 1.0 + e2
        comb_ref[...] = (jnp.where(lanef == i1, g_w / den, 0.0)
                         + jnp.where(lanef == i2, g_w * e2 / den, 0.0))

    @pl.when(live)
    def _():
        a = jnp.dot(hn_ref[...], wgu_ref[0], preferred_element_type=F32)
        ce = jnp.sum(jnp.where(lane == grp * MOE_EPG + e, comb_ref[...], 0.0), axis=-1, keepdims=True)
        act = _silu(a[:, :MOE_FF]) * a[:, MOE_FF:] * ce
        o_ref[...] += _dot(act, wd_ref[0])


def _moe_plan(gid, tm):
    t = gid.shape[0]
    onehot = (gid[:, None] == jnp.arange(MOE_GROUPS, dtype=jnp.int32)).astype(jnp.int32)
    csum = jnp.cumsum(onehot, axis=0)
    counts = csum[-1]
    rank = (csum * onehot).sum(axis=1) - 1
    padded = (counts + tm - 1) // tm * tm
    pend = jnp.cumsum(padded)
    dest = (pend - padded)[gid] + rank
    r_pad = t + MOE_GROUPS * tm
    row_token = (jnp.arange(r_pad, dtype=jnp.int32) % t).at[dest].set(jnp.arange(t, dtype=jnp.int32))
    tile_start = jnp.arange(r_pad // tm, dtype=jnp.int32) * tm
    tile_group = (tile_start[:, None] >= pend[None, :]).sum(axis=1)
    return row_token, dest.astype(jnp.int32), tile_group.astype(jnp.int32), pend[-1]


def moe(x2d, g_ffn, w_grp, w_exp, w_gate, w_up, w_down, tm, name):
    m = x2d.shape[0]
    gf = g_ffn.reshape(1, -1)
    wr32 = jnp.concatenate([w_exp, w_grp,
                            jnp.zeros((D_MODEL, LANES - MOE_EXPERTS - MOE_GROUPS), F32)], axis=1)
    wr_hi = wr32.astype(BF16)
    wr = jnp.stack([wr_hi, (wr32 - wr_hi.astype(F32)).astype(BF16)])
    wgu = jnp.concatenate([w_gate, w_up], axis=2).astype(BF16)
    tr = min(1024, m)
    gid = pl.pallas_call(
        _route_body,
        grid=(m // tr,),
        in_specs=[pl.BlockSpec((tr, D_MODEL), lambda i: (i, 0)),
                  pl.BlockSpec((1, D_MODEL), lambda i: (0, 0)),
                  pl.BlockSpec((2, D_MODEL, LANES), lambda i: (0, 0, 0))],
        out_specs=pl.BlockSpec((tr, LANES), lambda i: (i, 0)),
        out_shape=jax.ShapeDtypeStruct((m, LANES), jnp.int32),
        compiler_params=_cparams("arbitrary"),
        name=name + "_route",
    )(x2d, gf, wr)[:, 0]
    row_token, dest, tile_group, n_sorted = _moe_plan(gid, tm)
    rows_dma = min(MOE_GATHER_ROWS, tm)
    xs = gather_rows(x2d, row_token, n_sorted, rows_dma, name + "_gather")

    def expert(i, e, tg):
        return (jnp.minimum(tg[i], MOE_GROUPS - 1) * MOE_EPG + e, 0, 0)

    grid_spec = pltpu.PrefetchScalarGridSpec(
        num_scalar_prefetch=1,
        grid=(xs.shape[0] // tm, MOE_EPG),
        in_specs=[pl.BlockSpec((tm, D_MODEL), lambda i, e, tg: (i, 0)),
                  pl.BlockSpec((1, D_MODEL), lambda i, e, tg: (0, 0)),
                  pl.BlockSpec((2, D_MODEL, LANES), lambda i, e, tg: (0, 0, 0)),
                  pl.BlockSpec((1, D_MODEL, 2 * MOE_FF), expert),
                  pl.BlockSpec((1, MOE_FF, D_MODEL), expert)],
        out_specs=pl.BlockSpec((tm, D_MODEL), lambda i, e, tg: (i, 0)),
        scratch_shapes=[pltpu.VMEM((tm, D_MODEL), BF16), pltpu.VMEM((tm, LANES), F32)],
    )
    ys = pl.pallas_call(
        _moe_body,
        grid_spec=grid_spec,
        out_shape=jax.ShapeDtypeStruct(xs.shape, F32),
        compiler_params=_cparams("arbitrary", "arbitrary"),
        name=name,
    )(tile_group, xs, gf, wr, wgu, w_down.astype(BF16))
    return gather_rows(ys, dest, jnp.int32(m), rows_dma, name + "_scatter")


def _ple_body(x_ref, p_ref, g_ref, wg_ref, wp_ref, gf_ref, o_ref, *, final):
    x = x_ref[...]
    gate = _sigmoid(_dot(_rms(x, g_ref[...]), wg_ref[...]))
    xn = x + _dot(p_ref[...], wp_ref[...]) * gate
    if final:
        xn = _rms(xn, gf_ref[...])
    o_ref[...] = xn


def ple(x2d, p2d, g_ple, w_gate, w_proj, g_final, final, tm, name):
    m = x2d.shape[0]
    return pl.pallas_call(
        functools.partial(_ple_body, final=final),
        grid=(m // tm,),
        in_specs=[pl.BlockSpec((tm, D_MODEL), lambda i: (i, 0)),
                  pl.BlockSpec((tm, PLE_DIM), lambda i: (i, 0)),
                  pl.BlockSpec((1, D_MODEL), lambda i: (0, 0)),
                  pl.BlockSpec((D_MODEL, D_MODEL), lambda i: (0, 0)),
                  pl.BlockSpec((PLE_DIM, D_MODEL), lambda i: (0, 0)),
                  pl.BlockSpec((1, D_MODEL), lambda i: (0, 0))],
        out_specs=pl.BlockSpec((tm, D_MODEL), lambda i: (i, 0)),
        out_shape=jax.ShapeDtypeStruct((m, D_MODEL), F32),
        compiler_params=_cparams("arbitrary"),
        name=name,
    )(x2d, p2d, g_ple.reshape(1, -1), w_gate.astype(BF16), w_proj.astype(BF16),
      g_final.reshape(1, -1))


def _cols(w, *names):
    return [w[:, _OFF[n][0]:_OFF[n][0] + _OFF[n][1]] for n in names]


def _padcols(w, width):
    return jnp.pad(w, ((0, 0), (0, width - w.shape[1])))


def _pad_q_heads(wq):
    hd = ATT_HEAD_DIM
    zero = jnp.zeros((wq.shape[0], hd), wq.dtype)
    cols = []
    for h in range(ATT_HEADS):
        blk = wq[:, h * hd:(h + 1) * hd] * (hd ** -0.5)
        cols += [blk, zero] if h // ATT_REP == 0 else [zero, blk]
    return jnp.concatenate(cols, axis=1)


def _split_w_in(w):
    swa_q, nsa_q = _cols(w, 'swa_q', 'nsa_q')
    w_att = jnp.concatenate([_pad_q_heads(swa_q), _pad_q_heads(nsa_q)] + _cols(w, *KV_BLK), axis=1)
    w_hgrn = jnp.concatenate(_cols(w, 'hgrn_q', 'hgrn_f', 'hgrn_i', 'hgrn_g'), axis=1)
    (xbc, dt, ngate, z) = _cols(w, 'ssm_xbc', 'ssm_dt', 'nsa_gate', 'ssm_z')
    w_ssm = jnp.concatenate([xbc, _padcols(dt, LANES), _padcols(ngate, LANES), z,
                             jnp.repeat(dt, SSM_HEAD_DIM, axis=1)], axis=1)
    (w_mg,) = _cols(w, 'merge_gate')
    return [a.astype(BF16) for a in (w_att, w_hgrn, w_ssm, w_mg)]


def _mixers(i, att, hproj, sproj, attn_sinks, hgrn_lower_bounds, hgrn_norm_g, nsa_pos_k, nsa_pos_v,
            nsa_cmp_w1_k, nsa_cmp_w2_k, nsa_cmp_w1_v, nsa_cmp_w2_v, ssm_conv_w, ssm_conv_b,
            ssm_dt_bias, ssm_A_log, ssm_D, ssm_norm_g):
    bsz, seq, _ = att.shape
    kv = KV_BLK
    gate_blk = 7
    y_a = banded_attention(att, 0, kv['swa_k'], kv['swa_v'], window=SWA_WINDOW, sinks=attn_sinks[i],
                           name=f"swa{i}")
    y_b = hgrn2(hproj, hgrn_lower_bounds, hgrn_norm_g[i], i, name=f"hgrn{i}")
    ncp = seq // NSA_CMP_STRIDE

    def cmp_in(name):
        c0 = kv[name] * LANES
        return att[:, :, c0:c0 + LANES].reshape(bsz, ncp, NSA_CMP_STRIDE * LANES)

    kc = nsa_compress(cmp_in('nsa_k_cmp'), nsa_pos_k[i], nsa_cmp_w1_k[i], nsa_cmp_w2_k[i], name=f"cmpk{i}")
    vc = nsa_compress(cmp_in('nsa_v_cmp'), nsa_pos_v[i], nsa_cmp_w1_v[i], nsa_cmp_w2_v[i], name=f"cmpv{i}")
    y_c1, mask, act = nsa_cmp_select(att, 1, kc, vc, sproj, gate_blk, name=f"nsacmp{i}")
    y_c2 = nsa_selected(att, 1, kv['nsa_k_slc'], kv['nsa_v_slc'], mask, act, sproj, gate_blk,
                        name=f"nsasel{i}")
    y_c3 = banded_attention(att, 1, kv['nsa_k_win'], kv['nsa_v_win'], window=NSA_WINDOW, gate=sproj,
                            gate_blk=gate_blk, gate_col=2, name=f"nsawin{i}")
    y_d = mamba2(sproj, ssm_conv_w[i], ssm_conv_b[i], ssm_dt_bias[i], ssm_A_log[i], ssm_D[i],
                 ssm_norm_g[i], name=f"ssd{i}")
    return y_a, y_b, y_c1, y_c2, y_c3, y_d


def kernel(x, p, w_in, g_mix, attn_sinks, hgrn_lower_bounds, hgrn_norm_g, nsa_pos_k, nsa_pos_v,
           nsa_cmp_w1_k, nsa_cmp_w2_k, nsa_cmp_w1_v, nsa_cmp_w2_v, ssm_conv_w, ssm_conv_b,
           ssm_dt_bias, ssm_A_log, ssm_D, ssm_norm_g, w_branch, w_out, g_ffn, w_router_grp,
           w_router_exp, w_exp_gate, w_exp_up, w_exp_down, g_ple, w_ple_gate, w_ple_proj, g_final):
    bsz, seq, d = x.shape
    depth = w_in.shape[0]
    t = bsz * seq
    x2 = x.reshape(t, d)
    tm_proj = min(1024, t)
    tm_row = min(256, t)
    tm_moe = min(1024, t)
    for i in range(depth):
        w_att, w_hgrn, w_ssm, w_mg = _split_w_in(w_in[i])
        att = norm_mm(x2, g_mix[i], w_att, BF16, tm_proj, 512, f"proj_att{i}").reshape(bsz, seq, -1)
        hproj = norm_mm(x2, g_mix[i], w_hgrn, F32, tm_proj, 512, f"proj_hgrn{i}").reshape(bsz, seq, -1)
        sproj = norm_mm(x2, g_mix[i], w_ssm, F32, tm_proj, 512, f"proj_ssm{i}").reshape(bsz, seq, -1)
        ys = _mixers(i, att, hproj, sproj, attn_sinks, hgrn_lower_bounds, hgrn_norm_g, nsa_pos_k,
                     nsa_pos_v, nsa_cmp_w1_k, nsa_cmp_w2_k, nsa_cmp_w1_v, nsa_cmp_w2_v, ssm_conv_w,
                     ssm_conv_b, ssm_dt_bias, ssm_A_log, ssm_D, ssm_norm_g)
        ys = [y.reshape(t, -1) for y in ys]
        x2 = merge(*ys, x2, g_mix[i], w_mg, w_branch[i], w_out[i], tm_row, f"merge{i}")
        x2 = moe(x2, g_ffn[i], w_router_grp[i], w_router_exp[i], w_exp_gate[i], w_exp_up[i],
                 w_exp_down[i], tm_moe, f"moe{i}")
        x2 = ple(x2, p[i].reshape(t, -1), g_ple[i], w_ple_gate[i], w_ple_proj[i], g_final,
                 i == depth - 1, tm_row, f"ple{i}")
    return x2.reshape(bsz, seq, d)
```

```python
import functools

import numpy as np
import jax
import jax.numpy as jnp
from jax import lax
from jax.experimental import pallas as pl
from jax.experimental.pallas import tpu as pltpu

F32 = jnp.float32
BF16 = jnp.bfloat16

D_MODEL = 1024
PLE_DIM = 256
NORM_EPS = 1e-6
NEG_INF = -1e30
REMOVED = -3e38
N_BRANCH = 4
BRANCH_WIDTH = 512
ATTN_BLOCK = 128

ATT_HEADS = 8
ATT_KV_HEADS = 2
ATT_HEAD_DIM = 64
ATT_REP = ATT_HEADS // ATT_KV_HEADS
SWA_WINDOW = 128
NSA_WINDOW = 512
NSA_CMP_LEN = 32
NSA_CMP_STRIDE = 16
NSA_CMP_HIDDEN = 256
NSA_SEL_LEN = 64
NSA_TOP_N = 16
NSA_FORCE_SCORE = 1e6
CMP_KEY_CHUNK = 128
SEL_KEY_TILE = 512
SEL_FEATS = SEL_KEY_TILE // NSA_SEL_LEN
MASK_BIG = 1e30

HGRN_HEADS = 4
HGRN_DIM = 128
HGRN_CHUNK = 32
HGRN_MIN_F = 1e-6
HGRN_BLOCK = 256

SSM_HEADS = 8
SSM_HEAD_DIM = 64
SSM_GROUPS = 2
SSM_STATE = 64
SSM_CONV = 4
SSM_CHUNK = 64
SSM_INNER = 512
SSM_CONV_DIM = 768
SSM_BLOCK = 256

MOE_GROUPS = 4
MOE_EPG = 8
MOE_EXPERTS = 32
MOE_FF = 256
MOE_GATHER_ROWS = 512

LANES = 128
VMEM_LIMIT = 56 * 1024 * 1024

ALIBI_SLOPES = tuple(2.0 ** (-8.0 * (h + 1) / ATT_HEADS) for h in range(ATT_HEADS))

ATT_Q_WIDTH = ATT_HEADS * LANES
KV_BLK = {name: 2 * ATT_HEADS + j for j, name in enumerate(
    ('swa_k', 'swa_v', 'nsa_k_cmp', 'nsa_v_cmp', 'nsa_k_slc', 'nsa_v_slc', 'nsa_k_win', 'nsa_v_win'))}

_SPLITS = (
    ('swa_q', 512), ('swa_k', 128), ('swa_v', 128),
    ('hgrn_q', 512), ('hgrn_f', 512), ('hgrn_i', 512), ('hgrn_g', 512),
    ('nsa_q', 512), ('nsa_k_cmp', 128), ('nsa_v_cmp', 128), ('nsa_k_slc', 128),
    ('nsa_v_slc', 128), ('nsa_k_win', 128), ('nsa_v_win', 128), ('nsa_gate', 24),
    ('ssm_z', 512), ('ssm_xbc', 768), ('ssm_dt', 8), ('merge_gate', 4096),
)
_OFF = {}
_o = 0
for _n, _w in _SPLITS:
    _OFF[_n] = (_o, _w)
    _o += _w


def _cparams(*sem):
    return pltpu.CompilerParams(dimension_semantics=sem, vmem_limit_bytes=VMEM_LIMIT)


def _sigmoid(x):
    return 1.0 / (1.0 + jnp.exp(-x))


def _silu(x):
    return x * _sigmoid(x)


def _dot(a, b):
    return jnp.dot(a.astype(BF16), b.astype(BF16), preferred_element_type=F32)


def _dot_nt(a, b):
    return lax.dot_general(a.astype(BF16), b.astype(BF16), (((1,), (1,)), ((), ())),
                           preferred_element_type=F32)


def _dot_tn(a, b):
    return lax.dot_general(a.astype(BF16), b.astype(BF16), (((0,), (0,)), ((), ())),
                           preferred_element_type=F32)


def _split3(x):
    hi = x.astype(BF16)
    r1 = x - hi.astype(F32)
    mid = r1.astype(BF16)
    return hi, mid, (r1 - mid.astype(F32)).astype(BF16)


def _prefix_dot(tri, x):
    t = tri.astype(BF16)
    return sum(jnp.dot(t, p, preferred_element_type=F32) for p in _split3(x))


def _prefix_dot_tn(x, tri):
    t = tri.astype(BF16)
    return sum(lax.dot_general(p, t, (((0,), (0,)), ((), ())), preferred_element_type=F32)
               for p in _split3(x))


def _rms(x, g):
    ms = jnp.mean(x * x, axis=-1, keepdims=True)
    return x * lax.rsqrt(ms + NORM_EPS) * g


def _norm_mm_body(x_ref, g_ref, w_ref, o_ref, hn_ref):
    @pl.when(pl.program_id(1) == 0)
    def _():
        hn_ref[...] = _rms(x_ref[...], g_ref[...]).astype(BF16)

    o_ref[...] = jnp.dot(hn_ref[...], w_ref[...], preferred_element_type=F32).astype(o_ref.dtype)


def norm_mm(x2d, g, w, out_dtype, tm, tn, name):
    m, k = x2d.shape
    n = w.shape[1]
    return pl.pallas_call(
        _norm_mm_body,
        grid=(m // tm, n // tn),
        in_specs=[pl.BlockSpec((tm, k), lambda i, j: (i, 0)),
                  pl.BlockSpec((1, k), lambda i, j: (0, 0)),
                  pl.BlockSpec((k, tn), lambda i, j: (0, j))],
        out_specs=pl.BlockSpec((tm, tn), lambda i, j: (i, j)),
        out_shape=jax.ShapeDtypeStruct((m, n), out_dtype),
        scratch_shapes=[pltpu.VMEM((tm, k), BF16)],
        compiler_params=_cparams("arbitrary", "arbitrary"),
        name=name,
    )(x2d, g.reshape(1, k), w)


def _ones_row(g):
    return ATT_HEAD_DIM if g == 0 else 0


def _vt_body(v_ref, o_ref):
    vt = jnp.transpose(v_ref[0].astype(F32))
    rowid = lax.broadcasted_iota(jnp.int32, vt.shape, 0)
    for g in range(ATT_KV_HEADS):
        aug = jnp.where(rowid // ATT_HEAD_DIM == g, vt, jnp.where(rowid == _ones_row(g), 1.0, 0.0))
        o_ref[0, g] = aug.astype(BF16)


def v_transposed(arr, v_blk, name):
    bsz, seq, _ = arr.shape
    rows = min(2048, seq)
    return pl.pallas_call(
        _vt_body,
        grid=(bsz, seq // rows),
        in_specs=[pl.BlockSpec((1, rows, LANES), lambda b, n: (b, n, v_blk))],
        out_specs=pl.BlockSpec((1, ATT_KV_HEADS, LANES, rows), lambda b, n: (b, 0, 0, n)),
        out_shape=jax.ShapeDtypeStruct((bsz, ATT_KV_HEADS, LANES, seq), BF16),
        compiler_params=_cparams("arbitrary", "arbitrary"),
        name=name,
    )(arr)


def _q_stack(q_ref, g):
    return jnp.concatenate([q_ref[0, :, (g * ATT_REP + r) * LANES:(g * ATT_REP + r + 1) * LANES]
                            for r in range(ATT_REP)], axis=0)


def _head(x, r):
    return x[:, r * ATTN_BLOCK:(r + 1) * ATTN_BLOCK]


def _banded_body(*refs, window, nprev, use_sink, gate_col):
    if use_sink:
        q_ref, k_ref, vt_ref, sink_ref, o_ref, ot_ref = refs
    else:
        q_ref, k_ref, vt_ref, gate_ref, o_ref, ot_ref = refs
    blk = ATTN_BLOCK
    hd = ATT_HEAD_DIM
    n = pl.program_id(1)
    nk = (nprev + 1) * blk
    start = pl.multiple_of(jnp.maximum(n - nprev, 0) * blk, blk)
    k128 = k_ref[0, pl.ds(start, nk), :]
    krow = lax.broadcasted_iota(jnp.int32, (nk, blk), 0)
    qcol = lax.broadcasted_iota(jnp.int32, (nk, blk), 1)
    rel = (n * blk + qcol) - (start + krow)
    negrel = jnp.where((rel >= 0) & (rel < window), -rel.astype(F32), NEG_INF)
    if not use_sink:
        gate_t = jnp.transpose(_sigmoid(gate_ref[0]))
    st4s = [_dot_nt(k128, _q_stack(q_ref, g)) for g in range(ATT_KV_HEADS)]
    for g in range(ATT_KV_HEADS):
        one = _ones_row(g)
        st4 = st4s[g]
        ps, ms = [], []
        for r in range(ATT_REP):
            h = g * ATT_REP + r
            st = _head(st4, r) + ALIBI_SLOPES[h] * negrel
            m = jnp.max(st, axis=0, keepdims=True)
            if use_sink:
                m = jnp.maximum(m, sink_ref[h])
            ps.append(jnp.exp(st - m).astype(BF16))
            ms.append(m)
        acc4 = jnp.dot(vt_ref[0, g, :, pl.ds(start, nk)], jnp.concatenate(ps, axis=1),
                       preferred_element_type=F32)
        for r in range(ATT_REP):
            h = g * ATT_REP + r
            acc = _head(acc4, r)
            l = acc[one:one + 1]
            if use_sink:
                l = l + jnp.exp(sink_ref[h] - ms[r])
            ot = acc[g * hd:(g + 1) * hd] / l
            if not use_sink:
                c = 3 * h + gate_col
                ot = ot * gate_t[c:c + 1]
            ot_ref[h * hd:(h + 1) * hd, :] = ot
    o_ref[0] = jnp.transpose(ot_ref[...])


def banded_attention(att, q_blk, k_blk, v_blk, *, window, sinks=None, gate=None, gate_blk=None,
                     gate_col=0, name):
    bsz, seq, _ = att.shape
    blk = ATTN_BLOCK
    nprev = (window - 1 + blk - 1) // blk
    use_sink = sinks is not None
    vt = v_transposed(att, v_blk, name + "_vt")
    in_specs = [pl.BlockSpec((1, blk, ATT_Q_WIDTH), lambda b, n: (b, n, q_blk)),
                pl.BlockSpec((1, seq, LANES), lambda b, n: (b, 0, k_blk)),
                pl.BlockSpec((1, ATT_KV_HEADS, LANES, seq), lambda b, n: (b, 0, 0, 0))]
    if use_sink:
        in_specs.append(pl.BlockSpec(memory_space=pltpu.SMEM))
        extra = sinks
    else:
        in_specs.append(pl.BlockSpec((1, blk, LANES), lambda b, n: (b, n, gate_blk)))
        extra = gate
    return pl.pallas_call(
        functools.partial(_banded_body, window=window, nprev=nprev, use_sink=use_sink,
                          gate_col=gate_col),
        grid=(bsz, seq // blk),
        in_specs=in_specs,
        out_specs=pl.BlockSpec((1, blk, 512), lambda b, n: (b, n, 0)),
        out_shape=jax.ShapeDtypeStruct((bsz, seq, 512), F32),
        scratch_shapes=[pltpu.VMEM((512, blk), F32)],
        compiler_params=_cparams("arbitrary", "arbitrary"),
        name=name,
    )(att, att, vt, extra)


def _compress_body(x_ref, w1a_ref, w1b_ref, w1_ref, pos_ref, w2_ref, o_ref):
    x = x_ref[0]
    p = jnp.dot(x, w1a_ref[...], preferred_element_type=F32)
    q = jnp.dot(x, w1b_ref[...], preferred_element_type=F32)
    ncp = x.shape[0]
    q = pltpu.roll(q, shift=ncp - 1, axis=0)
    posb = jnp.broadcast_to(pos_ref[...], (8, pos_ref.shape[1]))
    bias = _dot(posb, w1_ref[...])[0:1]
    hid = NSA_CMP_HIDDEN
    outs = []
    for g in range(ATT_KV_HEADS):
        pre = p[:, g * hid:(g + 1) * hid] + q[:, g * hid:(g + 1) * hid] + bias
        outs.append(_dot(jax.nn.gelu(pre), w2_ref[...]))
    o_ref[0] = jnp.concatenate(outs, axis=1).astype(o_ref.dtype)


def _expand_w1(w1, half):
    hd, hid, ng = ATT_HEAD_DIM, NSA_CMP_HIDDEN, ATT_KV_HEADS
    w = w1.reshape(NSA_CMP_LEN, hd, hid)[half * 16:(half + 1) * 16]
    eye = jnp.eye(ng, dtype=w1.dtype)
    out = jnp.einsum('ldj,gh->lgdhj', w, eye)
    return out.reshape(16 * ng * hd, ng * hid)


def nsa_compress(x16, pos, w1, w2, name):
    bsz, ncp, wid = x16.shape
    hid = NSA_CMP_HIDDEN
    w1a = _expand_w1(w1, 0).astype(BF16)
    w1b = _expand_w1(w1, 1).astype(BF16)
    return pl.pallas_call(
        _compress_body,
        grid=(bsz,),
        in_specs=[pl.BlockSpec((1, ncp, wid), lambda b: (b, 0, 0)),
                  pl.BlockSpec((wid, 2 * hid), lambda b: (0, 0)),
                  pl.BlockSpec((wid, 2 * hid), lambda b: (0, 0)),
                  pl.BlockSpec((NSA_CMP_LEN * ATT_HEAD_DIM, hid), lambda b: (0, 0)),
                  pl.BlockSpec((1, NSA_CMP_LEN * ATT_HEAD_DIM), lambda b: (0, 0)),
                  pl.BlockSpec((hid, ATT_HEAD_DIM), lambda b: (0, 0))],
        out_specs=pl.BlockSpec((1, ncp, LANES), lambda b: (b, 0, 0)),
        out_shape=jax.ShapeDtypeStruct((bsz, ncp, LANES), BF16),
        compiler_params=_cparams("arbitrary"),
        name=name,
    )(x16, w1a, w1b, w1.astype(BF16), pos.reshape(1, -1), w2.astype(BF16))


def _nsa_cmp_body(q_ref, kc_ref, vct_ref, ovt_ref, gate_ref, o_ref, m_ref, act_ref, ot_ref, imp_ref, *,
                  top_n):
    blk = ATTN_BLOCK
    hd = ATT_HEAD_DIM
    n = pl.program_id(1)
    kc = kc_ref[0]
    ncp = kc.shape[0]
    ns = ovt_ref.shape[0]
    crow = lax.broadcasted_iota(jnp.int32, (ncp, blk), 0)
    qcol = lax.broadcasted_iota(jnp.int32, (ncp, blk), 1)
    dist = (n * blk + qcol) - (crow * NSA_CMP_STRIDE + (NSA_CMP_LEN - 1))
    valid = dist >= 0
    negd = jnp.where(valid, -dist.astype(F32), NEG_INF)
    jrow = lax.broadcasted_iota(jnp.int32, (ns, blk), 0)
    qpos = n * blk + lax.broadcasted_iota(jnp.int32, (ns, blk), 1)
    cur = qpos // NSA_SEL_LEN
    forced = (jrow == 0) | (jrow == cur) | (jrow == cur - 1)
    causal_blk = jrow * NSA_SEL_LEN <= qpos
    jrowf = jrow.astype(F32)
    gate_t = jnp.transpose(_sigmoid(gate_ref[0]))
    ones8 = jnp.ones((8, blk), BF16)

    def attend(rows):
        st4s = [_dot_nt(kc[:rows], _q_stack(q_ref, g)) for g in range(ATT_KV_HEADS)]
        for g in range(ATT_KV_HEADS):
            one = _ones_row(g)
            es = []
            for r in range(ATT_REP):
                st = _head(st4s[g], r) + ALIBI_SLOPES[g * ATT_REP + r] * negd[:rows]
                m = jnp.max(st, axis=0, keepdims=True)
                es.append(jnp.where(valid[:rows], jnp.exp(st - m), 0.0))
            acc4 = jnp.dot(vct_ref[0, g, :, :rows], jnp.concatenate([e.astype(BF16) for e in es], axis=1),
                           preferred_element_type=F32)
            psum = jnp.zeros((rows, blk), F32)
            for r in range(ATT_REP):
                h = g * ATT_REP + r
                acc = _head(acc4, r)
                l = acc[one:one + 1]
                inv = jnp.where(l > 0.0, 1.0 / jnp.where(l > 0.0, l, 1.0), 0.0)
                ot_ref[h * hd:(h + 1) * hd, :] = acc[g * hd:(g + 1) * hd] * inv * gate_t[3 * h:3 * h + 1]
                psum = psum + es[r] * inv
            imp_ref[g] = _dot(ovt_ref[:, :rows], psum)

    chunk = min(CMP_KEY_CHUNK, ncp)
    nchunks = ncp // chunk
    need = jnp.minimum((n * (blk // NSA_CMP_STRIDE) + blk // NSA_CMP_STRIDE - 1 + chunk - 1) // chunk,
                       nchunks)
    for c in range(1, nchunks + 1):
        pl.when(need == c)(functools.partial(attend, c * chunk))

    for g in range(ATT_KV_HEADS):
        score = jnp.where(forced, NSA_FORCE_SCORE, jnp.where(causal_blk, imp_ref[g], NEG_INF))
        sel = jnp.zeros((ns, blk), F32)
        for _ in range(top_n):
            mx = jnp.max(score, axis=0, keepdims=True)
            idx = jnp.min(jnp.where(score == mx, jrowf, float(ns)), axis=0, keepdims=True)
            hit = jrowf == idx
            sel = jnp.where(hit, 1.0, sel)
            score = jnp.where(hit, REMOVED, score)
        m_ref[0, g] = sel
        act_ref[0, 0, g * 8:(g + 1) * 8, :] = _dot_nt(ones8, sel)
    o_ref[0] = jnp.transpose(ot_ref[...])


def _overlap_matrix_t(seq):
    n_cmp = seq // NSA_CMP_STRIDE
    n_sel = seq // NSA_SEL_LEN
    cs = np.arange(n_cmp)[None, :] * NSA_CMP_STRIDE
    ss = np.arange(n_sel)[:, None] * NSA_SEL_LEN
    ov = np.clip(np.minimum(cs + NSA_CMP_LEN, ss + NSA_SEL_LEN) - np.maximum(cs, ss), 0, None)
    return jnp.asarray(ov / NSA_CMP_LEN, dtype=BF16)


def nsa_cmp_select(att, q_blk, kc, vc, gate, gate_blk, name):
    bsz, seq, _ = att.shape
    blk = ATTN_BLOCK
    nb = seq // blk
    ncp = seq // NSA_CMP_STRIDE
    ns = seq // NSA_SEL_LEN
    top_n = min(NSA_TOP_N, ns)
    vct = v_transposed(vc, 0, name + "_vt")
    return pl.pallas_call(
        functools.partial(_nsa_cmp_body, top_n=top_n),
        grid=(bsz, nb),
        in_specs=[pl.BlockSpec((1, blk, ATT_Q_WIDTH), lambda b, n: (b, n, q_blk)),
                  pl.BlockSpec((1, ncp, LANES), lambda b, n: (b, 0, 0)),
                  pl.BlockSpec((1, ATT_KV_HEADS, LANES, ncp), lambda b, n: (b, 0, 0, 0)),
                  pl.BlockSpec((ns, ncp), lambda b, n: (0, 0)),
                  pl.BlockSpec((1, blk, LANES), lambda b, n: (b, n, gate_blk))],
        out_specs=[pl.BlockSpec((1, blk, 512), lambda b, n: (b, n, 0)),
                   pl.BlockSpec((1, ATT_KV_HEADS, ns, blk), lambda b, n: (b, 0, 0, n)),
                   pl.BlockSpec((1, 1, ATT_KV_HEADS * 8, ns), lambda b, n: (b, n, 0, 0))],
        out_shape=[jax.ShapeDtypeStruct((bsz, seq, 512), F32),
                   jax.ShapeDtypeStruct((bsz, ATT_KV_HEADS, ns, seq), F32),
                   jax.ShapeDtypeStruct((bsz, nb, ATT_KV_HEADS * 8, ns), F32)],
        scratch_shapes=[pltpu.VMEM((512, blk), F32), pltpu.VMEM((ATT_KV_HEADS, ns, blk), F32)],
        compiler_params=_cparams("arbitrary", "arbitrary"),
        name=name,
    )(att, kc, vct, _overlap_matrix_t(seq), gate)


def _k_aug_body(k_ref, o_ref):
    k = k_ref[0].astype(F32)
    rows = k.shape[0]
    kin = lax.broadcasted_iota(jnp.int32, (rows, LANES), 0) % SEL_KEY_TILE
    lane = lax.broadcasted_iota(jnp.int32, (rows, LANES), 1)
    for g in range(ATT_KV_HEADS):
        f = lane - (ATT_HEAD_DIM if g == 0 else 0)
        feat = jnp.where(f == kin // NSA_SEL_LEN, 1.0, 0.0)
        feat = jnp.where(f == SEL_FEATS, (kin % 256).astype(F32), feat)
        feat = jnp.where(f == SEL_FEATS + 1, (kin // 256 * 256).astype(F32), feat)
        feat = jnp.where((f == SEL_FEATS + 2) | (f == SEL_FEATS + 3), 1.0, feat)
        o_ref[0, g] = jnp.where(lane // ATT_HEAD_DIM == g, k, feat).astype(BF16)


def k_augmented(arr, k_blk, name):
    bsz, seq, _ = arr.shape
    rows = min(2048, seq)
    return pl.pallas_call(
        _k_aug_body,
        grid=(bsz, seq // rows),
        in_specs=[pl.BlockSpec((1, rows, LANES), lambda b, n: (b, n, k_blk))],
        out_specs=pl.BlockSpec((1, ATT_KV_HEADS, rows, LANES), lambda b, n: (b, 0, n, 0)),
        out_shape=jax.ShapeDtypeStruct((bsz, ATT_KV_HEADS, seq, LANES), BF16),
        compiler_params=_cparams("arbitrary", "arbitrary"),
        name=name,
    )(arr)


def _query_feature_rows():
    out = np.zeros((ATT_KV_HEADS, 2 * SEL_FEATS, ATT_REP * ATTN_BLOCK), np.float32)
    qin = np.arange(ATTN_BLOCK, dtype=np.float32)
    for g in range(ATT_KV_HEADS):
        for r in range(ATT_REP):
            slope = ALIBI_SLOPES[g * ATT_REP + r]
            cols = slice(r * ATTN_BLOCK, (r + 1) * ATTN_BLOCK)
            out[g, 0, cols] = slope
            out[g, 1, cols] = slope
            out[g, 3, cols] = -slope * qin
            out[g, SEL_FEATS + 2, cols] = -slope
    return jnp.asarray(out)


def _nsa_sel_body(tiles_ref, cnt_ref, q_ref, k_ref, vt_ref, m_ref, gate_ref, fq_ref, o_ref, ot_ref,
                  acc_ref, mx_ref, dq_ref, sa_ref, sb_ref, *, ntl):
    blk = ATTN_BLOCK
    hd = ATT_HEAD_DIM
    tk = SEL_KEY_TILE
    sl = NSA_SEL_LEN
    per = tk // sl
    b = pl.program_id(0)
    n = pl.program_id(1)
    nb = pl.num_programs(1)
    krow = lax.broadcasted_iota(jnp.int32, (tk, blk), 0)
    qcol = lax.broadcasted_iota(jnp.int32, (tk, blk), 1)
    dq_ref[...] = (qcol - krow).astype(F32)
    gate_t = jnp.transpose(_sigmoid(gate_ref[0]))
    pad_rows = jnp.zeros((hd - 2 * SEL_FEATS, ATT_REP * blk), BF16)
    for g in range(ATT_KV_HEADS):
        one = _ones_row(g)
        lrow = (b * nb + n) * ATT_KV_HEADS + g
        qt = jnp.concatenate(
            [jnp.transpose(q_ref[0, :, (g * ATT_REP + r) * LANES:(g * ATT_REP + r + 1) * LANES]
                           .astype(F32))[g * hd:(g + 1) * hd] for r in range(ATT_REP)],
            axis=1).astype(BF16)
        acc_ref[...] = jnp.zeros_like(acc_ref)
        mx_ref[...] = jnp.full_like(mx_ref, NEG_INF)

        def tile_scores(j, live):
            t = tiles_ref[lrow * ntl + j]
            base = pl.multiple_of(t * tk, tk)
            sel8 = m_ref[0, g, pl.ds(pl.multiple_of(t * per, per), per), :]
            if live is not None:
                sel8 = jnp.where(live, sel8, 0.0)
            off = (n * blk - base).astype(F32)
            mask_rows = jnp.concatenate([(sel8 - 1.0) * MASK_BIG] * ATT_REP, axis=1)
            alibi_rows = fq_ref[g, 0:SEL_FEATS, :] + off * fq_ref[g, SEL_FEATS:2 * SEL_FEATS, :]
            feats = jnp.concatenate([mask_rows, alibi_rows], axis=0).astype(BF16)
            rhs = jnp.concatenate([qt, feats, pad_rows] if g == 0 else [feats, pad_rows, qt], axis=0)
            return jnp.dot(k_ref[0, g, pl.ds(base, tk), :], rhs,
                           preferred_element_type=F32)

        def tile_finish(j, st4, diag):
            base = pl.multiple_of(tiles_ref[lrow * ntl + j] * tk, tk)
            off = (n * blk - base).astype(F32)
            if diag:
                causal = (dq_ref[...] + off) >= 0.0
            ps, ms = [], []
            for r in range(ATT_REP):
                st = _head(st4, r)
                if diag:
                    st = jnp.where(causal, st, NEG_INF)
                m = jnp.max(st, axis=0, keepdims=True)
                ps.append(jnp.exp(st - m).astype(BF16))
                ms.append(m)
            acc = jnp.dot(vt_ref[0, g, :, pl.ds(base, tk)], jnp.concatenate(ps, axis=1),
                          preferred_element_type=F32)
            return jnp.concatenate(ms, axis=1), acc

        cnt = cnt_ref[lrow]

        def scores_into(s_ref, j):
            s_ref[...] = tile_scores(jnp.minimum(j, ntl - 1), j < cnt)

        def finish_from(s_ref, j, diag):
            m_t, acc_t = tile_finish(jnp.minimum(j, ntl - 1), s_ref[...], diag)
            m_old = mx_ref[...]
            m_new = jnp.maximum(m_old, m_t)
            acc_ref[...] = jnp.exp(m_old - m_new) * acc_ref[...] + jnp.exp(m_t - m_new) * acc_t
            mx_ref[...] = m_new

        def two_tiles(k, diag):
            scores_into(sb_ref, 2 * k + 1)
            finish_from(sa_ref, 2 * k, diag)
            scores_into(sa_ref, 2 * k + 2)
            finish_from(sb_ref, 2 * k + 1, False)

        scores_into(sa_ref, 0)
        two_tiles(0, True)

        def later(k, carry):
            two_tiles(k, False)
            return carry

        lax.fori_loop(1, (cnt + 1) // 2, later, 0)
        acc4 = acc_ref[...]
        for r in range(ATT_REP):
            h = g * ATT_REP + r
            acc = _head(acc4, r)
            ot = acc[g * hd:(g + 1) * hd] / acc[one:one + 1]
            ot_ref[h * hd:(h + 1) * hd, :] = ot * gate_t[3 * h + 1:3 * h + 2]
    o_ref[0] = jnp.transpose(ot_ref[...])


def _tile_lists(act, seq):
    bsz, nb = act.shape[:2]
    ns = act.shape[-1]
    tk = SEL_KEY_TILE
    ntl = seq // tk
    per = tk // NSA_SEL_LEN
    cnt_blk = act.reshape(bsz, nb, ATT_KV_HEADS, 8, ns)[:, :, :, 0, :]
    hit = cnt_blk.reshape(bsz, nb, ATT_KV_HEADS, ntl, per).sum(-1) > 0.5
    tidx = jnp.arange(ntl, dtype=jnp.int32)
    diag = (jnp.arange(nb, dtype=jnp.int32) * ATTN_BLOCK + ATTN_BLOCK - 1) // tk
    active = hit & (tidx[None, None, None, :] <= diag[None, :, None, None])
    act_i = active.astype(jnp.int32)
    rank = jnp.cumsum(act_i[..., ::-1], axis=-1)[..., ::-1] - 1
    slot = (active[..., :, None] & (rank[..., :, None] == tidx)).astype(jnp.int32)
    tiles = (slot * tidx[:, None]).sum(-2).reshape(-1)
    cnt = act_i.sum(-1).reshape(-1)
    return tiles, cnt, ntl


def nsa_selected(att, q_blk, k_blk, v_blk, mask, act, gate, gate_blk, name):
    bsz, seq, _ = att.shape
    blk = ATTN_BLOCK
    ns = seq // NSA_SEL_LEN
    tiles, cnt, ntl = _tile_lists(act, seq)
    vt = v_transposed(att, v_blk, name + "_vt")
    kaug = k_augmented(att, k_blk, name + "_k")
    grid_spec = pltpu.PrefetchScalarGridSpec(
        num_scalar_prefetch=2,
        grid=(bsz, seq // blk),
        in_specs=[pl.BlockSpec((1, blk, ATT_Q_WIDTH), lambda b, n, *_: (b, n, q_blk)),
                  pl.BlockSpec((1, ATT_KV_HEADS, seq, LANES), lambda b, n, *_: (b, 0, 0, 0)),
                  pl.BlockSpec((1, ATT_KV_HEADS, LANES, seq), lambda b, n, *_: (b, 0, 0, 0)),
                  pl.BlockSpec((1, ATT_KV_HEADS, ns, blk), lambda b, n, *_: (b, 0, 0, n)),
                  pl.BlockSpec((1, blk, LANES), lambda b, n, *_: (b, n, gate_blk)),
                  pl.BlockSpec((ATT_KV_HEADS, 2 * SEL_FEATS, ATT_REP * blk), lambda b, n, *_: (0, 0, 0))],
        out_specs=pl.BlockSpec((1, blk, 512), lambda b, n, *_: (b, n, 0)),
        scratch_shapes=[pltpu.VMEM((512, blk), F32),
                        pltpu.VMEM((LANES, ATT_REP * blk), F32),
                        pltpu.VMEM((1, ATT_REP * blk), F32),
                        pltpu.VMEM((SEL_KEY_TILE, blk), F32),
                        pltpu.VMEM((SEL_KEY_TILE, ATT_REP * blk), F32),
                        pltpu.VMEM((SEL_KEY_TILE, ATT_REP * blk), F32)],
    )
    return pl.pallas_call(
        functools.partial(_nsa_sel_body, ntl=ntl),
        grid_spec=grid_spec,
        out_shape=jax.ShapeDtypeStruct((bsz, seq, 512), F32),
        compiler_params=_cparams("arbitrary", "arbitrary"),
        name=name,
    )(tiles, cnt, att, kaug, vt, mask, gate, _query_feature_rows())


def _hgrn_body(q_ref, f_ref, i_ref, g_ref, lbp_ref, ng_ref, o_ref, st_ref, b_ref, k_ref, *, layer):
    blk = HGRN_BLOCK
    ch = HGRN_CHUNK
    dk = HGRN_DIM

    @pl.when(pl.program_id(1) == 0)
    def _():
        st_ref[...] = jnp.zeros_like(st_ref)

    lbp = lbp_ref[...]
    e = jnp.exp(lbp - jnp.max(lbp, axis=0, keepdims=True))
    sm = e / jnp.sum(e, axis=0, keepdims=True)
    lb = jnp.zeros((1, lbp.shape[1]), F32)
    for d in range(1, layer + 1):
        lb = lb + sm[d:d + 1]
    z = f_ref[0]
    f = lb + (1.0 - lb) * _sigmoid(z)
    logf = jnp.log(jnp.maximum(f, HGRN_MIN_F))
    k_ref[...] = (1.0 - lb) * _sigmoid(-z)
    tr = lax.broadcasted_iota(jnp.int32, (blk, blk), 0)
    tc = lax.broadcasted_iota(jnp.int32, (blk, blk), 1)
    tri = jnp.where((tr // ch == tc // ch) & (tc <= tr), 1.0, 0.0)
    b_ref[...] = _prefix_dot(tri, logf)
    cr = lax.broadcasted_iota(jnp.int32, (ch, ch), 0)
    cc = lax.broadcasted_iota(jnp.int32, (ch, ch), 1)
    causal = cc <= cr
    ng = ng_ref[...]

    for c in range(blk // ch):
        r0 = c * ch
        bc = b_ref[pl.ds(r0, ch), :]
        qc = q_ref[0, pl.ds(r0, ch), :]
        kc = k_ref[pl.ds(r0, ch), :]
        vc = i_ref[0, pl.ds(r0, ch), :]
        gc = g_ref[0, pl.ds(r0, ch), :]
        b_mid = bc[ch // 2:ch // 2 + 1]
        b_last = bc[ch - 1:ch]
        qa = qc * jnp.exp(bc - b_mid)
        ka = kc * jnp.exp(b_mid - bc)
        qe = qc * jnp.exp(bc)
        kl = kc * jnp.exp(b_last - bc)
        dec = jnp.exp(b_last)
        for h in range(HGRN_HEADS):
            sl = slice(h * dk, (h + 1) * dk)
            a = jnp.where(causal, _dot_nt(qa[:, sl], ka[:, sl]), 0.0)
            st = st_ref[h]
            o = _dot(a, vc[:, sl]) + _dot_nt(qe[:, sl], st)
            st_ref[h] = st * dec[:, sl] + _dot_tn(vc[:, sl], kl[:, sl])
            o = _rms(o, ng[:, sl]) * _silu(gc[:, sl])
            o_ref[0, pl.ds(r0, ch), sl] = o


def hgrn2(hproj, lower_bounds, norm_g, layer, name):
    bsz, seq, _ = hproj.shape
    blk = HGRN_BLOCK
    wid = HGRN_HEADS * HGRN_DIM
    depth = lower_bounds.shape[0]

    def col(j):
        return pl.BlockSpec((1, blk, wid), lambda b, n: (b, n, j))

    return pl.pallas_call(
        functools.partial(_hgrn_body, layer=layer),
        grid=(bsz, seq // blk),
        in_specs=[col(0), col(1), col(2), col(3),
                  pl.BlockSpec((depth, wid), lambda b, n: (0, 0)),
                  pl.BlockSpec((1, wid), lambda b, n: (0, 0))],
        out_specs=pl.BlockSpec((1, blk, wid), lambda b, n: (b, n, 0)),
        out_shape=jax.ShapeDtypeStruct((bsz, seq, wid), F32),
        scratch_shapes=[pltpu.VMEM((HGRN_HEADS, HGRN_DIM, HGRN_DIM), F32),
                        pltpu.VMEM((blk, wid), F32),
                        pltpu.VMEM((blk, wid), F32)],
        compiler_params=_cparams("arbitrary", "arbitrary"),
        name=name,
    )(hproj, hproj, hproj, hproj, lower_bounds, norm_g.reshape(1, wid))


def _ssd_body(xbc_ref, dt_ref, z_ref, cw_ref, cb_ref, dtb_ref, alog_ref, alogw_ref, dskw_ref, ng_ref,
              o_ref, xp_ref, act_ref, st_ref, y_ref):
    blk = SSM_BLOCK
    ch = SSM_CHUNK
    hp = SSM_HEAD_DIM
    ns = SSM_STATE
    rep = SSM_HEADS // SSM_GROUPS
    pad = 8

    @pl.when(pl.program_id(1) == 0)
    def _():
        xp_ref[0:pad, :] = jnp.zeros((pad, SSM_CONV_DIM), F32)
        st_ref[...] = jnp.zeros_like(st_ref)

    xin = xbc_ref[0]
    xp_ref[pad:pad + blk, :] = xin
    cw = cw_ref[...]
    conv = cb_ref[...] + cw[SSM_CONV - 1:SSM_CONV] * xin
    for j in range(SSM_CONV - 1):
        shift = SSM_CONV - 1 - j
        conv = conv + cw[j:j + 1] * xp_ref[pl.ds(pad - shift, blk), :]
    xp_ref[0:pad, :] = xin[blk - pad:blk]
    act_ref[...] = _silu(conv)

    def softplus(v):
        return jnp.maximum(v, 0.0) + jnp.log(1.0 + jnp.exp(-jnp.abs(v)))

    dt = softplus(dt_ref[0] + dtb_ref[...])
    a_all = dt * (-jnp.exp(alog_ref[...]))
    er = lax.broadcasted_iota(jnp.int32, (LANES, SSM_INNER), 0)
    ec = lax.broadcasted_iota(jnp.int32, (LANES, SSM_INNER), 1)
    spread = jnp.where(ec // hp == er, 1.0, 0.0).astype(BF16)
    dtw = sum(jnp.dot(p, spread, preferred_element_type=F32) for p in _split3(dt))
    aw = dtw * (-jnp.exp(alogw_ref[...]))
    cr = lax.broadcasted_iota(jnp.int32, (ch, ch), 0)
    cc = lax.broadcasted_iota(jnp.int32, (ch, ch), 1)
    causal = cc <= cr
    tri = jnp.where(causal, 1.0, 0.0)
    tri_t = jnp.where(cr <= cc, 1.0, 0.0)

    for c in range(blk // ch):
        r0 = c * ch
        acs_t = _prefix_dot_tn(a_all[r0:r0 + ch], tri_t)
        acs = _prefix_dot(tri, aw[r0:r0 + ch])
        last = acs[ch - 1:ch]
        dtc = dtw[r0:r0 + ch]
        xc = act_ref[r0:r0 + ch, 0:SSM_INNER]
        xdt = dtc * xc
        grow = jnp.exp(acs)
        wdec = jnp.exp(last - acs) * dtc
        keep = jnp.exp(last)
        skip = dskw_ref[...] * xc
        for g in range(SSM_GROUPS):
            bm = act_ref[r0:r0 + ch, SSM_INNER + g * ns:SSM_INNER + (g + 1) * ns]
            cm = act_ref[r0:r0 + ch, SSM_INNER + SSM_GROUPS * ns + g * ns:
                         SSM_INNER + SSM_GROUPS * ns + (g + 1) * ns]
            cb = _dot_nt(cm, bm)
            for r in range(rep):
                h = g * rep + r
                hl = slice(h * hp, (h + 1) * hp)
                lmat = jnp.where(causal, jnp.exp(acs[:, hl] - acs_t[h:h + 1, :]), 0.0)
                st = st_ref[h]
                y = _dot(cb * lmat, xdt[:, hl]) + grow[:, hl] * _dot(cm, st) + skip[:, hl]
                st_ref[h] = keep[:, hl] * st + _dot_tn(bm * wdec[:, hl], xc[:, hl])
                y_ref[r0:r0 + ch, hl] = y
    yz = y_ref[...] * _silu(z_ref[0])
    o_ref[0] = _rms(yz, ng_ref[...])


def mamba2(sproj, conv_w, conv_b, dt_bias, a_log, d_skip, norm_g, name):
    bsz, seq, _ = sproj.shape
    blk = SSM_BLOCK
    padh = LANES - SSM_HEADS
    dtb = jnp.pad(dt_bias, (0, padh)).reshape(1, LANES)
    alog = jnp.pad(a_log, (0, padh)).reshape(1, LANES)

    def wide(v):
        return jnp.repeat(v, SSM_HEAD_DIM).reshape(1, SSM_INNER)

    def row(w):
        return pl.BlockSpec((1, w), lambda b, n: (0, 0))

    return pl.pallas_call(
        _ssd_body,
        grid=(bsz, seq // blk),
        in_specs=[pl.BlockSpec((1, blk, SSM_CONV_DIM), lambda b, n: (b, n, 0)),
                  pl.BlockSpec((1, blk, LANES), lambda b, n: (b, n, 6)),
                  pl.BlockSpec((1, blk, SSM_INNER), lambda b, n: (b, n, 2)),
                  pl.BlockSpec((SSM_CONV, SSM_CONV_DIM), lambda b, n: (0, 0)),
                  row(SSM_CONV_DIM), row(LANES), row(LANES), row(SSM_INNER), row(SSM_INNER),
                  row(SSM_INNER)],
        out_specs=pl.BlockSpec((1, blk, SSM_INNER), lambda b, n: (b, n, 0)),
        out_shape=jax.ShapeDtypeStruct((bsz, seq, SSM_INNER), F32),
        scratch_shapes=[pltpu.VMEM((blk + 8, SSM_CONV_DIM), F32),
                        pltpu.VMEM((blk, SSM_CONV_DIM), F32),
                        pltpu.VMEM((SSM_HEADS, SSM_STATE, SSM_HEAD_DIM), F32),
                        pltpu.VMEM((blk, SSM_INNER), F32)],
        compiler_params=_cparams("arbitrary", "arbitrary"),
        name=name,
    )(sproj, sproj, sproj, conv_w, conv_b.reshape(1, -1), dtb, alog, wide(a_log), wide(d_skip),
      norm_g.reshape(1, -1))


def _merge_body(ya_ref, yb_ref, yc1_ref, yc2_ref, yc3_ref, yd_ref, x_ref, gm_ref, wmg_ref, wbr_ref,
                wout_ref, o_ref):
    ys = (ya_ref[...], yb_ref[...], yc1_ref[...] + yc2_ref[...] + yc3_ref[...], yd_ref[...])
    x = x_ref[...]
    hn = _rms(x, gm_ref[...]).astype(BF16)
    u = None
    for nbr in range(N_BRANCH):
        gate = jnp.dot(hn, wmg_ref[:, nbr * D_MODEL:(nbr + 1) * D_MODEL], preferred_element_type=F32)
        t = _sigmoid(gate) * _dot(ys[nbr], wbr_ref[nbr])
        u = t if u is None else u + t
    o_ref[...] = x + _dot(u, wout_ref[...])


def merge(ya, yb, yc1, yc2, yc3, yd, x2d, g_mix, w_mgate, w_branch, w_out, tm, name):
    m = x2d.shape[0]

    def rows(w):
        return pl.BlockSpec((tm, w), lambda i: (i, 0))

    return pl.pallas_call(
        _merge_body,
        grid=(m // tm,),
        in_specs=[rows(512)] * 6 + [rows(D_MODEL),
                                    pl.BlockSpec((1, D_MODEL), lambda i: (0, 0)),
                                    pl.BlockSpec((D_MODEL, N_BRANCH * D_MODEL), lambda i: (0, 0)),
                                    pl.BlockSpec((N_BRANCH, BRANCH_WIDTH, D_MODEL), lambda i: (0, 0, 0)),
                                    pl.BlockSpec((D_MODEL, D_MODEL), lambda i: (0, 0))],
        out_specs=rows(D_MODEL),
        out_shape=jax.ShapeDtypeStruct((m, D_MODEL), F32),
        compiler_params=_cparams("arbitrary"),
        name=name,
    )(ya, yb, yc1, yc2, yc3, yd, x2d, g_mix.reshape(1, -1), w_mgate, w_branch.astype(BF16),
      w_out.astype(BF16))


def _router_logits(hn, wr_ref):
    h_hi = hn.astype(BF16)
    h_lo = (hn - h_hi.astype(F32)).astype(BF16)
    return (jnp.dot(h_hi, wr_ref[0], preferred_element_type=F32)
            + jnp.dot(h_hi, wr_ref[1], preferred_element_type=F32)
            + jnp.dot(h_lo, wr_ref[0], preferred_element_type=F32))


def _group_lanes(lane):
    return (lane >= MOE_EXPERTS) & (lane < MOE_EXPERTS + MOE_GROUPS)


def _route_body(x_ref, g_ref, wr_ref, o_ref):
    tm = x_ref.shape[0]
    lane = lax.broadcasted_iota(jnp.int32, (tm, LANES), 1)
    logits = _router_logits(_rms(x_ref[...], g_ref[...]), wr_ref)
    lg = jnp.where(_group_lanes(lane), logits, -jnp.inf)
    mg = jnp.max(lg, axis=-1, keepdims=True)
    gi = jnp.min(jnp.where(lg == mg, lane.astype(F32), 1e9), axis=-1, keepdims=True) - MOE_EXPERTS
    o_ref[...] = jnp.broadcast_to(gi, (tm, LANES)).astype(jnp.int32)


def _gather_body(idx_ref, nlive_ref, src_ref, o_ref, buf, sem):
    rows = o_ref.shape[0]
    i = pl.program_id(0)
    nlive = nlive_ref[0]

    def live(tile):
        return tile * rows < nlive

    def issue_tile(tile):
        base = tile * rows
        slot = tile % 2

        def issue(r, carry):
            pltpu.make_async_copy(src_ref.at[pl.ds(idx_ref[base + r], 1)], buf.at[slot, pl.ds(r, 1)],
                                  sem.at[slot]).start()
            return carry

        lax.fori_loop(0, rows, issue, 0, unroll=8)

    @pl.when((i == 0) & live(0))
    def _():
        issue_tile(0)

    @pl.when((i + 1 < pl.num_programs(0)) & live(i + 1))
    def _():
        issue_tile(i + 1)

    @pl.when(live(i))
    def _():
        slot = i % 2
        pltpu.make_async_copy(src_ref.at[pl.ds(0, rows)], buf.at[slot], sem.at[slot]).wait()
        o_ref[...] = buf[slot]

    @pl.when(jnp.logical_not(live(i)))
    def _():
        o_ref[...] = jnp.zeros_like(o_ref)


def gather_rows(src, idx, nlive, rows, name):
    n = idx.shape[0]
    d = src.shape[1]
    grid_spec = pltpu.PrefetchScalarGridSpec(
        num_scalar_prefetch=2,
        grid=(n // rows,),
        in_specs=[pl.BlockSpec(memory_space=pl.ANY)],
        out_specs=pl.BlockSpec((rows, d), lambda i, *_: (i, 0)),
        scratch_shapes=[pltpu.VMEM((2, rows, d), src.dtype), pltpu.SemaphoreType.DMA((2,))],
    )
    return pl.pallas_call(
        _gather_body,
        grid_spec=grid_spec,
        out_shape=jax.ShapeDtypeStruct((n, d), src.dtype),
        compiler_params=_cparams("arbitrary"),
        name=name,
    )(idx, jnp.reshape(nlive, (1,)).astype(jnp.int32), src)


def _moe_body(tg_ref, x_ref, g_ref, wr_ref, wg_ref, wu_ref, wd_ref, o_ref, hn_ref, comb_ref):
    e = pl.program_id(1)
    grp = tg_ref[pl.program_id(0)]
    tm = x_ref.shape[0]
    lane = lax.broadcasted_iota(jnp.int32, (tm, LANES), 1)
    lanef = lane.astype(F32)
    live = grp < MOE_GROUPS

    @pl.when(e == 0)
    def _():
        o_ref[...] = x_ref[...]

    @pl.when((e == 0) & live)
    def _():
        hn = _rms(x_ref[...], g_ref[...])
        hn_ref[...] = hn.astype(BF16)
        logits = _router_logits(hn, wr_ref)
        is_grp = _group_lanes(lane)
        lg = jnp.where(is_grp, logits, -jnp.inf)
        mg = jnp.max(lg, axis=-1, keepdims=True)
        sg = jnp.sum(jnp.where(is_grp, jnp.exp(lg - mg), 0.0), axis=-1, keepdims=True)
        lt = jnp.sum(jnp.where(lane == MOE_EXPERTS + grp, logits, 0.0), axis=-1, keepdims=True)
        g_w = jnp.exp(lt - mg) / sg
        in_grp = (lane < MOE_EXPERTS) & (lane // MOE_EPG == grp)
        le = jnp.where(in_grp, logits, -jnp.inf)
        m1 = jnp.max(le, axis=-1, keepdims=True)
        i1 = jnp.min(jnp.where(le == m1, lanef, 1e9), axis=-1, keepdims=True)
        le2 = jnp.where(lanef == i1, -jnp.inf, le)
        m2 = jnp.max(le2, axis=-1, keepdims=True)
        i2 = jnp.min(jnp.where(le2 == m2, lanef, 1e9), axis=-1, keepdims=True)
        e2 = jnp.exp(m2 - m1)
        den = 1.0 + e2
        comb_ref[...] = (jnp.where(lanef == i1, g_w / den, 0.0)
                         + jnp.where(lanef == i2, g_w * e2 / den, 0.0))

    @pl.when(live)
    def _():
        hn = hn_ref[...]
        gate = jnp.dot(hn, wg_ref[0].astype(BF16), preferred_element_type=F32)
        up = jnp.dot(hn, wu_ref[0].astype(BF16), preferred_element_type=F32)
        ce = jnp.sum(jnp.where(lane == grp * MOE_EPG + e, comb_ref[...], 0.0), axis=-1, keepdims=True)
        o_ref[...] += _dot(_silu(gate) * up * ce, wd_ref[0])


def _moe_plan(gid, tm):
    t = gid.shape[0]
    onehot = (gid[:, None] == jnp.arange(MOE_GROUPS, dtype=jnp.int32)).astype(jnp.int32)
    csum = jnp.cumsum(onehot, axis=0)
    counts = csum[-1]
    rank = (csum * onehot).sum(axis=1) - 1
    padded = (counts + tm - 1) // tm * tm
    pend = jnp.cumsum(padded)
    dest = (pend - padded)[gid] + rank
    r_pad = t + MOE_GROUPS * tm
    row_token = (jnp.arange(r_pad, dtype=jnp.int32) % t).at[dest].set(jnp.arange(t, dtype=jnp.int32))
    tile_start = jnp.arange(r_pad // tm, dtype=jnp.int32) * tm
    tile_group = (tile_start[:, None] >= pend[None, :]).sum(axis=1)
    return row_token, dest.astype(jnp.int32), tile_group.astype(jnp.int32), pend[-1]


def moe(x2d, g_ffn, w_grp, w_exp, w_gate, w_up, w_down, tm, name):
    m = x2d.shape[0]
    gf = g_ffn.reshape(1, -1)
    wr32 = jnp.concatenate([w_exp, w_grp,
                            jnp.zeros((D_MODEL, LANES - MOE_EXPERTS - MOE_GROUPS), F32)], axis=1)
    wr_hi = wr32.astype(BF16)
    wr = jnp.stack([wr_hi, (wr32 - wr_hi.astype(F32)).astype(BF16)])
    tr = min(1024, m)
    gid = pl.pallas_call(
        _route_body,
        grid=(m // tr,),
        in_specs=[pl.BlockSpec((tr, D_MODEL), lambda i: (i, 0)),
                  pl.BlockSpec((1, D_MODEL), lambda i: (0, 0)),
                  pl.BlockSpec((2, D_MODEL, LANES), lambda i: (0, 0, 0))],
        out_specs=pl.BlockSpec((tr, LANES), lambda i: (i, 0)),
        out_shape=jax.ShapeDtypeStruct((m, LANES), jnp.int32),
        compiler_params=_cparams("arbitrary"),
        name=name + "_route",
    )(x2d, gf, wr)[:, 0]
    row_token, dest, tile_group, n_sorted = _moe_plan(gid, tm)
    rows_dma = min(MOE_GATHER_ROWS, tm)
    xs = gather_rows(x2d, row_token, n_sorted, rows_dma, name + "_gather")

    def expert(i, e, tg):
        return (jnp.minimum(tg[i], MOE_GROUPS - 1) * MOE_EPG + e, 0, 0)

    grid_spec = pltpu.PrefetchScalarGridSpec(
        num_scalar_prefetch=1,
        grid=(xs.shape[0] // tm, MOE_EPG),
        in_specs=[pl.BlockSpec((tm, D_MODEL), lambda i, e, tg: (i, 0)),
                  pl.BlockSpec((1, D_MODEL), lambda i, e, tg: (0, 0)),
                  pl.BlockSpec((2, D_MODEL, LANES), lambda i, e, tg: (0, 0, 0)),
                  pl.BlockSpec((1, D_MODEL, MOE_FF), expert),
                  pl.BlockSpec((1, D_MODEL, MOE_FF), expert),
                  pl.BlockSpec((1, MOE_FF, D_MODEL), expert)],
        out_specs=pl.BlockSpec((tm, D_MODEL), lambda i, e, tg: (i, 0)),
        scratch_shapes=[pltpu.VMEM((tm, D_MODEL), BF16), pltpu.VMEM((tm, LANES), F32)],
    )
    ys = pl.pallas_call(
        _moe_body,
        grid_spec=grid_spec,
        out_shape=jax.ShapeDtypeStruct(xs.shape, F32),
        compiler_params=_cparams("arbitrary", "arbitrary"),
        name=name,
    )(tile_group, xs, gf, wr, w_gate, w_up, w_down)
    return gather_rows(ys, dest, jnp.int32(m), rows_dma, name + "_scatter")


def _ple_body(x_ref, p_ref, g_ref, wg_ref, wp_ref, gf_ref, o_ref, *, final):
    x = x_ref[...]
    gate = _sigmoid(_dot(_rms(x, g_ref[...]), wg_ref[...]))
    xn = x + _dot(p_ref[...], wp_ref[...]) * gate
    if final:
        xn = _rms(xn, gf_ref[...])
    o_ref[...] = xn


def ple(x2d, p2d, g_ple, w_gate, w_proj, g_final, final, tm, name):
    m = x2d.shape[0]
    return pl.pallas_call(
        functools.partial(_ple_body, final=final),
        grid=(m // tm,),
        in_specs=[pl.BlockSpec((tm, D_MODEL), lambda i: (i, 0)),
                  pl.BlockSpec((tm, PLE_DIM), lambda i: (i, 0)),
                  pl.BlockSpec((1, D_MODEL), lambda i: (0, 0)),
                  pl.BlockSpec((D_MODEL, D_MODEL), lambda i: (0, 0)),
                  pl.BlockSpec((PLE_DIM, D_MODEL), lambda i: (0, 0)),
                  pl.BlockSpec((1, D_MODEL), lambda i: (0, 0))],
        out_specs=pl.BlockSpec((tm, D_MODEL), lambda i: (i, 0)),
        out_shape=jax.ShapeDtypeStruct((m, D_MODEL), F32),
        compiler_params=_cparams("arbitrary"),
        name=name,
    )(x2d, p2d, g_ple.reshape(1, -1), w_gate.astype(BF16), w_proj.astype(BF16),
      g_final.reshape(1, -1))


def _cols(w, *names):
    return [w[:, _OFF[n][0]:_OFF[n][0] + _OFF[n][1]] for n in names]


def _padcols(w, width):
    return jnp.pad(w, ((0, 0), (0, width - w.shape[1])))


def _pad_q_heads(wq):
    hd = ATT_HEAD_DIM
    zero = jnp.zeros((wq.shape[0], hd), wq.dtype)
    cols = []
    for h in range(ATT_HEADS):
        blk = wq[:, h * hd:(h + 1) * hd] * (hd ** -0.5)
        cols += [blk, zero] if h // ATT_REP == 0 else [zero, blk]
    return jnp.concatenate(cols, axis=1)


def _split_w_in(w):
    swa_q, nsa_q = _cols(w, 'swa_q', 'nsa_q')
    w_att = jnp.concatenate([_pad_q_heads(swa_q), _pad_q_heads(nsa_q)] + _cols(w, *KV_BLK), axis=1)
    w_hgrn = jnp.concatenate(_cols(w, 'hgrn_q', 'hgrn_f', 'hgrn_i', 'hgrn_g'), axis=1)
    (xbc, dt, ngate, z) = _cols(w, 'ssm_xbc', 'ssm_dt', 'nsa_gate', 'ssm_z')
    w_ssm = jnp.concatenate([xbc, _padcols(dt, LANES), _padcols(ngate, LANES), z], axis=1)
    (w_mg,) = _cols(w, 'merge_gate')
    return [a.astype(BF16) for a in (w_att, w_hgrn, w_ssm, w_mg)]


def _mixers(i, att, hproj, sproj, attn_sinks, hgrn_lower_bounds, hgrn_norm_g, nsa_pos_k, nsa_pos_v,
            nsa_cmp_w1_k, nsa_cmp_w2_k, nsa_cmp_w1_v, nsa_cmp_w2_v, ssm_conv_w, ssm_conv_b,
            ssm_dt_bias, ssm_A_log, ssm_D, ssm_norm_g):
    bsz, seq, _ = att.shape
    kv = KV_BLK
    gate_blk = 7
    y_a = banded_attention(att, 0, kv['swa_k'], kv['swa_v'], window=SWA_WINDOW, sinks=attn_sinks[i],
                           name=f"swa{i}")
    y_b = hgrn2(hproj, hgrn_lower_bounds, hgrn_norm_g[i], i, name=f"hgrn{i}")
    ncp = seq // NSA_CMP_STRIDE

    def cmp_in(name):
        c0 = kv[name] * LANES
        return att[:, :, c0:c0 + LANES].reshape(bsz, ncp, NSA_CMP_STRIDE * LANES)

    kc = nsa_compress(cmp_in('nsa_k_cmp'), nsa_pos_k[i], nsa_cmp_w1_k[i], nsa_cmp_w2_k[i], name=f"cmpk{i}")
    vc = nsa_compress(cmp_in('nsa_v_cmp'), nsa_pos_v[i], nsa_cmp_w1_v[i], nsa_cmp_w2_v[i], name=f"cmpv{i}")
    y_c1, mask, act = nsa_cmp_select(att, 1, kc, vc, sproj, gate_blk, name=f"nsacmp{i}")
    y_c2 = nsa_selected(att, 1, kv['nsa_k_slc'], kv['nsa_v_slc'], mask, act, sproj, gate_blk,
                        name=f"nsasel{i}")
    y_c3 = banded_attention(att, 1, kv['nsa_k_win'], kv['nsa_v_win'], window=NSA_WINDOW, gate=sproj,
                            gate_blk=gate_blk, gate_col=2, name=f"nsawin{i}")
    y_d = mamba2(sproj, ssm_conv_w[i], ssm_conv_b[i], ssm_dt_bias[i], ssm_A_log[i], ssm_D[i],
                 ssm_norm_g[i], name=f"ssd{i}")
    return y_a, y_b, y_c1, y_c2, y_c3, y_d


def kernel(x, p, w_in, g_mix, attn_sinks, hgrn_lower_bounds, hgrn_norm_g, nsa_pos_k, nsa_pos_v,
           nsa_cmp_w1_k, nsa_cmp_w2_k, nsa_cmp_w1_v, nsa_cmp_w2_v, ssm_conv_w, ssm_conv_b,
           ssm_dt_bias, ssm_A_log, ssm_D, ssm_norm_g, w_branch, w_out, g_ffn, w_router_grp,
           w_router_exp, w_exp_gate, w_exp_up, w_exp_down, g_ple, w_ple_gate, w_ple_proj, g_final):
    bsz, seq, d = x.shape
    depth = w_in.shape[0]
    t = bsz * seq
    x2 = x.reshape(t, d)
    tm_proj = min(1024, t)
    tm_row = min(256, t)
    tm_moe = min(1024, t)
    for i in range(depth):
        w_att, w_hgrn, w_ssm, w_mg = _split_w_in(w_in[i])
        att = norm_mm(x2, g_mix[i], w_att, BF16, tm_proj, 512, f"proj_att{i}").reshape(bsz, seq, -1)
        hproj = norm_mm(x2, g_mix[i], w_hgrn, F32, tm_proj, 512, f"proj_hgrn{i}").reshape(bsz, seq, -1)
        sproj = norm_mm(x2, g_mix[i], w_ssm, F32, tm_proj, 512, f"proj_ssm{i}").reshape(bsz, seq, -1)
        ys = _mixers(i, att, hproj, sproj, attn_sinks, hgrn_lower_bounds, hgrn_norm_g, nsa_pos_k,
                     nsa_pos_v, nsa_cmp_w1_k, nsa_cmp_w2_k, nsa_cmp_w1_v, nsa_cmp_w2_v, ssm_conv_w,
                     ssm_conv_b, ssm_dt_bias, ssm_A_log, ssm_D, ssm_norm_g)
        ys = [y.reshape(t, -1) for y in ys]
        x2 = merge(*ys, x2, g_mix[i], w_mg, w_branch[i], w_out[i], tm_row, f"merge{i}")
        x2 = moe(x2, g_ffn[i], w_router_grp[i], w_router_exp[i], w_exp_gate[i], w_exp_up[i],
                 w_exp_down[i], tm_moe, f"moe{i}")
        x2 = ple(x2, p[i].reshape(t, -1), g_ple[i], w_ple_gate[i], w_ple_proj[i], g_final,
                 i == depth - 1, tm_row, f"ple{i}")
    return x2.reshape(bsz, seq, d)
```

```python
import functools

import numpy as np
import jax
import jax.numpy as jnp
from jax import lax
from jax.experimental import pallas as pl
from jax.experimental.pallas import tpu as pltpu

F32 = jnp.float32
BF16 = jnp.bfloat16

D_MODEL = 1024
PLE_DIM = 256
NORM_EPS = 1e-6
NEG_INF = -1e30
REMOVED = -3e38
N_BRANCH = 4
BRANCH_WIDTH = 512
ATTN_BLOCK = 128

ATT_HEADS = 8
ATT_KV_HEADS = 2
ATT_HEAD_DIM = 64
ATT_REP = ATT_HEADS // ATT_KV_HEADS
SWA_WINDOW = 128
NSA_WINDOW = 512
NSA_CMP_LEN = 32
NSA_CMP_STRIDE = 16
NSA_CMP_HIDDEN = 256
NSA_SEL_LEN = 64
NSA_TOP_N = 16
NSA_FORCE_SCORE = 1e6
CMP_KEY_CHUNK = 128
SEL_KEY_TILE = 512
SEL_FEATS = SEL_KEY_TILE // NSA_SEL_LEN
MASK_BIG = 1e30

HGRN_HEADS = 4
HGRN_DIM = 128
HGRN_CHUNK = 32
HGRN_MIN_F = 1e-6
HGRN_BLOCK = 256

SSM_HEADS = 8
SSM_HEAD_DIM = 64
SSM_GROUPS = 2
SSM_STATE = 64
SSM_CONV = 4
SSM_CHUNK = 64
SSM_INNER = 512
SSM_CONV_DIM = 768
SSM_BLOCK = 256

MOE_GROUPS = 4
MOE_EPG = 8
MOE_EXPERTS = 32
MOE_FF = 256
MOE_GATHER_ROWS = 512

LANES = 128
VMEM_LIMIT = 56 * 1024 * 1024

ALIBI_SLOPES = tuple(2.0 ** (-8.0 * (h + 1) / ATT_HEADS) for h in range(ATT_HEADS))

ATT_Q_WIDTH = ATT_HEADS * LANES
KV_BLK = {name: 2 * ATT_HEADS + j for j, name in enumerate(
    ('swa_k', 'swa_v', 'nsa_k_cmp', 'nsa_v_cmp', 'nsa_k_slc', 'nsa_v_slc', 'nsa_k_win', 'nsa_v_win'))}

_SPLITS = (
    ('swa_q', 512), ('swa_k', 128), ('swa_v', 128),
    ('hgrn_q', 512), ('hgrn_f', 512), ('hgrn_i', 512), ('hgrn_g', 512),
    ('nsa_q', 512), ('nsa_k_cmp', 128), ('nsa_v_cmp', 128), ('nsa_k_slc', 128),
    ('nsa_v_slc', 128), ('nsa_k_win', 128), ('nsa_v_win', 128), ('nsa_gate', 24),
    ('ssm_z', 512), ('ssm_xbc', 768), ('ssm_dt', 8), ('merge_gate', 4096),
)
_OFF = {}
_o = 0
for _n, _w in _SPLITS:
    _OFF[_n] = (_o, _w)
    _o += _w


def _cparams(*sem):
    return pltpu.CompilerParams(dimension_semantics=sem, vmem_limit_bytes=VMEM_LIMIT)


def _sigmoid(x):
    return 1.0 / (1.0 + jnp.exp(-x))


def _silu(x):
    return x * _sigmoid(x)


def _dot(a, b):
    return jnp.dot(a.astype(BF16), b.astype(BF16), preferred_element_type=F32)


def _dot_nt(a, b):
    return lax.dot_general(a.astype(BF16), b.astype(BF16), (((1,), (1,)), ((), ())),
                           preferred_element_type=F32)


def _dot_tn(a, b):
    return lax.dot_general(a.astype(BF16), b.astype(BF16), (((0,), (0,)), ((), ())),
                           preferred_element_type=F32)


def _split3(x):
    hi = x.astype(BF16)
    r1 = x - hi.astype(F32)
    mid = r1.astype(BF16)
    return hi, mid, (r1 - mid.astype(F32)).astype(BF16)


def _prefix_dot(tri, x):
    t = tri.astype(BF16)
    return sum(jnp.dot(t, p, preferred_element_type=F32) for p in _split3(x))


def _prefix_dot_tn(x, tri):
    t = tri.astype(BF16)
    return sum(lax.dot_general(p, t, (((0,), (0,)), ((), ())), preferred_element_type=F32)
               for p in _split3(x))


def _rms(x, g):
    ms = jnp.mean(x * x, axis=-1, keepdims=True)
    return x * lax.rsqrt(ms + NORM_EPS) * g


def _norm_mm_body(x_ref, g_ref, w_ref, o_ref, hn_ref):
    @pl.when(pl.program_id(1) == 0)
    def _():
        hn_ref[...] = _rms(x_ref[...], g_ref[...]).astype(BF16)

    o_ref[...] = jnp.dot(hn_ref[...], w_ref[...], preferred_element_type=F32).astype(o_ref.dtype)


def norm_mm(x2d, g, w, out_dtype, tm, tn, name):
    m, k = x2d.shape
    n = w.shape[1]
    return pl.pallas_call(
        _norm_mm_body,
        grid=(m // tm, n // tn),
        in_specs=[pl.BlockSpec((tm, k), lambda i, j: (i, 0)),
                  pl.BlockSpec((1, k), lambda i, j: (0, 0)),
                  pl.BlockSpec((k, tn), lambda i, j: (0, j))],
        out_specs=pl.BlockSpec((tm, tn), lambda i, j: (i, j)),
        out_shape=jax.ShapeDtypeStruct((m, n), out_dtype),
        scratch_shapes=[pltpu.VMEM((tm, k), BF16)],
        compiler_params=_cparams("arbitrary", "arbitrary"),
        name=name,
    )(x2d, g.reshape(1, k), w)


def _ones_row(g):
    return ATT_HEAD_DIM if g == 0 else 0


def _vt_body(v_ref, o_ref):
    vt = jnp.transpose(v_ref[0].astype(F32))
    rowid = lax.broadcasted_iota(jnp.int32, vt.shape, 0)
    for g in range(ATT_KV_HEADS):
        aug = jnp.where(rowid // ATT_HEAD_DIM == g, vt, jnp.where(rowid == _ones_row(g), 1.0, 0.0))
        o_ref[0, g] = aug.astype(BF16)


def v_transposed(arr, v_blk, name):
    bsz, seq, _ = arr.shape
    rows = min(2048, seq)
    return pl.pallas_call(
        _vt_body,
        grid=(bsz, seq // rows),
        in_specs=[pl.BlockSpec((1, rows, LANES), lambda b, n: (b, n, v_blk))],
        out_specs=pl.BlockSpec((1, ATT_KV_HEADS, LANES, rows), lambda b, n: (b, 0, 0, n)),
        out_shape=jax.ShapeDtypeStruct((bsz, ATT_KV_HEADS, LANES, seq), BF16),
        compiler_params=_cparams("arbitrary", "arbitrary"),
        name=name,
    )(arr)


def _q_stack(q_ref, g):
    return jnp.concatenate([q_ref[0, :, (g * ATT_REP + r) * LANES:(g * ATT_REP + r + 1) * LANES]
                            for r in range(ATT_REP)], axis=0)


def _head(x, r):
    return x[:, r * ATTN_BLOCK:(r + 1) * ATTN_BLOCK]


def _banded_body(*refs, window, nprev, use_sink, gate_col):
    if use_sink:
        q_ref, k_ref, vt_ref, sink_ref, o_ref, ot_ref = refs
    else:
        q_ref, k_ref, vt_ref, gate_ref, o_ref, ot_ref = refs
    blk = ATTN_BLOCK
    hd = ATT_HEAD_DIM
    n = pl.program_id(1)
    nk = (nprev + 1) * blk
    start = pl.multiple_of(jnp.maximum(n - nprev, 0) * blk, blk)
    k128 = k_ref[0, pl.ds(start, nk), :]
    krow = lax.broadcasted_iota(jnp.int32, (nk, blk), 0)
    qcol = lax.broadcasted_iota(jnp.int32, (nk, blk), 1)
    rel = (n * blk + qcol) - (start + krow)
    negrel = jnp.where((rel >= 0) & (rel < window), -rel.astype(F32), NEG_INF)
    if not use_sink:
        gate_t = jnp.transpose(_sigmoid(gate_ref[0]))
    st4s = [_dot_nt(k128, _q_stack(q_ref, g)) for g in range(ATT_KV_HEADS)]
    for g in range(ATT_KV_HEADS):
        one = _ones_row(g)
        st4 = st4s[g]
        ps, ms = [], []
        for r in range(ATT_REP):
            h = g * ATT_REP + r
            st = _head(st4, r) + ALIBI_SLOPES[h] * negrel
            m = jnp.max(st, axis=0, keepdims=True)
            if use_sink:
                m = jnp.maximum(m, sink_ref[h])
            ps.append(jnp.exp(st - m).astype(BF16))
            ms.append(m)
        acc4 = jnp.dot(vt_ref[0, g, :, pl.ds(start, nk)], jnp.concatenate(ps, axis=1),
                       preferred_element_type=F32)
        for r in range(ATT_REP):
            h = g * ATT_REP + r
            acc = _head(acc4, r)
            l = acc[one:one + 1]
            if use_sink:
                l = l + jnp.exp(sink_ref[h] - ms[r])
            ot = acc[g * hd:(g + 1) * hd] / l
            if not use_sink:
                c = 3 * h + gate_col
                ot = ot * gate_t[c:c + 1]
            ot_ref[h * hd:(h + 1) * hd, :] = ot
    o_ref[0] = jnp.transpose(ot_ref[...])


def banded_attention(att, q_blk, k_blk, v_blk, *, window, sinks=None, gate=None, gate_blk=None,
                     gate_col=0, name):
    bsz, seq, _ = att.shape
    blk = ATTN_BLOCK
    nprev = (window - 1 + blk - 1) // blk
    use_sink = sinks is not None
    vt = v_transposed(att, v_blk, name + "_vt")
    in_specs = [pl.BlockSpec((1, blk, ATT_Q_WIDTH), lambda b, n: (b, n, q_blk)),
                pl.BlockSpec((1, seq, LANES), lambda b, n: (b, 0, k_blk)),
                pl.BlockSpec((1, ATT_KV_HEADS, LANES, seq), lambda b, n: (b, 0, 0, 0))]
    if use_sink:
        in_specs.append(pl.BlockSpec(memory_space=pltpu.SMEM))
        extra = sinks
    else:
        in_specs.append(pl.BlockSpec((1, blk, LANES), lambda b, n: (b, n, gate_blk)))
        extra = gate
    return pl.pallas_call(
        functools.partial(_banded_body, window=window, nprev=nprev, use_sink=use_sink,
                          gate_col=gate_col),
        grid=(bsz, seq // blk),
        in_specs=in_specs,
        out_specs=pl.BlockSpec((1, blk, 512), lambda b, n: (b, n, 0)),
        out_shape=jax.ShapeDtypeStruct((bsz, seq, 512), F32),
        scratch_shapes=[pltpu.VMEM((512, blk), F32)],
        compiler_params=_cparams("arbitrary", "arbitrary"),
        name=name,
    )(att, att, vt, extra)


def _compress_body(x_ref, w1a_ref, w1b_ref, w1_ref, pos_ref, w2_ref, o_ref):
    x = x_ref[0]
    p = jnp.dot(x, w1a_ref[...], preferred_element_type=F32)
    q = jnp.dot(x, w1b_ref[...], preferred_element_type=F32)
    ncp = x.shape[0]
    q = pltpu.roll(q, shift=ncp - 1, axis=0)
    posb = jnp.broadcast_to(pos_ref[...], (8, pos_ref.shape[1]))
    bias = _dot(posb, w1_ref[...])[0:1]
    hid = NSA_CMP_HIDDEN
    outs = []
    for g in range(ATT_KV_HEADS):
        pre = p[:, g * hid:(g + 1) * hid] + q[:, g * hid:(g + 1) * hid] + bias
        outs.append(_dot(jax.nn.gelu(pre), w2_ref[...]))
    o_ref[0] = jnp.concatenate(outs, axis=1).astype(o_ref.dtype)


def _expand_w1(w1, half):
    hd, hid, ng = ATT_HEAD_DIM, NSA_CMP_HIDDEN, ATT_KV_HEADS
    w = w1.reshape(NSA_CMP_LEN, hd, hid)[half * 16:(half + 1) * 16]
    eye = jnp.eye(ng, dtype=w1.dtype)
    out = jnp.einsum('ldj,gh->lgdhj', w, eye)
    return out.reshape(16 * ng * hd, ng * hid)


def nsa_compress(x16, pos, w1, w2, name):
    bsz, ncp, wid = x16.shape
    hid = NSA_CMP_HIDDEN
    w1a = _expand_w1(w1, 0).astype(BF16)
    w1b = _expand_w1(w1, 1).astype(BF16)
    return pl.pallas_call(
        _compress_body,
        grid=(bsz,),
        in_specs=[pl.BlockSpec((1, ncp, wid), lambda b: (b, 0, 0)),
                  pl.BlockSpec((wid, 2 * hid), lambda b: (0, 0)),
                  pl.BlockSpec((wid, 2 * hid), lambda b: (0, 0)),
                  pl.BlockSpec((NSA_CMP_LEN * ATT_HEAD_DIM, hid), lambda b: (0, 0)),
                  pl.BlockSpec((1, NSA_CMP_LEN * ATT_HEAD_DIM), lambda b: (0, 0)),
                  pl.BlockSpec((hid, ATT_HEAD_DIM), lambda b: (0, 0))],
        out_specs=pl.BlockSpec((1, ncp, LANES), lambda b: (b, 0, 0)),
        out_shape=jax.ShapeDtypeStruct((bsz, ncp, LANES), BF16),
        compiler_params=_cparams("arbitrary"),
        name=name,
    )(x16, w1a, w1b, w1.astype(BF16), pos.reshape(1, -1), w2.astype(BF16))


def _nsa_cmp_body(q_ref, kc_ref, vct_ref, ovt_ref, gate_ref, o_ref, m_ref, act_ref, ot_ref, imp_ref, *,
                  top_n):
    blk = ATTN_BLOCK
    hd = ATT_HEAD_DIM
    n = pl.program_id(1)
    kc = kc_ref[0]
    ncp = kc.shape[0]
    ns = ovt_ref.shape[0]
    crow = lax.broadcasted_iota(jnp.int32, (ncp, blk), 0)
    qcol = lax.broadcasted_iota(jnp.int32, (ncp, blk), 1)
    dist = (n * blk + qcol) - (crow * NSA_CMP_STRIDE + (NSA_CMP_LEN - 1))
    valid = dist >= 0
    negd = jnp.where(valid, -dist.astype(F32), NEG_INF)
    jrow = lax.broadcasted_iota(jnp.int32, (ns, blk), 0)
    qpos = n * blk + lax.broadcasted_iota(jnp.int32, (ns, blk), 1)
    cur = qpos // NSA_SEL_LEN
    forced = (jrow == 0) | (jrow == cur) | (jrow == cur - 1)
    causal_blk = jrow * NSA_SEL_LEN <= qpos
    jrowf = jrow.astype(F32)
    gate_t = jnp.transpose(_sigmoid(gate_ref[0]))
    ones8 = jnp.ones((8, blk), BF16)

    def attend(rows):
        st4s = [_dot_nt(kc[:rows], _q_stack(q_ref, g)) for g in range(ATT_KV_HEADS)]
        for g in range(ATT_KV_HEADS):
            one = _ones_row(g)
            es = []
            for r in range(ATT_REP):
                st = _head(st4s[g], r) + ALIBI_SLOPES[g * ATT_REP + r] * negd[:rows]
                m = jnp.max(st, axis=0, keepdims=True)
                es.append(jnp.where(valid[:rows], jnp.exp(st - m), 0.0))
            acc4 = jnp.dot(vct_ref[0, g, :, :rows], jnp.concatenate([e.astype(BF16) for e in es], axis=1),
                           preferred_element_type=F32)
            psum = jnp.zeros((rows, blk), F32)
            for r in range(ATT_REP):
                h = g * ATT_REP + r
                acc = _head(acc4, r)
                l = acc[one:one + 1]
                inv = jnp.where(l > 0.0, 1.0 / jnp.where(l > 0.0, l, 1.0), 0.0)
                ot_ref[h * hd:(h + 1) * hd, :] = acc[g * hd:(g + 1) * hd] * inv * gate_t[3 * h:3 * h + 1]
                psum = psum + es[r] * inv
            imp_ref[g] = _dot(ovt_ref[:, :rows], psum)

    chunk = min(CMP_KEY_CHUNK, ncp)
    nchunks = ncp // chunk
    need = jnp.minimum((n * (blk // NSA_CMP_STRIDE) + blk // NSA_CMP_STRIDE - 1 + chunk - 1) // chunk,
                       nchunks)
    for c in range(1, nchunks + 1):
        pl.when(need == c)(functools.partial(attend, c * chunk))

    for g in range(ATT_KV_HEADS):
        score = jnp.where(forced, NSA_FORCE_SCORE, jnp.where(causal_blk, imp_ref[g], NEG_INF))
        sel = jnp.zeros((ns, blk), F32)
        for _ in range(top_n):
            mx = jnp.max(score, axis=0, keepdims=True)
            idx = jnp.min(jnp.where(score == mx, jrowf, float(ns)), axis=0, keepdims=True)
            hit = jrowf == idx
            sel = jnp.where(hit, 1.0, sel)
            score = jnp.where(hit, REMOVED, score)
        m_ref[0, g] = sel
        act_ref[0, 0, g * 8:(g + 1) * 8, :] = _dot_nt(ones8, sel)
    o_ref[0] = jnp.transpose(ot_ref[...])


def _overlap_matrix_t(seq):
    n_cmp = seq // NSA_CMP_STRIDE
    n_sel = seq // NSA_SEL_LEN
    cs = np.arange(n_cmp)[None, :] * NSA_CMP_STRIDE
    ss = np.arange(n_sel)[:, None] * NSA_SEL_LEN
    ov = np.clip(np.minimum(cs + NSA_CMP_LEN, ss + NSA_SEL_LEN) - np.maximum(cs, ss), 0, None)
    return jnp.asarray(ov / NSA_CMP_LEN, dtype=BF16)


def nsa_cmp_select(att, q_blk, kc, vc, gate, gate_blk, name):
    bsz, seq, _ = att.shape
    blk = ATTN_BLOCK
    nb = seq // blk
    ncp = seq // NSA_CMP_STRIDE
    ns = seq // NSA_SEL_LEN
    top_n = min(NSA_TOP_N, ns)
    vct = v_transposed(vc, 0, name + "_vt")
    return pl.pallas_call(
        functools.partial(_nsa_cmp_body, top_n=top_n),
        grid=(bsz, nb),
        in_specs=[pl.BlockSpec((1, blk, ATT_Q_WIDTH), lambda b, n: (b, n, q_blk)),
                  pl.BlockSpec((1, ncp, LANES), lambda b, n: (b, 0, 0)),
                  pl.BlockSpec((1, ATT_KV_HEADS, LANES, ncp), lambda b, n: (b, 0, 0, 0)),
                  pl.BlockSpec((ns, ncp), lambda b, n: (0, 0)),
                  pl.BlockSpec((1, blk, LANES), lambda b, n: (b, n, gate_blk))],
        out_specs=[pl.BlockSpec((1, blk, 512), lambda b, n: (b, n, 0)),
                   pl.BlockSpec((1, ATT_KV_HEADS, ns, blk), lambda b, n: (b, 0, 0, n)),
                   pl.BlockSpec((1, 1, ATT_KV_HEADS * 8, ns), lambda b, n: (b, n, 0, 0))],
        out_shape=[jax.ShapeDtypeStruct((bsz, seq, 512), F32),
                   jax.ShapeDtypeStruct((bsz, ATT_KV_HEADS, ns, seq), F32),
                   jax.ShapeDtypeStruct((bsz, nb, ATT_KV_HEADS * 8, ns), F32)],
        scratch_shapes=[pltpu.VMEM((512, blk), F32), pltpu.VMEM((ATT_KV_HEADS, ns, blk), F32)],
        compiler_params=_cparams("arbitrary", "arbitrary"),
        name=name,
    )(att, kc, vct, _overlap_matrix_t(seq), gate)


def _k_aug_body(k_ref, o_ref):
    k = k_ref[0].astype(F32)
    rows = k.shape[0]
    kin = lax.broadcasted_iota(jnp.int32, (rows, LANES), 0) % SEL_KEY_TILE
    lane = lax.broadcasted_iota(jnp.int32, (rows, LANES), 1)
    for g in range(ATT_KV_HEADS):
        f = lane - (ATT_HEAD_DIM if g == 0 else 0)
        feat = jnp.where(f == kin // NSA_SEL_LEN, 1.0, 0.0)
        feat = jnp.where(f == SEL_FEATS, (kin % 256).astype(F32), feat)
        feat = jnp.where(f == SEL_FEATS + 1, (kin // 256 * 256).astype(F32), feat)
        feat = jnp.where((f == SEL_FEATS + 2) | (f == SEL_FEATS + 3), 1.0, feat)
        o_ref[0, g] = jnp.where(lane // ATT_HEAD_DIM == g, k, feat).astype(BF16)


def k_augmented(arr, k_blk, name):
    bsz, seq, _ = arr.shape
    rows = min(2048, seq)
    return pl.pallas_call(
        _k_aug_body,
        grid=(bsz, seq // rows),
        in_specs=[pl.BlockSpec((1, rows, LANES), lambda b, n: (b, n, k_blk))],
        out_specs=pl.BlockSpec((1, ATT_KV_HEADS, rows, LANES), lambda b, n: (b, 0, n, 0)),
        out_shape=jax.ShapeDtypeStruct((bsz, ATT_KV_HEADS, seq, LANES), BF16),
        compiler_params=_cparams("arbitrary", "arbitrary"),
        name=name,
    )(arr)


def _query_feature_rows():
    out = np.zeros((ATT_KV_HEADS, 2 * SEL_FEATS, ATT_REP * ATTN_BLOCK), np.float32)
    qin = np.arange(ATTN_BLOCK, dtype=np.float32)
    for g in range(ATT_KV_HEADS):
        for r in range(ATT_REP):
            slope = ALIBI_SLOPES[g * ATT_REP + r]
            cols = slice(r * ATTN_BLOCK, (r + 1) * ATTN_BLOCK)
            out[g, 0, cols] = slope
            out[g, 1, cols] = slope
            out[g, 3, cols] = -slope * qin
            out[g, SEL_FEATS + 2, cols] = -slope
    return jnp.asarray(out)


def _nsa_sel_body(tiles_ref, cnt_ref, q_ref, k_ref, vt_ref, m_ref, gate_ref, fq_ref, o_ref, ot_ref,
                  acc_ref, mx_ref, dq_ref, sa_ref, sb_ref, *, ntl):
    blk = ATTN_BLOCK
    hd = ATT_HEAD_DIM
    tk = SEL_KEY_TILE
    sl = NSA_SEL_LEN
    per = tk // sl
    b = pl.program_id(0)
    n = pl.program_id(1)
    nb = pl.num_programs(1)
    krow = lax.broadcasted_iota(jnp.int32, (tk, blk), 0)
    qcol = lax.broadcasted_iota(jnp.int32, (tk, blk), 1)
    dq_ref[...] = (qcol - krow).astype(F32)
    gate_t = jnp.transpose(_sigmoid(gate_ref[0]))
    pad_rows = jnp.zeros((hd - 2 * SEL_FEATS, ATT_REP * blk), BF16)
    for g in range(ATT_KV_HEADS):
        one = _ones_row(g)
        lrow = (b * nb + n) * ATT_KV_HEADS + g
        qt = jnp.concatenate(
            [jnp.transpose(q_ref[0, :, (g * ATT_REP + r) * LANES:(g * ATT_REP + r + 1) * LANES]
                           .astype(F32))[g * hd:(g + 1) * hd] for r in range(ATT_REP)],
            axis=1).astype(BF16)
        acc_ref[...] = jnp.zeros_like(acc_ref)
        mx_ref[...] = jnp.full_like(mx_ref, NEG_INF)

        def tile_scores(j, live):
            t = tiles_ref[lrow * ntl + j]
            base = pl.multiple_of(t * tk, tk)
            sel8 = m_ref[0, g, pl.ds(pl.multiple_of(t * per, per), per), :]
            if live is not None:
                sel8 = jnp.where(live, sel8, 0.0)
            off = (n * blk - base).astype(F32)
            mask_rows = jnp.concatenate([(sel8 - 1.0) * MASK_BIG] * ATT_REP, axis=1)
            alibi_rows = fq_ref[g, 0:SEL_FEATS, :] + off * fq_ref[g, SEL_FEATS:2 * SEL_FEATS, :]
            feats = jnp.concatenate([mask_rows, alibi_rows], axis=0).astype(BF16)
            rhs = jnp.concatenate([qt, feats, pad_rows] if g == 0 else [feats, pad_rows, qt], axis=0)
            return jnp.dot(k_ref[0, g, pl.ds(base, tk), :], rhs,
                           preferred_element_type=F32)

        def tile_finish(j, st4, diag):
            base = pl.multiple_of(tiles_ref[lrow * ntl + j] * tk, tk)
            off = (n * blk - base).astype(F32)
            if diag:
                causal = (dq_ref[...] + off) >= 0.0
            ps, ms = [], []
            for r in range(ATT_REP):
                st = _head(st4, r)
                if diag:
                    st = jnp.where(causal, st, NEG_INF)
                m = jnp.max(st, axis=0, keepdims=True)
                ps.append(jnp.exp(st - m).astype(BF16))
                ms.append(m)
            acc = jnp.dot(vt_ref[0, g, :, pl.ds(base, tk)], jnp.concatenate(ps, axis=1),
                          preferred_element_type=F32)
            return jnp.concatenate(ms, axis=1), acc

        cnt = cnt_ref[lrow]

        def scores_into(s_ref, j):
            s_ref[...] = tile_scores(jnp.minimum(j, ntl - 1), j < cnt)

        def finish_from(s_ref, j, diag):
            m_t, acc_t = tile_finish(jnp.minimum(j, ntl - 1), s_ref[...], diag)
            m_old = mx_ref[...]
            m_new = jnp.maximum(m_old, m_t)
            acc_ref[...] = jnp.exp(m_old - m_new) * acc_ref[...] + jnp.exp(m_t - m_new) * acc_t
            mx_ref[...] = m_new

        def two_tiles(k, diag):
            scores_into(sb_ref, 2 * k + 1)
            finish_from(sa_ref, 2 * k, diag)
            scores_into(sa_ref, 2 * k + 2)
            finish_from(sb_ref, 2 * k + 1, False)

        scores_into(sa_ref, 0)
        two_tiles(0, True)

        def later(k, carry):
            two_tiles(k, False)
            return carry

        lax.fori_loop(1, (cnt + 1) // 2, later, 0)
        acc4 = acc_ref[...]
        for r in range(ATT_REP):
            h = g * ATT_REP + r
            acc = _head(acc4, r)
            ot = acc[g * hd:(g + 1) * hd] / acc[one:one + 1]
            ot_ref[h * hd:(h + 1) * hd, :] = ot * gate_t[3 * h + 1:3 * h + 2]
    o_ref[0] = jnp.transpose(ot_ref[...])


def _tile_lists(act, seq):
    bsz, nb = act.shape[:2]
    ns = act.shape[-1]
    tk = SEL_KEY_TILE
    ntl = seq // tk
    per = tk // NSA_SEL_LEN
    cnt_blk = act.reshape(bsz, nb, ATT_KV_HEADS, 8, ns)[:, :, :, 0, :]
    hit = cnt_blk.reshape(bsz, nb, ATT_KV_HEADS, ntl, per).sum(-1) > 0.5
    tidx = jnp.arange(ntl, dtype=jnp.int32)
    diag = (jnp.arange(nb, dtype=jnp.int32) * ATTN_BLOCK + ATTN_BLOCK - 1) // tk
    active = hit & (tidx[None, None, None, :] <= diag[None, :, None, None])
    act_i = active.astype(jnp.int32)
    rank = jnp.cumsum(act_i[..., ::-1], axis=-1)[..., ::-1] - 1
    slot = (active[..., :, None] & (rank[..., :, None] == tidx)).astype(jnp.int32)
    tiles = (slot * tidx[:, None]).sum(-2).reshape(-1)
    cnt = act_i.sum(-1).reshape(-1)
    return tiles, cnt, ntl


def nsa_selected(att, q_blk, k_blk, v_blk, mask, act, gate, gate_blk, name):
    bsz, seq, _ = att.shape
    blk = ATTN_BLOCK
    ns = seq // NSA_SEL_LEN
    tiles, cnt, ntl = _tile_lists(act, seq)
    vt = v_transposed(att, v_blk, name + "_vt")
    kaug = k_augmented(att, k_blk, name + "_k")
    grid_spec = pltpu.PrefetchScalarGridSpec(
        num_scalar_prefetch=2,
        grid=(bsz, seq // blk),
        in_specs=[pl.BlockSpec((1, blk, ATT_Q_WIDTH), lambda b, n, *_: (b, n, q_blk)),
                  pl.BlockSpec((1, ATT_KV_HEADS, seq, LANES), lambda b, n, *_: (b, 0, 0, 0)),
                  pl.BlockSpec((1, ATT_KV_HEADS, LANES, seq), lambda b, n, *_: (b, 0, 0, 0)),
                  pl.BlockSpec((1, ATT_KV_HEADS, ns, blk), lambda b, n, *_: (b, 0, 0, n)),
                  pl.BlockSpec((1, blk, LANES), lambda b, n, *_: (b, n, gate_blk)),
                  pl.BlockSpec((ATT_KV_HEADS, 2 * SEL_FEATS, ATT_REP * blk), lambda b, n, *_: (0, 0, 0))],
        out_specs=pl.BlockSpec((1, blk, 512), lambda b, n, *_: (b, n, 0)),
        scratch_shapes=[pltpu.VMEM((512, blk), F32),
                        pltpu.VMEM((LANES, ATT_REP * blk), F32),
                        pltpu.VMEM((1, ATT_REP * blk), F32),
                        pltpu.VMEM((SEL_KEY_TILE, blk), F32),
                        pltpu.VMEM((SEL_KEY_TILE, ATT_REP * blk), F32),
                        pltpu.VMEM((SEL_KEY_TILE, ATT_REP * blk), F32)],
    )
    return pl.pallas_call(
        functools.partial(_nsa_sel_body, ntl=ntl),
        grid_spec=grid_spec,
        out_shape=jax.ShapeDtypeStruct((bsz, seq, 512), F32),
        compiler_params=_cparams("arbitrary", "arbitrary"),
        name=name,
    )(tiles, cnt, att, kaug, vt, mask, gate, _query_feature_rows())


def _hgrn_body(q_ref, f_ref, i_ref, g_ref, lbp_ref, ng_ref, o_ref, st_ref, b_ref, k_ref, *, layer):
    blk = HGRN_BLOCK
    ch = HGRN_CHUNK
    dk = HGRN_DIM

    @pl.when(pl.program_id(1) == 0)
    def _():
        st_ref[...] = jnp.zeros_like(st_ref)

    lbp = lbp_ref[...]
    e = jnp.exp(lbp - jnp.max(lbp, axis=0, keepdims=True))
    sm = e / jnp.sum(e, axis=0, keepdims=True)
    lb = jnp.zeros((1, lbp.shape[1]), F32)
    for d in range(1, layer + 1):
        lb = lb + sm[d:d + 1]
    z = f_ref[0]
    f = lb + (1.0 - lb) * _sigmoid(z)
    logf = jnp.log(jnp.maximum(f, HGRN_MIN_F))
    k_ref[...] = (1.0 - lb) * _sigmoid(-z)
    tr = lax.broadcasted_iota(jnp.int32, (blk, blk), 0)
    tc = lax.broadcasted_iota(jnp.int32, (blk, blk), 1)
    tri = jnp.where((tr // ch == tc // ch) & (tc <= tr), 1.0, 0.0)
    b_ref[...] = _prefix_dot(tri, logf)
    cr = lax.broadcasted_iota(jnp.int32, (ch, ch), 0)
    cc = lax.broadcasted_iota(jnp.int32, (ch, ch), 1)
    causal = cc <= cr
    ng = ng_ref[...]

    for c in range(blk // ch):
        r0 = c * ch
        bc = b_ref[pl.ds(r0, ch), :]
        qc = q_ref[0, pl.ds(r0, ch), :]
        kc = k_ref[pl.ds(r0, ch), :]
        vc = i_ref[0, pl.ds(r0, ch), :]
        gc = g_ref[0, pl.ds(r0, ch), :]
        b_mid = bc[ch // 2:ch // 2 + 1]
        b_last = bc[ch - 1:ch]
        qa = qc * jnp.exp(bc - b_mid)
        ka = kc * jnp.exp(b_mid - bc)
        qe = qc * jnp.exp(bc)
        kl = kc * jnp.exp(b_last - bc)
        dec = jnp.exp(b_last)
        for h in range(HGRN_HEADS):
            sl = slice(h * dk, (h + 1) * dk)
            a = jnp.where(causal, _dot_nt(qa[:, sl], ka[:, sl]), 0.0)
            st = st_ref[h]
            o = _dot(a, vc[:, sl]) + _dot_nt(qe[:, sl], st)
            st_ref[h] = st * dec[:, sl] + _dot_tn(vc[:, sl], kl[:, sl])
            o = _rms(o, ng[:, sl]) * _silu(gc[:, sl])
            o_ref[0, pl.ds(r0, ch), sl] = o


def hgrn2(hproj, first_blk, lower_bounds, norm_g, layer, name):
    bsz, seq, _ = hproj.shape
    blk = HGRN_BLOCK
    wid = HGRN_HEADS * HGRN_DIM
    depth = lower_bounds.shape[0]

    def col(j):
        return pl.BlockSpec((1, blk, wid), lambda b, n: (b, n, first_blk + j))

    return pl.pallas_call(
        functools.partial(_hgrn_body, layer=layer),
        grid=(bsz, seq // blk),
        in_specs=[col(0), col(1), col(2), col(3),
                  pl.BlockSpec((depth, wid), lambda b, n: (0, 0)),
                  pl.BlockSpec((1, wid), lambda b, n: (0, 0))],
        out_specs=pl.BlockSpec((1, blk, wid), lambda b, n: (b, n, 0)),
        out_shape=jax.ShapeDtypeStruct((bsz, seq, wid), F32),
        scratch_shapes=[pltpu.VMEM((HGRN_HEADS, HGRN_DIM, HGRN_DIM), F32),
                        pltpu.VMEM((blk, wid), F32),
                        pltpu.VMEM((blk, wid), F32)],
        compiler_params=_cparams("arbitrary", "arbitrary"),
        name=name,
    )(hproj, hproj, hproj, hproj, lower_bounds, norm_g.reshape(1, wid))


def _ssd_body(xbc_ref, dt_ref, z_ref, cw_ref, cb_ref, dtb_ref, alog_ref, alogw_ref, dskw_ref, ng_ref,
              o_ref, xp_ref, act_ref, st_ref, y_ref):
    blk = SSM_BLOCK
    ch = SSM_CHUNK
    hp = SSM_HEAD_DIM
    ns = SSM_STATE
    rep = SSM_HEADS // SSM_GROUPS
    pad = 8

    @pl.when(pl.program_id(1) == 0)
    def _():
        xp_ref[0:pad, :] = jnp.zeros((pad, SSM_CONV_DIM), F32)
        st_ref[...] = jnp.zeros_like(st_ref)

    xin = xbc_ref[0]
    xp_ref[pad:pad + blk, :] = xin
    cw = cw_ref[...]
    conv = cb_ref[...] + cw[SSM_CONV - 1:SSM_CONV] * xin
    for j in range(SSM_CONV - 1):
        shift = SSM_CONV - 1 - j
        conv = conv + cw[j:j + 1] * xp_ref[pl.ds(pad - shift, blk), :]
    xp_ref[0:pad, :] = xin[blk - pad:blk]
    act_ref[...] = _silu(conv)

    def softplus(v):
        return jnp.maximum(v, 0.0) + jnp.log(1.0 + jnp.exp(-jnp.abs(v)))

    dt = softplus(dt_ref[0] + dtb_ref[...])
    a_all = dt * (-jnp.exp(alog_ref[...]))
    er = lax.broadcasted_iota(jnp.int32, (LANES, SSM_INNER), 0)
    ec = lax.broadcasted_iota(jnp.int32, (LANES, SSM_INNER), 1)
    spread = jnp.where(ec // hp == er, 1.0, 0.0).astype(BF16)
    dtw = sum(jnp.dot(p, spread, preferred_element_type=F32) for p in _split3(dt))
    aw = dtw * (-jnp.exp(alogw_ref[...]))
    cr = lax.broadcasted_iota(jnp.int32, (ch, ch), 0)
    cc = lax.broadcasted_iota(jnp.int32, (ch, ch), 1)
    causal = cc <= cr
    tri = jnp.where(causal, 1.0, 0.0)
    tri_t = jnp.where(cr <= cc, 1.0, 0.0)

    for c in range(blk // ch):
        r0 = c * ch
        acs_t = _prefix_dot_tn(a_all[r0:r0 + ch], tri_t)
        acs = _prefix_dot(tri, aw[r0:r0 + ch])
        last = acs[ch - 1:ch]
        dtc = dtw[r0:r0 + ch]
        xc = act_ref[r0:r0 + ch, 0:SSM_INNER]
        xdt = dtc * xc
        grow = jnp.exp(acs)
        wdec = jnp.exp(last - acs) * dtc
        keep = jnp.exp(last)
        skip = dskw_ref[...] * xc
        for g in range(SSM_GROUPS):
            bm = act_ref[r0:r0 + ch, SSM_INNER + g * ns:SSM_INNER + (g + 1) * ns]
            cm = act_ref[r0:r0 + ch, SSM_INNER + SSM_GROUPS * ns + g * ns:
                         SSM_INNER + SSM_GROUPS * ns + (g + 1) * ns]
            cb = _dot_nt(cm, bm)
            for r in range(rep):
                h = g * rep + r
                hl = slice(h * hp, (h + 1) * hp)
                lmat = jnp.where(causal, jnp.exp(acs[:, hl] - acs_t[h:h + 1, :]), 0.0)
                st = st_ref[h]
                y = _dot(cb * lmat, xdt[:, hl]) + grow[:, hl] * _dot(cm, st) + skip[:, hl]
                st_ref[h] = keep[:, hl] * st + _dot_tn(bm * wdec[:, hl], xc[:, hl])
                y_ref[r0:r0 + ch, hl] = y
    yz = y_ref[...] * _silu(z_ref[0])
    o_ref[0] = _rms(yz, ng_ref[...])


def mamba2(sproj, conv_w, conv_b, dt_bias, a_log, d_skip, norm_g, name):
    bsz, seq, _ = sproj.shape
    blk = SSM_BLOCK
    padh = LANES - SSM_HEADS
    dtb = jnp.pad(dt_bias, (0, padh)).reshape(1, LANES)
    alog = jnp.pad(a_log, (0, padh)).reshape(1, LANES)

    def wide(v):
        return jnp.repeat(v, SSM_HEAD_DIM).reshape(1, SSM_INNER)

    def row(w):
        return pl.BlockSpec((1, w), lambda b, n: (0, 0))

    return pl.pallas_call(
        _ssd_body,
        grid=(bsz, seq // blk),
        in_specs=[pl.BlockSpec((1, blk, SSM_CONV_DIM), lambda b, n: (b, n, 0)),
                  pl.BlockSpec((1, blk, LANES), lambda b, n: (b, n, 6)),
                  pl.BlockSpec((1, blk, SSM_INNER), lambda b, n: (b, n, 2)),
                  pl.BlockSpec((SSM_CONV, SSM_CONV_DIM), lambda b, n: (0, 0)),
                  row(SSM_CONV_DIM), row(LANES), row(LANES), row(SSM_INNER), row(SSM_INNER),
                  row(SSM_INNER)],
        out_specs=pl.BlockSpec((1, blk, SSM_INNER), lambda b, n: (b, n, 0)),
        out_shape=jax.ShapeDtypeStruct((bsz, seq, SSM_INNER), F32),
        scratch_shapes=[pltpu.VMEM((blk + 8, SSM_CONV_DIM), F32),
                        pltpu.VMEM((blk, SSM_CONV_DIM), F32),
                        pltpu.VMEM((SSM_HEADS, SSM_STATE, SSM_HEAD_DIM), F32),
                        pltpu.VMEM((blk, SSM_INNER), F32)],
        compiler_params=_cparams("arbitrary", "arbitrary"),
        name=name,
    )(sproj, sproj, sproj, conv_w, conv_b.reshape(1, -1), dtb, alog, wide(a_log), wide(d_skip),
      norm_g.reshape(1, -1))


def _merge_body(ya_ref, yb_ref, yc1_ref, yc2_ref, yc3_ref, yd_ref, x_ref, gm_ref, wmg_ref, wbr_ref,
                wout_ref, gf_ref, wr_ref, o_ref, gid_ref):
    ys = (ya_ref[...], yb_ref[...], yc1_ref[...] + yc2_ref[...] + yc3_ref[...], yd_ref[...])
    x = x_ref[...]
    hn = _rms(x, gm_ref[...]).astype(BF16)
    u = None
    for nbr in range(N_BRANCH):
        gate = jnp.dot(hn, wmg_ref[:, nbr * D_MODEL:(nbr + 1) * D_MODEL], preferred_element_type=F32)
        t = _sigmoid(gate) * _dot(ys[nbr], wbr_ref[nbr])
        u = t if u is None else u + t
    xn = x + _dot(u, wout_ref[...])
    o_ref[...] = xn
    gid_ref[...] = _top_group(xn, gf_ref, wr_ref)


def merge(ya, yb, yc1, yc2, yc3, yd, x2d, g_mix, w_mgate, w_branch, w_out, g_ffn, wr, tm, name):
    m = x2d.shape[0]

    def rows(w):
        return pl.BlockSpec((tm, w), lambda i: (i, 0))

    return pl.pallas_call(
        _merge_body,
        grid=(m // tm,),
        in_specs=[rows(512)] * 6 + [rows(D_MODEL),
                                    pl.BlockSpec((1, D_MODEL), lambda i: (0, 0)),
                                    pl.BlockSpec((D_MODEL, N_BRANCH * D_MODEL), lambda i: (0, 0)),
                                    pl.BlockSpec((N_BRANCH, BRANCH_WIDTH, D_MODEL), lambda i: (0, 0, 0)),
                                    pl.BlockSpec((D_MODEL, D_MODEL), lambda i: (0, 0)),
                                    pl.BlockSpec((1, D_MODEL), lambda i: (0, 0)),
                                    pl.BlockSpec((2, D_MODEL, LANES), lambda i: (0, 0, 0))],
        out_specs=[rows(D_MODEL), rows(LANES)],
        out_shape=[jax.ShapeDtypeStruct((m, D_MODEL), F32), jax.ShapeDtypeStruct((m, LANES), jnp.int32)],
        compiler_params=_cparams("arbitrary"),
        name=name,
    )(ya, yb, yc1, yc2, yc3, yd, x2d, g_mix.reshape(1, -1), w_mgate, w_branch.astype(BF16),
      w_out.astype(BF16), g_ffn.reshape(1, -1), wr)


def _router_logits(hn, wr_ref):
    h_hi = hn.astype(BF16)
    h_lo = (hn - h_hi.astype(F32)).astype(BF16)
    return (jnp.dot(h_hi, wr_ref[0], preferred_element_type=F32)
            + jnp.dot(h_hi, wr_ref[1], preferred_element_type=F32)
            + jnp.dot(h_lo, wr_ref[0], preferred_element_type=F32))


def _group_lanes(lane):
    return (lane >= MOE_EXPERTS) & (lane < MOE_EXPERTS + MOE_GROUPS)


def _top_group(x, g_ref, wr_ref):
    tm = x.shape[0]
    lane = lax.broadcasted_iota(jnp.int32, (tm, LANES), 1)
    logits = _router_logits(_rms(x, g_ref[...]), wr_ref)
    lg = jnp.where(_group_lanes(lane), logits, -jnp.inf)
    mg = jnp.max(lg, axis=-1, keepdims=True)
    gi = jnp.min(jnp.where(lg == mg, lane.astype(F32), 1e9), axis=-1, keepdims=True) - MOE_EXPERTS
    return jnp.broadcast_to(gi, (tm, LANES)).astype(jnp.int32)


def _gather_body(idx_ref, nlive_ref, src_ref, o_ref, buf, sem):
    rows = o_ref.shape[0]
    i = pl.program_id(0)
    nlive = nlive_ref[0]

    def live(tile):
        return tile * rows < nlive

    def issue_tile(tile):
        base = tile * rows
        slot = tile % 2

        def issue(r, carry):
            pltpu.make_async_copy(src_ref.at[pl.ds(idx_ref[base + r], 1)], buf.at[slot, pl.ds(r, 1)],
                                  sem.at[slot]).start()
            return carry

        lax.fori_loop(0, rows, issue, 0, unroll=8)

    @pl.when((i == 0) & live(0))
    def _():
        issue_tile(0)

    @pl.when((i + 1 < pl.num_programs(0)) & live(i + 1))
    def _():
        issue_tile(i + 1)

    @pl.when(live(i))
    def _():
        slot = i % 2
        pltpu.make_async_copy(src_ref.at[pl.ds(0, rows)], buf.at[slot], sem.at[slot]).wait()
        o_ref[...] = buf[slot]

    @pl.when(jnp.logical_not(live(i)))
    def _():
        o_ref[...] = jnp.zeros_like(o_ref)


def gather_rows(src, idx, nlive, rows, name):
    n = idx.shape[0]
    d = src.shape[1]
    grid_spec = pltpu.PrefetchScalarGridSpec(
        num_scalar_prefetch=2,
        grid=(n // rows,),
        in_specs=[pl.BlockSpec(memory_space=pl.ANY)],
        out_specs=pl.BlockSpec((rows, d), lambda i, *_: (i, 0)),
        scratch_shapes=[pltpu.VMEM((2, rows, d), src.dtype), pltpu.SemaphoreType.DMA((2,))],
    )
    return pl.pallas_call(
        _gather_body,
        grid_spec=grid_spec,
        out_shape=jax.ShapeDtypeStruct((n, d), src.dtype),
        compiler_params=_cparams("arbitrary"),
        name=name,
    )(idx, jnp.reshape(nlive, (1,)).astype(jnp.int32), src)


def _moe_body(tg_ref, x_ref, g_ref, wr_ref, wg_ref, wu_ref, wd_ref, o_ref, hn_ref, comb_ref):
    e = pl.program_id(1)
    grp = tg_ref[pl.program_id(0)]
    tm = x_ref.shape[0]
    lane = lax.broadcasted_iota(jnp.int32, (tm, LANES), 1)
    lanef = lane.astype(F32)
    live = grp < MOE_GROUPS

    @pl.when(e == 0)
    def _():
        o_ref[...] = x_ref[...]

    @pl.when((e == 0) & live)
    def _():
        hn = _rms(x_ref[...], g_ref[...])
        hn_ref[...] = hn.astype(BF16)
        logits = _router_logits(hn, wr_ref)
        is_grp = _group_lanes(lane)
        lg = jnp.where(is_grp, logits, -jnp.inf)
        mg = jnp.max(lg, axis=-1, keepdims=True)
        sg = jnp.sum(jnp.where(is_grp, jnp.exp(lg - mg), 0.0), axis=-1, keepdims=True)
        lt = jnp.sum(jnp.where(lane == MOE_EXPERTS + grp, logits, 0.0), axis=-1, keepdims=True)
        g_w = jnp.exp(lt - mg) / sg
        in_grp = (lane < MOE_EXPERTS) & (lane // MOE_EPG == grp)
        le = jnp.where(in_grp, logits, -jnp.inf)
        m1 = jnp.max(le, axis=-1, keepdims=True)
        i1 = jnp.min(jnp.where(le == m1, lanef, 1e9), axis=-1, keepdims=True)
        le2 = jnp.where(lanef == i1, -jnp.inf, le)
        m2 = jnp.max(le2, axis=-1, keepdims=True)
        i2 = jnp.min(jnp.where(le2 == m2, lanef, 1e9), axis=-1, keepdims=True)
        e2 = jnp.exp(m2 - m1)
        den = 1.0 + e2
        comb_ref[...] = (jnp.where(lanef == i1, g_w / den, 0.0)
                         + jnp.where(lanef == i2, g_w * e2 / den, 0.0))

    @pl.when(live)
    def _():
        hn = hn_ref[...]
        gate = jnp.dot(hn, wg_ref[0].astype(BF16), preferred_element_type=F32)
        up = jnp.dot(hn, wu_ref[0].astype(BF16), preferred_element_type=F32)
        ce = jnp.sum(jnp.where(lane == grp * MOE_EPG + e, comb_ref[...], 0.0), axis=-1, keepdims=True)
        o_ref[...] += _dot(_silu(gate) * up * ce, wd_ref[0])


def _moe_plan(gid, tm):
    t = gid.shape[0]
    onehot = (gid[:, None] == jnp.arange(MOE_GROUPS, dtype=jnp.int32)).astype(jnp.int32)
    csum = jnp.cumsum(onehot, axis=0)
    counts = csum[-1]
    rank = (csum * onehot).sum(axis=1) - 1
    padded = (counts + tm - 1) // tm * tm
    pend = jnp.cumsum(padded)
    dest = (pend - padded)[gid] + rank
    r_pad = t + MOE_GROUPS * tm
    row_token = (jnp.arange(r_pad, dtype=jnp.int32) % t).at[dest].set(jnp.arange(t, dtype=jnp.int32))
    tile_start = jnp.arange(r_pad // tm, dtype=jnp.int32) * tm
    tile_group = (tile_start[:, None] >= pend[None, :]).sum(axis=1)
    return row_token, dest.astype(jnp.int32), tile_group.astype(jnp.int32), pend[-1]


def router_weights(w_grp, w_exp):
    wr32 = jnp.concatenate([w_exp, w_grp,
                            jnp.zeros((D_MODEL, LANES - MOE_EXPERTS - MOE_GROUPS), F32)], axis=1)
    wr_hi = wr32.astype(BF16)
    return jnp.stack([wr_hi, (wr32 - wr_hi.astype(F32)).astype(BF16)])


def moe(x2d, gid, g_ffn, wr, w_gate, w_up, w_down, layer, tm, name):
    m = x2d.shape[0]
    gf = g_ffn.reshape(1, -1)
    row_token, dest, tile_group, n_sorted = _moe_plan(gid, tm)
    rows_dma = min(MOE_GATHER_ROWS, tm)
    xs = gather_rows(x2d, row_token, n_sorted, rows_dma, name + "_gather")

    def expert(i, e, tg):
        return (layer * MOE_EXPERTS + jnp.minimum(tg[i], MOE_GROUPS - 1) * MOE_EPG + e, 0, 0)

    grid_spec = pltpu.PrefetchScalarGridSpec(
        num_scalar_prefetch=1,
        grid=(xs.shape[0] // tm, MOE_EPG),
        in_specs=[pl.BlockSpec((tm, D_MODEL), lambda i, e, tg: (i, 0)),
                  pl.BlockSpec((1, D_MODEL), lambda i, e, tg: (0, 0)),
                  pl.BlockSpec((2, D_MODEL, LANES), lambda i, e, tg: (0, 0, 0)),
                  pl.BlockSpec((1, D_MODEL, MOE_FF), expert),
                  pl.BlockSpec((1, D_MODEL, MOE_FF), expert),
                  pl.BlockSpec((1, MOE_FF, D_MODEL), expert)],
        out_specs=pl.BlockSpec((tm, D_MODEL), lambda i, e, tg: (i, 0)),
        scratch_shapes=[pltpu.VMEM((tm, D_MODEL), BF16), pltpu.VMEM((tm, LANES), F32)],
    )
    ys = pl.pallas_call(
        _moe_body,
        grid_spec=grid_spec,
        out_shape=jax.ShapeDtypeStruct(xs.shape, F32),
        compiler_params=_cparams("arbitrary", "arbitrary"),
        name=name,
    )(tile_group, xs, gf, wr, *[w.reshape((-1,) + w.shape[2:]) for w in (w_gate, w_up, w_down)])
    return gather_rows(ys, dest, jnp.int32(m), rows_dma, name + "_scatter")


def _ple_body(x_ref, p_ref, g_ref, wg_ref, wp_ref, gf_ref, o_ref, *, final):
    x = x_ref[...]
    gate = _sigmoid(_dot(_rms(x, g_ref[...]), wg_ref[...]))
    xn = x + _dot(p_ref[...], wp_ref[...]) * gate
    if final:
        xn = _rms(xn, gf_ref[...])
    o_ref[...] = xn


def ple(x2d, p2d, g_ple, w_gate, w_proj, g_final, final, tm, name):
    m = x2d.shape[0]
    return pl.pallas_call(
        functools.partial(_ple_body, final=final),
        grid=(m // tm,),
        in_specs=[pl.BlockSpec((tm, D_MODEL), lambda i: (i, 0)),
                  pl.BlockSpec((tm, PLE_DIM), lambda i: (i, 0)),
                  pl.BlockSpec((1, D_MODEL), lambda i: (0, 0)),
                  pl.BlockSpec((D_MODEL, D_MODEL), lambda i: (0, 0)),
                  pl.BlockSpec((PLE_DIM, D_MODEL), lambda i: (0, 0)),
                  pl.BlockSpec((1, D_MODEL), lambda i: (0, 0))],
        out_specs=pl.BlockSpec((tm, D_MODEL), lambda i: (i, 0)),
        out_shape=jax.ShapeDtypeStruct((m, D_MODEL), F32),
        compiler_params=_cparams("arbitrary"),
        name=name,
    )(x2d, p2d, g_ple.reshape(1, -1), w_gate.astype(BF16), w_proj.astype(BF16),
      g_final.reshape(1, -1))


def _cols(w, *names):
    return [w[:, _OFF[n][0]:_OFF[n][0] + _OFF[n][1]] for n in names]


def _padcols(w, width):
    return jnp.pad(w, ((0, 0), (0, width - w.shape[1])))


def _pad_q_heads(wq):
    hd = ATT_HEAD_DIM
    zero = jnp.zeros((wq.shape[0], hd), wq.dtype)
    cols = []
    for h in range(ATT_HEADS):
        blk = wq[:, h * hd:(h + 1) * hd] * (hd ** -0.5)
        cols += [blk, zero] if h // ATT_REP == 0 else [zero, blk]
    return jnp.concatenate(cols, axis=1)


def _split_w_in(w):
    swa_q, nsa_q = _cols(w, 'swa_q', 'nsa_q')
    w_att = jnp.concatenate([_pad_q_heads(swa_q), _pad_q_heads(nsa_q)] + _cols(w, *KV_BLK), axis=1)
    (xbc, dt, ngate, z) = _cols(w, 'ssm_xbc', 'ssm_dt', 'nsa_gate', 'ssm_z')
    w_rec = jnp.concatenate([xbc, _padcols(dt, LANES), _padcols(ngate, LANES), z]
                            + _cols(w, 'hgrn_q', 'hgrn_f', 'hgrn_i', 'hgrn_g'), axis=1)
    (w_mg,) = _cols(w, 'merge_gate')
    return [a.astype(BF16) for a in (w_att, w_rec, w_mg)]


def _mixers(i, att, sproj, attn_sinks, hgrn_lower_bounds, hgrn_norm_g, nsa_pos_k, nsa_pos_v,
            nsa_cmp_w1_k, nsa_cmp_w2_k, nsa_cmp_w1_v, nsa_cmp_w2_v, ssm_conv_w, ssm_conv_b,
            ssm_dt_bias, ssm_A_log, ssm_D, ssm_norm_g):
    bsz, seq, _ = att.shape
    kv = KV_BLK
    gate_blk = 7
    y_a = banded_attention(att, 0, kv['swa_k'], kv['swa_v'], window=SWA_WINDOW, sinks=attn_sinks[i],
                           name=f"swa{i}")
    y_b = hgrn2(sproj, 3, hgrn_lower_bounds, hgrn_norm_g[i], i, name=f"hgrn{i}")
    ncp = seq // NSA_CMP_STRIDE

    def cmp_in(name):
        c0 = kv[name] * LANES
        return att[:, :, c0:c0 + LANES].reshape(bsz, ncp, NSA_CMP_STRIDE * LANES)

    kc = nsa_compress(cmp_in('nsa_k_cmp'), nsa_pos_k[i], nsa_cmp_w1_k[i], nsa_cmp_w2_k[i], name=f"cmpk{i}")
    vc = nsa_compress(cmp_in('nsa_v_cmp'), nsa_pos_v[i], nsa_cmp_w1_v[i], nsa_cmp_w2_v[i], name=f"cmpv{i}")
    y_c1, mask, act = nsa_cmp_select(att, 1, kc, vc, sproj, gate_blk, name=f"nsacmp{i}")
    y_c2 = nsa_selected(att, 1, kv['nsa_k_slc'], kv['nsa_v_slc'], mask, act, sproj, gate_blk,
                        name=f"nsasel{i}")
    y_c3 = banded_attention(att, 1, kv['nsa_k_win'], kv['nsa_v_win'], window=NSA_WINDOW, gate=sproj,
                            gate_blk=gate_blk, gate_col=2, name=f"nsawin{i}")
    y_d = mamba2(sproj, ssm_conv_w[i], ssm_conv_b[i], ssm_dt_bias[i], ssm_A_log[i], ssm_D[i],
                 ssm_norm_g[i], name=f"ssd{i}")
    return y_a, y_b, y_c1, y_c2, y_c3, y_d


def kernel(x, p, w_in, g_mix, attn_sinks, hgrn_lower_bounds, hgrn_norm_g, nsa_pos_k, nsa_pos_v,
           nsa_cmp_w1_k, nsa_cmp_w2_k, nsa_cmp_w1_v, nsa_cmp_w2_v, ssm_conv_w, ssm_conv_b,
           ssm_dt_bias, ssm_A_log, ssm_D, ssm_norm_g, w_branch, w_out, g_ffn, w_router_grp,
           w_router_exp, w_exp_gate, w_exp_up, w_exp_down, g_ple, w_ple_gate, w_ple_proj, g_final):
    bsz, seq, d = x.shape
    depth = w_in.shape[0]
    t = bsz * seq
    x2 = x.reshape(t, d)
    tm_proj = min(1024, t)
    tm_row = min(256, t)
    tm_moe = min(1024, t)
    for i in range(depth):
        w_att, w_rec, w_mg = _split_w_in(w_in[i])
        att = norm_mm(x2, g_mix[i], w_att, BF16, tm_proj, 512, f"proj_att{i}").reshape(bsz, seq, -1)
        sproj = norm_mm(x2, g_mix[i], w_rec, F32, tm_proj, 512, f"proj_rec{i}").reshape(bsz, seq, -1)
        ys = _mixers(i, att, sproj, attn_sinks, hgrn_lower_bounds, hgrn_norm_g, nsa_pos_k,
                     nsa_pos_v, nsa_cmp_w1_k, nsa_cmp_w2_k, nsa_cmp_w1_v, nsa_cmp_w2_v, ssm_conv_w,
                     ssm_conv_b, ssm_dt_bias, ssm_A_log, ssm_D, ssm_norm_g)
        ys = [y.reshape(t, -1) for y in ys]
        wr = router_weights(w_router_grp[i], w_router_exp[i])
        x2, gid = merge(*ys, x2, g_mix[i], w_mg, w_branch[i], w_out[i], g_ffn[i], wr, tm_row, f"merge{i}")
        x2 = moe(x2, gid[:, 0], g_ffn[i], wr, w_exp_gate, w_exp_up, w_exp_down, i, tm_moe, f"moe{i}")
        x2 = ple(x2, p[i].reshape(t, -1), g_ple[i], w_ple_gate[i], w_ple_proj[i], g_final,
                 i == depth - 1, tm_row, f"ple{i}")
    return x2.reshape(bsz, seq, d)
```

```python
import functools

import numpy as np
import jax
import jax.numpy as jnp
from jax import lax
from jax.experimental import pallas as pl
from jax.experimental.pallas import tpu as pltpu

F32 = jnp.float32
BF16 = jnp.bfloat16

D_MODEL = 1024
PLE_DIM = 256
NORM_EPS = 1e-6
NEG_INF = -1e30
REMOVED = -3e38
N_BRANCH = 4
BRANCH_WIDTH = 512
ATTN_BLOCK = 128

ATT_HEADS = 8
ATT_KV_HEADS = 2
ATT_HEAD_DIM = 64
ATT_REP = ATT_HEADS // ATT_KV_HEADS
SWA_WINDOW = 128
NSA_WINDOW = 512
NSA_CMP_LEN = 32
NSA_CMP_STRIDE = 16
NSA_CMP_HIDDEN = 256
NSA_SEL_LEN = 64
NSA_TOP_N = 16
NSA_FORCE_SCORE = 1e6
CMP_KEY_CHUNK = 128
SEL_KEY_TILE = 512
SEL_FEATS = SEL_KEY_TILE // NSA_SEL_LEN
MASK_BIG = 1e30

HGRN_HEADS = 4
HGRN_DIM = 128
HGRN_CHUNK = 32
HGRN_MIN_F = 1e-6
HGRN_BLOCK = 256

SSM_HEADS = 8
SSM_HEAD_DIM = 64
SSM_GROUPS = 2
SSM_STATE = 64
SSM_CONV = 4
SSM_CHUNK = 64
SSM_INNER = 512
SSM_CONV_DIM = 768
SSM_BLOCK = 256

MOE_GROUPS = 4
MOE_EPG = 8
MOE_EXPERTS = 32
MOE_FF = 256
MOE_GATHER_ROWS = 512

LANES = 128
VMEM_LIMIT = 56 * 1024 * 1024

ALIBI_SLOPES = tuple(2.0 ** (-8.0 * (h + 1) / ATT_HEADS) for h in range(ATT_HEADS))

ATT_Q_WIDTH = ATT_HEADS * LANES
KV_BLK = {name: 2 * ATT_HEADS + j for j, name in enumerate(
    ('swa_k', 'swa_v', 'nsa_k_cmp', 'nsa_v_cmp', 'nsa_k_slc', 'nsa_v_slc', 'nsa_k_win', 'nsa_v_win'))}

_SPLITS = (
    ('swa_q', 512), ('swa_k', 128), ('swa_v', 128),
    ('hgrn_q', 512), ('hgrn_f', 512), ('hgrn_i', 512), ('hgrn_g', 512),
    ('nsa_q', 512), ('nsa_k_cmp', 128), ('nsa_v_cmp', 128), ('nsa_k_slc', 128),
    ('nsa_v_slc', 128), ('nsa_k_win', 128), ('nsa_v_win', 128), ('nsa_gate', 24),
    ('ssm_z', 512), ('ssm_xbc', 768), ('ssm_dt', 8), ('merge_gate', 4096),
)
_OFF = {}
_o = 0
for _n, _w in _SPLITS:
    _OFF[_n] = (_o, _w)
    _o += _w


def _cparams(*sem):
    return pltpu.CompilerParams(dimension_semantics=sem, vmem_limit_bytes=VMEM_LIMIT)


def _sigmoid(x):
    return 1.0 / (1.0 + jnp.exp(-x))


def _silu(x):
    return x * _sigmoid(x)


def _dot(a, b):
    return jnp.dot(a.astype(BF16), b.astype(BF16), preferred_element_type=F32)


def _dot_nt(a, b):
    return lax.dot_general(a.astype(BF16), b.astype(BF16), (((1,), (1,)), ((), ())),
                           preferred_element_type=F32)


def _dot_tn(a, b):
    return lax.dot_general(a.astype(BF16), b.astype(BF16), (((0,), (0,)), ((), ())),
                           preferred_element_type=F32)


def _split3(x):
    hi = x.astype(BF16)
    r1 = x - hi.astype(F32)
    mid = r1.astype(BF16)
    return hi, mid, (r1 - mid.astype(F32)).astype(BF16)


def _prefix_dot(tri, x):
    t = tri.astype(BF16)
    return sum(jnp.dot(t, p, preferred_element_type=F32) for p in _split3(x))


def _prefix_dot_tn(x, tri):
    t = tri.astype(BF16)
    return sum(lax.dot_general(p, t, (((0,), (0,)), ((), ())), preferred_element_type=F32)
               for p in _split3(x))


def _rms(x, g):
    ms = jnp.mean(x * x, axis=-1, keepdims=True)
    return x * lax.rsqrt(ms + NORM_EPS) * g


def _norm_mm_body(x_ref, g_ref, w_ref, o_ref, hn_ref):
    @pl.when(pl.program_id(1) == 0)
    def _():
        hn_ref[...] = _rms(x_ref[...], g_ref[...]).astype(BF16)

    o_ref[...] = jnp.dot(hn_ref[...], w_ref[...], preferred_element_type=F32).astype(o_ref.dtype)


def norm_mm(x2d, g, w, out_dtype, tm, tn, name):
    m, k = x2d.shape
    n = w.shape[1]
    return pl.pallas_call(
        _norm_mm_body,
        grid=(m // tm, n // tn),
        in_specs=[pl.BlockSpec((tm, k), lambda i, j: (i, 0)),
                  pl.BlockSpec((1, k), lambda i, j: (0, 0)),
                  pl.BlockSpec((k, tn), lambda i, j: (0, j))],
        out_specs=pl.BlockSpec((tm, tn), lambda i, j: (i, j)),
        out_shape=jax.ShapeDtypeStruct((m, n), out_dtype),
        scratch_shapes=[pltpu.VMEM((tm, k), BF16)],
        compiler_params=_cparams("arbitrary", "arbitrary"),
        name=name,
    )(x2d, g.reshape(1, k), w)


def _ones_row(g):
    return ATT_HEAD_DIM if g == 0 else 0


def _vt_body(v_ref, o_ref):
    vt = jnp.transpose(v_ref[0].astype(F32))
    rowid = lax.broadcasted_iota(jnp.int32, vt.shape, 0)
    for g in range(ATT_KV_HEADS):
        aug = jnp.where(rowid // ATT_HEAD_DIM == g, vt, jnp.where(rowid == _ones_row(g), 1.0, 0.0))
        o_ref[0, g] = aug.astype(BF16)


def v_transposed(arr, v_blk, name):
    bsz, seq, _ = arr.shape
    rows = min(2048, seq)
    return pl.pallas_call(
        _vt_body,
        grid=(bsz, seq // rows),
        in_specs=[pl.BlockSpec((1, rows, LANES), lambda b, n: (b, n, v_blk))],
        out_specs=pl.BlockSpec((1, ATT_KV_HEADS, LANES, rows), lambda b, n: (b, 0, 0, n)),
        out_shape=jax.ShapeDtypeStruct((bsz, ATT_KV_HEADS, LANES, seq), BF16),
        compiler_params=_cparams("arbitrary", "arbitrary"),
        name=name,
    )(arr)


def _q_stack(q_ref, g):
    return jnp.concatenate([q_ref[0, :, (g * ATT_REP + r) * LANES:(g * ATT_REP + r + 1) * LANES]
                            for r in range(ATT_REP)], axis=0)


def _head(x, r):
    return x[:, r * ATTN_BLOCK:(r + 1) * ATTN_BLOCK]


def _banded_body(*refs, window, nprev, use_sink, gate_col):
    if use_sink:
        q_ref, k_ref, vt_ref, sink_ref, o_ref, ot_ref = refs
    else:
        q_ref, k_ref, vt_ref, gate_ref, o_ref, ot_ref = refs
    blk = ATTN_BLOCK
    hd = ATT_HEAD_DIM
    n = pl.program_id(1)
    nk = (nprev + 1) * blk
    start = pl.multiple_of(jnp.maximum(n - nprev, 0) * blk, blk)
    k128 = k_ref[0, pl.ds(start, nk), :]
    krow = lax.broadcasted_iota(jnp.int32, (nk, blk), 0)
    qcol = lax.broadcasted_iota(jnp.int32, (nk, blk), 1)
    rel = (n * blk + qcol) - (start + krow)
    negrel = jnp.where((rel >= 0) & (rel < window), -rel.astype(F32), NEG_INF)
    if not use_sink:
        gate_t = jnp.transpose(_sigmoid(gate_ref[0]))
    st4s = [_dot_nt(k128, _q_stack(q_ref, g)) for g in range(ATT_KV_HEADS)]
    for g in range(ATT_KV_HEADS):
        one = _ones_row(g)
        st4 = st4s[g]
        ps, ms = [], []
        for r in range(ATT_REP):
            h = g * ATT_REP + r
            st = _head(st4, r) + ALIBI_SLOPES[h] * negrel
            m = jnp.max(st, axis=0, keepdims=True)
            if use_sink:
                m = jnp.maximum(m, sink_ref[h])
            ps.append(jnp.exp(st - m).astype(BF16))
            ms.append(m)
        acc4 = jnp.dot(vt_ref[0, g, :, pl.ds(start, nk)], jnp.concatenate(ps, axis=1),
                       preferred_element_type=F32)
        for r in range(ATT_REP):
            h = g * ATT_REP + r
            acc = _head(acc4, r)
            l = acc[one:one + 1]
            if use_sink:
                l = l + jnp.exp(sink_ref[h] - ms[r])
            ot = acc[g * hd:(g + 1) * hd] / l
            if not use_sink:
                c = 3 * h + gate_col
                ot = ot * gate_t[c:c + 1]
            ot_ref[h * hd:(h + 1) * hd, :] = ot
    o_ref[0] = jnp.transpose(ot_ref[...])


def banded_attention(att, q_blk, k_blk, v_blk, *, window, sinks=None, gate=None, gate_blk=None,
                     gate_col=0, name):
    bsz, seq, _ = att.shape
    blk = ATTN_BLOCK
    nprev = (window - 1 + blk - 1) // blk
    use_sink = sinks is not None
    vt = v_transposed(att, v_blk, name + "_vt")
    in_specs = [pl.BlockSpec((1, blk, ATT_Q_WIDTH), lambda b, n: (b, n, q_blk)),
                pl.BlockSpec((1, seq, LANES), lambda b, n: (b, 0, k_blk)),
                pl.BlockSpec((1, ATT_KV_HEADS, LANES, seq), lambda b, n: (b, 0, 0, 0))]
    if use_sink:
        in_specs.append(pl.BlockSpec(memory_space=pltpu.SMEM))
        extra = sinks
    else:
        in_specs.append(pl.BlockSpec((1, blk, LANES), lambda b, n: (b, n, gate_blk)))
        extra = gate
    return pl.pallas_call(
        functools.partial(_banded_body, window=window, nprev=nprev, use_sink=use_sink,
                          gate_col=gate_col),
        grid=(bsz, seq // blk),
        in_specs=in_specs,
        out_specs=pl.BlockSpec((1, blk, 512), lambda b, n: (b, n, 0)),
        out_shape=jax.ShapeDtypeStruct((bsz, seq, 512), F32),
        scratch_shapes=[pltpu.VMEM((512, blk), F32)],
        compiler_params=_cparams("arbitrary", "arbitrary"),
        name=name,
    )(att, att, vt, extra)


def _compress_body(x_ref, w1a_ref, w1b_ref, w1_ref, pos_ref, w2_ref, o_ref):
    x = x_ref[0]
    p = jnp.dot(x, w1a_ref[...], preferred_element_type=F32)
    q = jnp.dot(x, w1b_ref[...], preferred_element_type=F32)
    ncp = x.shape[0]
    q = pltpu.roll(q, shift=ncp - 1, axis=0)
    posb = jnp.broadcast_to(pos_ref[...], (8, pos_ref.shape[1]))
    bias = _dot(posb, w1_ref[...])[0:1]
    hid = NSA_CMP_HIDDEN
    outs = []
    for g in range(ATT_KV_HEADS):
        pre = p[:, g * hid:(g + 1) * hid] + q[:, g * hid:(g + 1) * hid] + bias
        outs.append(_dot(jax.nn.gelu(pre), w2_ref[...]))
    o_ref[0] = jnp.concatenate(outs, axis=1).astype(o_ref.dtype)


def _expand_w1(w1, half):
    hd, hid, ng = ATT_HEAD_DIM, NSA_CMP_HIDDEN, ATT_KV_HEADS
    w = w1.reshape(NSA_CMP_LEN, hd, hid)[half * 16:(half + 1) * 16]
    eye = jnp.eye(ng, dtype=w1.dtype)
    out = jnp.einsum('ldj,gh->lgdhj', w, eye)
    return out.reshape(16 * ng * hd, ng * hid)


def nsa_compress(x16, pos, w1, w2, name):
    bsz, ncp, wid = x16.shape
    hid = NSA_CMP_HIDDEN
    w1a = _expand_w1(w1, 0).astype(BF16)
    w1b = _expand_w1(w1, 1).astype(BF16)
    return pl.pallas_call(
        _compress_body,
        grid=(bsz,),
        in_specs=[pl.BlockSpec((1, ncp, wid), lambda b: (b, 0, 0)),
                  pl.BlockSpec((wid, 2 * hid), lambda b: (0, 0)),
                  pl.BlockSpec((wid, 2 * hid), lambda b: (0, 0)),
                  pl.BlockSpec((NSA_CMP_LEN * ATT_HEAD_DIM, hid), lambda b: (0, 0)),
                  pl.BlockSpec((1, NSA_CMP_LEN * ATT_HEAD_DIM), lambda b: (0, 0)),
                  pl.BlockSpec((hid, ATT_HEAD_DIM), lambda b: (0, 0))],
        out_specs=pl.BlockSpec((1, ncp, LANES), lambda b: (b, 0, 0)),
        out_shape=jax.ShapeDtypeStruct((bsz, ncp, LANES), BF16),
        compiler_params=_cparams("arbitrary"),
        name=name,
    )(x16, w1a, w1b, w1.astype(BF16), pos.reshape(1, -1), w2.astype(BF16))


def _nsa_cmp_body(q_ref, kc_ref, vct_ref, ovt_ref, gate_ref, o_ref, m_ref, act_ref, ot_ref, imp_ref, *,
                  top_n):
    blk = ATTN_BLOCK
    hd = ATT_HEAD_DIM
    n = pl.program_id(1)
    kc = kc_ref[0]
    ncp = kc.shape[0]
    ns = ovt_ref.shape[0]
    crow = lax.broadcasted_iota(jnp.int32, (ncp, blk), 0)
    qcol = lax.broadcasted_iota(jnp.int32, (ncp, blk), 1)
    dist = (n * blk + qcol) - (crow * NSA_CMP_STRIDE + (NSA_CMP_LEN - 1))
    valid = dist >= 0
    negd = jnp.where(valid, -dist.astype(F32), NEG_INF)
    jrow = lax.broadcasted_iota(jnp.int32, (ns, blk), 0)
    qpos = n * blk + lax.broadcasted_iota(jnp.int32, (ns, blk), 1)
    cur = qpos // NSA_SEL_LEN
    forced = (jrow == 0) | (jrow == cur) | (jrow == cur - 1)
    causal_blk = jrow * NSA_SEL_LEN <= qpos
    jrowf = jrow.astype(F32)
    gate_t = jnp.transpose(_sigmoid(gate_ref[0]))
    ones8 = jnp.ones((8, blk), BF16)

    def attend(rows):
        st4s = [_dot_nt(kc[:rows], _q_stack(q_ref, g)) for g in range(ATT_KV_HEADS)]
        for g in range(ATT_KV_HEADS):
            one = _ones_row(g)
            es = []
            for r in range(ATT_REP):
                st = _head(st4s[g], r) + ALIBI_SLOPES[g * ATT_REP + r] * negd[:rows]
                m = jnp.max(st, axis=0, keepdims=True)
                es.append(jnp.where(valid[:rows], jnp.exp(st - m), 0.0))
            acc4 = jnp.dot(vct_ref[0, g, :, :rows], jnp.concatenate([e.astype(BF16) for e in es], axis=1),
                           preferred_element_type=F32)
            psum = jnp.zeros((rows, blk), F32)
            for r in range(ATT_REP):
                h = g * ATT_REP + r
                acc = _head(acc4, r)
                l = acc[one:one + 1]
                inv = jnp.where(l > 0.0, 1.0 / jnp.where(l > 0.0, l, 1.0), 0.0)
                ot_ref[h * hd:(h + 1) * hd, :] = acc[g * hd:(g + 1) * hd] * inv * gate_t[3 * h:3 * h + 1]
                psum = psum + es[r] * inv
            imp_ref[g] = _dot(ovt_ref[:, :rows], psum)

    chunk = min(CMP_KEY_CHUNK, ncp)
    nchunks = ncp // chunk
    need = jnp.minimum((n * (blk // NSA_CMP_STRIDE) + blk // NSA_CMP_STRIDE - 1 + chunk - 1) // chunk,
                       nchunks)
    for c in range(1, nchunks + 1):
        pl.when(need == c)(functools.partial(attend, c * chunk))

    for g in range(ATT_KV_HEADS):
        score = jnp.where(forced, NSA_FORCE_SCORE, jnp.where(causal_blk, imp_ref[g], NEG_INF))
        sel = jnp.zeros((ns, blk), F32)
        for _ in range(top_n):
            mx = jnp.max(score, axis=0, keepdims=True)
            idx = jnp.min(jnp.where(score == mx, jrowf, float(ns)), axis=0, keepdims=True)
            hit = jrowf == idx
            sel = jnp.where(hit, 1.0, sel)
            score = jnp.where(hit, REMOVED, score)
        m_ref[0, g] = sel
        act_ref[0, 0, g * 8:(g + 1) * 8, :] = _dot_nt(ones8, sel)
    o_ref[0] = jnp.transpose(ot_ref[...])


def _overlap_matrix_t(seq):
    n_cmp = seq // NSA_CMP_STRIDE
    n_sel = seq // NSA_SEL_LEN
    cs = np.arange(n_cmp)[None, :] * NSA_CMP_STRIDE
    ss = np.arange(n_sel)[:, None] * NSA_SEL_LEN
    ov = np.clip(np.minimum(cs + NSA_CMP_LEN, ss + NSA_SEL_LEN) - np.maximum(cs, ss), 0, None)
    return jnp.asarray(ov / NSA_CMP_LEN, dtype=BF16)


def nsa_cmp_select(att, q_blk, kc, vc, gate, gate_blk, name):
    bsz, seq, _ = att.shape
    blk = ATTN_BLOCK
    nb = seq // blk
    ncp = seq // NSA_CMP_STRIDE
    ns = seq // NSA_SEL_LEN
    top_n = min(NSA_TOP_N, ns)
    vct = v_transposed(vc, 0, name + "_vt")
    return pl.pallas_call(
        functools.partial(_nsa_cmp_body, top_n=top_n),
        grid=(bsz, nb),
        in_specs=[pl.BlockSpec((1, blk, ATT_Q_WIDTH), lambda b, n: (b, n, q_blk)),
                  pl.BlockSpec((1, ncp, LANES), lambda b, n: (b, 0, 0)),
                  pl.BlockSpec((1, ATT_KV_HEADS, LANES, ncp), lambda b, n: (b, 0, 0, 0)),
                  pl.BlockSpec((ns, ncp), lambda b, n: (0, 0)),
                  pl.BlockSpec((1, blk, LANES), lambda b, n: (b, n, gate_blk))],
        out_specs=[pl.BlockSpec((1, blk, 512), lambda b, n: (b, n, 0)),
                   pl.BlockSpec((1, ATT_KV_HEADS, ns, blk), lambda b, n: (b, 0, 0, n)),
                   pl.BlockSpec((1, 1, ATT_KV_HEADS * 8, ns), lambda b, n: (b, n, 0, 0))],
        out_shape=[jax.ShapeDtypeStruct((bsz, seq, 512), F32),
                   jax.ShapeDtypeStruct((bsz, ATT_KV_HEADS, ns, seq), F32),
                   jax.ShapeDtypeStruct((bsz, nb, ATT_KV_HEADS * 8, ns), F32)],
        scratch_shapes=[pltpu.VMEM((512, blk), F32), pltpu.VMEM((ATT_KV_HEADS, ns, blk), F32)],
        compiler_params=_cparams("arbitrary", "arbitrary"),
        name=name,
    )(att, kc, vct, _overlap_matrix_t(seq), gate)


def _k_aug_body(k_ref, o_ref):
    k = k_ref[0].astype(F32)
    rows = k.shape[0]
    kin = lax.broadcasted_iota(jnp.int32, (rows, LANES), 0) % SEL_KEY_TILE
    lane = lax.broadcasted_iota(jnp.int32, (rows, LANES), 1)
    for g in range(ATT_KV_HEADS):
        f = lane - (ATT_HEAD_DIM if g == 0 else 0)
        feat = jnp.where(f == kin // NSA_SEL_LEN, 1.0, 0.0)
        feat = jnp.where(f == SEL_FEATS, (kin % 256).astype(F32), feat)
        feat = jnp.where(f == SEL_FEATS + 1, (kin // 256 * 256).astype(F32), feat)
        feat = jnp.where((f == SEL_FEATS + 2) | (f == SEL_FEATS + 3), 1.0, feat)
        o_ref[0, g] = jnp.where(lane // ATT_HEAD_DIM == g, k, feat).astype(BF16)


def k_augmented(arr, k_blk, name):
    bsz, seq, _ = arr.shape
    rows = min(2048, seq)
    return pl.pallas_call(
        _k_aug_body,
        grid=(bsz, seq // rows),
        in_specs=[pl.BlockSpec((1, rows, LANES), lambda b, n: (b, n, k_blk))],
        out_specs=pl.BlockSpec((1, ATT_KV_HEADS, rows, LANES), lambda b, n: (b, 0, n, 0)),
        out_shape=jax.ShapeDtypeStruct((bsz, ATT_KV_HEADS, seq, LANES), BF16),
        compiler_params=_cparams("arbitrary", "arbitrary"),
        name=name,
    )(arr)


def _query_feature_rows():
    out = np.zeros((ATT_KV_HEADS, 2 * SEL_FEATS, ATT_REP * ATTN_BLOCK), np.float32)
    qin = np.arange(ATTN_BLOCK, dtype=np.float32)
    for g in range(ATT_KV_HEADS):
        for r in range(ATT_REP):
            slope = ALIBI_SLOPES[g * ATT_REP + r]
            cols = slice(r * ATTN_BLOCK, (r + 1) * ATTN_BLOCK)
            out[g, 0, cols] = slope
            out[g, 1, cols] = slope
            out[g, 3, cols] = -slope * qin
            out[g, SEL_FEATS + 2, cols] = -slope
    return jnp.asarray(out)


def _nsa_sel_body(tiles_ref, cnt_ref, q_ref, k_ref, vt_ref, m_ref, gate_ref, fq_ref, o_ref, ot_ref,
                  acc_ref, mx_ref, dq_ref, sa_ref, sb_ref, *, ntl):
    blk = ATTN_BLOCK
    hd = ATT_HEAD_DIM
    tk = SEL_KEY_TILE
    sl = NSA_SEL_LEN
    per = tk // sl
    b = pl.program_id(0)
    n = pl.program_id(1)
    nb = pl.num_programs(1)
    krow = lax.broadcasted_iota(jnp.int32, (tk, blk), 0)
    qcol = lax.broadcasted_iota(jnp.int32, (tk, blk), 1)
    dq_ref[...] = (qcol - krow).astype(F32)
    gate_t = jnp.transpose(_sigmoid(gate_ref[0]))
    pad_rows = jnp.zeros((hd - 2 * SEL_FEATS, ATT_REP * blk), BF16)
    for g in range(ATT_KV_HEADS):
        one = _ones_row(g)
        lrow = (b * nb + n) * ATT_KV_HEADS + g
        qt = jnp.concatenate(
            [jnp.transpose(q_ref[0, :, (g * ATT_REP + r) * LANES:(g * ATT_REP + r + 1) * LANES]
                           .astype(F32))[g * hd:(g + 1) * hd] for r in range(ATT_REP)],
            axis=1).astype(BF16)
        acc_ref[...] = jnp.zeros_like(acc_ref)
        mx_ref[...] = jnp.full_like(mx_ref, NEG_INF)

        def tile_scores(j, live):
            t = tiles_ref[lrow * ntl + j]
            base = pl.multiple_of(t * tk, tk)
            sel8 = m_ref[0, g, pl.ds(pl.multiple_of(t * per, per), per), :]
            if live is not None:
                sel8 = jnp.where(live, sel8, 0.0)
            off = (n * blk - base).astype(F32)
            mask_rows = jnp.concatenate([(sel8 - 1.0) * MASK_BIG] * ATT_REP, axis=1)
            alibi_rows = fq_ref[g, 0:SEL_FEATS, :] + off * fq_ref[g, SEL_FEATS:2 * SEL_FEATS, :]
            feats = jnp.concatenate([mask_rows, alibi_rows], axis=0).astype(BF16)
            rhs = jnp.concatenate([qt, feats, pad_rows] if g == 0 else [feats, pad_rows, qt], axis=0)
            return jnp.dot(k_ref[0, g, pl.ds(base, tk), :], rhs,
                           preferred_element_type=F32)

        def tile_finish(j, st4, diag):
            base = pl.multiple_of(tiles_ref[lrow * ntl + j] * tk, tk)
            off = (n * blk - base).astype(F32)
            if diag:
                causal = (dq_ref[...] + off) >= 0.0
            ps, ms = [], []
            for r in range(ATT_REP):
                st = _head(st4, r)
                if diag:
                    st = jnp.where(causal, st, NEG_INF)
                m = jnp.max(st, axis=0, keepdims=True)
                ps.append(jnp.exp(st - m).astype(BF16))
                ms.append(m)
            acc = jnp.dot(vt_ref[0, g, :, pl.ds(base, tk)], jnp.concatenate(ps, axis=1),
                          preferred_element_type=F32)
            return jnp.concatenate(ms, axis=1), acc

        cnt = cnt_ref[lrow]

        def scores_into(s_ref, j):
            s_ref[...] = tile_scores(jnp.minimum(j, ntl - 1), j < cnt)

        def finish_from(s_ref, j, diag):
            m_t, acc_t = tile_finish(jnp.minimum(j, ntl - 1), s_ref[...], diag)
            m_old = mx_ref[...]
            m_new = jnp.maximum(m_old, m_t)
            acc_ref[...] = jnp.exp(m_old - m_new) * acc_ref[...] + jnp.exp(m_t - m_new) * acc_t
            mx_ref[...] = m_new

        def two_tiles(k, diag):
            scores_into(sb_ref, 2 * k + 1)
            finish_from(sa_ref, 2 * k, diag)
            scores_into(sa_ref, 2 * k + 2)
            finish_from(sb_ref, 2 * k + 1, False)

        scores_into(sa_ref, 0)
        two_tiles(0, True)

        def later(k, carry):
            two_tiles(k, False)
            return carry

        lax.fori_loop(1, (cnt + 1) // 2, later, 0)
        acc4 = acc_ref[...]
        for r in range(ATT_REP):
            h = g * ATT_REP + r
            acc = _head(acc4, r)
            ot = acc[g * hd:(g + 1) * hd] / acc[one:one + 1]
            ot_ref[h * hd:(h + 1) * hd, :] = ot * gate_t[3 * h + 1:3 * h + 2]
    o_ref[0] = jnp.transpose(ot_ref[...])


def _tile_lists(act, seq):
    bsz, nb = act.shape[:2]
    ns = act.shape[-1]
    tk = SEL_KEY_TILE
    ntl = seq // tk
    per = tk // NSA_SEL_LEN
    cnt_blk = act.reshape(bsz, nb, ATT_KV_HEADS, 8, ns)[:, :, :, 0, :]
    hit = cnt_blk.reshape(bsz, nb, ATT_KV_HEADS, ntl, per).sum(-1) > 0.5
    tidx = jnp.arange(ntl, dtype=jnp.int32)
    diag = (jnp.arange(nb, dtype=jnp.int32) * ATTN_BLOCK + ATTN_BLOCK - 1) // tk
    active = hit & (tidx[None, None, None, :] <= diag[None, :, None, None])
    act_i = active.astype(jnp.int32)
    rank = jnp.cumsum(act_i[..., ::-1], axis=-1)[..., ::-1] - 1
    slot = (active[..., :, None] & (rank[..., :, None] == tidx)).astype(jnp.int32)
    tiles = (slot * tidx[:, None]).sum(-2).reshape(-1)
    cnt = act_i.sum(-1).reshape(-1)
    return tiles, cnt, ntl


def nsa_selected(att, q_blk, k_blk, v_blk, mask, act, gate, gate_blk, name):
    bsz, seq, _ = att.shape
    blk = ATTN_BLOCK
    ns = seq // NSA_SEL_LEN
    tiles, cnt, ntl = _tile_lists(act, seq)
    vt = v_transposed(att, v_blk, name + "_vt")
    kaug = k_augmented(att, k_blk, name + "_k")
    grid_spec = pltpu.PrefetchScalarGridSpec(
        num_scalar_prefetch=2,
        grid=(bsz, seq // blk),
        in_specs=[pl.BlockSpec((1, blk, ATT_Q_WIDTH), lambda b, n, *_: (b, n, q_blk)),
                  pl.BlockSpec((1, ATT_KV_HEADS, seq, LANES), lambda b, n, *_: (b, 0, 0, 0)),
                  pl.BlockSpec((1, ATT_KV_HEADS, LANES, seq), lambda b, n, *_: (b, 0, 0, 0)),
                  pl.BlockSpec((1, ATT_KV_HEADS, ns, blk), lambda b, n, *_: (b, 0, 0, n)),
                  pl.BlockSpec((1, blk, LANES), lambda b, n, *_: (b, n, gate_blk)),
                  pl.BlockSpec((ATT_KV_HEADS, 2 * SEL_FEATS, ATT_REP * blk), lambda b, n, *_: (0, 0, 0))],
        out_specs=pl.BlockSpec((1, blk, 512), lambda b, n, *_: (b, n, 0)),
        scratch_shapes=[pltpu.VMEM((512, blk), F32),
                        pltpu.VMEM((LANES, ATT_REP * blk), F32),
                        pltpu.VMEM((1, ATT_REP * blk), F32),
                        pltpu.VMEM((SEL_KEY_TILE, blk), F32),
                        pltpu.VMEM((SEL_KEY_TILE, ATT_REP * blk), F32),
                        pltpu.VMEM((SEL_KEY_TILE, ATT_REP * blk), F32)],
    )
    return pl.pallas_call(
        functools.partial(_nsa_sel_body, ntl=ntl),
        grid_spec=grid_spec,
        out_shape=jax.ShapeDtypeStruct((bsz, seq, 512), F32),
        compiler_params=_cparams("arbitrary", "arbitrary"),
        name=name,
    )(tiles, cnt, att, kaug, vt, mask, gate, _query_feature_rows())


def _hgrn_body(q_ref, f_ref, i_ref, g_ref, lbp_ref, ng_ref, o_ref, st_ref, b_ref, k_ref, *, layer):
    blk = HGRN_BLOCK
    ch = HGRN_CHUNK
    dk = HGRN_DIM

    @pl.when(pl.program_id(1) == 0)
    def _():
        st_ref[...] = jnp.zeros_like(st_ref)

    lbp = lbp_ref[...]
    e = jnp.exp(lbp - jnp.max(lbp, axis=0, keepdims=True))
    sm = e / jnp.sum(e, axis=0, keepdims=True)
    lb = jnp.zeros((1, lbp.shape[1]), F32)
    for d in range(1, layer + 1):
        lb = lb + sm[d:d + 1]
    z = f_ref[0]
    f = lb + (1.0 - lb) * _sigmoid(z)
    logf = jnp.log(jnp.maximum(f, HGRN_MIN_F))
    k_ref[...] = (1.0 - lb) * _sigmoid(-z)
    tr = lax.broadcasted_iota(jnp.int32, (blk, blk), 0)
    tc = lax.broadcasted_iota(jnp.int32, (blk, blk), 1)
    tri = jnp.where((tr // ch == tc // ch) & (tc <= tr), 1.0, 0.0)
    b_ref[...] = _prefix_dot(tri, logf)
    cr = lax.broadcasted_iota(jnp.int32, (ch, ch), 0)
    cc = lax.broadcasted_iota(jnp.int32, (ch, ch), 1)
    causal = cc <= cr
    ng = ng_ref[...]

    for c in range(blk // ch):
        r0 = c * ch
        bc = b_ref[pl.ds(r0, ch), :]
        qc = q_ref[0, pl.ds(r0, ch), :]
        kc = k_ref[pl.ds(r0, ch), :]
        vc = i_ref[0, pl.ds(r0, ch), :]
        gc = g_ref[0, pl.ds(r0, ch), :]
        b_mid = bc[ch // 2:ch // 2 + 1]
        b_last = bc[ch - 1:ch]
        qa = qc * jnp.exp(bc - b_mid)
        ka = kc * jnp.exp(b_mid - bc)
        qe = qc * jnp.exp(bc)
        kl = kc * jnp.exp(b_last - bc)
        dec = jnp.exp(b_last)
        for h in range(HGRN_HEADS):
            sl = slice(h * dk, (h + 1) * dk)
            a = jnp.where(causal, _dot_nt(qa[:, sl], ka[:, sl]), 0.0)
            st = st_ref[h]
            o = _dot(a, vc[:, sl]) + _dot_nt(qe[:, sl], st)
            st_ref[h] = st * dec[:, sl] + _dot_tn(vc[:, sl], kl[:, sl])
            o = _rms(o, ng[:, sl]) * _silu(gc[:, sl])
            o_ref[0, pl.ds(r0, ch), sl] = o


def hgrn2(hproj, first_blk, lower_bounds, norm_g, layer, name):
    bsz, seq, _ = hproj.shape
    blk = HGRN_BLOCK
    wid = HGRN_HEADS * HGRN_DIM
    depth = lower_bounds.shape[0]

    def col(j):
        return pl.BlockSpec((1, blk, wid), lambda b, n: (b, n, first_blk + j))

    return pl.pallas_call(
        functools.partial(_hgrn_body, layer=layer),
        grid=(bsz, seq // blk),
        in_specs=[col(0), col(1), col(2), col(3),
                  pl.BlockSpec((depth, wid), lambda b, n: (0, 0)),
                  pl.BlockSpec((1, wid), lambda b, n: (0, 0))],
        out_specs=pl.BlockSpec((1, blk, wid), lambda b, n: (b, n, 0)),
        out_shape=jax.ShapeDtypeStruct((bsz, seq, wid), F32),
        scratch_shapes=[pltpu.VMEM((HGRN_HEADS, HGRN_DIM, HGRN_DIM), F32),
                        pltpu.VMEM((blk, wid), F32),
                        pltpu.VMEM((blk, wid), F32)],
        compiler_params=_cparams("arbitrary", "arbitrary"),
        name=name,
    )(hproj, hproj, hproj, hproj, lower_bounds, norm_g.reshape(1, wid))


def _ssd_body(xbc_ref, dt_ref, z_ref, cw_ref, cb_ref, dtb_ref, alog_ref, alogw_ref, dskw_ref, ng_ref,
              o_ref, xp_ref, act_ref, st_ref, y_ref):
    blk = SSM_BLOCK
    ch = SSM_CHUNK
    hp = SSM_HEAD_DIM
    ns = SSM_STATE
    rep = SSM_HEADS // SSM_GROUPS
    pad = 8

    @pl.when(pl.program_id(1) == 0)
    def _():
        xp_ref[0:pad, :] = jnp.zeros((pad, SSM_CONV_DIM), F32)
        st_ref[...] = jnp.zeros_like(st_ref)

    xin = xbc_ref[0]
    xp_ref[pad:pad + blk, :] = xin
    cw = cw_ref[...]
    conv = cb_ref[...] + cw[SSM_CONV - 1:SSM_CONV] * xin
    for j in range(SSM_CONV - 1):
        shift = SSM_CONV - 1 - j
        conv = conv + cw[j:j + 1] * xp_ref[pl.ds(pad - shift, blk), :]
    xp_ref[0:pad, :] = xin[blk - pad:blk]
    act_ref[...] = _silu(conv)

    def softplus(v):
        return jnp.maximum(v, 0.0) + jnp.log(1.0 + jnp.exp(-jnp.abs(v)))

    dt = softplus(dt_ref[0] + dtb_ref[...])
    a_all = dt * (-jnp.exp(alog_ref[...]))
    er = lax.broadcasted_iota(jnp.int32, (LANES, SSM_INNER), 0)
    ec = lax.broadcasted_iota(jnp.int32, (LANES, SSM_INNER), 1)
    spread = jnp.where(ec // hp == er, 1.0, 0.0).astype(BF16)
    dtw = sum(jnp.dot(p, spread, preferred_element_type=F32) for p in _split3(dt))
    aw = dtw * (-jnp.exp(alogw_ref[...]))
    cr = lax.broadcasted_iota(jnp.int32, (ch, ch), 0)
    cc = lax.broadcasted_iota(jnp.int32, (ch, ch), 1)
    causal = cc <= cr
    tri = jnp.where(causal, 1.0, 0.0)
    tri_t = jnp.where(cr <= cc, 1.0, 0.0)

    for c in range(blk // ch):
        r0 = c * ch
        acs_t = _prefix_dot_tn(a_all[r0:r0 + ch], tri_t)
        acs = _prefix_dot(tri, aw[r0:r0 + ch])
        last = acs[ch - 1:ch]
        dtc = dtw[r0:r0 + ch]
        xc = act_ref[r0:r0 + ch, 0:SSM_INNER]
        xdt = dtc * xc
        grow = jnp.exp(acs)
        wdec = jnp.exp(last - acs) * dtc
        keep = jnp.exp(last)
        skip = dskw_ref[...] * xc
        for g in range(SSM_GROUPS):
            bm = act_ref[r0:r0 + ch, SSM_INNER + g * ns:SSM_INNER + (g + 1) * ns]
            cm = act_ref[r0:r0 + ch, SSM_INNER + SSM_GROUPS * ns + g * ns:
                         SSM_INNER + SSM_GROUPS * ns + (g + 1) * ns]
            cb = _dot_nt(cm, bm)
            for r in range(rep):
                h = g * rep + r
                hl = slice(h * hp, (h + 1) * hp)
                lmat = jnp.where(causal, jnp.exp(acs[:, hl] - acs_t[h:h + 1, :]), 0.0)
                st = st_ref[h]
                y = _dot(cb * lmat, xdt[:, hl]) + grow[:, hl] * _dot(cm, st) + skip[:, hl]
                st_ref[h] = keep[:, hl] * st + _dot_tn(bm * wdec[:, hl], xc[:, hl])
                y_ref[r0:r0 + ch, hl] = y
    yz = y_ref[...] * _silu(z_ref[0])
    o_ref[0] = _rms(yz, ng_ref[...])


def mamba2(sproj, conv_w, conv_b, dt_bias, a_log, d_skip, norm_g, name):
    bsz, seq, _ = sproj.shape
    blk = SSM_BLOCK
    padh = LANES - SSM_HEADS
    dtb = jnp.pad(dt_bias, (0, padh)).reshape(1, LANES)
    alog = jnp.pad(a_log, (0, padh)).reshape(1, LANES)

    def wide(v):
        return jnp.repeat(v, SSM_HEAD_DIM).reshape(1, SSM_INNER)

    def row(w):
        return pl.BlockSpec((1, w), lambda b, n: (0, 0))

    return pl.pallas_call(
        _ssd_body,
        grid=(bsz, seq // blk),
        in_specs=[pl.BlockSpec((1, blk, SSM_CONV_DIM), lambda b, n: (b, n, 0)),
                  pl.BlockSpec((1, blk, LANES), lambda b, n: (b, n, 6)),
                  pl.BlockSpec((1, blk, SSM_INNER), lambda b, n: (b, n, 2)),
                  pl.BlockSpec((SSM_CONV, SSM_CONV_DIM), lambda b, n: (0, 0)),
                  row(SSM_CONV_DIM), row(LANES), row(LANES), row(SSM_INNER), row(SSM_INNER),
                  row(SSM_INNER)],
        out_specs=pl.BlockSpec((1, blk, SSM_INNER), lambda b, n: (b, n, 0)),
        out_shape=jax.ShapeDtypeStruct((bsz, seq, SSM_INNER), F32),
        scratch_shapes=[pltpu.VMEM((blk + 8, SSM_CONV_DIM), F32),
                        pltpu.VMEM((blk, SSM_CONV_DIM), F32),
                        pltpu.VMEM((SSM_HEADS, SSM_STATE, SSM_HEAD_DIM), F32),
                        pltpu.VMEM((blk, SSM_INNER), F32)],
        compiler_params=_cparams("arbitrary", "arbitrary"),
        name=name,
    )(sproj, sproj, sproj, conv_w, conv_b.reshape(1, -1), dtb, alog, wide(a_log), wide(d_skip),
      norm_g.reshape(1, -1))


def _merge_body(ya_ref, yb_ref, yc1_ref, yc2_ref, yc3_ref, yd_ref, x_ref, gm_ref, wmg_ref, wbr_ref,
                wout_ref, gf_ref, wr_ref, o_ref, gid_ref):
    ys = (ya_ref[...], yb_ref[...], yc1_ref[...] + yc2_ref[...] + yc3_ref[...], yd_ref[...])
    x = x_ref[...]
    hn = _rms(x, gm_ref[...]).astype(BF16)
    u = None
    for nbr in range(N_BRANCH):
        gate = jnp.dot(hn, wmg_ref[:, nbr * D_MODEL:(nbr + 1) * D_MODEL], preferred_element_type=F32)
        t = _sigmoid(gate) * _dot(ys[nbr], wbr_ref[nbr])
        u = t if u is None else u + t
    xn = x + _dot(u, wout_ref[...])
    o_ref[...] = xn
    gid_ref[...] = _top_group(xn, gf_ref, wr_ref)


def merge(ya, yb, yc1, yc2, yc3, yd, x2d, g_mix, w_mgate, w_branch, w_out, g_ffn, wr, tm, name):
    m = x2d.shape[0]

    def rows(w):
        return pl.BlockSpec((tm, w), lambda i: (i, 0))

    return pl.pallas_call(
        _merge_body,
        grid=(m // tm,),
        in_specs=[rows(512)] * 6 + [rows(D_MODEL),
                                    pl.BlockSpec((1, D_MODEL), lambda i: (0, 0)),
                                    pl.BlockSpec((D_MODEL, N_BRANCH * D_MODEL), lambda i: (0, 0)),
                                    pl.BlockSpec((N_BRANCH, BRANCH_WIDTH, D_MODEL), lambda i: (0, 0, 0)),
                                    pl.BlockSpec((D_MODEL, D_MODEL), lambda i: (0, 0)),
                                    pl.BlockSpec((1, D_MODEL), lambda i: (0, 0)),
                                    pl.BlockSpec((2, D_MODEL, LANES), lambda i: (0, 0, 0))],
        out_specs=[rows(D_MODEL), rows(LANES)],
        out_shape=[jax.ShapeDtypeStruct((m, D_MODEL), F32), jax.ShapeDtypeStruct((m, LANES), jnp.int32)],
        compiler_params=_cparams("arbitrary"),
        name=name,
    )(ya, yb, yc1, yc2, yc3, yd, x2d, g_mix.reshape(1, -1), w_mgate, w_branch.astype(BF16),
      w_out.astype(BF16), g_ffn.reshape(1, -1), wr)


def _router_logits(hn, wr_ref):
    h_hi = hn.astype(BF16)
    h_lo = (hn - h_hi.astype(F32)).astype(BF16)
    return (jnp.dot(h_hi, wr_ref[0], preferred_element_type=F32)
            + jnp.dot(h_hi, wr_ref[1], preferred_element_type=F32)
            + jnp.dot(h_lo, wr_ref[0], preferred_element_type=F32))


def _group_lanes(lane):
    return (lane >= MOE_EXPERTS) & (lane < MOE_EXPERTS + MOE_GROUPS)


def _top_group(x, g_ref, wr_ref):
    tm = x.shape[0]
    lane = lax.broadcasted_iota(jnp.int32, (tm, LANES), 1)
    logits = _router_logits(_rms(x, g_ref[...]), wr_ref)
    lg = jnp.where(_group_lanes(lane), logits, -jnp.inf)
    mg = jnp.max(lg, axis=-1, keepdims=True)
    gi = jnp.min(jnp.where(lg == mg, lane.astype(F32), 1e9), axis=-1, keepdims=True) - MOE_EXPERTS
    return jnp.broadcast_to(gi, (tm, LANES)).astype(jnp.int32)


def _gather_body(idx_ref, nlive_ref, src_ref, o_ref, buf, sem):
    rows = o_ref.shape[0]
    i = pl.program_id(0)
    nlive = nlive_ref[0]

    def live(tile):
        return tile * rows < nlive

    def issue_tile(tile):
        base = tile * rows
        slot = tile % 2

        def issue(r, carry):
            pltpu.make_async_copy(src_ref.at[pl.ds(idx_ref[base + r], 1)], buf.at[slot, pl.ds(r, 1)],
                                  sem.at[slot]).start()
            return carry

        lax.fori_loop(0, rows, issue, 0, unroll=8)

    @pl.when((i == 0) & live(0))
    def _():
        issue_tile(0)

    @pl.when((i + 1 < pl.num_programs(0)) & live(i + 1))
    def _():
        issue_tile(i + 1)

    @pl.when(live(i))
    def _():
        slot = i % 2
        pltpu.make_async_copy(src_ref.at[pl.ds(0, rows)], buf.at[slot], sem.at[slot]).wait()
        o_ref[...] = buf[slot]

    @pl.when(jnp.logical_not(live(i)))
    def _():
        o_ref[...] = jnp.zeros_like(o_ref)


def gather_rows(src, idx, nlive, rows, name):
    n = idx.shape[0]
    d = src.shape[1]
    grid_spec = pltpu.PrefetchScalarGridSpec(
        num_scalar_prefetch=2,
        grid=(n // rows,),
        in_specs=[pl.BlockSpec(memory_space=pl.ANY)],
        out_specs=pl.BlockSpec((rows, d), lambda i, *_: (i, 0)),
        scratch_shapes=[pltpu.VMEM((2, rows, d), src.dtype), pltpu.SemaphoreType.DMA((2,))],
    )
    return pl.pallas_call(
        _gather_body,
        grid_spec=grid_spec,
        out_shape=jax.ShapeDtypeStruct((n, d), src.dtype),
        compiler_params=_cparams("arbitrary"),
        name=name,
    )(idx, jnp.reshape(nlive, (1,)).astype(jnp.int32), src)


def _moe_body(tg_ref, x_ref, g_ref, wr_ref, wg_ref, wu_ref, wd_ref, o_ref, hn_ref, comb_ref):
    e = pl.program_id(1)
    grp = tg_ref[pl.program_id(0)]
    tm = x_ref.shape[0]
    lane = lax.broadcasted_iota(jnp.int32, (tm, LANES), 1)
    lanef = lane.astype(F32)
    live = grp < MOE_GROUPS

    @pl.when(e == 0)
    def _():
        o_ref[...] = x_ref[...]

    @pl.when((e == 0) & live)
    def _():
        hn = _rms(x_ref[...], g_ref[...])
        hn_ref[...] = hn.astype(BF16)
        logits = _router_logits(hn, wr_ref)
        is_grp = _group_lanes(lane)
        lg = jnp.where(is_grp, logits, -jnp.inf)
        mg = jnp.max(lg, axis=-1, keepdims=True)
        sg = jnp.sum(jnp.where(is_grp, jnp.exp(lg - mg), 0.0), axis=-1, keepdims=True)
        lt = jnp.sum(jnp.where(lane == MOE_EXPERTS + grp, logits, 0.0), axis=-1, keepdims=True)
        g_w = jnp.exp(lt - mg) / sg
        in_grp = (lane < MOE_EXPERTS) & (lane // MOE_EPG == grp)
        le = jnp.where(in_grp, logits, -jnp.inf)
        m1 = jnp.max(le, axis=-1, keepdims=True)
        i1 = jnp.min(jnp.where(le == m1, lanef, 1e9), axis=-1, keepdims=True)
        le2 = jnp.where(lanef == i1, -jnp.inf, le)
        m2 = jnp.max(le2, axis=-1, keepdims=True)
        i2 = jnp.min(jnp.where(le2 == m2, lanef, 1e9), axis=-1, keepdims=True)
        e2 = jnp.exp(m2 - m1)
        den = 1.0 + e2
        comb_ref[...] = (jnp.where(lanef == i1, g_w / den, 0.0)
                         + jnp.where(lanef == i2, g_w * e2 / den, 0.0))

    @pl.when(live)
    def _():
        hn = hn_ref[...]
        gate = jnp.dot(hn, wg_ref[0].astype(BF16), preferred_element_type=F32)
        up = jnp.dot(hn, wu_ref[0].astype(BF16), preferred_element_type=F32)
        ce = jnp.sum(jnp.where(lane == grp * MOE_EPG + e, comb_ref[...], 0.0), axis=-1, keepdims=True)
        o_ref[...] += _dot(_silu(gate) * up * ce, wd_ref[0])


def _moe_plan(gid, tm):
    t = gid.shape[0]
    onehot = (gid[:, None] == jnp.arange(MOE_GROUPS, dtype=jnp.int32)).astype(jnp.int32)
    csum = jnp.cumsum(onehot, axis=0)
    counts = csum[-1]
    rank = (csum * onehot).sum(axis=1) - 1
    padded = (counts + tm - 1) // tm * tm
    pend = jnp.cumsum(padded)
    dest = (pend - padded)[gid] + rank
    r_pad = t + MOE_GROUPS * tm
    row_token = (jnp.arange(r_pad, dtype=jnp.int32) % t).at[dest].set(jnp.arange(t, dtype=jnp.int32))
    tile_start = jnp.arange(r_pad // tm, dtype=jnp.int32) * tm
    tile_group = (tile_start[:, None] >= pend[None, :]).sum(axis=1)
    return row_token, dest.astype(jnp.int32), tile_group.astype(jnp.int32), pend[-1]


def router_weights(w_grp, w_exp):
    wr32 = jnp.concatenate([w_exp, w_grp,
                            jnp.zeros((D_MODEL, LANES - MOE_EXPERTS - MOE_GROUPS), F32)], axis=1)
    wr_hi = wr32.astype(BF16)
    return jnp.stack([wr_hi, (wr32 - wr_hi.astype(F32)).astype(BF16)])


def moe(x2d, gid, g_ffn, wr, w_gate, w_up, w_down, layer, tm, name):
    m = x2d.shape[0]
    gf = g_ffn.reshape(1, -1)
    row_token, dest, tile_group, n_sorted = _moe_plan(gid, tm)
    rows_dma = min(MOE_GATHER_ROWS, tm)
    xs = gather_rows(x2d, row_token, n_sorted, rows_dma, name + "_gather")

    def expert(i, e, tg):
        return (layer * MOE_EXPERTS + jnp.minimum(tg[i], MOE_GROUPS - 1) * MOE_EPG + e, 0, 0)

    grid_spec = pltpu.PrefetchScalarGridSpec(
        num_scalar_prefetch=1,
        grid=(xs.shape[0] // tm, MOE_EPG),
        in_specs=[pl.BlockSpec((tm, D_MODEL), lambda i, e, tg: (i, 0)),
                  pl.BlockSpec((1, D_MODEL), lambda i, e, tg: (0, 0)),
                  pl.BlockSpec((2, D_MODEL, LANES), lambda i, e, tg: (0, 0, 0)),
                  pl.BlockSpec((1, D_MODEL, MOE_FF), expert),
                  pl.BlockSpec((1, D_MODEL, MOE_FF), expert),
                  pl.BlockSpec((1, MOE_FF, D_MODEL), expert)],
        out_specs=pl.BlockSpec((tm, D_MODEL), lambda i, e, tg: (i, 0)),
        scratch_shapes=[pltpu.VMEM((tm, D_MODEL), BF16), pltpu.VMEM((tm, LANES), F32)],
    )
    ys = pl.pallas_call(
        _moe_body,
        grid_spec=grid_spec,
        out_shape=jax.ShapeDtypeStruct(xs.shape, F32),
        compiler_params=_cparams("arbitrary", "arbitrary"),
        name=name,
    )(tile_group, xs, gf, wr, *[w.reshape((-1,) + w.shape[2:]) for w in (w_gate, w_up, w_down)])
    return gather_rows(ys, dest, jnp.int32(m), rows_dma, name + "_scatter")


def _ple_body(x_ref, p_ref, g_ref, wg_ref, wp_ref, gf_ref, o_ref, *, final):
    x = x_ref[...]
    gate = _sigmoid(_dot(_rms(x, g_ref[...]), wg_ref[...]))
    xn = x + _dot(p_ref[...], wp_ref[...]) * gate
    if final:
        xn = _rms(xn, gf_ref[...])
    o_ref[...] = xn


def ple(x2d, p2d, g_ple, w_gate, w_proj, g_final, final, tm, name):
    m = x2d.shape[0]
    return pl.pallas_call(
        functools.partial(_ple_body, final=final),
        grid=(m // tm,),
        in_specs=[pl.BlockSpec((tm, D_MODEL), lambda i: (i, 0)),
                  pl.BlockSpec((tm, PLE_DIM), lambda i: (i, 0)),
                  pl.BlockSpec((1, D_MODEL), lambda i: (0, 0)),
                  pl.BlockSpec((D_MODEL, D_MODEL), lambda i: (0, 0)),
                  pl.BlockSpec((PLE_DIM, D_MODEL), lambda i: (0, 0)),
                  pl.BlockSpec((1, D_MODEL), lambda i: (0, 0))],
        out_specs=pl.BlockSpec((tm, D_MODEL), lambda i: (i, 0)),
        out_shape=jax.ShapeDtypeStruct((m, D_MODEL), F32),
        compiler_params=_cparams("arbitrary"),
        name=name,
    )(x2d, p2d, g_ple.reshape(1, -1), w_gate.astype(BF16), w_proj.astype(BF16),
      g_final.reshape(1, -1))


def _cols(w, *names):
    return [w[:, _OFF[n][0]:_OFF[n][0] + _OFF[n][1]] for n in names]


def _padcols(w, width):
    return jnp.pad(w, ((0, 0), (0, width - w.shape[1])))


def _pad_q_heads(wq):
    hd = ATT_HEAD_DIM
    zero = jnp.zeros((wq.shape[0], hd), wq.dtype)
    cols = []
    for h in range(ATT_HEADS):
        blk = wq[:, h * hd:(h + 1) * hd] * (hd ** -0.5)
        cols += [blk, zero] if h // ATT_REP == 0 else [zero, blk]
    return jnp.concatenate(cols, axis=1)


def _split_w_in(w):
    swa_q, nsa_q = _cols(w, 'swa_q', 'nsa_q')
    w_att = jnp.concatenate([_pad_q_heads(swa_q), _pad_q_heads(nsa_q)] + _cols(w, *KV_BLK), axis=1)
    (xbc, dt, ngate, z) = _cols(w, 'ssm_xbc', 'ssm_dt', 'nsa_gate', 'ssm_z')
    w_rec = jnp.concatenate([xbc, _padcols(dt, LANES), _padcols(ngate, LANES), z]
                            + _cols(w, 'hgrn_q', 'hgrn_f', 'hgrn_i', 'hgrn_g'), axis=1)
    (w_mg,) = _cols(w, 'merge_gate')
    return [a.astype(BF16) for a in (w_att, w_rec, w_mg)]


def _mixers(i, att, sproj, attn_sinks, hgrn_lower_bounds, hgrn_norm_g, nsa_pos_k, nsa_pos_v,
            nsa_cmp_w1_k, nsa_cmp_w2_k, nsa_cmp_w1_v, nsa_cmp_w2_v, ssm_conv_w, ssm_conv_b,
            ssm_dt_bias, ssm_A_log, ssm_D, ssm_norm_g):
    bsz, seq, _ = att.shape
    kv = KV_BLK
    gate_blk = 7
    y_a = banded_attention(att, 0, kv['swa_k'], kv['swa_v'], window=SWA_WINDOW, sinks=attn_sinks[i],
                           name=f"swa{i}")
    y_b = hgrn2(sproj, 3, hgrn_lower_bounds, hgrn_norm_g[i], i, name=f"hgrn{i}")
    ncp = seq // NSA_CMP_STRIDE

    def cmp_in(name):
        c0 = kv[name] * LANES
        return att[:, :, c0:c0 + LANES].reshape(bsz, ncp, NSA_CMP_STRIDE * LANES)

    kc = nsa_compress(cmp_in('nsa_k_cmp'), nsa_pos_k[i], nsa_cmp_w1_k[i], nsa_cmp_w2_k[i], name=f"cmpk{i}")
    vc = nsa_compress(cmp_in('nsa_v_cmp'), nsa_pos_v[i], nsa_cmp_w1_v[i], nsa_cmp_w2_v[i], name=f"cmpv{i}")
    y_c1, mask, act = nsa_cmp_select(att, 1, kc, vc, sproj, gate_blk, name=f"nsacmp{i}")
    y_c2 = nsa_selected(att, 1, kv['nsa_k_slc'], kv['nsa_v_slc'], mask, act, sproj, gate_blk,
                        name=f"nsasel{i}")
    y_c3 = banded_attention(att, 1, kv['nsa_k_win'], kv['nsa_v_win'], window=NSA_WINDOW, gate=sproj,
                            gate_blk=gate_blk, gate_col=2, name=f"nsawin{i}")
    y_d = mamba2(sproj, ssm_conv_w[i], ssm_conv_b[i], ssm_dt_bias[i], ssm_A_log[i], ssm_D[i],
                 ssm_norm_g[i], name=f"ssd{i}")
    return y_a, y_b, y_c1, y_c2, y_c3, y_d


def kernel(x, p, w_in, g_mix, attn_sinks, hgrn_lower_bounds, hgrn_norm_g, nsa_pos_k, nsa_pos_v,
           nsa_cmp_w1_k, nsa_cmp_w2_k, nsa_cmp_w1_v, nsa_cmp_w2_v, ssm_conv_w, ssm_conv_b,
           ssm_dt_bias, ssm_A_log, ssm_D, ssm_norm_g, w_branch, w_out, g_ffn, w_router_grp,
           w_router_exp, w_exp_gate, w_exp_up, w_exp_down, g_ple, w_ple_gate, w_ple_proj, g_final):
    bsz, seq, d = x.shape
    depth = w_in.shape[0]
    t = bsz * seq
    x2 = x.reshape(t, d)
    tm_proj = min(1024, t)
    tm_merge = min(256, t)
    tm_ple = min(512, t)
    tm_moe = min(1024, t)
    for i in range(depth):
        w_att, w_rec, w_mg = _split_w_in(w_in[i])
        att = norm_mm(x2, g_mix[i], w_att, BF16, tm_proj, 1024, f"proj_att{i}").reshape(bsz, seq, -1)
        sproj = norm_mm(x2, g_mix[i], w_rec, F32, tm_proj, 512, f"proj_rec{i}").reshape(bsz, seq, -1)
        ys = _mixers(i, att, sproj, attn_sinks, hgrn_lower_bounds, hgrn_norm_g, nsa_pos_k,
                     nsa_pos_v, nsa_cmp_w1_k, nsa_cmp_w2_k, nsa_cmp_w1_v, nsa_cmp_w2_v, ssm_conv_w,
                     ssm_conv_b, ssm_dt_bias, ssm_A_log, ssm_D, ssm_norm_g)
        ys = [y.reshape(t, -1) for y in ys]
        wr = router_weights(w_router_grp[i], w_router_exp[i])
        x2, gid = merge(*ys, x2, g_mix[i], w_mg, w_branch[i], w_out[i], g_ffn[i], wr, tm_merge, f"merge{i}")
        x2 = moe(x2, gid[:, 0], g_ffn[i], wr, w_exp_gate, w_exp_up, w_exp_down, i, tm_moe, f"moe{i}")
        x2 = ple(x2, p[i].reshape(t, -1), g_ple[i], w_ple_gate[i], w_ple_proj[i], g_final,
                 i == depth - 1, tm_ple, f"ple{i}")
    return x2.reshape(bsz, seq, d)
```

```python
import functools

import numpy as np
import jax
import jax.numpy as jnp
from jax import lax
from jax.experimental import pallas as pl
from jax.experimental.pallas import tpu as pltpu

F32 = jnp.float32
BF16 = jnp.bfloat16

D_MODEL = 1024
PLE_DIM = 256
NORM_EPS = 1e-6
NEG_INF = -1e30
REMOVED = -3e38
N_BRANCH = 4
BRANCH_WIDTH = 512
ATTN_BLOCK = 128

ATT_HEADS = 8
ATT_KV_HEADS = 2
ATT_HEAD_DIM = 64
ATT_REP = ATT_HEADS // ATT_KV_HEADS
SWA_WINDOW = 128
NSA_WINDOW = 512
NSA_CMP_LEN = 32
NSA_CMP_STRIDE = 16
NSA_CMP_HIDDEN = 256
NSA_SEL_LEN = 64
NSA_TOP_N = 16
NSA_FORCE_SCORE = 1e6
CMP_KEY_CHUNK = 128
SEL_KEY_TILE = 512
SEL_FEATS = SEL_KEY_TILE // NSA_SEL_LEN
MASK_BIG = 1e30

HGRN_HEADS = 4
HGRN_DIM = 128
HGRN_CHUNK = 32
HGRN_MIN_F = 1e-6
HGRN_BLOCK = 256

SSM_HEADS = 8
SSM_HEAD_DIM = 64
SSM_GROUPS = 2
SSM_STATE = 64
SSM_CONV = 4
SSM_CHUNK = 64
SSM_INNER = 512
SSM_CONV_DIM = 768
SSM_BLOCK = 256

MOE_GROUPS = 4
MOE_EPG = 8
MOE_EXPERTS = 32
MOE_FF = 256
MOE_GATHER_ROWS = 1024

LANES = 128
VMEM_LIMIT = 56 * 1024 * 1024

ALIBI_SLOPES = tuple(2.0 ** (-8.0 * (h + 1) / ATT_HEADS) for h in range(ATT_HEADS))

ATT_Q_WIDTH = ATT_HEADS * LANES
KV_BLK = {name: 2 * ATT_HEADS + j for j, name in enumerate(
    ('swa_k', 'swa_v', 'nsa_k_cmp', 'nsa_v_cmp', 'nsa_k_slc', 'nsa_v_slc', 'nsa_k_win', 'nsa_v_win'))}

_SPLITS = (
    ('swa_q', 512), ('swa_k', 128), ('swa_v', 128),
    ('hgrn_q', 512), ('hgrn_f', 512), ('hgrn_i', 512), ('hgrn_g', 512),
    ('nsa_q', 512), ('nsa_k_cmp', 128), ('nsa_v_cmp', 128), ('nsa_k_slc', 128),
    ('nsa_v_slc', 128), ('nsa_k_win', 128), ('nsa_v_win', 128), ('nsa_gate', 24),
    ('ssm_z', 512), ('ssm_xbc', 768), ('ssm_dt', 8), ('merge_gate', 4096),
)
_OFF = {}
_o = 0
for _n, _w in _SPLITS:
    _OFF[_n] = (_o, _w)
    _o += _w


def _cparams(*sem):
    return pltpu.CompilerParams(dimension_semantics=sem, vmem_limit_bytes=VMEM_LIMIT)


def _sigmoid(x):
    return 1.0 / (1.0 + jnp.exp(-x))


def _silu(x):
    return x * _sigmoid(x)


def _dot(a, b):
    return jnp.dot(a.astype(BF16), b.astype(BF16), preferred_element_type=F32)


def _dot_nt(a, b):
    return lax.dot_general(a.astype(BF16), b.astype(BF16), (((1,), (1,)), ((), ())),
                           preferred_element_type=F32)


def _dot_tn(a, b):
    return lax.dot_general(a.astype(BF16), b.astype(BF16), (((0,), (0,)), ((), ())),
                           preferred_element_type=F32)


def _split3(x):
    hi = x.astype(BF16)
    r1 = x - hi.astype(F32)
    mid = r1.astype(BF16)
    return hi, mid, (r1 - mid.astype(F32)).astype(BF16)


def _prefix_dot(tri, x):
    t = tri.astype(BF16)
    return sum(jnp.dot(t, p, preferred_element_type=F32) for p in _split3(x))


def _prefix_dot_tn(x, tri):
    t = tri.astype(BF16)
    return sum(lax.dot_general(p, t, (((0,), (0,)), ((), ())), preferred_element_type=F32)
               for p in _split3(x))


def _rms(x, g):
    ms = jnp.mean(x * x, axis=-1, keepdims=True)
    return x * lax.rsqrt(ms + NORM_EPS) * g


def _norm_mm_body(x_ref, g_ref, w_ref, o_ref, hn_ref):
    @pl.when(pl.program_id(1) == 0)
    def _():
        hn_ref[...] = _rms(x_ref[...], g_ref[...]).astype(BF16)

    o_ref[...] = jnp.dot(hn_ref[...], w_ref[...], preferred_element_type=F32).astype(o_ref.dtype)


def norm_mm(x2d, g, w, out_dtype, tm, tn, name):
    m, k = x2d.shape
    n = w.shape[1]
    return pl.pallas_call(
        _norm_mm_body,
        grid=(m // tm, n // tn),
        in_specs=[pl.BlockSpec((tm, k), lambda i, j: (i, 0)),
                  pl.BlockSpec((1, k), lambda i, j: (0, 0)),
                  pl.BlockSpec((k, tn), lambda i, j: (0, j))],
        out_specs=pl.BlockSpec((tm, tn), lambda i, j: (i, j)),
        out_shape=jax.ShapeDtypeStruct((m, n), out_dtype),
        scratch_shapes=[pltpu.VMEM((tm, k), BF16)],
        compiler_params=_cparams("arbitrary", "arbitrary"),
        name=name,
    )(x2d, g.reshape(1, k), w)


def _ones_row(g):
    return ATT_HEAD_DIM if g == 0 else 0


def _vt_body(v_ref, o_ref):
    vt = jnp.transpose(v_ref[0].astype(F32))
    rowid = lax.broadcasted_iota(jnp.int32, vt.shape, 0)
    for g in range(ATT_KV_HEADS):
        aug = jnp.where(rowid // ATT_HEAD_DIM == g, vt, jnp.where(rowid == _ones_row(g), 1.0, 0.0))
        o_ref[0, g] = aug.astype(BF16)


def v_transposed(arr, v_blk, name):
    bsz, seq, _ = arr.shape
    rows = min(2048, seq)
    return pl.pallas_call(
        _vt_body,
        grid=(bsz, seq // rows),
        in_specs=[pl.BlockSpec((1, rows, LANES), lambda b, n: (b, n, v_blk))],
        out_specs=pl.BlockSpec((1, ATT_KV_HEADS, LANES, rows), lambda b, n: (b, 0, 0, n)),
        out_shape=jax.ShapeDtypeStruct((bsz, ATT_KV_HEADS, LANES, seq), BF16),
        compiler_params=_cparams("arbitrary", "arbitrary"),
        name=name,
    )(arr)


def _q_stack(q_ref, g):
    return jnp.concatenate([q_ref[0, :, (g * ATT_REP + r) * LANES:(g * ATT_REP + r + 1) * LANES]
                            for r in range(ATT_REP)], axis=0)


def _head(x, r):
    return x[:, r * ATTN_BLOCK:(r + 1) * ATTN_BLOCK]


def _banded_body(*refs, window, nprev, use_sink, gate_col):
    if use_sink:
        q_ref, k_ref, vt_ref, sink_ref, o_ref, ot_ref = refs
    else:
        q_ref, k_ref, vt_ref, gate_ref, o_ref, ot_ref = refs
    blk = ATTN_BLOCK
    hd = ATT_HEAD_DIM
    n = pl.program_id(1)
    nk = (nprev + 1) * blk
    start = pl.multiple_of(jnp.maximum(n - nprev, 0) * blk, blk)
    k128 = k_ref[0, pl.ds(start, nk), :]
    krow = lax.broadcasted_iota(jnp.int32, (nk, blk), 0)
    qcol = lax.broadcasted_iota(jnp.int32, (nk, blk), 1)
    rel = (n * blk + qcol) - (start + krow)
    negrel = jnp.where((rel >= 0) & (rel < window), -rel.astype(F32), NEG_INF)
    if not use_sink:
        gate_t = jnp.transpose(_sigmoid(gate_ref[0]))
    st4s = [_dot_nt(k128, _q_stack(q_ref, g)) for g in range(ATT_KV_HEADS)]
    for g in range(ATT_KV_HEADS):
        one = _ones_row(g)
        st4 = st4s[g]
        ps, ms = [], []
        for r in range(ATT_REP):
            h = g * ATT_REP + r
            st = _head(st4, r) + ALIBI_SLOPES[h] * negrel
            m = jnp.max(st, axis=0, keepdims=True)
            if use_sink:
                m = jnp.maximum(m, sink_ref[h])
            ps.append(jnp.exp(st - m).astype(BF16))
            ms.append(m)
        acc4 = jnp.dot(vt_ref[0, g, :, pl.ds(start, nk)], jnp.concatenate(ps, axis=1),
                       preferred_element_type=F32)
        for r in range(ATT_REP):
            h = g * ATT_REP + r
            acc = _head(acc4, r)
            l = acc[one:one + 1]
            if use_sink:
                l = l + jnp.exp(sink_ref[h] - ms[r])
            ot = acc[g * hd:(g + 1) * hd] / l
            if not use_sink:
                c = 3 * h + gate_col
                ot = ot * gate_t[c:c + 1]
            ot_ref[h * hd:(h + 1) * hd, :] = ot
    o_ref[0] = jnp.transpose(ot_ref[...])


def banded_attention(att, q_blk, k_blk, v_blk, *, window, sinks=None, gate=None, gate_blk=None,
                     gate_col=0, name):
    bsz, seq, _ = att.shape
    blk = ATTN_BLOCK
    nprev = (window - 1 + blk - 1) // blk
    use_sink = sinks is not None
    vt = v_transposed(att, v_blk, name + "_vt")
    in_specs = [pl.BlockSpec((1, blk, ATT_Q_WIDTH), lambda b, n: (b, n, q_blk)),
                pl.BlockSpec((1, seq, LANES), lambda b, n: (b, 0, k_blk)),
                pl.BlockSpec((1, ATT_KV_HEADS, LANES, seq), lambda b, n: (b, 0, 0, 0))]
    if use_sink:
        in_specs.append(pl.BlockSpec(memory_space=pltpu.SMEM))
        extra = sinks
    else:
        in_specs.append(pl.BlockSpec((1, blk, LANES), lambda b, n: (b, n, gate_blk)))
        extra = gate
    return pl.pallas_call(
        functools.partial(_banded_body, window=window, nprev=nprev, use_sink=use_sink,
                          gate_col=gate_col),
        grid=(bsz, seq // blk),
        in_specs=in_specs,
        out_specs=pl.BlockSpec((1, blk, 512), lambda b, n: (b, n, 0)),
        out_shape=jax.ShapeDtypeStruct((bsz, seq, 512), F32),
        scratch_shapes=[pltpu.VMEM((512, blk), F32)],
        compiler_params=_cparams("arbitrary", "arbitrary"),
        name=name,
    )(att, att, vt, extra)


def _compress_body(x_ref, w1a_ref, w1b_ref, w1_ref, pos_ref, w2_ref, o_ref):
    x = x_ref[0]
    p = jnp.dot(x, w1a_ref[...], preferred_element_type=F32)
    q = jnp.dot(x, w1b_ref[...], preferred_element_type=F32)
    ncp = x.shape[0]
    q = pltpu.roll(q, shift=ncp - 1, axis=0)
    posb = jnp.broadcast_to(pos_ref[...], (8, pos_ref.shape[1]))
    bias = _dot(posb, w1_ref[...])[0:1]
    hid = NSA_CMP_HIDDEN
    outs = []
    for g in range(ATT_KV_HEADS):
        pre = p[:, g * hid:(g + 1) * hid] + q[:, g * hid:(g + 1) * hid] + bias
        outs.append(_dot(jax.nn.gelu(pre), w2_ref[...]))
    o_ref[0] = jnp.concatenate(outs, axis=1).astype(o_ref.dtype)


def _expand_w1(w1, half):
    hd, hid, ng = ATT_HEAD_DIM, NSA_CMP_HIDDEN, ATT_KV_HEADS
    w = w1.reshape(NSA_CMP_LEN, hd, hid)[half * 16:(half + 1) * 16]
    eye = jnp.eye(ng, dtype=w1.dtype)
    out = jnp.einsum('ldj,gh->lgdhj', w, eye)
    return out.reshape(16 * ng * hd, ng * hid)


def nsa_compress(x16, pos, w1, w2, name):
    bsz, ncp, wid = x16.shape
    hid = NSA_CMP_HIDDEN
    w1a = _expand_w1(w1, 0).astype(BF16)
    w1b = _expand_w1(w1, 1).astype(BF16)
    return pl.pallas_call(
        _compress_body,
        grid=(bsz,),
        in_specs=[pl.BlockSpec((1, ncp, wid), lambda b: (b, 0, 0)),
                  pl.BlockSpec((wid, 2 * hid), lambda b: (0, 0)),
                  pl.BlockSpec((wid, 2 * hid), lambda b: (0, 0)),
                  pl.BlockSpec((NSA_CMP_LEN * ATT_HEAD_DIM, hid), lambda b: (0, 0)),
                  pl.BlockSpec((1, NSA_CMP_LEN * ATT_HEAD_DIM), lambda b: (0, 0)),
                  pl.BlockSpec((hid, ATT_HEAD_DIM), lambda b: (0, 0))],
        out_specs=pl.BlockSpec((1, ncp, LANES), lambda b: (b, 0, 0)),
        out_shape=jax.ShapeDtypeStruct((bsz, ncp, LANES), BF16),
        compiler_params=_cparams("arbitrary"),
        name=name,
    )(x16, w1a, w1b, w1.astype(BF16), pos.reshape(1, -1), w2.astype(BF16))


def _nsa_cmp_body(q_ref, kc_ref, vct_ref, ovt_ref, gate_ref, o_ref, m_ref, act_ref, ot_ref, imp_ref, *,
                  top_n):
    blk = ATTN_BLOCK
    hd = ATT_HEAD_DIM
    n = pl.program_id(1)
    kc = kc_ref[0]
    ncp = kc.shape[0]
    ns = ovt_ref.shape[0]
    crow = lax.broadcasted_iota(jnp.int32, (ncp, blk), 0)
    qcol = lax.broadcasted_iota(jnp.int32, (ncp, blk), 1)
    dist = (n * blk + qcol) - (crow * NSA_CMP_STRIDE + (NSA_CMP_LEN - 1))
    valid = dist >= 0
    negd = jnp.where(valid, -dist.astype(F32), NEG_INF)
    jrow = lax.broadcasted_iota(jnp.int32, (ns, blk), 0)
    qpos = n * blk + lax.broadcasted_iota(jnp.int32, (ns, blk), 1)
    cur = qpos // NSA_SEL_LEN
    forced = (jrow == 0) | (jrow == cur) | (jrow == cur - 1)
    causal_blk = jrow * NSA_SEL_LEN <= qpos
    jrowf = jrow.astype(F32)
    gate_t = jnp.transpose(_sigmoid(gate_ref[0]))
    ones8 = jnp.ones((8, blk), BF16)

    def attend(rows):
        st4s = [_dot_nt(kc[:rows], _q_stack(q_ref, g)) for g in range(ATT_KV_HEADS)]
        for g in range(ATT_KV_HEADS):
            one = _ones_row(g)
            es = []
            for r in range(ATT_REP):
                st = _head(st4s[g], r) + ALIBI_SLOPES[g * ATT_REP + r] * negd[:rows]
                m = jnp.max(st, axis=0, keepdims=True)
                es.append(jnp.where(valid[:rows], jnp.exp(st - m), 0.0))
            acc4 = jnp.dot(vct_ref[0, g, :, :rows], jnp.concatenate([e.astype(BF16) for e in es], axis=1),
                           preferred_element_type=F32)
            psum = jnp.zeros((rows, blk), F32)
            for r in range(ATT_REP):
                h = g * ATT_REP + r
                acc = _head(acc4, r)
                l = acc[one:one + 1]
                inv = jnp.where(l > 0.0, 1.0 / jnp.where(l > 0.0, l, 1.0), 0.0)
                ot_ref[h * hd:(h + 1) * hd, :] = acc[g * hd:(g + 1) * hd] * inv * gate_t[3 * h:3 * h + 1]
                psum = psum + es[r] * inv
            imp_ref[g] = _dot(ovt_ref[:, :rows], psum)

    chunk = min(CMP_KEY_CHUNK, ncp)
    nchunks = ncp // chunk
    need = jnp.minimum((n * (blk // NSA_CMP_STRIDE) + blk // NSA_CMP_STRIDE - 1 + chunk - 1) // chunk,
                       nchunks)
    for c in range(1, nchunks + 1):
        pl.when(need == c)(functools.partial(attend, c * chunk))

    for g in range(ATT_KV_HEADS):
        score = jnp.where(forced, NSA_FORCE_SCORE, jnp.where(causal_blk, imp_ref[g], NEG_INF))
        sel = jnp.zeros((ns, blk), F32)
        for _ in range(top_n):
            mx = jnp.max(score, axis=0, keepdims=True)
            idx = jnp.min(jnp.where(score == mx, jrowf, float(ns)), axis=0, keepdims=True)
            hit = jrowf == idx
            sel = jnp.where(hit, 1.0, sel)
            score = jnp.where(hit, REMOVED, score)
        m_ref[0, g] = sel
        act_ref[0, 0, g * 8:(g + 1) * 8, :] = _dot_nt(ones8, sel)
    o_ref[0] = jnp.transpose(ot_ref[...])


def _overlap_matrix_t(seq):
    n_cmp = seq // NSA_CMP_STRIDE
    n_sel = seq // NSA_SEL_LEN
    cs = np.arange(n_cmp)[None, :] * NSA_CMP_STRIDE
    ss = np.arange(n_sel)[:, None] * NSA_SEL_LEN
    ov = np.clip(np.minimum(cs + NSA_CMP_LEN, ss + NSA_SEL_LEN) - np.maximum(cs, ss), 0, None)
    return jnp.asarray(ov / NSA_CMP_LEN, dtype=BF16)


def nsa_cmp_select(att, q_blk, kc, vc, gate, gate_blk, name):
    bsz, seq, _ = att.shape
    blk = ATTN_BLOCK
    nb = seq // blk
    ncp = seq // NSA_CMP_STRIDE
    ns = seq // NSA_SEL_LEN
    top_n = min(NSA_TOP_N, ns)
    vct = v_transposed(vc, 0, name + "_vt")
    return pl.pallas_call(
        functools.partial(_nsa_cmp_body, top_n=top_n),
        grid=(bsz, nb),
        in_specs=[pl.BlockSpec((1, blk, ATT_Q_WIDTH), lambda b, n: (b, n, q_blk)),
                  pl.BlockSpec((1, ncp, LANES), lambda b, n: (b, 0, 0)),
                  pl.BlockSpec((1, ATT_KV_HEADS, LANES, ncp), lambda b, n: (b, 0, 0, 0)),
                  pl.BlockSpec((ns, ncp), lambda b, n: (0, 0)),
                  pl.BlockSpec((1, blk, LANES), lambda b, n: (b, n, gate_blk))],
        out_specs=[pl.BlockSpec((1, blk, 512), lambda b, n: (b, n, 0)),
                   pl.BlockSpec((1, ATT_KV_HEADS, ns, blk), lambda b, n: (b, 0, 0, n)),
                   pl.BlockSpec((1, 1, ATT_KV_HEADS * 8, ns), lambda b, n: (b, n, 0, 0))],
        out_shape=[jax.ShapeDtypeStruct((bsz, seq, 512), F32),
                   jax.ShapeDtypeStruct((bsz, ATT_KV_HEADS, ns, seq), F32),
                   jax.ShapeDtypeStruct((bsz, nb, ATT_KV_HEADS * 8, ns), F32)],
        scratch_shapes=[pltpu.VMEM((512, blk), F32), pltpu.VMEM((ATT_KV_HEADS, ns, blk), F32)],
        compiler_params=_cparams("arbitrary", "arbitrary"),
        name=name,
    )(att, kc, vct, _overlap_matrix_t(seq), gate)


def _k_aug_body(k_ref, o_ref):
    k = k_ref[0].astype(F32)
    rows = k.shape[0]
    kin = lax.broadcasted_iota(jnp.int32, (rows, LANES), 0) % SEL_KEY_TILE
    lane = lax.broadcasted_iota(jnp.int32, (rows, LANES), 1)
    for g in range(ATT_KV_HEADS):
        f = lane - (ATT_HEAD_DIM if g == 0 else 0)
        feat = jnp.where(f == kin // NSA_SEL_LEN, 1.0, 0.0)
        feat = jnp.where(f == SEL_FEATS, (kin % 256).astype(F32), feat)
        feat = jnp.where(f == SEL_FEATS + 1, (kin // 256 * 256).astype(F32), feat)
        feat = jnp.where((f == SEL_FEATS + 2) | (f == SEL_FEATS + 3), 1.0, feat)
        o_ref[0, g] = jnp.where(lane // ATT_HEAD_DIM == g, k, feat).astype(BF16)


def k_augmented(arr, k_blk, name):
    bsz, seq, _ = arr.shape
    rows = min(2048, seq)
    return pl.pallas_call(
        _k_aug_body,
        grid=(bsz, seq // rows),
        in_specs=[pl.BlockSpec((1, rows, LANES), lambda b, n: (b, n, k_blk))],
        out_specs=pl.BlockSpec((1, ATT_KV_HEADS, rows, LANES), lambda b, n: (b, 0, n, 0)),
        out_shape=jax.ShapeDtypeStruct((bsz, ATT_KV_HEADS, seq, LANES), BF16),
        compiler_params=_cparams("arbitrary", "arbitrary"),
        name=name,
    )(arr)


def _query_feature_rows():
    out = np.zeros((ATT_KV_HEADS, 2 * SEL_FEATS, ATT_REP * ATTN_BLOCK), np.float32)
    qin = np.arange(ATTN_BLOCK, dtype=np.float32)
    for g in range(ATT_KV_HEADS):
        for r in range(ATT_REP):
            slope = ALIBI_SLOPES[g * ATT_REP + r]
            cols = slice(r * ATTN_BLOCK, (r + 1) * ATTN_BLOCK)
            out[g, 0, cols] = slope
            out[g, 1, cols] = slope
            out[g, 3, cols] = -slope * qin
            out[g, SEL_FEATS + 2, cols] = -slope
    return jnp.asarray(out)


def _nsa_sel_body(tiles_ref, cnt_ref, q_ref, k_ref, vt_ref, m_ref, gate_ref, fq_ref, o_ref, ot_ref,
                  acc_ref, mx_ref, dq_ref, sa_ref, sb_ref, *, ntl):
    blk = ATTN_BLOCK
    hd = ATT_HEAD_DIM
    tk = SEL_KEY_TILE
    sl = NSA_SEL_LEN
    per = tk // sl
    b = pl.program_id(0)
    n = pl.program_id(1)
    nb = pl.num_programs(1)
    krow = lax.broadcasted_iota(jnp.int32, (tk, blk), 0)
    qcol = lax.broadcasted_iota(jnp.int32, (tk, blk), 1)
    dq_ref[...] = (qcol - krow).astype(F32)
    gate_t = jnp.transpose(_sigmoid(gate_ref[0]))
    pad_rows = jnp.zeros((hd - 2 * SEL_FEATS, ATT_REP * blk), BF16)
    for g in range(ATT_KV_HEADS):
        one = _ones_row(g)
        lrow = (b * nb + n) * ATT_KV_HEADS + g
        qt = jnp.concatenate(
            [jnp.transpose(q_ref[0, :, (g * ATT_REP + r) * LANES:(g * ATT_REP + r + 1) * LANES]
                           .astype(F32))[g * hd:(g + 1) * hd] for r in range(ATT_REP)],
            axis=1).astype(BF16)
        acc_ref[...] = jnp.zeros_like(acc_ref)
        mx_ref[...] = jnp.full_like(mx_ref, NEG_INF)

        def tile_scores(j, live):
            t = tiles_ref[lrow * ntl + j]
            base = pl.multiple_of(t * tk, tk)
            sel8 = m_ref[0, g, pl.ds(pl.multiple_of(t * per, per), per), :]
            if live is not None:
                sel8 = jnp.where(live, sel8, 0.0)
            off = (n * blk - base).astype(F32)
            mask_rows = jnp.concatenate([(sel8 - 1.0) * MASK_BIG] * ATT_REP, axis=1)
            alibi_rows = fq_ref[g, 0:SEL_FEATS, :] + off * fq_ref[g, SEL_FEATS:2 * SEL_FEATS, :]
            feats = jnp.concatenate([mask_rows, alibi_rows], axis=0).astype(BF16)
            rhs = jnp.concatenate([qt, feats, pad_rows] if g == 0 else [feats, pad_rows, qt], axis=0)
            return jnp.dot(k_ref[0, g, pl.ds(base, tk), :], rhs,
                           preferred_element_type=F32)

        def tile_finish(j, st4, diag):
            base = pl.multiple_of(tiles_ref[lrow * ntl + j] * tk, tk)
            off = (n * blk - base).astype(F32)
            if diag:
                causal = (dq_ref[...] + off) >= 0.0
            ps, ms = [], []
            for r in range(ATT_REP):
                st = _head(st4, r)
                if diag:
                    st = jnp.where(causal, st, NEG_INF)
                m = jnp.max(st, axis=0, keepdims=True)
                ps.append(jnp.exp(st - m).astype(BF16))
                ms.append(m)
            acc = jnp.dot(vt_ref[0, g, :, pl.ds(base, tk)], jnp.concatenate(ps, axis=1),
                          preferred_element_type=F32)
            return jnp.concatenate(ms, axis=1), acc

        cnt = cnt_ref[lrow]

        def scores_into(s_ref, j):
            s_ref[...] = tile_scores(jnp.minimum(j, ntl - 1), j < cnt)

        def finish_from(s_ref, j, diag):
            m_t, acc_t = tile_finish(jnp.minimum(j, ntl - 1), s_ref[...], diag)
            m_old = mx_ref[...]
            m_new = jnp.maximum(m_old, m_t)
            acc_ref[...] = jnp.exp(m_old - m_new) * acc_ref[...] + jnp.exp(m_t - m_new) * acc_t
            mx_ref[...] = m_new

        def two_tiles(k, diag):
            scores_into(sb_ref, 2 * k + 1)
            finish_from(sa_ref, 2 * k, diag)
            scores_into(sa_ref, 2 * k + 2)
            finish_from(sb_ref, 2 * k + 1, False)

        scores_into(sa_ref, 0)
        two_tiles(0, True)

        def later(k, carry):
            two_tiles(k, False)
            return carry

        lax.fori_loop(1, (cnt + 1) // 2, later, 0)
        acc4 = acc_ref[...]
        for r in range(ATT_REP):
            h = g * ATT_REP + r
            acc = _head(acc4, r)
            ot = acc[g * hd:(g + 1) * hd] / acc[one:one + 1]
            ot_ref[h * hd:(h + 1) * hd, :] = ot * gate_t[3 * h + 1:3 * h + 2]
    o_ref[0] = jnp.transpose(ot_ref[...])


def _tile_lists(act, seq):
    bsz, nb = act.shape[:2]
    ns = act.shape[-1]
    tk = SEL_KEY_TILE
    ntl = seq // tk
    per = tk // NSA_SEL_LEN
    cnt_blk = act.reshape(bsz, nb, ATT_KV_HEADS, 8, ns)[:, :, :, 0, :]
    hit = cnt_blk.reshape(bsz, nb, ATT_KV_HEADS, ntl, per).sum(-1) > 0.5
    tidx = jnp.arange(ntl, dtype=jnp.int32)
    diag = (jnp.arange(nb, dtype=jnp.int32) * ATTN_BLOCK + ATTN_BLOCK - 1) // tk
    active = hit & (tidx[None, None, None, :] <= diag[None, :, None, None])
    act_i = active.astype(jnp.int32)
    rank = jnp.cumsum(act_i[..., ::-1], axis=-1)[..., ::-1] - 1
    slot = (active[..., :, None] & (rank[..., :, None] == tidx)).astype(jnp.int32)
    tiles = (slot * tidx[:, None]).sum(-2).reshape(-1)
    cnt = act_i.sum(-1).reshape(-1)
    return tiles, cnt, ntl


def nsa_selected(att, q_blk, k_blk, v_blk, mask, act, gate, gate_blk, name):
    bsz, seq, _ = att.shape
    blk = ATTN_BLOCK
    ns = seq // NSA_SEL_LEN
    tiles, cnt, ntl = _tile_lists(act, seq)
    vt = v_transposed(att, v_blk, name + "_vt")
    kaug = k_augmented(att, k_blk, name + "_k")
    grid_spec = pltpu.PrefetchScalarGridSpec(
        num_scalar_prefetch=2,
        grid=(bsz, seq // blk),
        in_specs=[pl.BlockSpec((1, blk, ATT_Q_WIDTH), lambda b, n, *_: (b, n, q_blk)),
                  pl.BlockSpec((1, ATT_KV_HEADS, seq, LANES), lambda b, n, *_: (b, 0, 0, 0)),
                  pl.BlockSpec((1, ATT_KV_HEADS, LANES, seq), lambda b, n, *_: (b, 0, 0, 0)),
                  pl.BlockSpec((1, ATT_KV_HEADS, ns, blk), lambda b, n, *_: (b, 0, 0, n)),
                  pl.BlockSpec((1, blk, LANES), lambda b, n, *_: (b, n, gate_blk)),
                  pl.BlockSpec((ATT_KV_HEADS, 2 * SEL_FEATS, ATT_REP * blk), lambda b, n, *_: (0, 0, 0))],
        out_specs=pl.BlockSpec((1, blk, 512), lambda b, n, *_: (b, n, 0)),
        scratch_shapes=[pltpu.VMEM((512, blk), F32),
                        pltpu.VMEM((LANES, ATT_REP * blk), F32),
                        pltpu.VMEM((1, ATT_REP * blk), F32),
                        pltpu.VMEM((SEL_KEY_TILE, blk), F32),
                        pltpu.VMEM((SEL_KEY_TILE, ATT_REP * blk), F32),
                        pltpu.VMEM((SEL_KEY_TILE, ATT_REP * blk), F32)],
    )
    return pl.pallas_call(
        functools.partial(_nsa_sel_body, ntl=ntl),
        grid_spec=grid_spec,
        out_shape=jax.ShapeDtypeStruct((bsz, seq, 512), F32),
        compiler_params=_cparams("arbitrary", "arbitrary"),
        name=name,
    )(tiles, cnt, att, kaug, vt, mask, gate, _query_feature_rows())


def _hgrn_body(q_ref, f_ref, i_ref, g_ref, lbp_ref, ng_ref, o_ref, st_ref, b_ref, k_ref, *, layer):
    blk = HGRN_BLOCK
    ch = HGRN_CHUNK
    dk = HGRN_DIM

    @pl.when(pl.program_id(1) == 0)
    def _():
        st_ref[...] = jnp.zeros_like(st_ref)

    lbp = lbp_ref[...]
    e = jnp.exp(lbp - jnp.max(lbp, axis=0, keepdims=True))
    sm = e / jnp.sum(e, axis=0, keepdims=True)
    lb = jnp.zeros((1, lbp.shape[1]), F32)
    for d in range(1, layer + 1):
        lb = lb + sm[d:d + 1]
    z = f_ref[0]
    f = lb + (1.0 - lb) * _sigmoid(z)
    logf = jnp.log(jnp.maximum(f, HGRN_MIN_F))
    k_ref[...] = (1.0 - lb) * _sigmoid(-z)
    tr = lax.broadcasted_iota(jnp.int32, (blk, blk), 0)
    tc = lax.broadcasted_iota(jnp.int32, (blk, blk), 1)
    tri = jnp.where((tr // ch == tc // ch) & (tc <= tr), 1.0, 0.0)
    b_ref[...] = _prefix_dot(tri, logf)
    cr = lax.broadcasted_iota(jnp.int32, (ch, ch), 0)
    cc = lax.broadcasted_iota(jnp.int32, (ch, ch), 1)
    causal = cc <= cr
    ng = ng_ref[...]

    for c in range(blk // ch):
        r0 = c * ch
        bc = b_ref[pl.ds(r0, ch), :]
        qc = q_ref[0, pl.ds(r0, ch), :]
        kc = k_ref[pl.ds(r0, ch), :]
        vc = i_ref[0, pl.ds(r0, ch), :]
        gc = g_ref[0, pl.ds(r0, ch), :]
        b_mid = bc[ch // 2:ch // 2 + 1]
        b_last = bc[ch - 1:ch]
        qa = qc * jnp.exp(bc - b_mid)
        ka = kc * jnp.exp(b_mid - bc)
        qe = qc * jnp.exp(bc)
        kl = kc * jnp.exp(b_last - bc)
        dec = jnp.exp(b_last)
        for h in range(HGRN_HEADS):
            sl = slice(h * dk, (h + 1) * dk)
            a = jnp.where(causal, _dot_nt(qa[:, sl], ka[:, sl]), 0.0)
            st = st_ref[h]
            o = _dot(a, vc[:, sl]) + _dot_nt(qe[:, sl], st)
            st_ref[h] = st * dec[:, sl] + _dot_tn(vc[:, sl], kl[:, sl])
            o = _rms(o, ng[:, sl]) * _silu(gc[:, sl])
            o_ref[0, pl.ds(r0, ch), sl] = o


def hgrn2(hproj, first_blk, lower_bounds, norm_g, layer, name):
    bsz, seq, _ = hproj.shape
    blk = HGRN_BLOCK
    wid = HGRN_HEADS * HGRN_DIM
    depth = lower_bounds.shape[0]

    def col(j):
        return pl.BlockSpec((1, blk, wid), lambda b, n: (b, n, first_blk + j))

    return pl.pallas_call(
        functools.partial(_hgrn_body, layer=layer),
        grid=(bsz, seq // blk),
        in_specs=[col(0), col(1), col(2), col(3),
                  pl.BlockSpec((depth, wid), lambda b, n: (0, 0)),
                  pl.BlockSpec((1, wid), lambda b, n: (0, 0))],
        out_specs=pl.BlockSpec((1, blk, wid), lambda b, n: (b, n, 0)),
        out_shape=jax.ShapeDtypeStruct((bsz, seq, wid), F32),
        scratch_shapes=[pltpu.VMEM((HGRN_HEADS, HGRN_DIM, HGRN_DIM), F32),
                        pltpu.VMEM((blk, wid), F32),
                        pltpu.VMEM((blk, wid), F32)],
        compiler_params=_cparams("arbitrary", "arbitrary"),
        name=name,
    )(hproj, hproj, hproj, hproj, lower_bounds, norm_g.reshape(1, wid))


def _ssd_body(xbc_ref, dt_ref, z_ref, cw_ref, cb_ref, dtb_ref, alog_ref, alogw_ref, dskw_ref, ng_ref,
              o_ref, xp_ref, act_ref, st_ref, y_ref):
    blk = SSM_BLOCK
    ch = SSM_CHUNK
    hp = SSM_HEAD_DIM
    ns = SSM_STATE
    rep = SSM_HEADS // SSM_GROUPS
    pad = 8

    @pl.when(pl.program_id(1) == 0)
    def _():
        xp_ref[0:pad, :] = jnp.zeros((pad, SSM_CONV_DIM), F32)
        st_ref[...] = jnp.zeros_like(st_ref)

    xin = xbc_ref[0]
    xp_ref[pad:pad + blk, :] = xin
    cw = cw_ref[...]
    conv = cb_ref[...] + cw[SSM_CONV - 1:SSM_CONV] * xin
    for j in range(SSM_CONV - 1):
        shift = SSM_CONV - 1 - j
        conv = conv + cw[j:j + 1] * xp_ref[pl.ds(pad - shift, blk), :]
    xp_ref[0:pad, :] = xin[blk - pad:blk]
    act_ref[...] = _silu(conv)

    def softplus(v):
        return jnp.maximum(v, 0.0) + jnp.log(1.0 + jnp.exp(-jnp.abs(v)))

    dt = softplus(dt_ref[0] + dtb_ref[...])
    a_all = dt * (-jnp.exp(alog_ref[...]))
    er = lax.broadcasted_iota(jnp.int32, (LANES, SSM_INNER), 0)
    ec = lax.broadcasted_iota(jnp.int32, (LANES, SSM_INNER), 1)
    spread = jnp.where(ec // hp == er, 1.0, 0.0).astype(BF16)
    dtw = sum(jnp.dot(p, spread, preferred_element_type=F32) for p in _split3(dt))
    aw = dtw * (-jnp.exp(alogw_ref[...]))
    cr = lax.broadcasted_iota(jnp.int32, (ch, ch), 0)
    cc = lax.broadcasted_iota(jnp.int32, (ch, ch), 1)
    causal = cc <= cr
    tri = jnp.where(causal, 1.0, 0.0)
    tri_t = jnp.where(cr <= cc, 1.0, 0.0)

    for c in range(blk // ch):
        r0 = c * ch
        acs_t = _prefix_dot_tn(a_all[r0:r0 + ch], tri_t)
        acs = _prefix_dot(tri, aw[r0:r0 + ch])
        last = acs[ch - 1:ch]
        dtc = dtw[r0:r0 + ch]
        xc = act_ref[r0:r0 + ch, 0:SSM_INNER]
        xdt = dtc * xc
        grow = jnp.exp(acs)
        wdec = jnp.exp(last - acs) * dtc
        keep = jnp.exp(last)
        skip = dskw_ref[...] * xc
        for g in range(SSM_GROUPS):
            bm = act_ref[r0:r0 + ch, SSM_INNER + g * ns:SSM_INNER + (g + 1) * ns]
            cm = act_ref[r0:r0 + ch, SSM_INNER + SSM_GROUPS * ns + g * ns:
                         SSM_INNER + SSM_GROUPS * ns + (g + 1) * ns]
            cb = _dot_nt(cm, bm)
            for r in range(rep):
                h = g * rep + r
                hl = slice(h * hp, (h + 1) * hp)
                lmat = jnp.where(causal, jnp.exp(acs[:, hl] - acs_t[h:h + 1, :]), 0.0)
                st = st_ref[h]
                y = _dot(cb * lmat, xdt[:, hl]) + grow[:, hl] * _dot(cm, st) + skip[:, hl]
                st_ref[h] = keep[:, hl] * st + _dot_tn(bm * wdec[:, hl], xc[:, hl])
                y_ref[r0:r0 + ch, hl] = y
    yz = y_ref[...] * _silu(z_ref[0])
    o_ref[0] = _rms(yz, ng_ref[...])


def mamba2(sproj, conv_w, conv_b, dt_bias, a_log, d_skip, norm_g, name):
    bsz, seq, _ = sproj.shape
    blk = SSM_BLOCK
    padh = LANES - SSM_HEADS
    dtb = jnp.pad(dt_bias, (0, padh)).reshape(1, LANES)
    alog = jnp.pad(a_log, (0, padh)).reshape(1, LANES)

    def wide(v):
        return jnp.repeat(v, SSM_HEAD_DIM).reshape(1, SSM_INNER)

    def row(w):
        return pl.BlockSpec((1, w), lambda b, n: (0, 0))

    return pl.pallas_call(
        _ssd_body,
        grid=(bsz, seq // blk),
        in_specs=[pl.BlockSpec((1, blk, SSM_CONV_DIM), lambda b, n: (b, n, 0)),
                  pl.BlockSpec((1, blk, LANES), lambda b, n: (b, n, 6)),
                  pl.BlockSpec((1, blk, SSM_INNER), lambda b, n: (b, n, 2)),
                  pl.BlockSpec((SSM_CONV, SSM_CONV_DIM), lambda b, n: (0, 0)),
                  row(SSM_CONV_DIM), row(LANES), row(LANES), row(SSM_INNER), row(SSM_INNER),
                  row(SSM_INNER)],
        out_specs=pl.BlockSpec((1, blk, SSM_INNER), lambda b, n: (b, n, 0)),
        out_shape=jax.ShapeDtypeStruct((bsz, seq, SSM_INNER), F32),
        scratch_shapes=[pltpu.VMEM((blk + 8, SSM_CONV_DIM), F32),
                        pltpu.VMEM((blk, SSM_CONV_DIM), F32),
                        pltpu.VMEM((SSM_HEADS, SSM_STATE, SSM_HEAD_DIM), F32),
                        pltpu.VMEM((blk, SSM_INNER), F32)],
        compiler_params=_cparams("arbitrary", "arbitrary"),
        name=name,
    )(sproj, sproj, sproj, conv_w, conv_b.reshape(1, -1), dtb, alog, wide(a_log), wide(d_skip),
      norm_g.reshape(1, -1))


def _merge_body(ya_ref, yb_ref, yc1_ref, yc2_ref, yc3_ref, yd_ref, x_ref, gm_ref, wmg_ref, wbr_ref,
                wout_ref, gf_ref, wr_ref, o_ref, gid_ref):
    ys = (ya_ref[...], yb_ref[...], yc1_ref[...] + yc2_ref[...] + yc3_ref[...], yd_ref[...])
    x = x_ref[...]
    hn = _rms(x, gm_ref[...]).astype(BF16)
    u = None
    for nbr in range(N_BRANCH):
        gate = jnp.dot(hn, wmg_ref[:, nbr * D_MODEL:(nbr + 1) * D_MODEL], preferred_element_type=F32)
        t = _sigmoid(gate) * _dot(ys[nbr], wbr_ref[nbr])
        u = t if u is None else u + t
    xn = x + _dot(u, wout_ref[...])
    o_ref[...] = xn
    gid_ref[...] = _top_group(xn, gf_ref, wr_ref)


def merge(ya, yb, yc1, yc2, yc3, yd, x2d, g_mix, w_mgate, w_branch, w_out, g_ffn, wr, tm, name):
    m = x2d.shape[0]

    def rows(w):
        return pl.BlockSpec((tm, w), lambda i: (i, 0))

    return pl.pallas_call(
        _merge_body,
        grid=(m // tm,),
        in_specs=[rows(512)] * 6 + [rows(D_MODEL),
                                    pl.BlockSpec((1, D_MODEL), lambda i: (0, 0)),
                                    pl.BlockSpec((D_MODEL, N_BRANCH * D_MODEL), lambda i: (0, 0)),
                                    pl.BlockSpec((N_BRANCH, BRANCH_WIDTH, D_MODEL), lambda i: (0, 0, 0)),
                                    pl.BlockSpec((D_MODEL, D_MODEL), lambda i: (0, 0)),
                                    pl.BlockSpec((1, D_MODEL), lambda i: (0, 0)),
                                    pl.BlockSpec((2, D_MODEL, LANES), lambda i: (0, 0, 0))],
        out_specs=[rows(D_MODEL), rows(LANES)],
        out_shape=[jax.ShapeDtypeStruct((m, D_MODEL), F32), jax.ShapeDtypeStruct((m, LANES), jnp.int32)],
        compiler_params=_cparams("arbitrary"),
        name=name,
    )(ya, yb, yc1, yc2, yc3, yd, x2d, g_mix.reshape(1, -1), w_mgate, w_branch.astype(BF16),
      w_out.astype(BF16), g_ffn.reshape(1, -1), wr)


def _router_logits(hn, wr_ref):
    h_hi = hn.astype(BF16)
    h_lo = (hn - h_hi.astype(F32)).astype(BF16)
    return (jnp.dot(h_hi, wr_ref[0], preferred_element_type=F32)
            + jnp.dot(h_hi, wr_ref[1], preferred_element_type=F32)
            + jnp.dot(h_lo, wr_ref[0], preferred_element_type=F32))


def _group_lanes(lane):
    return (lane >= MOE_EXPERTS) & (lane < MOE_EXPERTS + MOE_GROUPS)


def _top_group(x, g_ref, wr_ref):
    tm = x.shape[0]
    lane = lax.broadcasted_iota(jnp.int32, (tm, LANES), 1)
    logits = _router_logits(_rms(x, g_ref[...]), wr_ref)
    lg = jnp.where(_group_lanes(lane), logits, -jnp.inf)
    mg = jnp.max(lg, axis=-1, keepdims=True)
    gi = jnp.min(jnp.where(lg == mg, lane.astype(F32), 1e9), axis=-1, keepdims=True) - MOE_EXPERTS
    return jnp.broadcast_to(gi, (tm, LANES)).astype(jnp.int32)


def _gather_body(idx_ref, nlive_ref, src_ref, o_ref, buf, sem):
    rows = o_ref.shape[0]
    i = pl.program_id(0)
    nlive = nlive_ref[0]

    def live(tile):
        return tile * rows < nlive

    def issue_tile(tile):
        base = tile * rows
        slot = tile % 2

        def issue(r, carry):
            pltpu.make_async_copy(src_ref.at[pl.ds(idx_ref[base + r], 1)], buf.at[slot, pl.ds(r, 1)],
                                  sem.at[slot]).start()
            return carry

        lax.fori_loop(0, rows, issue, 0, unroll=8)

    @pl.when((i == 0) & live(0))
    def _():
        issue_tile(0)

    @pl.when((i + 1 < pl.num_programs(0)) & live(i + 1))
    def _():
        issue_tile(i + 1)

    @pl.when(live(i))
    def _():
        slot = i % 2
        pltpu.make_async_copy(src_ref.at[pl.ds(0, rows)], buf.at[slot], sem.at[slot]).wait()
        o_ref[...] = buf[slot]

    @pl.when(jnp.logical_not(live(i)))
    def _():
        o_ref[...] = jnp.zeros_like(o_ref)


def gather_rows(src, idx, nlive, rows, name):
    n = idx.shape[0]
    d = src.shape[1]
    grid_spec = pltpu.PrefetchScalarGridSpec(
        num_scalar_prefetch=2,
        grid=(n // rows,),
        in_specs=[pl.BlockSpec(memory_space=pl.ANY)],
        out_specs=pl.BlockSpec((rows, d), lambda i, *_: (i, 0)),
        scratch_shapes=[pltpu.VMEM((2, rows, d), src.dtype), pltpu.SemaphoreType.DMA((2,))],
    )
    return pl.pallas_call(
        _gather_body,
        grid_spec=grid_spec,
        out_shape=jax.ShapeDtypeStruct((n, d), src.dtype),
        compiler_params=_cparams("arbitrary"),
        name=name,
    )(idx, jnp.reshape(nlive, (1,)).astype(jnp.int32), src)


def _moe_body(tg_ref, x_ref, g_ref, wr_ref, wg_ref, wu_ref, wd_ref, o_ref, hn_ref, comb_ref):
    e = pl.program_id(1)
    grp = tg_ref[pl.program_id(0)]
    tm = x_ref.shape[0]
    lane = lax.broadcasted_iota(jnp.int32, (tm, LANES), 1)
    lanef = lane.astype(F32)
    live = grp < MOE_GROUPS

    @pl.when(e == 0)
    def _():
        o_ref[...] = x_ref[...]

    @pl.when((e == 0) & live)
    def _():
        hn = _rms(x_ref[...], g_ref[...])
        hn_ref[...] = hn.astype(BF16)
        logits = _router_logits(hn, wr_ref)
        is_grp = _group_lanes(lane)
        lg = jnp.where(is_grp, logits, -jnp.inf)
        mg = jnp.max(lg, axis=-1, keepdims=True)
        sg = jnp.sum(jnp.where(is_grp, jnp.exp(lg - mg), 0.0), axis=-1, keepdims=True)
        lt = jnp.sum(jnp.where(lane == MOE_EXPERTS + grp, logits, 0.0), axis=-1, keepdims=True)
        g_w = jnp.exp(lt - mg) / sg
        in_grp = (lane < MOE_EXPERTS) & (lane // MOE_EPG == grp)
        le = jnp.where(in_grp, logits, -jnp.inf)
        m1 = jnp.max(le, axis=-1, keepdims=True)
        i1 = jnp.min(jnp.where(le == m1, lanef, 1e9), axis=-1, keepdims=True)
        le2 = jnp.where(lanef == i1, -jnp.inf, le)
        m2 = jnp.max(le2, axis=-1, keepdims=True)
        i2 = jnp.min(jnp.where(le2 == m2, lanef, 1e9), axis=-1, keepdims=True)
        e2 = jnp.exp(m2 - m1)
        den = 1.0 + e2
        comb_ref[...] = (jnp.where(lanef == i1, g_w / den, 0.0)
                         + jnp.where(lanef == i2, g_w * e2 / den, 0.0))

    @pl.when(live)
    def _():
        hn = hn_ref[...]
        gate = jnp.dot(hn, wg_ref[0].astype(BF16), preferred_element_type=F32)
        up = jnp.dot(hn, wu_ref[0].astype(BF16), preferred_element_type=F32)
        ce = jnp.sum(jnp.where(lane == grp * MOE_EPG + e, comb_ref[...], 0.0), axis=-1, keepdims=True)
        o_ref[...] += _dot(_silu(gate) * up * ce, wd_ref[0])


def _moe_plan(gid, tm):
    t = gid.shape[0]
    onehot = (gid[:, None] == jnp.arange(MOE_GROUPS, dtype=jnp.int32)).astype(jnp.int32)
    csum = jnp.cumsum(onehot, axis=0)
    counts = csum[-1]
    rank = (csum * onehot).sum(axis=1) - 1
    padded = (counts + tm - 1) // tm * tm
    pend = jnp.cumsum(padded)
    dest = (pend - padded)[gid] + rank
    r_pad = t + MOE_GROUPS * tm
    row_token = (jnp.arange(r_pad, dtype=jnp.int32) % t).at[dest].set(jnp.arange(t, dtype=jnp.int32))
    tile_start = jnp.arange(r_pad // tm, dtype=jnp.int32) * tm
    tile_group = (tile_start[:, None] >= pend[None, :]).sum(axis=1)
    return row_token, dest.astype(jnp.int32), tile_group.astype(jnp.int32), pend[-1]


def router_weights(w_grp, w_exp):
    wr32 = jnp.concatenate([w_exp, w_grp,
                            jnp.zeros((D_MODEL, LANES - MOE_EXPERTS - MOE_GROUPS), F32)], axis=1)
    wr_hi = wr32.astype(BF16)
    return jnp.stack([wr_hi, (wr32 - wr_hi.astype(F32)).astype(BF16)])


def moe(x2d, gid, g_ffn, wr, w_gate, w_up, w_down, layer, tm, name):
    m = x2d.shape[0]
    gf = g_ffn.reshape(1, -1)
    row_token, dest, tile_group, n_sorted = _moe_plan(gid, tm)
    rows_dma = min(MOE_GATHER_ROWS, tm)
    xs = gather_rows(x2d, row_token, n_sorted, rows_dma, name + "_gather")

    def expert(i, e, tg):
        return (layer * MOE_EXPERTS + jnp.minimum(tg[i], MOE_GROUPS - 1) * MOE_EPG + e, 0, 0)

    grid_spec = pltpu.PrefetchScalarGridSpec(
        num_scalar_prefetch=1,
        grid=(xs.shape[0] // tm, MOE_EPG),
        in_specs=[pl.BlockSpec((tm, D_MODEL), lambda i, e, tg: (i, 0)),
                  pl.BlockSpec((1, D_MODEL), lambda i, e, tg: (0, 0)),
                  pl.BlockSpec((2, D_MODEL, LANES), lambda i, e, tg: (0, 0, 0)),
                  pl.BlockSpec((1, D_MODEL, MOE_FF), expert),
                  pl.BlockSpec((1, D_MODEL, MOE_FF), expert),
                  pl.BlockSpec((1, MOE_FF, D_MODEL), expert)],
        out_specs=pl.BlockSpec((tm, D_MODEL), lambda i, e, tg: (i, 0)),
        scratch_shapes=[pltpu.VMEM((tm, D_MODEL), BF16), pltpu.VMEM((tm, LANES), F32)],
    )
    ys = pl.pallas_call(
        _moe_body,
        grid_spec=grid_spec,
        out_shape=jax.ShapeDtypeStruct(xs.shape, F32),
        compiler_params=_cparams("arbitrary", "arbitrary"),
        name=name,
    )(tile_group, xs, gf, wr, *[w.reshape((-1,) + w.shape[2:]) for w in (w_gate, w_up, w_down)])
    return gather_rows(ys, dest, jnp.int32(m), rows_dma, name + "_scatter")


def _ple_body(x_ref, p_ref, g_ref, wg_ref, wp_ref, gf_ref, o_ref, *, final):
    x = x_ref[...]
    gate = _sigmoid(_dot(_rms(x, g_ref[...]), wg_ref[...]))
    xn = x + _dot(p_ref[...], wp_ref[...]) * gate
    if final:
        xn = _rms(xn, gf_ref[...])
    o_ref[...] = xn


def ple(x2d, p2d, g_ple, w_gate, w_proj, g_final, final, tm, name):
    m = x2d.shape[0]
    return pl.pallas_call(
        functools.partial(_ple_body, final=final),
        grid=(m // tm,),
        in_specs=[pl.BlockSpec((tm, D_MODEL), lambda i: (i, 0)),
                  pl.BlockSpec((tm, PLE_DIM), lambda i: (i, 0)),
                  pl.BlockSpec((1, D_MODEL), lambda i: (0, 0)),
                  pl.BlockSpec((D_MODEL, D_MODEL), lambda i: (0, 0)),
                  pl.BlockSpec((PLE_DIM, D_MODEL), lambda i: (0, 0)),
                  pl.BlockSpec((1, D_MODEL), lambda i: (0, 0))],
        out_specs=pl.BlockSpec((tm, D_MODEL), lambda i: (i, 0)),
        out_shape=jax.ShapeDtypeStruct((m, D_MODEL), F32),
        compiler_params=_cparams("arbitrary"),
        name=name,
    )(x2d, p2d, g_ple.reshape(1, -1), w_gate.astype(BF16), w_proj.astype(BF16),
      g_final.reshape(1, -1))


def _cols(w, *names):
    return [w[:, _OFF[n][0]:_OFF[n][0] + _OFF[n][1]] for n in names]


def _padcols(w, width):
    return jnp.pad(w, ((0, 0), (0, width - w.shape[1])))


def _pad_q_heads(wq):
    hd = ATT_HEAD_DIM
    zero = jnp.zeros((wq.shape[0], hd), wq.dtype)
    cols = []
    for h in range(ATT_HEADS):
        blk = wq[:, h * hd:(h + 1) * hd] * (hd ** -0.5)
        cols += [blk, zero] if h // ATT_REP == 0 else [zero, blk]
    return jnp.concatenate(cols, axis=1)


def _split_w_in(w):
    swa_q, nsa_q = _cols(w, 'swa_q', 'nsa_q')
    w_att = jnp.concatenate([_pad_q_heads(swa_q), _pad_q_heads(nsa_q)] + _cols(w, *KV_BLK), axis=1)
    (xbc, dt, ngate, z) = _cols(w, 'ssm_xbc', 'ssm_dt', 'nsa_gate', 'ssm_z')
    w_rec = jnp.concatenate([xbc, _padcols(dt, LANES), _padcols(ngate, LANES), z]
                            + _cols(w, 'hgrn_q', 'hgrn_f', 'hgrn_i', 'hgrn_g'), axis=1)
    (w_mg,) = _cols(w, 'merge_gate')
    return [a.astype(BF16) for a in (w_att, w_rec, w_mg)]


def _mixers(i, att, sproj, attn_sinks, hgrn_lower_bounds, hgrn_norm_g, nsa_pos_k, nsa_pos_v,
            nsa_cmp_w1_k, nsa_cmp_w2_k, nsa_cmp_w1_v, nsa_cmp_w2_v, ssm_conv_w, ssm_conv_b,
            ssm_dt_bias, ssm_A_log, ssm_D, ssm_norm_g):
    bsz, seq, _ = att.shape
    kv = KV_BLK
    gate_blk = 7
    y_a = banded_attention(att, 0, kv['swa_k'], kv['swa_v'], window=SWA_WINDOW, sinks=attn_sinks[i],
                           name=f"swa{i}")
    y_b = hgrn2(sproj, 3, hgrn_lower_bounds, hgrn_norm_g[i], i, name=f"hgrn{i}")
    ncp = seq // NSA_CMP_STRIDE

    def cmp_in(name):
        c0 = kv[name] * LANES
        return att[:, :, c0:c0 + LANES].reshape(bsz, ncp, NSA_CMP_STRIDE * LANES)

    kc = nsa_compress(cmp_in('nsa_k_cmp'), nsa_pos_k[i], nsa_cmp_w1_k[i], nsa_cmp_w2_k[i], name=f"cmpk{i}")
    vc = nsa_compress(cmp_in('nsa_v_cmp'), nsa_pos_v[i], nsa_cmp_w1_v[i], nsa_cmp_w2_v[i], name=f"cmpv{i}")
    y_c1, mask, act = nsa_cmp_select(att, 1, kc, vc, sproj, gate_blk, name=f"nsacmp{i}")
    y_c2 = nsa_selected(att, 1, kv['nsa_k_slc'], kv['nsa_v_slc'], mask, act, sproj, gate_blk,
                        name=f"nsasel{i}")
    y_c3 = banded_attention(att, 1, kv['nsa_k_win'], kv['nsa_v_win'], window=NSA_WINDOW, gate=sproj,
                            gate_blk=gate_blk, gate_col=2, name=f"nsawin{i}")
    y_d = mamba2(sproj, ssm_conv_w[i], ssm_conv_b[i], ssm_dt_bias[i], ssm_A_log[i], ssm_D[i],
                 ssm_norm_g[i], name=f"ssd{i}")
    return y_a, y_b, y_c1, y_c2, y_c3, y_d


def kernel(x, p, w_in, g_mix, attn_sinks, hgrn_lower_bounds, hgrn_norm_g, nsa_pos_k, nsa_pos_v,
           nsa_cmp_w1_k, nsa_cmp_w2_k, nsa_cmp_w1_v, nsa_cmp_w2_v, ssm_conv_w, ssm_conv_b,
           ssm_dt_bias, ssm_A_log, ssm_D, ssm_norm_g, w_branch, w_out, g_ffn, w_router_grp,
           w_router_exp, w_exp_gate, w_exp_up, w_exp_down, g_ple, w_ple_gate, w_ple_proj, g_final):
    bsz, seq, d = x.shape
    depth = w_in.shape[0]
    t = bsz * seq
    x2 = x.reshape(t, d)
    tm_proj = min(2048, t)
    tm_merge = min(256, t)
    tm_ple = min(512, t)
    tm_moe = min(1024, t)
    for i in range(depth):
        w_att, w_rec, w_mg = _split_w_in(w_in[i])
        att = norm_mm(x2, g_mix[i], w_att, BF16, tm_proj, 1024, f"proj_att{i}").reshape(bsz, seq, -1)
        sproj = norm_mm(x2, g_mix[i], w_rec, F32, tm_proj, 512, f"proj_rec{i}").reshape(bsz, seq, -1)
        ys = _mixers(i, att, sproj, attn_sinks, hgrn_lower_bounds, hgrn_norm_g, nsa_pos_k,
                     nsa_pos_v, nsa_cmp_w1_k, nsa_cmp_w2_k, nsa_cmp_w1_v, nsa_cmp_w2_v, ssm_conv_w,
                     ssm_conv_b, ssm_dt_bias, ssm_A_log, ssm_D, ssm_norm_g)
        ys = [y.reshape(t, -1) for y in ys]
        wr = router_weights(w_router_grp[i], w_router_exp[i])
        x2, gid = merge(*ys, x2, g_mix[i], w_mg, w_branch[i], w_out[i], g_ffn[i], wr, tm_merge, f"merge{i}")
        x2 = moe(x2, gid[:, 0], g_ffn[i], wr, w_exp_gate, w_exp_up, w_exp_down, i, tm_moe, f"moe{i}")
        x2 = ple(x2, p[i].reshape(t, -1), g_ple[i], w_ple_gate[i], w_ple_proj[i], g_final,
                 i == depth - 1, tm_ple, f"ple{i}")
    return x2.reshape(bsz, seq, d)
```

```python
import functools

import numpy as np
import jax
import jax.numpy as jnp
from jax import lax
from jax.experimental import pallas as pl
from jax.experimental.pallas import tpu as pltpu

F32 = jnp.float32
BF16 = jnp.bfloat16

D_MODEL = 1024
PLE_DIM = 256
NORM_EPS = 1e-6
NEG_INF = -1e30
REMOVED = -3e38
N_BRANCH = 4
BRANCH_WIDTH = 512
ATTN_BLOCK = 128

ATT_HEADS = 8
ATT_KV_HEADS = 2
ATT_HEAD_DIM = 64
ATT_REP = ATT_HEADS // ATT_KV_HEADS
SWA_WINDOW = 128
NSA_WINDOW = 512
NSA_CMP_LEN = 32
NSA_CMP_STRIDE = 16
NSA_CMP_HIDDEN = 256
NSA_SEL_LEN = 64
NSA_TOP_N = 16
NSA_FORCE_SCORE = 1e6
CMP_KEY_CHUNK = 128
SEL_KEY_TILE = 512
SEL_FEATS = SEL_KEY_TILE // NSA_SEL_LEN
MASK_BIG = 1e30

HGRN_HEADS = 4
HGRN_DIM = 128
HGRN_CHUNK = 32
HGRN_MIN_F = 1e-6
HGRN_BLOCK = 256

SSM_HEADS = 8
SSM_HEAD_DIM = 64
SSM_GROUPS = 2
SSM_STATE = 64
SSM_CONV = 4
SSM_CHUNK = 64
SSM_INNER = 512
SSM_CONV_DIM = 768
SSM_BLOCK = 512

MOE_GROUPS = 4
MOE_EPG = 8
MOE_EXPERTS = 32
MOE_FF = 256
MOE_GATHER_ROWS = 1024

LANES = 128
VMEM_LIMIT = 56 * 1024 * 1024

ALIBI_SLOPES = tuple(2.0 ** (-8.0 * (h + 1) / ATT_HEADS) for h in range(ATT_HEADS))

ATT_Q_WIDTH = ATT_HEADS * LANES
KV_BLK = {name: 2 * ATT_HEADS + j for j, name in enumerate(
    ('swa_k', 'swa_v', 'nsa_k_cmp', 'nsa_v_cmp', 'nsa_k_slc', 'nsa_v_slc', 'nsa_k_win', 'nsa_v_win'))}

_SPLITS = (
    ('swa_q', 512), ('swa_k', 128), ('swa_v', 128),
    ('hgrn_q', 512), ('hgrn_f', 512), ('hgrn_i', 512), ('hgrn_g', 512),
    ('nsa_q', 512), ('nsa_k_cmp', 128), ('nsa_v_cmp', 128), ('nsa_k_slc', 128),
    ('nsa_v_slc', 128), ('nsa_k_win', 128), ('nsa_v_win', 128), ('nsa_gate', 24),
    ('ssm_z', 512), ('ssm_xbc', 768), ('ssm_dt', 8), ('merge_gate', 4096),
)
_OFF = {}
_o = 0
for _n, _w in _SPLITS:
    _OFF[_n] = (_o, _w)
    _o += _w


def _cparams(*sem):
    return pltpu.CompilerParams(dimension_semantics=sem, vmem_limit_bytes=VMEM_LIMIT)


def _sigmoid(x):
    return 1.0 / (1.0 + jnp.exp(-x))


def _silu(x):
    return x * _sigmoid(x)


def _dot(a, b):
    return jnp.dot(a.astype(BF16), b.astype(BF16), preferred_element_type=F32)


def _dot_nt(a, b):
    return lax.dot_general(a.astype(BF16), b.astype(BF16), (((1,), (1,)), ((), ())),
                           preferred_element_type=F32)


def _dot_tn(a, b):
    return lax.dot_general(a.astype(BF16), b.astype(BF16), (((0,), (0,)), ((), ())),
                           preferred_element_type=F32)


def _split3(x):
    hi = x.astype(BF16)
    r1 = x - hi.astype(F32)
    mid = r1.astype(BF16)
    return hi, mid, (r1 - mid.astype(F32)).astype(BF16)


def _prefix_dot(tri, x):
    t = tri.astype(BF16)
    return sum(jnp.dot(t, p, preferred_element_type=F32) for p in _split3(x))


def _prefix_dot_tn(x, tri):
    t = tri.astype(BF16)
    return sum(lax.dot_general(p, t, (((0,), (0,)), ((), ())), preferred_element_type=F32)
               for p in _split3(x))


def _rms(x, g):
    ms = jnp.mean(x * x, axis=-1, keepdims=True)
    return x * lax.rsqrt(ms + NORM_EPS) * g


def _norm_mm_body(x_ref, g_ref, w_ref, o_ref, hn_ref):
    @pl.when(pl.program_id(1) == 0)
    def _():
        hn_ref[...] = _rms(x_ref[...], g_ref[...]).astype(BF16)

    o_ref[...] = jnp.dot(hn_ref[...], w_ref[...], preferred_element_type=F32).astype(o_ref.dtype)


def norm_mm(x2d, g, w, out_dtype, tm, tn, name):
    m, k = x2d.shape
    n = w.shape[1]
    return pl.pallas_call(
        _norm_mm_body,
        grid=(m // tm, n // tn),
        in_specs=[pl.BlockSpec((tm, k), lambda i, j: (i, 0)),
                  pl.BlockSpec((1, k), lambda i, j: (0, 0)),
                  pl.BlockSpec((k, tn), lambda i, j: (0, j))],
        out_specs=pl.BlockSpec((tm, tn), lambda i, j: (i, j)),
        out_shape=jax.ShapeDtypeStruct((m, n), out_dtype),
        scratch_shapes=[pltpu.VMEM((tm, k), BF16)],
        compiler_params=_cparams("arbitrary", "arbitrary"),
        name=name,
    )(x2d, g.reshape(1, k), w)


def _ones_row(g):
    return ATT_HEAD_DIM if g == 0 else 0


def _vt_body(v_ref, o_ref):
    vt = jnp.transpose(v_ref[0].astype(F32))
    rowid = lax.broadcasted_iota(jnp.int32, vt.shape, 0)
    for g in range(ATT_KV_HEADS):
        aug = jnp.where(rowid // ATT_HEAD_DIM == g, vt, jnp.where(rowid == _ones_row(g), 1.0, 0.0))
        o_ref[0, g] = aug.astype(BF16)


def v_transposed(arr, v_blk, name):
    bsz, seq, _ = arr.shape
    rows = min(2048, seq)
    return pl.pallas_call(
        _vt_body,
        grid=(bsz, seq // rows),
        in_specs=[pl.BlockSpec((1, rows, LANES), lambda b, n: (b, n, v_blk))],
        out_specs=pl.BlockSpec((1, ATT_KV_HEADS, LANES, rows), lambda b, n: (b, 0, 0, n)),
        out_shape=jax.ShapeDtypeStruct((bsz, ATT_KV_HEADS, LANES, seq), BF16),
        compiler_params=_cparams("arbitrary", "arbitrary"),
        name=name,
    )(arr)


def _q_stack(q_ref, g):
    return jnp.concatenate([q_ref[0, :, (g * ATT_REP + r) * LANES:(g * ATT_REP + r + 1) * LANES]
                            for r in range(ATT_REP)], axis=0)


def _head(x, r):
    return x[:, r * ATTN_BLOCK:(r + 1) * ATTN_BLOCK]


def _banded_body(*refs, window, nprev, use_sink, gate_col):
    if use_sink:
        q_ref, k_ref, vt_ref, sink_ref, o_ref, ot_ref = refs
    else:
        q_ref, k_ref, vt_ref, gate_ref, o_ref, ot_ref = refs
    blk = ATTN_BLOCK
    hd = ATT_HEAD_DIM
    n = pl.program_id(1)
    nk = (nprev + 1) * blk
    start = pl.multiple_of(jnp.maximum(n - nprev, 0) * blk, blk)
    k128 = k_ref[0, pl.ds(start, nk), :]
    krow = lax.broadcasted_iota(jnp.int32, (nk, blk), 0)
    qcol = lax.broadcasted_iota(jnp.int32, (nk, blk), 1)
    rel = (n * blk + qcol) - (start + krow)
    negrel = jnp.where((rel >= 0) & (rel < window), -rel.astype(F32), NEG_INF)
    if not use_sink:
        gate_t = jnp.transpose(_sigmoid(gate_ref[0]))
    st4s = [_dot_nt(k128, _q_stack(q_ref, g)) for g in range(ATT_KV_HEADS)]
    for g in range(ATT_KV_HEADS):
        one = _ones_row(g)
        st4 = st4s[g]
        ps, ms = [], []
        for r in range(ATT_REP):
            h = g * ATT_REP + r
            st = _head(st4, r) + ALIBI_SLOPES[h] * negrel
            m = jnp.max(st, axis=0, keepdims=True)
            if use_sink:
                m = jnp.maximum(m, sink_ref[h])
            ps.append(jnp.exp(st - m).astype(BF16))
            ms.append(m)
        acc4 = jnp.dot(vt_ref[0, g, :, pl.ds(start, nk)], jnp.concatenate(ps, axis=1),
                       preferred_element_type=F32)
        for r in range(ATT_REP):
            h = g * ATT_REP + r
            acc = _head(acc4, r)
            l = acc[one:one + 1]
            if use_sink:
                l = l + jnp.exp(sink_ref[h] - ms[r])
            ot = acc[g * hd:(g + 1) * hd] / l
            if not use_sink:
                c = 3 * h + gate_col
                ot = ot * gate_t[c:c + 1]
            ot_ref[h * hd:(h + 1) * hd, :] = ot
    o_ref[0] = jnp.transpose(ot_ref[...])


def banded_attention(att, q_blk, k_blk, v_blk, *, window, sinks=None, gate=None, gate_blk=None,
                     gate_col=0, name):
    bsz, seq, _ = att.shape
    blk = ATTN_BLOCK
    nprev = (window - 1 + blk - 1) // blk
    use_sink = sinks is not None
    vt = v_transposed(att, v_blk, name + "_vt")
    in_specs = [pl.BlockSpec((1, blk, ATT_Q_WIDTH), lambda b, n: (b, n, q_blk)),
                pl.BlockSpec((1, seq, LANES), lambda b, n: (b, 0, k_blk)),
                pl.BlockSpec((1, ATT_KV_HEADS, LANES, seq), lambda b, n: (b, 0, 0, 0))]
    if use_sink:
        in_specs.append(pl.BlockSpec(memory_space=pltpu.SMEM))
        extra = sinks
    else:
        in_specs.append(pl.BlockSpec((1, blk, LANES), lambda b, n: (b, n, gate_blk)))
        extra = gate
    return pl.pallas_call(
        functools.partial(_banded_body, window=window, nprev=nprev, use_sink=use_sink,
                          gate_col=gate_col),
        grid=(bsz, seq // blk),
        in_specs=in_specs,
        out_specs=pl.BlockSpec((1, blk, 512), lambda b, n: (b, n, 0)),
        out_shape=jax.ShapeDtypeStruct((bsz, seq, 512), F32),
        scratch_shapes=[pltpu.VMEM((512, blk), F32)],
        compiler_params=_cparams("arbitrary", "arbitrary"),
        name=name,
    )(att, att, vt, extra)


def _compress_body(x_ref, w1a_ref, w1b_ref, w1_ref, pos_ref, w2_ref, o_ref):
    x = x_ref[0]
    p = jnp.dot(x, w1a_ref[...], preferred_element_type=F32)
    q = jnp.dot(x, w1b_ref[...], preferred_element_type=F32)
    ncp = x.shape[0]
    q = pltpu.roll(q, shift=ncp - 1, axis=0)
    posb = jnp.broadcast_to(pos_ref[...], (8, pos_ref.shape[1]))
    bias = _dot(posb, w1_ref[...])[0:1]
    hid = NSA_CMP_HIDDEN
    outs = []
    for g in range(ATT_KV_HEADS):
        pre = p[:, g * hid:(g + 1) * hid] + q[:, g * hid:(g + 1) * hid] + bias
        outs.append(_dot(jax.nn.gelu(pre), w2_ref[...]))
    o_ref[0] = jnp.concatenate(outs, axis=1).astype(o_ref.dtype)


def _expand_w1(w1, half):
    hd, hid, ng = ATT_HEAD_DIM, NSA_CMP_HIDDEN, ATT_KV_HEADS
    w = w1.reshape(NSA_CMP_LEN, hd, hid)[half * 16:(half + 1) * 16]
    eye = jnp.eye(ng, dtype=w1.dtype)
    out = jnp.einsum('ldj,gh->lgdhj', w, eye)
    return out.reshape(16 * ng * hd, ng * hid)


def nsa_compress(x16, pos, w1, w2, name):
    bsz, ncp, wid = x16.shape
    hid = NSA_CMP_HIDDEN
    w1a = _expand_w1(w1, 0).astype(BF16)
    w1b = _expand_w1(w1, 1).astype(BF16)
    return pl.pallas_call(
        _compress_body,
        grid=(bsz,),
        in_specs=[pl.BlockSpec((1, ncp, wid), lambda b: (b, 0, 0)),
                  pl.BlockSpec((wid, 2 * hid), lambda b: (0, 0)),
                  pl.BlockSpec((wid, 2 * hid), lambda b: (0, 0)),
                  pl.BlockSpec((NSA_CMP_LEN * ATT_HEAD_DIM, hid), lambda b: (0, 0)),
                  pl.BlockSpec((1, NSA_CMP_LEN * ATT_HEAD_DIM), lambda b: (0, 0)),
                  pl.BlockSpec((hid, ATT_HEAD_DIM), lambda b: (0, 0))],
        out_specs=pl.BlockSpec((1, ncp, LANES), lambda b: (b, 0, 0)),
        out_shape=jax.ShapeDtypeStruct((bsz, ncp, LANES), BF16),
        compiler_params=_cparams("arbitrary"),
        name=name,
    )(x16, w1a, w1b, w1.astype(BF16), pos.reshape(1, -1), w2.astype(BF16))


def _nsa_cmp_body(q_ref, kc_ref, vct_ref, ovt_ref, gate_ref, o_ref, m_ref, act_ref, ot_ref, imp_ref, *,
                  top_n):
    blk = ATTN_BLOCK
    hd = ATT_HEAD_DIM
    n = pl.program_id(1)
    kc = kc_ref[0]
    ncp = kc.shape[0]
    ns = ovt_ref.shape[0]
    crow = lax.broadcasted_iota(jnp.int32, (ncp, blk), 0)
    qcol = lax.broadcasted_iota(jnp.int32, (ncp, blk), 1)
    dist = (n * blk + qcol) - (crow * NSA_CMP_STRIDE + (NSA_CMP_LEN - 1))
    valid = dist >= 0
    negd = jnp.where(valid, -dist.astype(F32), NEG_INF)
    jrow = lax.broadcasted_iota(jnp.int32, (ns, blk), 0)
    qpos = n * blk + lax.broadcasted_iota(jnp.int32, (ns, blk), 1)
    cur = qpos // NSA_SEL_LEN
    forced = (jrow == 0) | (jrow == cur) | (jrow == cur - 1)
    causal_blk = jrow * NSA_SEL_LEN <= qpos
    jrowf = jrow.astype(F32)
    gate_t = jnp.transpose(_sigmoid(gate_ref[0]))
    ones8 = jnp.ones((8, blk), BF16)

    def attend(rows):
        st4s = [_dot_nt(kc[:rows], _q_stack(q_ref, g)) for g in range(ATT_KV_HEADS)]
        for g in range(ATT_KV_HEADS):
            one = _ones_row(g)
            es = []
            for r in range(ATT_REP):
                st = _head(st4s[g], r) + ALIBI_SLOPES[g * ATT_REP + r] * negd[:rows]
                m = jnp.max(st, axis=0, keepdims=True)
                es.append(jnp.where(valid[:rows], jnp.exp(st - m), 0.0))
            acc4 = jnp.dot(vct_ref[0, g, :, :rows], jnp.concatenate([e.astype(BF16) for e in es], axis=1),
                           preferred_element_type=F32)
            psum = jnp.zeros((rows, blk), F32)
            for r in range(ATT_REP):
                h = g * ATT_REP + r
                acc = _head(acc4, r)
                l = acc[one:one + 1]
                inv = jnp.where(l > 0.0, 1.0 / jnp.where(l > 0.0, l, 1.0), 0.0)
                ot_ref[h * hd:(h + 1) * hd, :] = acc[g * hd:(g + 1) * hd] * inv * gate_t[3 * h:3 * h + 1]
                psum = psum + es[r] * inv
            imp_ref[g] = _dot(ovt_ref[:, :rows], psum)

    chunk = min(CMP_KEY_CHUNK, ncp)
    nchunks = ncp // chunk
    need = jnp.minimum((n * (blk // NSA_CMP_STRIDE) + blk // NSA_CMP_STRIDE - 1 + chunk - 1) // chunk,
                       nchunks)
    for c in range(1, nchunks + 1):
        pl.when(need == c)(functools.partial(attend, c * chunk))

    for g in range(ATT_KV_HEADS):
        score = jnp.where(forced, NSA_FORCE_SCORE, jnp.where(causal_blk, imp_ref[g], NEG_INF))
        sel = jnp.zeros((ns, blk), F32)
        for _ in range(top_n):
            mx = jnp.max(score, axis=0, keepdims=True)
            idx = jnp.min(jnp.where(score == mx, jrowf, float(ns)), axis=0, keepdims=True)
            hit = jrowf == idx
            sel = jnp.where(hit, 1.0, sel)
            score = jnp.where(hit, REMOVED, score)
        m_ref[0, g] = sel
        act_ref[0, 0, g * 8:(g + 1) * 8, :] = _dot_nt(ones8, sel)
    o_ref[0] = jnp.transpose(ot_ref[...])


def _overlap_matrix_t(seq):
    n_cmp = seq // NSA_CMP_STRIDE
    n_sel = seq // NSA_SEL_LEN
    cs = np.arange(n_cmp)[None, :] * NSA_CMP_STRIDE
    ss = np.arange(n_sel)[:, None] * NSA_SEL_LEN
    ov = np.clip(np.minimum(cs + NSA_CMP_LEN, ss + NSA_SEL_LEN) - np.maximum(cs, ss), 0, None)
    return jnp.asarray(ov / NSA_CMP_LEN, dtype=BF16)


def nsa_cmp_select(att, q_blk, kc, vc, gate, gate_blk, name):
    bsz, seq, _ = att.shape
    blk = ATTN_BLOCK
    nb = seq // blk
    ncp = seq // NSA_CMP_STRIDE
    ns = seq // NSA_SEL_LEN
    top_n = min(NSA_TOP_N, ns)
    vct = v_transposed(vc, 0, name + "_vt")
    return pl.pallas_call(
        functools.partial(_nsa_cmp_body, top_n=top_n),
        grid=(bsz, nb),
        in_specs=[pl.BlockSpec((1, blk, ATT_Q_WIDTH), lambda b, n: (b, n, q_blk)),
                  pl.BlockSpec((1, ncp, LANES), lambda b, n: (b, 0, 0)),
                  pl.BlockSpec((1, ATT_KV_HEADS, LANES, ncp), lambda b, n: (b, 0, 0, 0)),
                  pl.BlockSpec((ns, ncp), lambda b, n: (0, 0)),
                  pl.BlockSpec((1, blk, LANES), lambda b, n: (b, n, gate_blk))],
        out_specs=[pl.BlockSpec((1, blk, 512), lambda b, n: (b, n, 0)),
                   pl.BlockSpec((1, ATT_KV_HEADS, ns, blk), lambda b, n: (b, 0, 0, n)),
                   pl.BlockSpec((1, 1, ATT_KV_HEADS * 8, ns), lambda b, n: (b, n, 0, 0))],
        out_shape=[jax.ShapeDtypeStruct((bsz, seq, 512), F32),
                   jax.ShapeDtypeStruct((bsz, ATT_KV_HEADS, ns, seq), F32),
                   jax.ShapeDtypeStruct((bsz, nb, ATT_KV_HEADS * 8, ns), F32)],
        scratch_shapes=[pltpu.VMEM((512, blk), F32), pltpu.VMEM((ATT_KV_HEADS, ns, blk), F32)],
        compiler_params=_cparams("arbitrary", "arbitrary"),
        name=name,
    )(att, kc, vct, _overlap_matrix_t(seq), gate)


def _k_aug_body(k_ref, o_ref):
    k = k_ref[0].astype(F32)
    rows = k.shape[0]
    kin = lax.broadcasted_iota(jnp.int32, (rows, LANES), 0) % SEL_KEY_TILE
    lane = lax.broadcasted_iota(jnp.int32, (rows, LANES), 1)
    for g in range(ATT_KV_HEADS):
        f = lane - (ATT_HEAD_DIM if g == 0 else 0)
        feat = jnp.where(f == kin // NSA_SEL_LEN, 1.0, 0.0)
        feat = jnp.where(f == SEL_FEATS, (kin % 256).astype(F32), feat)
        feat = jnp.where(f == SEL_FEATS + 1, (kin // 256 * 256).astype(F32), feat)
        feat = jnp.where((f == SEL_FEATS + 2) | (f == SEL_FEATS + 3), 1.0, feat)
        o_ref[0, g] = jnp.where(lane // ATT_HEAD_DIM == g, k, feat).astype(BF16)


def k_augmented(arr, k_blk, name):
    bsz, seq, _ = arr.shape
    rows = min(2048, seq)
    return pl.pallas_call(
        _k_aug_body,
        grid=(bsz, seq // rows),
        in_specs=[pl.BlockSpec((1, rows, LANES), lambda b, n: (b, n, k_blk))],
        out_specs=pl.BlockSpec((1, ATT_KV_HEADS, rows, LANES), lambda b, n: (b, 0, n, 0)),
        out_shape=jax.ShapeDtypeStruct((bsz, ATT_KV_HEADS, seq, LANES), BF16),
        compiler_params=_cparams("arbitrary", "arbitrary"),
        name=name,
    )(arr)


def _query_feature_rows():
    out = np.zeros((ATT_KV_HEADS, 2 * SEL_FEATS, ATT_REP * ATTN_BLOCK), np.float32)
    qin = np.arange(ATTN_BLOCK, dtype=np.float32)
    for g in range(ATT_KV_HEADS):
        for r in range(ATT_REP):
            slope = ALIBI_SLOPES[g * ATT_REP + r]
            cols = slice(r * ATTN_BLOCK, (r + 1) * ATTN_BLOCK)
            out[g, 0, cols] = slope
            out[g, 1, cols] = slope
            out[g, 3, cols] = -slope * qin
            out[g, SEL_FEATS + 2, cols] = -slope
    return jnp.asarray(out)


def _nsa_sel_body(tiles_ref, cnt_ref, q_ref, k_ref, vt_ref, m_ref, gate_ref, fq_ref, o_ref, ot_ref,
                  acc_ref, mx_ref, dq_ref, sa_ref, sb_ref, *, ntl):
    blk = ATTN_BLOCK
    hd = ATT_HEAD_DIM
    tk = SEL_KEY_TILE
    sl = NSA_SEL_LEN
    per = tk // sl
    b = pl.program_id(0)
    n = pl.program_id(1)
    nb = pl.num_programs(1)
    krow = lax.broadcasted_iota(jnp.int32, (tk, blk), 0)
    qcol = lax.broadcasted_iota(jnp.int32, (tk, blk), 1)
    dq_ref[...] = (qcol - krow).astype(F32)
    gate_t = jnp.transpose(_sigmoid(gate_ref[0]))
    pad_rows = jnp.zeros((hd - 2 * SEL_FEATS, ATT_REP * blk), BF16)
    for g in range(ATT_KV_HEADS):
        one = _ones_row(g)
        lrow = (b * nb + n) * ATT_KV_HEADS + g
        qt = jnp.concatenate(
            [jnp.transpose(q_ref[0, :, (g * ATT_REP + r) * LANES:(g * ATT_REP + r + 1) * LANES]
                           .astype(F32))[g * hd:(g + 1) * hd] for r in range(ATT_REP)],
            axis=1).astype(BF16)
        acc_ref[...] = jnp.zeros_like(acc_ref)
        mx_ref[...] = jnp.full_like(mx_ref, NEG_INF)

        def tile_scores(j, live):
            t = tiles_ref[lrow * ntl + j]
            base = pl.multiple_of(t * tk, tk)
            sel8 = m_ref[0, g, pl.ds(pl.multiple_of(t * per, per), per), :]
            if live is not None:
                sel8 = jnp.where(live, sel8, 0.0)
            off = (n * blk - base).astype(F32)
            mask_rows = jnp.concatenate([(sel8 - 1.0) * MASK_BIG] * ATT_REP, axis=1)
            alibi_rows = fq_ref[g, 0:SEL_FEATS, :] + off * fq_ref[g, SEL_FEATS:2 * SEL_FEATS, :]
            feats = jnp.concatenate([mask_rows, alibi_rows], axis=0).astype(BF16)
            rhs = jnp.concatenate([qt, feats, pad_rows] if g == 0 else [feats, pad_rows, qt], axis=0)
            return jnp.dot(k_ref[0, g, pl.ds(base, tk), :], rhs,
                           preferred_element_type=F32)

        def tile_finish(j, st4, diag):
            base = pl.multiple_of(tiles_ref[lrow * ntl + j] * tk, tk)
            off = (n * blk - base).astype(F32)
            if diag:
                causal = (dq_ref[...] + off) >= 0.0
            ps, ms = [], []
            for r in range(ATT_REP):
                st = _head(st4, r)
                if diag:
                    st = jnp.where(causal, st, NEG_INF)
                m = jnp.max(st, axis=0, keepdims=True)
                ps.append(jnp.exp(st - m).astype(BF16))
                ms.append(m)
            acc = jnp.dot(vt_ref[0, g, :, pl.ds(base, tk)], jnp.concatenate(ps, axis=1),
                          preferred_element_type=F32)
            return jnp.concatenate(ms, axis=1), acc

        cnt = cnt_ref[lrow]

        def scores_into(s_ref, j):
            s_ref[...] = tile_scores(jnp.minimum(j, ntl - 1), j < cnt)

        def finish_from(s_ref, j, diag):
            m_t, acc_t = tile_finish(jnp.minimum(j, ntl - 1), s_ref[...], diag)
            m_old = mx_ref[...]
            m_new = jnp.maximum(m_old, m_t)
            acc_ref[...] = jnp.exp(m_old - m_new) * acc_ref[...] + jnp.exp(m_t - m_new) * acc_t
            mx_ref[...] = m_new

        def two_tiles(k, diag):
            scores_into(sb_ref, 2 * k + 1)
            finish_from(sa_ref, 2 * k, diag)
            scores_into(sa_ref, 2 * k + 2)
            finish_from(sb_ref, 2 * k + 1, False)

        scores_into(sa_ref, 0)
        two_tiles(0, True)

        def later(k, carry):
            two_tiles(k, False)
            return carry

        lax.fori_loop(1, (cnt + 1) // 2, later, 0)
        acc4 = acc_ref[...]
        for r in range(ATT_REP):
            h = g * ATT_REP + r
            acc = _head(acc4, r)
            ot = acc[g * hd:(g + 1) * hd] / acc[one:one + 1]
            ot_ref[h * hd:(h + 1) * hd, :] = ot * gate_t[3 * h + 1:3 * h + 2]
    o_ref[0] = jnp.transpose(ot_ref[...])


def _tile_lists(act, seq):
    bsz, nb = act.shape[:2]
    ns = act.shape[-1]
    tk = SEL_KEY_TILE
    ntl = seq // tk
    per = tk // NSA_SEL_LEN
    cnt_blk = act.reshape(bsz, nb, ATT_KV_HEADS, 8, ns)[:, :, :, 0, :]
    hit = cnt_blk.reshape(bsz, nb, ATT_KV_HEADS, ntl, per).sum(-1) > 0.5
    tidx = jnp.arange(ntl, dtype=jnp.int32)
    diag = (jnp.arange(nb, dtype=jnp.int32) * ATTN_BLOCK + ATTN_BLOCK - 1) // tk
    active = hit & (tidx[None, None, None, :] <= diag[None, :, None, None])
    act_i = active.astype(jnp.int32)
    rank = jnp.cumsum(act_i[..., ::-1], axis=-1)[..., ::-1] - 1
    slot = (active[..., :, None] & (rank[..., :, None] == tidx)).astype(jnp.int32)
    tiles = (slot * tidx[:, None]).sum(-2).reshape(-1)
    cnt = act_i.sum(-1).reshape(-1)
    return tiles, cnt, ntl


def nsa_selected(att, q_blk, k_blk, v_blk, mask, act, gate, gate_blk, name):
    bsz, seq, _ = att.shape
    blk = ATTN_BLOCK
    ns = seq // NSA_SEL_LEN
    tiles, cnt, ntl = _tile_lists(act, seq)
    vt = v_transposed(att, v_blk, name + "_vt")
    kaug = k_augmented(att, k_blk, name + "_k")
    grid_spec = pltpu.PrefetchScalarGridSpec(
        num_scalar_prefetch=2,
        grid=(bsz, seq // blk),
        in_specs=[pl.BlockSpec((1, blk, ATT_Q_WIDTH), lambda b, n, *_: (b, n, q_blk)),
                  pl.BlockSpec((1, ATT_KV_HEADS, seq, LANES), lambda b, n, *_: (b, 0, 0, 0)),
                  pl.BlockSpec((1, ATT_KV_HEADS, LANES, seq), lambda b, n, *_: (b, 0, 0, 0)),
                  pl.BlockSpec((1, ATT_KV_HEADS, ns, blk), lambda b, n, *_: (b, 0, 0, n)),
                  pl.BlockSpec((1, blk, LANES), lambda b, n, *_: (b, n, gate_blk)),
                  pl.BlockSpec((ATT_KV_HEADS, 2 * SEL_FEATS, ATT_REP * blk), lambda b, n, *_: (0, 0, 0))],
        out_specs=pl.BlockSpec((1, blk, 512), lambda b, n, *_: (b, n, 0)),
        scratch_shapes=[pltpu.VMEM((512, blk), F32),
                        pltpu.VMEM((LANES, ATT_REP * blk), F32),
                        pltpu.VMEM((1, ATT_REP * blk), F32),
                        pltpu.VMEM((SEL_KEY_TILE, blk), F32),
                        pltpu.VMEM((SEL_KEY_TILE, ATT_REP * blk), F32),
                        pltpu.VMEM((SEL_KEY_TILE, ATT_REP * blk), F32)],
    )
    return pl.pallas_call(
        functools.partial(_nsa_sel_body, ntl=ntl),
        grid_spec=grid_spec,
        out_shape=jax.ShapeDtypeStruct((bsz, seq, 512), F32),
        compiler_params=_cparams("arbitrary", "arbitrary"),
        name=name,
    )(tiles, cnt, att, kaug, vt, mask, gate, _query_feature_rows())


def _hgrn_body(q_ref, f_ref, i_ref, g_ref, lbp_ref, ng_ref, o_ref, st_ref, b_ref, k_ref, *, layer):
    blk = HGRN_BLOCK
    ch = HGRN_CHUNK
    dk = HGRN_DIM

    @pl.when(pl.program_id(1) == 0)
    def _():
        st_ref[...] = jnp.zeros_like(st_ref)

    lbp = lbp_ref[...]
    e = jnp.exp(lbp - jnp.max(lbp, axis=0, keepdims=True))
    sm = e / jnp.sum(e, axis=0, keepdims=True)
    lb = jnp.zeros((1, lbp.shape[1]), F32)
    for d in range(1, layer + 1):
        lb = lb + sm[d:d + 1]
    z = f_ref[0]
    f = lb + (1.0 - lb) * _sigmoid(z)
    logf = jnp.log(jnp.maximum(f, HGRN_MIN_F))
    k_ref[...] = (1.0 - lb) * _sigmoid(-z)
    tr = lax.broadcasted_iota(jnp.int32, (blk, blk), 0)
    tc = lax.broadcasted_iota(jnp.int32, (blk, blk), 1)
    tri = jnp.where((tr // ch == tc // ch) & (tc <= tr), 1.0, 0.0)
    b_ref[...] = _prefix_dot(tri, logf)
    cr = lax.broadcasted_iota(jnp.int32, (ch, ch), 0)
    cc = lax.broadcasted_iota(jnp.int32, (ch, ch), 1)
    causal = cc <= cr
    ng = ng_ref[...]

    for c in range(blk // ch):
        r0 = c * ch
        bc = b_ref[pl.ds(r0, ch), :]
        qc = q_ref[0, pl.ds(r0, ch), :]
        kc = k_ref[pl.ds(r0, ch), :]
        vc = i_ref[0, pl.ds(r0, ch), :]
        gc = g_ref[0, pl.ds(r0, ch), :]
        b_mid = bc[ch // 2:ch // 2 + 1]
        b_last = bc[ch - 1:ch]
        qa = qc * jnp.exp(bc - b_mid)
        ka = kc * jnp.exp(b_mid - bc)
        qe = qc * jnp.exp(bc)
        kl = kc * jnp.exp(b_last - bc)
        dec = jnp.exp(b_last)
        for h in range(HGRN_HEADS):
            sl = slice(h * dk, (h + 1) * dk)
            a = jnp.where(causal, _dot_nt(qa[:, sl], ka[:, sl]), 0.0)
            st = st_ref[h]
            o = _dot(a, vc[:, sl]) + _dot_nt(qe[:, sl], st)
            st_ref[h] = st * dec[:, sl] + _dot_tn(vc[:, sl], kl[:, sl])
            o = _rms(o, ng[:, sl]) * _silu(gc[:, sl])
            o_ref[0, pl.ds(r0, ch), sl] = o


def hgrn2(hproj, first_blk, lower_bounds, norm_g, layer, name):
    bsz, seq, _ = hproj.shape
    blk = HGRN_BLOCK
    wid = HGRN_HEADS * HGRN_DIM
    depth = lower_bounds.shape[0]

    def col(j):
        return pl.BlockSpec((1, blk, wid), lambda b, n: (b, n, first_blk + j))

    return pl.pallas_call(
        functools.partial(_hgrn_body, layer=layer),
        grid=(bsz, seq // blk),
        in_specs=[col(0), col(1), col(2), col(3),
                  pl.BlockSpec((depth, wid), lambda b, n: (0, 0)),
                  pl.BlockSpec((1, wid), lambda b, n: (0, 0))],
        out_specs=pl.BlockSpec((1, blk, wid), lambda b, n: (b, n, 0)),
        out_shape=jax.ShapeDtypeStruct((bsz, seq, wid), F32),
        scratch_shapes=[pltpu.VMEM((HGRN_HEADS, HGRN_DIM, HGRN_DIM), F32),
                        pltpu.VMEM((blk, wid), F32),
                        pltpu.VMEM((blk, wid), F32)],
        compiler_params=_cparams("arbitrary", "arbitrary"),
        name=name,
    )(hproj, hproj, hproj, hproj, lower_bounds, norm_g.reshape(1, wid))


def _ssd_body(xbc_ref, dt_ref, z_ref, cw_ref, cb_ref, dtb_ref, alog_ref, alogw_ref, dskw_ref, ng_ref,
              o_ref, xp_ref, act_ref, st_ref, y_ref):
    blk = SSM_BLOCK
    ch = SSM_CHUNK
    hp = SSM_HEAD_DIM
    ns = SSM_STATE
    rep = SSM_HEADS // SSM_GROUPS
    pad = 8

    @pl.when(pl.program_id(1) == 0)
    def _():
        xp_ref[0:pad, :] = jnp.zeros((pad, SSM_CONV_DIM), F32)
        st_ref[...] = jnp.zeros_like(st_ref)

    xin = xbc_ref[0]
    xp_ref[pad:pad + blk, :] = xin
    cw = cw_ref[...]
    conv = cb_ref[...] + cw[SSM_CONV - 1:SSM_CONV] * xin
    for j in range(SSM_CONV - 1):
        shift = SSM_CONV - 1 - j
        conv = conv + cw[j:j + 1] * xp_ref[pl.ds(pad - shift, blk), :]
    xp_ref[0:pad, :] = xin[blk - pad:blk]
    act_ref[...] = _silu(conv)

    def softplus(v):
        return jnp.maximum(v, 0.0) + jnp.log(1.0 + jnp.exp(-jnp.abs(v)))

    dt = softplus(dt_ref[0] + dtb_ref[...])
    a_all = dt * (-jnp.exp(alog_ref[...]))
    er = lax.broadcasted_iota(jnp.int32, (LANES, SSM_INNER), 0)
    ec = lax.broadcasted_iota(jnp.int32, (LANES, SSM_INNER), 1)
    spread = jnp.where(ec // hp == er, 1.0, 0.0).astype(BF16)
    dtw = sum(jnp.dot(p, spread, preferred_element_type=F32) for p in _split3(dt))
    aw = dtw * (-jnp.exp(alogw_ref[...]))
    cr = lax.broadcasted_iota(jnp.int32, (ch, ch), 0)
    cc = lax.broadcasted_iota(jnp.int32, (ch, ch), 1)
    causal = cc <= cr
    tri = jnp.where(causal, 1.0, 0.0)
    tri_t = jnp.where(cr <= cc, 1.0, 0.0)

    for c in range(blk // ch):
        r0 = c * ch
        acs_t = _prefix_dot_tn(a_all[r0:r0 + ch], tri_t)
        acs = _prefix_dot(tri, aw[r0:r0 + ch])
        last = acs[ch - 1:ch]
        dtc = dtw[r0:r0 + ch]
        xc = act_ref[r0:r0 + ch, 0:SSM_INNER]
        xdt = dtc * xc
        grow = jnp.exp(acs)
        wdec = jnp.exp(last - acs) * dtc
        keep = jnp.exp(last)
        skip = dskw_ref[...] * xc
        for g in range(SSM_GROUPS):
            bm = act_ref[r0:r0 + ch, SSM_INNER + g * ns:SSM_INNER + (g + 1) * ns]
            cm = act_ref[r0:r0 + ch, SSM_INNER + SSM_GROUPS * ns + g * ns:
                         SSM_INNER + SSM_GROUPS * ns + (g + 1) * ns]
            cb = _dot_nt(cm, bm)
            for r in range(rep):
                h = g * rep + r
                hl = slice(h * hp, (h + 1) * hp)
                lmat = jnp.where(causal, jnp.exp(acs[:, hl] - acs_t[h:h + 1, :]), 0.0)
                st = st_ref[h]
                y = _dot(cb * lmat, xdt[:, hl]) + grow[:, hl] * _dot(cm, st) + skip[:, hl]
                st_ref[h] = keep[:, hl] * st + _dot_tn(bm * wdec[:, hl], xc[:, hl])
                y_ref[r0:r0 + ch, hl] = y
    yz = y_ref[...] * _silu(z_ref[0])
    o_ref[0] = _rms(yz, ng_ref[...])


def mamba2(sproj, conv_w, conv_b, dt_bias, a_log, d_skip, norm_g, name):
    bsz, seq, _ = sproj.shape
    blk = SSM_BLOCK
    padh = LANES - SSM_HEADS
    dtb = jnp.pad(dt_bias, (0, padh)).reshape(1, LANES)
    alog = jnp.pad(a_log, (0, padh)).reshape(1, LANES)

    def wide(v):
        return jnp.repeat(v, SSM_HEAD_DIM).reshape(1, SSM_INNER)

    def row(w):
        return pl.BlockSpec((1, w), lambda b, n: (0, 0))

    return pl.pallas_call(
        _ssd_body,
        grid=(bsz, seq // blk),
        in_specs=[pl.BlockSpec((1, blk, SSM_CONV_DIM), lambda b, n: (b, n, 0)),
                  pl.BlockSpec((1, blk, LANES), lambda b, n: (b, n, 6)),
                  pl.BlockSpec((1, blk, SSM_INNER), lambda b, n: (b, n, 2)),
                  pl.BlockSpec((SSM_CONV, SSM_CONV_DIM), lambda b, n: (0, 0)),
                  row(SSM_CONV_DIM), row(LANES), row(LANES), row(SSM_INNER), row(SSM_INNER),
                  row(SSM_INNER)],
        out_specs=pl.BlockSpec((1, blk, SSM_INNER), lambda b, n: (b, n, 0)),
        out_shape=jax.ShapeDtypeStruct((bsz, seq, SSM_INNER), F32),
        scratch_shapes=[pltpu.VMEM((blk + 8, SSM_CONV_DIM), F32),
                        pltpu.VMEM((blk, SSM_CONV_DIM), F32),
                        pltpu.VMEM((SSM_HEADS, SSM_STATE, SSM_HEAD_DIM), F32),
                        pltpu.VMEM((blk, SSM_INNER), F32)],
        compiler_params=_cparams("arbitrary", "arbitrary"),
        name=name,
    )(sproj, sproj, sproj, conv_w, conv_b.reshape(1, -1), dtb, alog, wide(a_log), wide(d_skip),
      norm_g.reshape(1, -1))


def _merge_body(ya_ref, yb_ref, yc1_ref, yc2_ref, yc3_ref, yd_ref, x_ref, gm_ref, wmg_ref, wbr_ref,
                wout_ref, gf_ref, wr_ref, o_ref, gid_ref):
    ys = (ya_ref[...], yb_ref[...], yc1_ref[...] + yc2_ref[...] + yc3_ref[...], yd_ref[...])
    x = x_ref[...]
    hn = _rms(x, gm_ref[...]).astype(BF16)
    u = None
    for nbr in range(N_BRANCH):
        gate = jnp.dot(hn, wmg_ref[:, nbr * D_MODEL:(nbr + 1) * D_MODEL], preferred_element_type=F32)
        t = _sigmoid(gate) * _dot(ys[nbr], wbr_ref[nbr])
        u = t if u is None else u + t
    xn = x + _dot(u, wout_ref[...])
    o_ref[...] = xn
    gid_ref[...] = _top_group(xn, gf_ref, wr_ref)


def merge(ya, yb, yc1, yc2, yc3, yd, x2d, g_mix, w_mgate, w_branch, w_out, g_ffn, wr, tm, name):
    m = x2d.shape[0]

    def rows(w):
        return pl.BlockSpec((tm, w), lambda i: (i, 0))

    return pl.pallas_call(
        _merge_body,
        grid=(m // tm,),
        in_specs=[rows(512)] * 6 + [rows(D_MODEL),
                                    pl.BlockSpec((1, D_MODEL), lambda i: (0, 0)),
                                    pl.BlockSpec((D_MODEL, N_BRANCH * D_MODEL), lambda i: (0, 0)),
                                    pl.BlockSpec((N_BRANCH, BRANCH_WIDTH, D_MODEL), lambda i: (0, 0, 0)),
                                    pl.BlockSpec((D_MODEL, D_MODEL), lambda i: (0, 0)),
                                    pl.BlockSpec((1, D_MODEL), lambda i: (0, 0)),
                                    pl.BlockSpec((2, D_MODEL, LANES), lambda i: (0, 0, 0))],
        out_specs=[rows(D_MODEL), rows(LANES)],
        out_shape=[jax.ShapeDtypeStruct((m, D_MODEL), F32), jax.ShapeDtypeStruct((m, LANES), jnp.int32)],
        compiler_params=_cparams("arbitrary"),
        name=name,
    )(ya, yb, yc1, yc2, yc3, yd, x2d, g_mix.reshape(1, -1), w_mgate, w_branch.astype(BF16),
      w_out.astype(BF16), g_ffn.reshape(1, -1), wr)


def _router_logits(hn, wr_ref):
    h_hi = hn.astype(BF16)
    h_lo = (hn - h_hi.astype(F32)).astype(BF16)
    return (jnp.dot(h_hi, wr_ref[0], preferred_element_type=F32)
            + jnp.dot(h_hi, wr_ref[1], preferred_element_type=F32)
            + jnp.dot(h_lo, wr_ref[0], preferred_element_type=F32))


def _group_lanes(lane):
    return (lane >= MOE_EXPERTS) & (lane < MOE_EXPERTS + MOE_GROUPS)


def _top_group(x, g_ref, wr_ref):
    tm = x.shape[0]
    lane = lax.broadcasted_iota(jnp.int32, (tm, LANES), 1)
    logits = _router_logits(_rms(x, g_ref[...]), wr_ref)
    lg = jnp.where(_group_lanes(lane), logits, -jnp.inf)
    mg = jnp.max(lg, axis=-1, keepdims=True)
    gi = jnp.min(jnp.where(lg == mg, lane.astype(F32), 1e9), axis=-1, keepdims=True) - MOE_EXPERTS
    return jnp.broadcast_to(gi, (tm, LANES)).astype(jnp.int32)


def _gather_body(idx_ref, nlive_ref, src_ref, o_ref, buf, sem):
    rows = o_ref.shape[0]
    i = pl.program_id(0)
    nlive = nlive_ref[0]

    def live(tile):
        return tile * rows < nlive

    def issue_tile(tile):
        base = tile * rows
        slot = tile % 2

        def issue(r, carry):
            pltpu.make_async_copy(src_ref.at[pl.ds(idx_ref[base + r], 1)], buf.at[slot, pl.ds(r, 1)],
                                  sem.at[slot]).start()
            return carry

        lax.fori_loop(0, rows, issue, 0, unroll=8)

    @pl.when((i == 0) & live(0))
    def _():
        issue_tile(0)

    @pl.when((i + 1 < pl.num_programs(0)) & live(i + 1))
    def _():
        issue_tile(i + 1)

    @pl.when(live(i))
    def _():
        slot = i % 2
        pltpu.make_async_copy(src_ref.at[pl.ds(0, rows)], buf.at[slot], sem.at[slot]).wait()
        o_ref[...] = buf[slot]

    @pl.when(jnp.logical_not(live(i)))
    def _():
        o_ref[...] = jnp.zeros_like(o_ref)


def gather_rows(src, idx, nlive, rows, name):
    n = idx.shape[0]
    d = src.shape[1]
    grid_spec = pltpu.PrefetchScalarGridSpec(
        num_scalar_prefetch=2,
        grid=(n // rows,),
        in_specs=[pl.BlockSpec(memory_space=pl.ANY)],
        out_specs=pl.BlockSpec((rows, d), lambda i, *_: (i, 0)),
        scratch_shapes=[pltpu.VMEM((2, rows, d), src.dtype), pltpu.SemaphoreType.DMA((2,))],
    )
    return pl.pallas_call(
        _gather_body,
        grid_spec=grid_spec,
        out_shape=jax.ShapeDtypeStruct((n, d), src.dtype),
        compiler_params=_cparams("arbitrary"),
        name=name,
    )(idx, jnp.reshape(nlive, (1,)).astype(jnp.int32), src)


def _moe_body(tg_ref, x_ref, g_ref, wr_ref, wg_ref, wu_ref, wd_ref, o_ref, hn_ref, comb_ref):
    e = pl.program_id(1)
    grp = tg_ref[pl.program_id(0)]
    tm = x_ref.shape[0]
    lane = lax.broadcasted_iota(jnp.int32, (tm, LANES), 1)
    lanef = lane.astype(F32)
    live = grp < MOE_GROUPS

    @pl.when(e == 0)
    def _():
        o_ref[...] = x_ref[...]

    @pl.when((e == 0) & live)
    def _():
        hn = _rms(x_ref[...], g_ref[...])
        hn_ref[...] = hn.astype(BF16)
        logits = _router_logits(hn, wr_ref)
        is_grp = _group_lanes(lane)
        lg = jnp.where(is_grp, logits, -jnp.inf)
        mg = jnp.max(lg, axis=-1, keepdims=True)
        sg = jnp.sum(jnp.where(is_grp, jnp.exp(lg - mg), 0.0), axis=-1, keepdims=True)
        lt = jnp.sum(jnp.where(lane == MOE_EXPERTS + grp, logits, 0.0), axis=-1, keepdims=True)
        g_w = jnp.exp(lt - mg) / sg
        in_grp = (lane < MOE_EXPERTS) & (lane // MOE_EPG == grp)
        le = jnp.where(in_grp, logits, -jnp.inf)
        m1 = jnp.max(le, axis=-1, keepdims=True)
        i1 = jnp.min(jnp.where(le == m1, lanef, 1e9), axis=-1, keepdims=True)
        le2 = jnp.where(lanef == i1, -jnp.inf, le)
        m2 = jnp.max(le2, axis=-1, keepdims=True)
        i2 = jnp.min(jnp.where(le2 == m2, lanef, 1e9), axis=-1, keepdims=True)
        e2 = jnp.exp(m2 - m1)
        den = 1.0 + e2
        comb_ref[...] = (jnp.where(lanef == i1, g_w / den, 0.0)
                         + jnp.where(lanef == i2, g_w * e2 / den, 0.0))

    @pl.when(live)
    def _():
        hn = hn_ref[...]
        gate = jnp.dot(hn, wg_ref[0].astype(BF16), preferred_element_type=F32)
        up = jnp.dot(hn, wu_ref[0].astype(BF16), preferred_element_type=F32)
        ce = jnp.sum(jnp.where(lane == grp * MOE_EPG + e, comb_ref[...], 0.0), axis=-1, keepdims=True)
        o_ref[...] += _dot(_silu(gate) * up * ce, wd_ref[0])


def _moe_plan(gid, tm):
    t = gid.shape[0]
    onehot = (gid[:, None] == jnp.arange(MOE_GROUPS, dtype=jnp.int32)).astype(jnp.int32)
    csum = jnp.cumsum(onehot, axis=0)
    counts = csum[-1]
    rank = (csum * onehot).sum(axis=1) - 1
    padded = (counts + tm - 1) // tm * tm
    pend = jnp.cumsum(padded)
    dest = (pend - padded)[gid] + rank
    r_pad = t + MOE_GROUPS * tm
    row_token = (jnp.arange(r_pad, dtype=jnp.int32) % t).at[dest].set(jnp.arange(t, dtype=jnp.int32))
    tile_start = jnp.arange(r_pad // tm, dtype=jnp.int32) * tm
    tile_group = (tile_start[:, None] >= pend[None, :]).sum(axis=1)
    return row_token, dest.astype(jnp.int32), tile_group.astype(jnp.int32), pend[-1]


def router_weights(w_grp, w_exp):
    wr32 = jnp.concatenate([w_exp, w_grp,
                            jnp.zeros((D_MODEL, LANES - MOE_EXPERTS - MOE_GROUPS), F32)], axis=1)
    wr_hi = wr32.astype(BF16)
    return jnp.stack([wr_hi, (wr32 - wr_hi.astype(F32)).astype(BF16)])


def moe(x2d, gid, g_ffn, wr, w_gate, w_up, w_down, layer, tm, name):
    m = x2d.shape[0]
    gf = g_ffn.reshape(1, -1)
    row_token, dest, tile_group, n_sorted = _moe_plan(gid, tm)
    rows_dma = min(MOE_GATHER_ROWS, tm)
    xs = gather_rows(x2d, row_token, n_sorted, rows_dma, name + "_gather")

    def expert(i, e, tg):
        return (layer * MOE_EXPERTS + jnp.minimum(tg[i], MOE_GROUPS - 1) * MOE_EPG + e, 0, 0)

    grid_spec = pltpu.PrefetchScalarGridSpec(
        num_scalar_prefetch=1,
        grid=(xs.shape[0] // tm, MOE_EPG),
        in_specs=[pl.BlockSpec((tm, D_MODEL), lambda i, e, tg: (i, 0)),
                  pl.BlockSpec((1, D_MODEL), lambda i, e, tg: (0, 0)),
                  pl.BlockSpec((2, D_MODEL, LANES), lambda i, e, tg: (0, 0, 0)),
                  pl.BlockSpec((1, D_MODEL, MOE_FF), expert),
                  pl.BlockSpec((1, D_MODEL, MOE_FF), expert),
                  pl.BlockSpec((1, MOE_FF, D_MODEL), expert)],
        out_specs=pl.BlockSpec((tm, D_MODEL), lambda i, e, tg: (i, 0)),
        scratch_shapes=[pltpu.VMEM((tm, D_MODEL), BF16), pltpu.VMEM((tm, LANES), F32)],
    )
    ys = pl.pallas_call(
        _moe_body,
        grid_spec=grid_spec,
        out_shape=jax.ShapeDtypeStruct(xs.shape, F32),
        compiler_params=_cparams("arbitrary", "arbitrary"),
        name=name,
    )(tile_group, xs, gf, wr, *[w.reshape((-1,) + w.shape[2:]) for w in (w_gate, w_up, w_down)])
    return gather_rows(ys, dest, jnp.int32(m), rows_dma, name + "_scatter")


def _ple_body(x_ref, p_ref, g_ref, wg_ref, wp_ref, gf_ref, o_ref, *, final):
    x = x_ref[...]
    gate = _sigmoid(_dot(_rms(x, g_ref[...]), wg_ref[...]))
    xn = x + _dot(p_ref[...], wp_ref[...]) * gate
    if final:
        xn = _rms(xn, gf_ref[...])
    o_ref[...] = xn


def ple(x2d, p2d, g_ple, w_gate, w_proj, g_final, final, tm, name):
    m = x2d.shape[0]
    return pl.pallas_call(
        functools.partial(_ple_body, final=final),
        grid=(m // tm,),
        in_specs=[pl.BlockSpec((tm, D_MODEL), lambda i: (i, 0)),
                  pl.BlockSpec((tm, PLE_DIM), lambda i: (i, 0)),
                  pl.BlockSpec((1, D_MODEL), lambda i: (0, 0)),
                  pl.BlockSpec((D_MODEL, D_MODEL), lambda i: (0, 0)),
                  pl.BlockSpec((PLE_DIM, D_MODEL), lambda i: (0, 0)),
                  pl.BlockSpec((1, D_MODEL), lambda i: (0, 0))],
        out_specs=pl.BlockSpec((tm, D_MODEL), lambda i: (i, 0)),
        out_shape=jax.ShapeDtypeStruct((m, D_MODEL), F32),
        compiler_params=_cparams("arbitrary"),
        name=name,
    )(x2d, p2d, g_ple.reshape(1, -1), w_gate.astype(BF16), w_proj.astype(BF16),
      g_final.reshape(1, -1))


def _cols(w, *names):
    return [w[:, _OFF[n][0]:_OFF[n][0] + _OFF[n][1]] for n in names]


def _padcols(w, width):
    return jnp.pad(w, ((0, 0), (0, width - w.shape[1])))


def _pad_q_heads(wq):
    hd = ATT_HEAD_DIM
    zero = jnp.zeros((wq.shape[0], hd), wq.dtype)
    cols = []
    for h in range(ATT_HEADS):
        blk = wq[:, h * hd:(h + 1) * hd] * (hd ** -0.5)
        cols += [blk, zero] if h // ATT_REP == 0 else [zero, blk]
    return jnp.concatenate(cols, axis=1)


def _split_w_in(w):
    swa_q, nsa_q = _cols(w, 'swa_q', 'nsa_q')
    w_att = jnp.concatenate([_pad_q_heads(swa_q), _pad_q_heads(nsa_q)] + _cols(w, *KV_BLK), axis=1)
    (xbc, dt, ngate, z) = _cols(w, 'ssm_xbc', 'ssm_dt', 'nsa_gate', 'ssm_z')
    w_rec = jnp.concatenate([xbc, _padcols(dt, LANES), _padcols(ngate, LANES), z]
                            + _cols(w, 'hgrn_q', 'hgrn_f', 'hgrn_i', 'hgrn_g'), axis=1)
    (w_mg,) = _cols(w, 'merge_gate')
    return [a.astype(BF16) for a in (w_att, w_rec, w_mg)]


def _mixers(i, att, sproj, attn_sinks, hgrn_lower_bounds, hgrn_norm_g, nsa_pos_k, nsa_pos_v,
            nsa_cmp_w1_k, nsa_cmp_w2_k, nsa_cmp_w1_v, nsa_cmp_w2_v, ssm_conv_w, ssm_conv_b,
            ssm_dt_bias, ssm_A_log, ssm_D, ssm_norm_g):
    bsz, seq, _ = att.shape
    kv = KV_BLK
    gate_blk = 7
    y_a = banded_attention(att, 0, kv['swa_k'], kv['swa_v'], window=SWA_WINDOW, sinks=attn_sinks[i],
                           name=f"swa{i}")
    y_b = hgrn2(sproj, 3, hgrn_lower_bounds, hgrn_norm_g[i], i, name=f"hgrn{i}")
    ncp = seq // NSA_CMP_STRIDE

    def cmp_in(name):
        c0 = kv[name] * LANES
        return att[:, :, c0:c0 + LANES].reshape(bsz, ncp, NSA_CMP_STRIDE * LANES)

    kc = nsa_compress(cmp_in('nsa_k_cmp'), nsa_pos_k[i], nsa_cmp_w1_k[i], nsa_cmp_w2_k[i], name=f"cmpk{i}")
    vc = nsa_compress(cmp_in('nsa_v_cmp'), nsa_pos_v[i], nsa_cmp_w1_v[i], nsa_cmp_w2_v[i], name=f"cmpv{i}")
    y_c1, mask, act = nsa_cmp_select(att, 1, kc, vc, sproj, gate_blk, name=f"nsacmp{i}")
    y_c2 = nsa_selected(att, 1, kv['nsa_k_slc'], kv['nsa_v_slc'], mask, act, sproj, gate_blk,
                        name=f"nsasel{i}")
    y_c3 = banded_attention(att, 1, kv['nsa_k_win'], kv['nsa_v_win'], window=NSA_WINDOW, gate=sproj,
                            gate_blk=gate_blk, gate_col=2, name=f"nsawin{i}")
    y_d = mamba2(sproj, ssm_conv_w[i], ssm_conv_b[i], ssm_dt_bias[i], ssm_A_log[i], ssm_D[i],
                 ssm_norm_g[i], name=f"ssd{i}")
    return y_a, y_b, y_c1, y_c2, y_c3, y_d


def kernel(x, p, w_in, g_mix, attn_sinks, hgrn_lower_bounds, hgrn_norm_g, nsa_pos_k, nsa_pos_v,
           nsa_cmp_w1_k, nsa_cmp_w2_k, nsa_cmp_w1_v, nsa_cmp_w2_v, ssm_conv_w, ssm_conv_b,
           ssm_dt_bias, ssm_A_log, ssm_D, ssm_norm_g, w_branch, w_out, g_ffn, w_router_grp,
           w_router_exp, w_exp_gate, w_exp_up, w_exp_down, g_ple, w_ple_gate, w_ple_proj, g_final):
    bsz, seq, d = x.shape
    depth = w_in.shape[0]
    t = bsz * seq
    x2 = x.reshape(t, d)
    tm_proj = min(2048, t)
    tm_merge = min(256, t)
    tm_ple = min(512, t)
    tm_moe = min(1024, t)
    for i in range(depth):
        w_att, w_rec, w_mg = _split_w_in(w_in[i])
        att = norm_mm(x2, g_mix[i], w_att, BF16, tm_proj, 1024, f"proj_att{i}").reshape(bsz, seq, -1)
        sproj = norm_mm(x2, g_mix[i], w_rec, F32, tm_proj, 512, f"proj_rec{i}").reshape(bsz, seq, -1)
        ys = _mixers(i, att, sproj, attn_sinks, hgrn_lower_bounds, hgrn_norm_g, nsa_pos_k,
                     nsa_pos_v, nsa_cmp_w1_k, nsa_cmp_w2_k, nsa_cmp_w1_v, nsa_cmp_w2_v, ssm_conv_w,
                     ssm_conv_b, ssm_dt_bias, ssm_A_log, ssm_D, ssm_norm_g)
        ys = [y.reshape(t, -1) for y in ys]
        wr = router_weights(w_router_grp[i], w_router_exp[i])
        x2, gid = merge(*ys, x2, g_mix[i], w_mg, w_branch[i], w_out[i], g_ffn[i], wr, tm_merge, f"merge{i}")
        x2 = moe(x2, gid[:, 0], g_ffn[i], wr, w_exp_gate, w_exp_up, w_exp_down, i, tm_moe, f"moe{i}")
        x2 = ple(x2, p[i].reshape(t, -1), g_ple[i], w_ple_gate[i], w_ple_proj[i], g_final,
                 i == depth - 1, tm_ple, f"ple{i}")
    return x2.reshape(bsz, seq, d)
```
